```python
import jax, jax.numpy as jnp
from jax import lax
import numpy as np

D_MODEL = 1024
BATCH = 32
SEQ = 256
DEPTH = 1
DEC_BATCH = 4
DEC_SEQ = 1024
PAST_LEN = 256

GRID_W = 64
N_Q_HEADS = 8
N_KV_HEADS = 2
Q_PER_KV = N_Q_HEADS // N_KV_HEADS
HEAD_DIM = 64
WINDOW = 128
BLOCK = 128
ROPE_BASE = 10000.0
GLA_HEADS = 4
GLA_DK = 64
GLA_DV = 128
GLA_LOW_RANK = 16
GLA_TAU = 16.0
GLA_CHUNK = 16
D_FF = 2816
N_MOD = 9
EPS = 1e-6
NEG = -1e30

ATT_Q = N_Q_HEADS * HEAD_DIM
ATT_KV = N_KV_HEADS * HEAD_DIM
GLA_QK = GLA_HEADS * GLA_DK
GLA_V = GLA_HEADS * GLA_DV
MIX_WIDTH = ATT_Q + GLA_V
SPLIT_SIZES = (ATT_Q, ATT_KV, ATT_KV, GLA_QK, GLA_QK, GLA_V, GLA_V, GLA_LOW_RANK, GLA_LOW_RANK)
IN_WIDTH = ATT_Q + 2 * ATT_KV + 2 * GLA_QK + 2 * GLA_V + 2 * GLA_LOW_RANK

kernel_name = 'hybrid_gla_swa_diffusion_step'


def rmsnorm(x, g):
    xf = x.astype(jnp.float32)
    y = xf * lax.rsqrt(jnp.mean(xf * xf, axis=-1, keepdims=True) + EPS)
    return (y * g.astype(jnp.float32)).astype(x.dtype)


def swiglu(h, w1, w3, w2):
    return (jax.nn.silu(h @ w1) * (h @ w3)) @ w2


def ada_modulation(cond, w_ada, b_ada):
    m = jax.nn.silu(cond) @ w_ada + b_ada
    return m.reshape(cond.shape[0], N_MOD, D_MODEL)


def split_proj(p):
    points, acc = [], 0
    for s in SPLIT_SIZES[:-1]:
        acc += s
        points.append(acc)
    return jnp.split(p, points, axis=-1)


def _rotate_half(x, ang):
    f = ang.shape[-1]
    cos = jnp.cos(ang)[:, None, :]
    sin = jnp.sin(ang)[:, None, :]
    x1, x2 = x[..., :f], x[..., f:]
    return jnp.concatenate([x1 * cos - x2 * sin, x1 * sin + x2 * cos], axis=-1)


def rope_2d(x):
    L = x.shape[1]
    rows = L // GRID_W
    row = jnp.repeat(jnp.arange(rows, dtype=jnp.float32), GRID_W)
    col = jnp.tile(jnp.arange(GRID_W, dtype=jnp.float32), rows)
    half = HEAD_DIM // 2
    inv_freq = ROPE_BASE ** (-jnp.arange(0, half, 2, dtype=jnp.float32) / half)
    xf = x.astype(jnp.float32)
    xr = _rotate_half(xf[..., :half], row[:, None] * inv_freq[None, :])
    xc = _rotate_half(xf[..., half:], col[:, None] * inv_freq[None, :])
    return jnp.concatenate([xr, xc], axis=-1).astype(x.dtype)


def sink_softmax(s, sink):
    sk = sink.astype(jnp.float32).reshape(N_KV_HEADS, Q_PER_KV)[None, :, :, None, None]
    m = jnp.maximum(jnp.max(s, axis=-1, keepdims=True), sk)
    p = jnp.exp(s - m)
    return p / (jnp.sum(p, axis=-1, keepdims=True) + jnp.exp(sk - m))


def context_attention(q, k, v, sink):
    B, L = q.shape[0], q.shape[1]
    nb = L // BLOCK
    scale = HEAD_DIM ** -0.5
    qb = q.reshape(B, nb, BLOCK, N_KV_HEADS, Q_PER_KV, HEAD_DIM).swapaxes(0, 1)

    def one_block(qi):
        s = jnp.einsum('bqgrd,bkgd->bgrqk', qi, k, preferred_element_type=jnp.float32) * scale
        p = sink_softmax(s, sink)
        return jnp.einsum('bgrqk,bkgd->bqgrd', p.astype(v.dtype), v)

    o = lax.map(one_block, qb)
    return o.swapaxes(0, 1).reshape(B, L, ATT_Q)


def latent_attention(q, k, v, k_ctx, v_ctx, sink):
    B, L = q.shape[0], q.shape[1]
    nb = L // BLOCK
    scale = HEAD_DIM ** -0.5
    pad = ((0, 0), (BLOCK, BLOCK), (0, 0), (0, 0))
    kp = jnp.pad(k, pad).reshape(B, nb + 2, BLOCK, N_KV_HEADS, HEAD_DIM)
    vp = jnp.pad(v, pad).reshape(B, nb + 2, BLOCK, N_KV_HEADS, HEAD_DIM)
    k_band = jnp.concatenate([kp[:, :-2], kp[:, 1:-1], kp[:, 2:]], axis=2).swapaxes(0, 1)
    v_band = jnp.concatenate([vp[:, :-2], vp[:, 1:-1], vp[:, 2:]], axis=2).swapaxes(0, 1)
    qb = q.reshape(B, nb, BLOCK, N_KV_HEADS, Q_PER_KV, HEAD_DIM).swapaxes(0, 1)
    r = jnp.arange(BLOCK)[:, None]
    j = jnp.arange(3 * BLOCK)[None, :]
    in_window = jnp.abs(j - BLOCK - r) <= WINDOW

    def one_block(args):
        i, qi, ki, vi = args
        kpos = (i - 1) * BLOCK + j
        valid = in_window & (kpos >= 0) & (kpos < L)
        s_loc = jnp.einsum('bqgrd,bkgd->bgrqk', qi, ki, preferred_element_type=jnp.float32) * scale
        s_loc = jnp.where(valid, s_loc, NEG)
        s_ctx = jnp.einsum('bqgrd,bkgd->bgrqk', qi, k_ctx, preferred_element_type=jnp.float32) * scale
        p = sink_softmax(jnp.concatenate([s_loc, s_ctx], axis=-1), sink)
        vals = jnp.concatenate([vi, v_ctx], axis=1)
        return jnp.einsum('bgrqk,bkgd->bqgrd', p.astype(vals.dtype), vals)

    o = lax.map(one_block, (jnp.arange(nb), qb, k_band, v_band))
    return o.swapaxes(0, 1).reshape(B, L, ATT_Q)


def gla_scan(q, k, v, log_a, s0):
    B, L, H, dk = q.shape
    dv = v.shape[-1]
    n, C = L // GLA_CHUNK, GLA_CHUNK

    def chunks(t):
        return t.reshape(B, n, C, H, t.shape[-1]).transpose(0, 1, 3, 2, 4)

    qc, kc, vc, ac = chunks(q), chunks(k), chunks(v), chunks(log_a)
    b = jnp.cumsum(ac, axis=3)
    tri = jnp.tril(jnp.ones((C, C), dtype=bool))[:, :, None]
    diff = b[:, :, :, :, None, :] - b[:, :, :, None, :, :]
    decay = jnp.exp(jnp.where(tri, diff, NEG))
    scores = jnp.einsum('bnhtd,bnhsd,bnhtsd->bnhts', qc, kc, decay)
    o_intra = jnp.einsum('bnhts,bnhsv->bnhtv', scores, vc)
    b_last = b[:, :, :, -1:, :]
    q_in = qc * jnp.exp(b)
    k_in = kc * jnp.exp(b_last - b)
    a_last = jnp.exp(b_last[:, :, :, 0, :])

    def step(S, xs):
        qi, ki, vi, ai = xs
        o = jnp.einsum('bhtd,bhdv->bhtv', qi, S)
        S = ai[..., None] * S + jnp.einsum('bhtd,bhtv->bhdv', ki, vi)
        return S, o

    xs = (q_in.swapaxes(0, 1), k_in.swapaxes(0, 1), vc.swapaxes(0, 1), a_last.swapaxes(0, 1))
    s_final, o_inter = lax.scan(step, s0, xs)
    o = o_intra + o_inter.swapaxes(0, 1)
    return o.transpose(0, 1, 3, 2, 4).reshape(B, L, H, dv), s_final


def gla_mixer(gq, gk, gv, gg, lr_f, lr_b, w_up, b_up, norm_w, s0_f, s0_b):
    B, L = gq.shape[0], gq.shape[1]
    f32 = jnp.float32
    q = gq.astype(f32).reshape(B, L, GLA_HEADS, GLA_DK) * (GLA_DK ** -0.5)
    k = gk.astype(f32).reshape(B, L, GLA_HEADS, GLA_DK)
    v = gv.astype(f32).reshape(B, L, GLA_HEADS, GLA_DV)

    def log_gate(lr, d):
        z = lr.astype(f32) @ w_up[d].astype(f32) + b_up[d].astype(f32)
        return (jax.nn.log_sigmoid(z) / GLA_TAU).reshape(B, L, GLA_HEADS, GLA_DK)

    def flip(t):
        return jnp.flip(t, axis=1)

    o_f, s_f = gla_scan(q, k, v, log_gate(lr_f, 0), s0_f)
    o_b, s_b = gla_scan(flip(q), flip(k), flip(v), flip(log_gate(lr_b, 1)), s0_b)
    o = o_f + flip(o_b)
    o = o * lax.rsqrt(jnp.mean(o * o, axis=-1, keepdims=True) + EPS) * norm_w.astype(f32)
    o = o * jax.nn.silu(gg.astype(f32).reshape(B, L, GLA_HEADS, GLA_DV))
    return o.reshape(B, L, GLA_V).astype(gq.dtype), s_f, s_b


def mixer_context(h, w):
    B, L = h.shape[0], h.shape[1]
    aq, ak, av, gq, gk, gv, gg, lf, lb = split_proj(h @ w['w_in'])
    k = ak.reshape(B, L, N_KV_HEADS, HEAD_DIM)
    v = av.reshape(B, L, N_KV_HEADS, HEAD_DIM)
    att = context_attention(aq.reshape(B, L, N_Q_HEADS, HEAD_DIM), k, v, w['attn_sink'])
    s0 = jnp.zeros((B, GLA_HEADS, GLA_DK, GLA_DV), jnp.float32)
    gla, s_f, s_b = gla_mixer(gq, gk, gv, gg, lf, lb, w['gla_w_up'], w['gla_b_up'], w['gla_norm'], s0, s0)
    out = jnp.concatenate([att, gla], axis=-1) @ w['w_out']
    return out, (k, v, s_f, s_b)


def mixer_latent(h, w, k_ctx, v_ctx, s0_f, s0_b):
    B, L = h.shape[0], h.shape[1]
    aq, ak, av, gq, gk, gv, gg, lf, lb = split_proj(h @ w['w_in'])
    q = rope_2d(aq.reshape(B, L, N_Q_HEADS, HEAD_DIM))
    k = rope_2d(ak.reshape(B, L, N_KV_HEADS, HEAD_DIM))
    v = av.reshape(B, L, N_KV_HEADS, HEAD_DIM)
    att = latent_attention(q, k, v, k_ctx, v_ctx, w['attn_sink'])
    gla, _, _ = gla_mixer(gq, gk, gv, gg, lf, lb, w['gla_w_up'], w['gla_b_up'], w['gla_norm'],
                          s0_f.astype(jnp.float32), s0_b.astype(jnp.float32))
    out = jnp.concatenate([att, gla], axis=-1) @ w['w_out']
    return out, ()


def trunk_layer(x, mod, w, mixer_fn):
    def modulated(x, i):
        return rmsnorm(x, w['norm_pre'][i]) * (1 + mod[:, 3 * i + 1, None]) + mod[:, 3 * i, None]

    def residual(x, out, i, weight):
        return x + weight * mod[:, 3 * i + 2, None] * rmsnorm(out, w['norm_post'][i])

    x = residual(x, swiglu(modulated(x, 0), w['ffn_w1'][0], w['ffn_w3'][0], w['ffn_w2'][0]), 0, 0.5)
    mix, ctx_tensors = mixer_fn(modulated(x, 1))
    x = residual(x, mix, 1, 1.0)
    x = residual(x, swiglu(modulated(x, 2), w['ffn_w1'][1], w['ffn_w3'][1], w['ffn_w2'][1]), 2, 0.5)
    return x, ctx_tensors


def setup_inputs(seed: int = 0) -> dict:
    key = jax.random.key(seed)
    ks = jax.random.split(key, 21)
    D = D_MODEL

    def nrm(k, shape, s):
        return jax.random.normal(k, shape, jnp.float32) * s

    return {
        'x_prompt': nrm(ks[0], (BATCH, SEQ, D), 1.0),
        'x_sample': nrm(ks[1], (DEC_BATCH, DEC_SEQ, D), 1.0),
        'cache_k': nrm(ks[2], (DEC_BATCH, DEPTH, PAST_LEN, N_KV_HEADS, HEAD_DIM), 1.0),
        'cache_v': nrm(ks[3], (DEC_BATCH, DEPTH, PAST_LEN, N_KV_HEADS, HEAD_DIM), 1.0),
        'state_gla_fwd': nrm(ks[4], (DEC_BATCH, DEPTH, GLA_HEADS, GLA_DK, GLA_DV), 0.5),
        'state_gla_bwd': nrm(ks[5], (DEC_BATCH, DEPTH, GLA_HEADS, GLA_DK, GLA_DV), 0.5),
        'c': nrm(ks[6], (DEC_BATCH, D), 1.0),
        'c_ctx': nrm(ks[7], (D,), 1.0),
        'w_ada': nrm(ks[8], (DEPTH, D, N_MOD * D), 0.5 * D ** -0.5),
        'b_ada': nrm(ks[9], (DEPTH, N_MOD * D), 0.02),
        'norm_pre': 1.0 + nrm(ks[10], (DEPTH, 3, D), 0.1),
        'norm_post': 1.0 + nrm(ks[11], (DEPTH, 3, D), 0.1),
        'ffn_w1': nrm(ks[12], (DEPTH, 2, D, D_FF), D ** -0.5),
        'ffn_w3': nrm(ks[13], (DEPTH, 2, D, D_FF), D ** -0.5),
        'ffn_w2': nrm(ks[14], (DEPTH, 2, D_FF, D), D_FF ** -0.5),
        'w_in': nrm(ks[15], (DEPTH, D, IN_WIDTH), D ** -0.5),
        'gla_w_up': nrm(ks[16], (DEPTH, 2, GLA_LOW_RANK, GLA_QK), GLA_LOW_RANK ** -0.5),
        'gla_b_up': nrm(ks[17], (DEPTH, 2, GLA_QK), 0.5),
        'gla_norm': 1.0 + nrm(ks[18], (DEPTH, GLA_DV), 0.1),
        'attn_sink': nrm(ks[19], (DEPTH, N_Q_HEADS), 0.5),
        'w_out': nrm(ks[20], (DEPTH, MIX_WIDTH, D), MIX_WIDTH ** -0.5),
    }


def reference(x_prompt, x_sample, cache_k, cache_v, state_gla_fwd, state_gla_bwd, c, c_ctx,
              w_ada, b_ada, norm_pre, norm_post, ffn_w1, ffn_w3, ffn_w2, w_in,
              gla_w_up, gla_b_up, gla_norm, attn_sink, w_out):
    y_prompt = x_prompt
    y_sample = x_sample
    ks_new, vs_new, sf_new, sb_new = [], [], [], []
    for l in range(DEPTH):
        w = {'w_in': w_in[l], 'w_out': w_out[l], 'gla_w_up': gla_w_up[l], 'gla_b_up': gla_b_up[l],
             'gla_norm': gla_norm[l], 'attn_sink': attn_sink[l], 'norm_pre': norm_pre[l],
             'norm_post': norm_post[l], 'ffn_w1': ffn_w1[l], 'ffn_w3': ffn_w3[l], 'ffn_w2': ffn_w2[l]}
        mod_ctx = ada_modulation(c_ctx[None, :], w_ada[l], b_ada[l])
        mod_lat = ada_modulation(c, w_ada[l], b_ada[l])
        y_prompt, (k_l, v_l, sf_l, sb_l) = trunk_layer(
            y_prompt, mod_ctx, w, lambda h: mixer_context(h, w))
        y_sample, _ = trunk_layer(
            y_sample, mod_lat, w,
            lambda h: mixer_latent(h, w, cache_k[:, l], cache_v[:, l], state_gla_fwd[:, l], state_gla_bwd[:, l]))
        ks_new.append(k_l)
        vs_new.append(v_l)
        sf_new.append(sf_l)
        sb_new.append(sb_l)
    new_cache_k = jnp.stack(ks_new, axis=1)
    new_cache_v = jnp.stack(vs_new, axis=1)
    new_state_gla_fwd = jnp.stack(sf_new, axis=1).astype(x_prompt.dtype)
    new_state_gla_bwd = jnp.stack(sb_new, axis=1).astype(x_prompt.dtype)
    return (y_prompt, y_sample, new_cache_k, new_cache_v, new_state_gla_fwd, new_state_gla_bwd)
```

```python
import functools

import jax
import jax.numpy as jnp
from jax import lax
from jax.experimental import pallas as pl
from jax.experimental.pallas import tpu as pltpu

F32 = jnp.float32
BF16 = jnp.bfloat16

D_MODEL = 1024
GRID_W = 64
N_Q_HEADS = 8
N_KV_HEADS = 2
HEAD_DIM = 64
WINDOW = 128
BLOCK = 128
ROPE_BASE = 10000.0
GLA_HEADS = 4
GLA_DK = 64
GLA_DV = 128
GLA_LOW_RANK = 16
GLA_TAU = 16.0
D_FF = 2816
N_MOD = 9
EPS = 1e-6
NEG = -1e30

ATT_Q = N_Q_HEADS * HEAD_DIM
ATT_KV = N_KV_HEADS * HEAD_DIM
GLA_QK = GLA_HEADS * GLA_DK
GLA_V = GLA_HEADS * GLA_DV
OFF_Q = 0
OFF_K = OFF_Q + ATT_Q
OFF_V = OFF_K + ATT_KV
OFF_GQ = OFF_V + ATT_KV
OFF_GK = OFF_GQ + GLA_QK
OFF_GV = OFF_GK + GLA_QK
OFF_GG = OFF_GV + GLA_V
OFF_LR = OFF_GG + GLA_V
IN_WIDTH = OFF_LR + 2 * GLA_LOW_RANK

LANES = 128
SUBLANES = 8
HALF = LANES // 2
VMEM_LIMIT = 56 * 1024 * 1024

TOKEN_TILE = 512
ADA_TILE = 1152
FF_SPLITS = (0, 1536, D_FF)
GLA_CHUNK = 128
MOD_ROWS = 8


def _params(n_axes):
    return pltpu.CompilerParams(
        dimension_semantics=("arbitrary",) * n_axes, vmem_limit_bytes=VMEM_LIMIT)


def _resident(shape):
    zeros = (0,) * len(shape)
    return pl.BlockSpec(shape, lambda *_: zeros, pipeline_mode=pl.Buffered(1))


def _sigmoid(x):
    return 1.0 / (1.0 + jnp.exp(-x))


def _silu(x):
    return x * _sigmoid(x)


def _rms(x, g):
    return x * lax.rsqrt(jnp.mean(x * x, axis=-1, keepdims=True) + EPS) * g


def _dot(a, b):
    return jnp.dot(a, b, preferred_element_type=F32)


def _dot_t(a, b):
    return lax.dot_general(a, b, (((1,), (1,)), ((), ())), preferred_element_type=F32)


def _ada_kernel(cond_ref, w_ref, b_ref, o_ref):
    c = cond_ref[...]
    o_ref[...] = _dot(_silu(c).astype(BF16), w_ref[...].astype(BF16)) + b_ref[...]


def _ada_modulation(cond, w_ada, b_ada):
    n = w_ada.shape[1]
    return pl.pallas_call(
        _ada_kernel,
        out_shape=jax.ShapeDtypeStruct((MOD_ROWS, n), F32),
        grid=(n // ADA_TILE,),
        in_specs=[
            pl.BlockSpec((MOD_ROWS, D_MODEL), lambda j: (0, 0)),
            pl.BlockSpec((D_MODEL, ADA_TILE), lambda j: (0, j)),
            pl.BlockSpec((1, ADA_TILE), lambda j: (0, j)),
        ],
        out_specs=pl.BlockSpec((MOD_ROWS, ADA_TILE), lambda j: (0, j)),
        compiler_params=_params(1),
        name="ada_modulation",
    )(cond, w_ada, b_ada)


def _modulated(x, mod_ref, npre_ref, i):
    shift = mod_ref[0, 3 * i:3 * i + 1, :]
    scale = mod_ref[0, 3 * i + 1:3 * i + 2, :]
    return _rms(x, npre_ref[i:i + 1, :]) * (1.0 + scale) + shift


def _residual(x, out, mod_ref, npost_ref, i, weight):
    gate = mod_ref[0, 3 * i + 2:3 * i + 3, :]
    return x + (weight * gate) * _rms(out, npost_ref[i:i + 1, :])


def _ffn_sublayer(x, i, mod_ref, npre_ref, npost_ref, w1_ref, w3_ref, w2_ref):
    h = _modulated(x, mod_ref, npre_ref, i).astype(BF16)
    acc = None
    for lo, hi in zip(FF_SPLITS[:-1], FF_SPLITS[1:]):
        a = _dot(h, w1_ref[:, lo:hi])
        g = _dot(h, w3_ref[:, lo:hi])
        part = _dot((_silu(a) * g).astype(BF16), w2_ref[lo:hi, :])
        acc = part if acc is None else acc + part
    return _residual(x, acc, mod_ref, npost_ref, i, 0.5)


def _ffn_first_kernel(x_ref, mod_ref, npre_ref, npost_ref, w1_ref, w3_ref, w2_ref, o_ref):
    o_ref[...] = _ffn_sublayer(x_ref[...], 0, mod_ref, npre_ref, npost_ref, w1_ref, w3_ref, w2_ref)


def _mix_ffn_kernel(x_ref, att_ref, gla_ref, wo_ref, mod_ref, npre_ref, npost_ref,
                    w1_ref, w3_ref, w2_ref, o_ref):
    mix = (_dot(att_ref[...].astype(BF16), wo_ref[0:ATT_Q, :])
           + _dot(gla_ref[...].astype(BF16), wo_ref[ATT_Q:, :]))
    x = _residual(x_ref[...], mix, mod_ref, npost_ref, 1, 1.0)
    o_ref[...] = _ffn_sublayer(x, 2, mod_ref, npre_ref, npost_ref, w1_ref, w3_ref, w2_ref)


def _rope_tile(x, cos, sin_up, sin_dn):
    up = pltpu.roll(x, LANES - 16, axis=1)
    dn = pltpu.roll(x, 16, axis=1)
    return x * cos + up * sin_up + dn * sin_dn


def _proj_kernel(*refs, rope):
    if rope:
        (x_ref, mod_ref, npre_ref, win_ref, wup_ref, bup_ref, cos_ref, sup_ref, sdn_ref,
         q_ref, k_ref, v_ref, gq_ref, gk_ref, gv_ref, gg_ref, la_ref) = refs
    else:
        (x_ref, mod_ref, npre_ref, win_ref, wup_ref, bup_ref,
         q_ref, k_ref, v_ref, gq_ref, gk_ref, gv_ref, gg_ref, la_ref) = refs
    h = _modulated(x_ref[...], mod_ref, npre_ref, 1).astype(BF16)
    q = _dot(h, win_ref[:, OFF_Q:OFF_K])
    kv = _dot(h, win_ref[:, OFF_K:OFF_GQ])
    k = kv[:, :ATT_KV]
    if rope:
        cos, sup, sdn = cos_ref[...], sup_ref[...], sdn_ref[...]
        for j in range(ATT_Q // LANES):
            q_ref[:, j * LANES:(j + 1) * LANES] = _rope_tile(
                q[:, j * LANES:(j + 1) * LANES], cos, sup, sdn)
        k_ref[...] = _rope_tile(k, cos, sup, sdn)
    else:
        q_ref[...] = q
        k_ref[...] = k
    v_ref[...] = kv[:, ATT_KV:]
    gqk = _dot(h, win_ref[:, OFF_GQ:OFF_GV])
    gq_ref[...] = gqk[:, :GLA_QK]
    gk_ref[...] = gqk[:, GLA_QK:]
    gv_ref[...] = _dot(h, win_ref[:, OFF_GV:OFF_GG])
    gg_ref[...] = _dot(h, win_ref[:, OFF_GG:OFF_LR])
    lr = _dot(h, win_ref[:, OFF_LR:IN_WIDTH])
    z = _dot(lr.astype(BF16), wup_ref[...]) + bup_ref[...]
    log_sig = jnp.minimum(z, 0.0) - jnp.log1p(jnp.exp(-jnp.abs(z)))
    la_ref[...] = log_sig * (1.0 / GLA_TAU)


def _token_spec(width):
    return pl.BlockSpec((TOKEN_TILE, width), lambda i: (i, 0))


def _mod_spec(seq_len):
    if seq_len is None:
        return pl.BlockSpec((1, N_MOD, D_MODEL), lambda i: (0, 0, 0))
    tiles = seq_len // TOKEN_TILE
    return pl.BlockSpec((1, N_MOD, D_MODEL), lambda i: (1 + i // tiles, 0, 0))


def _ffn_first(x, mod3, latent_len, npre, npost, w1, w3, w2):
    t = x.shape[0]
    return pl.pallas_call(
        _ffn_first_kernel,
        out_shape=jax.ShapeDtypeStruct((t, D_MODEL), F32),
        grid=(t // TOKEN_TILE,),
        in_specs=[_token_spec(D_MODEL), _mod_spec(latent_len), _resident(npre.shape),
                  _resident(npost.shape), _resident(w1.shape), _resident(w3.shape),
                  _resident(w2.shape)],
        out_specs=_token_spec(D_MODEL),
        compiler_params=_params(1),
        name="ffn_first",
    )(x, mod3, npre, npost, w1, w3, w2)


def _mix_ffn(x, att, gla, w_out, mod3, latent_len, npre, npost, w1, w3, w2):
    t = x.shape[0]
    return pl.pallas_call(
        _mix_ffn_kernel,
        out_shape=jax.ShapeDtypeStruct((t, D_MODEL), F32),
        grid=(t // TOKEN_TILE,),
        in_specs=[_token_spec(D_MODEL), _token_spec(ATT_Q), _token_spec(GLA_V),
                  _resident(w_out.shape), _mod_spec(latent_len), _resident(npre.shape),
                  _resident(npost.shape), _resident(w1.shape), _resident(w3.shape),
                  _resident(w2.shape)],
        out_specs=_token_spec(D_MODEL),
        compiler_params=_params(1),
        name="mix_ffn",
    )(x, att, gla, w_out, mod3, npre, npost, w1, w3, w2)


def _project(x, mod3, latent_len, npre, w_in, w_up, b_up, rope_tables):
    t = x.shape[0]
    rope = rope_tables is not None
    in_specs = [_token_spec(D_MODEL), _mod_spec(latent_len), _resident(npre.shape),
                _resident(w_in.shape), _resident(w_up.shape), _resident(b_up.shape)]
    args = [x, mod3, npre, w_in, w_up, b_up]
    if rope:
        tiles = latent_len // TOKEN_TILE
        in_specs += [pl.BlockSpec((TOKEN_TILE, LANES), lambda i: (i % tiles, 0))] * 3
        args += list(rope_tables)
    widths = (ATT_Q, ATT_KV, ATT_KV, GLA_QK, GLA_QK, GLA_V, GLA_V, 2 * GLA_QK)
    return pl.pallas_call(
        functools.partial(_proj_kernel, rope=rope),
        out_shape=[jax.ShapeDtypeStruct((t, w), F32) for w in widths],
        grid=(t // TOKEN_TILE,),
        in_specs=in_specs,
        out_specs=[_token_spec(w) for w in widths],
        compiler_params=_params(1),
        name="project",
    )(*args)


def _rope_tables(seq_len):
    half = HEAD_DIM // 2
    inv_freq = ROPE_BASE ** (-jnp.arange(0, half, 2, dtype=F32) / half)
    pos = jnp.arange(seq_len)
    row = (pos // GRID_W).astype(F32)
    col = (pos % GRID_W).astype(F32)
    lane = jnp.arange(LANES)
    within = lane % HEAD_DIM
    idx = within % half
    freq = inv_freq[idx % (half // 2)]
    p = jnp.where((within // half == 0)[None, :], row[:, None], col[:, None])
    ang = p * freq[None, :]
    cos, sin = jnp.cos(ang), jnp.sin(ang)
    first = (idx < half // 2)[None, :]
    return cos, jnp.where(first, -sin, 0.0), jnp.where(first, 0.0, sin)


def _attn_kernel(*refs, windowed, seq_len):
    if windowed:
        sink_ref, q_ref, k_ref, v_ref, kc_ref, vc_ref, o_ref = refs
    else:
        sink_ref, q_ref, k_ref, v_ref, o_ref = refs
    nq = q_ref.shape[0]
    lane = lax.broadcasted_iota(jnp.int32, (1, LANES), 1)
    half_of = [lane < HALF, lane >= HALF]

    if windowed:
        i = pl.program_id(1)
        span = 3 * BLOCK
        start = pl.multiple_of(jnp.clip((i - 1) * BLOCK, 0, seq_len - span), BLOCK)
        kpos = start + lax.broadcasted_iota(jnp.int32, (1, span), 1)
        qpos = i * BLOCK + lax.broadcasted_iota(jnp.int32, (nq, 1), 0)
        valid = jnp.abs(kpos - qpos) <= WINDOW
        valid2 = jnp.concatenate([valid, valid], axis=0)
        segs = [(k_ref[pl.ds(start, span), :], v_ref[pl.ds(start, span), :], valid2),
                (kc_ref[0], vc_ref[0], None)]
    else:
        segs = [(k_ref[...], v_ref[...], None)]
    segs = [(k, pltpu.roll(k, HALF, axis=1), v, pltpu.roll(v, HALF, axis=1), m)
            for k, v, m in segs]

    scale = HEAD_DIM ** -0.5
    rows = lax.broadcasted_iota(jnp.int32, (2 * nq, 1), 0)
    out = [None] * (ATT_Q // LANES)
    for g in range(N_KV_HEADS):
        tiles = (2 * g, 2 * g + 1)
        for e in range(2):
            qm = jnp.concatenate(
                [jnp.where(half_of[e], q_ref[:, j * LANES:(j + 1) * LANES] * scale, 0.0)
                 for j in tiles], axis=0).astype(BF16)
            sink = jnp.where(rows < nq, sink_ref[2 * tiles[0] + e], sink_ref[2 * tiles[1] + e])
            scores = []
            for k, k_sw, _, _, m in segs:
                s = _dot_t(qm, (k if e == g else k_sw).astype(BF16))
                scores.append(s if m is None else jnp.where(m, s, NEG))
            mx = sink
            for s in scores:
                mx = jnp.maximum(mx, jnp.max(s, axis=-1, keepdims=True))
            probs = [jnp.exp(s - mx) for s in scores]
            den = jnp.exp(sink - mx)
            for p in probs:
                den = den + jnp.sum(p, axis=-1, keepdims=True)
            inv = 1.0 / den
            o = None
            for p, (_, _, v, v_sw, _) in zip(probs, segs):
                vv = jnp.where(half_of[e], v if e == g else v_sw, 0.0).astype(BF16)
                part = _dot((p * inv).astype(BF16), vv)
                o = part if o is None else o + part
            for r, j in enumerate(tiles):
                blk = o[r * nq:(r + 1) * nq, :]
                out[j] = blk if out[j] is None else out[j] + blk
    for j, blk in enumerate(out):
        o_ref[:, j * LANES:(j + 1) * LANES] = blk


def _smem_spec():
    return pl.BlockSpec(memory_space=pltpu.SMEM)


def _context_attention(sink, q, k, v, batch, seq_len):
    def seq(width):
        return pl.BlockSpec((seq_len, width), lambda b: (b, 0))
    return pl.pallas_call(
        functools.partial(_attn_kernel, windowed=False, seq_len=seq_len),
        out_shape=jax.ShapeDtypeStruct(q.shape, F32),
        grid=(batch,),
        in_specs=[_smem_spec(), seq(ATT_Q), seq(ATT_KV), seq(ATT_KV)],
        out_specs=seq(ATT_Q),
        compiler_params=_params(1),
        name="context_attention",
    )(sink, q, k, v)


def _latent_attention(sink, q, k, v, k_ctx, v_ctx, batch, seq_len):
    nb = seq_len // BLOCK
    past = k_ctx.shape[1]
    return pl.pallas_call(
        functools.partial(_attn_kernel, windowed=True, seq_len=seq_len),
        out_shape=jax.ShapeDtypeStruct(q.shape, F32),
        grid=(batch, nb),
        in_specs=[_smem_spec(),
                  pl.BlockSpec((BLOCK, ATT_Q), lambda b, i: (b * nb + i, 0)),
                  pl.BlockSpec((seq_len, ATT_KV), lambda b, i: (b, 0)),
                  pl.BlockSpec((seq_len, ATT_KV), lambda b, i: (b, 0)),
                  pl.BlockSpec((1, past, ATT_KV), lambda b, i: (b, 0, 0)),
                  pl.BlockSpec((1, past, ATT_KV), lambda b, i: (b, 0, 0))],
        out_specs=pl.BlockSpec((BLOCK, ATT_Q), lambda b, i: (b * nb + i, 0)),
        compiler_params=_params(2),
        name="latent_attention",
    )(sink, q, k, v, k_ctx, v_ctx)


def _split3(x):
    hi = x.astype(BF16)
    r = x - hi.astype(F32)
    mid = r.astype(BF16)
    lo = (r - mid.astype(F32)).astype(BF16)
    return hi, mid, lo


def _gla_kernel(*refs, n_chunks, has_s0, emit_state):
    refs = list(refs)
    gq_ref, gk_ref, gv_ref, gg_ref, la_ref, gn_ref = refs[:6]
    pos = 6
    s0_refs = refs[pos:pos + 2] if has_s0 else None
    pos += 2 if has_s0 else 0
    o_ref = refs[pos]
    pos += 1
    sfin_refs = refs[pos:pos + 2] if emit_state else None
    pos += 2 if emit_state else 0
    cum_ref, sent_ref, st_ref = refs[pos:]

    C = GLA_CHUNK
    n_pairs = GLA_QK // LANES
    lane = lax.broadcasted_iota(jnp.int32, (1, LANES), 1)
    half_of = [lane < HALF, lane >= HALF]
    r_i = lax.broadcasted_iota(jnp.int32, (C, C), 0)
    c_i = lax.broadcasted_iota(jnp.int32, (C, C), 1)
    lower = c_i <= r_i
    upper = c_i >= r_i
    tri = [jnp.where(lower, 1.0, 0.0).astype(BF16), jnp.where(upper, 1.0, 0.0).astype(BF16)]

    for d in range(2):
        for p in range(n_pairs):
            if has_s0:
                s0 = s0_refs[d][0, 0, 2 * p:2 * p + 2, :, :].reshape(2 * GLA_DK, GLA_DV)
                st_ref[d, p] = s0.T
            else:
                st_ref[d, p] = jnp.zeros((GLA_DV, 2 * GLA_DK), F32)

    def scan_step(i, carry):
        for d, n in ((0, i), (1, n_chunks - 1 - i)):
            rows = pl.ds(pl.multiple_of(n * C, C), C)
            la = la_ref[rows, d * GLA_QK:(d + 1) * GLA_QK]
            cum = sum(_dot(tri[d], part) for part in _split3(la))
            cum_ref[d, rows, :] = cum
            tot = cum[C - 1:C, :] if d == 0 else cum[0:1, :]
            k_in = gk_ref[rows, :] * jnp.exp(tot - cum)
            decay = jnp.exp(tot)
            for p in range(n_pairs):
                k2 = k_in[:, p * LANES:(p + 1) * LANES]
                kv_t = None
                for e in range(2):
                    h = 2 * p + e
                    v_t = gv_ref[rows, h * GLA_DV:(h + 1) * GLA_DV].T.astype(BF16)
                    part = _dot(v_t, jnp.where(half_of[e], k2, 0.0).astype(BF16))
                    kv_t = part if kv_t is None else kv_t + part
                st = st_ref[d, p]
                sent_ref[d, n, p] = st.astype(BF16)
                st_ref[d, p] = decay[:, p * LANES:(p + 1) * LANES] * st + kv_t
        return carry

    lax.fori_loop(0, n_chunks, scan_step, 0)

    gnorm = gn_ref[...]
    qscale = GLA_DK ** -0.5

    def out_step(n, carry):
        rows = pl.ds(pl.multiple_of(n * C, C), C)
        q = gq_ref[rows, :] * qscale
        k = gk_ref[rows, :]
        qs, ks, qin = [], [], []
        for d in range(2):
            cum = cum_ref[d, rows, :]
            ref = cum[C // 2:C // 2 + 1, :]
            qs.append(q * jnp.exp(cum - ref))
            ks.append((k * jnp.exp(ref - cum)).astype(BF16))
            qin.append(q * jnp.exp(cum))
        for p in range(n_pairs):
            sl = slice(p * LANES, (p + 1) * LANES)
            for e in range(2):
                h = 2 * p + e
                hs = slice(h * GLA_DV, (h + 1) * GLA_DV)
                m = half_of[e]
                s_f = _dot_t(jnp.where(m, qs[0][:, sl], 0.0).astype(BF16), ks[0][:, sl])
                s_b = _dot_t(jnp.where(m, qs[1][:, sl], 0.0).astype(BF16), ks[1][:, sl])
                prob = jnp.where(lower, s_f, 0.0) + jnp.where(upper, s_b, 0.0)
                o = _dot(prob.astype(BF16), gv_ref[rows, hs].astype(BF16))
                for d in range(2):
                    o = o + _dot_t(jnp.where(m, qin[d][:, sl], 0.0).astype(BF16), sent_ref[d, n, p])
                o = o * lax.rsqrt(jnp.mean(o * o, axis=-1, keepdims=True) + EPS) * gnorm
                o_ref[rows, hs] = o * _silu(gg_ref[rows, hs])
        return carry

    lax.fori_loop(0, n_chunks, out_step, 0)

    if emit_state:
        for d in range(2):
            for p in range(n_pairs):
                sfin_refs[d][0, 0, 2 * p:2 * p + 2, :, :] = st_ref[d, p].T.reshape(2, GLA_DK, GLA_DV)


def _gla(gq, gk, gv, gg, la, gnorm, batch, seq_len, s0=None, emit_state=False):
    n_chunks = seq_len // GLA_CHUNK
    has_s0 = s0 is not None

    def seq(width):
        return pl.BlockSpec((seq_len, width), lambda b: (b, 0))
    state_spec = pl.BlockSpec((1, 1, GLA_HEADS, GLA_DK, GLA_DV), lambda b: (b, 0, 0, 0, 0))
    in_specs = [seq(GLA_QK), seq(GLA_QK), seq(GLA_V), seq(GLA_V), seq(2 * GLA_QK),
                pl.BlockSpec((1, GLA_DV), lambda b: (0, 0))]
    args = [gq, gk, gv, gg, la, gnorm]
    if has_s0:
        in_specs += [state_spec, state_spec]
        args += list(s0)
    out_shape = [jax.ShapeDtypeStruct((batch * seq_len, GLA_V), F32)]
    out_specs = [seq(GLA_V)]
    if emit_state:
        out_shape += [jax.ShapeDtypeStruct((batch, 1, GLA_HEADS, GLA_DK, GLA_DV), F32)] * 2
        out_specs += [state_spec, state_spec]
    n_pairs = GLA_QK // LANES
    return pl.pallas_call(
        functools.partial(_gla_kernel, n_chunks=n_chunks, has_s0=has_s0, emit_state=emit_state),
        out_shape=out_shape,
        grid=(batch,),
        in_specs=in_specs,
        out_specs=out_specs,
        scratch_shapes=[pltpu.VMEM((2, seq_len, GLA_QK), F32),
                        pltpu.VMEM((2, n_chunks, n_pairs, GLA_DV, LANES), BF16),
                        pltpu.VMEM((2, n_pairs, GLA_DV, LANES), F32)],
        compiler_params=_params(1),
        name="gla",
    )(*args)


def kernel(x_prompt, x_sample, cache_k, cache_v, state_gla_fwd, state_gla_bwd, c, c_ctx,
           w_ada, b_ada, norm_pre, norm_post, ffn_w1, ffn_w3, ffn_w2, w_in,
           gla_w_up, gla_b_up, gla_norm, attn_sink, w_out):
    depth = w_in.shape[0]
    assert depth == 1, "single trunk layer"
    batch, seq = x_prompt.shape[0], x_prompt.shape[1]
    dec_batch, dec_seq = x_sample.shape[0], x_sample.shape[1]
    past = cache_k.shape[2]
    l = 0

    cond = jnp.concatenate(
        [c_ctx[None, :], c, jnp.zeros((MOD_ROWS - 1 - dec_batch, D_MODEL), F32)], axis=0)
    mod3 = _ada_modulation(cond, w_ada[l], b_ada[l][None, :]).reshape(MOD_ROWS, N_MOD, D_MODEL)

    npre, npost = norm_pre[l], norm_post[l]
    w1 = ffn_w1[l].astype(BF16)
    w3 = ffn_w3[l].astype(BF16)
    w2 = ffn_w2[l].astype(BF16)
    w_in_b = w_in[l].astype(BF16)
    w_out_b = w_out[l].astype(BF16)
    zeros = jnp.zeros((GLA_LOW_RANK, GLA_QK), F32)
    w_up = jnp.concatenate(
        [jnp.concatenate([gla_w_up[l, 0], zeros], axis=1),
         jnp.concatenate([zeros, gla_w_up[l, 1]], axis=1)], axis=0).astype(BF16)
    b_up = gla_b_up[l].reshape(1, 2 * GLA_QK)
    gnorm = gla_norm[l][None, :]
    sink = attn_sink[l]

    def trunk(x, latent):
        n_batch, n_seq = (dec_batch, dec_seq) if latent else (batch, seq)
        latent_len = n_seq if latent else None
        x1 = _ffn_first(x, mod3, latent_len, npre, npost, w1[0], w3[0], w2[0])
        q, k, v, gq, gk, gv, gg, la = _project(
            x1, mod3, latent_len, npre, w_in_b, w_up, b_up,
            _rope_tables(n_seq) if latent else None)
        if latent:
            att = _latent_attention(sink, q, k, v, cache_k[:, l].reshape(dec_batch, past, ATT_KV),
                                    cache_v[:, l].reshape(dec_batch, past, ATT_KV), n_batch, n_seq)
            (gla,) = _gla(gq, gk, gv, gg, la, gnorm, n_batch, n_seq,
                          s0=(state_gla_fwd[:, l:l + 1], state_gla_bwd[:, l:l + 1]))
            extras = ()
        else:
            att = _context_attention(sink, q, k, v, n_batch, n_seq)
            gla, s_f, s_b = _gla(gq, gk, gv, gg, la, gnorm, n_batch, n_seq, emit_state=True)
            extras = (k.reshape(n_batch, 1, n_seq, N_KV_HEADS, HEAD_DIM),
                      v.reshape(n_batch, 1, n_seq, N_KV_HEADS, HEAD_DIM), s_f, s_b)
        y = _mix_ffn(x1, att, gla, w_out_b, mod3, latent_len, npre, npost, w1[1], w3[1], w2[1])
        return y.reshape(n_batch, n_seq, D_MODEL), extras

    y_prompt, (k_new, v_new, s_f, s_b) = trunk(x_prompt.reshape(batch * seq, D_MODEL), False)
    y_sample, _ = trunk(x_sample.reshape(dec_batch * dec_seq, D_MODEL), True)
    return (y_prompt, y_sample, k_new, v_new, s_f, s_b)
```

```python
import functools

import numpy as np
import jax
import jax.numpy as jnp
from jax import lax
from jax.experimental import pallas as pl
from jax.experimental.pallas import tpu as pltpu

F32 = jnp.float32
BF16 = jnp.bfloat16

D_MODEL = 1024
GRID_W = 64
N_Q_HEADS = 8
N_KV_HEADS = 2
HEAD_DIM = 64
WINDOW = 128
BLOCK = 128
ROPE_BASE = 10000.0
GLA_HEADS = 4
GLA_DK = 64
GLA_DV = 128
GLA_LOW_RANK = 16
GLA_TAU = 16.0
D_FF = 2816
N_MOD = 9
EPS = 1e-6
NEG = -1e30

ATT_Q = N_Q_HEADS * HEAD_DIM
ATT_KV = N_KV_HEADS * HEAD_DIM
GLA_QK = GLA_HEADS * GLA_DK
GLA_V = GLA_HEADS * GLA_DV
OFF_Q = 0
OFF_K = OFF_Q + ATT_Q
OFF_V = OFF_K + ATT_KV
OFF_GQ = OFF_V + ATT_KV
OFF_GK = OFF_GQ + GLA_QK
OFF_GV = OFF_GK + GLA_QK
OFF_GG = OFF_GV + GLA_V
OFF_LR = OFF_GG + GLA_V
IN_WIDTH = OFF_LR + 2 * GLA_LOW_RANK

LANES = 128
SUBLANES = 8
HALF = LANES // 2
VMEM_LIMIT = 56 * 1024 * 1024

TOKEN_TILE = 512
FF_SPLITS = (0, 1536, D_FF)
GLA_CHUNK = 128
GLA_UNROLL = 2
MOD_ROWS = 8


def _params(n_axes):
    return pltpu.CompilerParams(
        dimension_semantics=("arbitrary",) * n_axes, vmem_limit_bytes=VMEM_LIMIT)


def _resident(shape):
    zeros = (0,) * len(shape)
    return pl.BlockSpec(shape, lambda *_: zeros, pipeline_mode=pl.Buffered(1))


def _sigmoid(x):
    return 1.0 / (1.0 + jnp.exp(-x))


def _silu(x):
    return x * _sigmoid(x)


def _rms(x, g):
    return x * lax.rsqrt(jnp.mean(x * x, axis=-1, keepdims=True) + EPS) * g


def _dot(a, b):
    return jnp.dot(a, b, preferred_element_type=F32)


def _dot_t(a, b):
    return lax.dot_general(a, b, (((1,), (1,)), ((), ())), preferred_element_type=F32)


def _ada_kernel(cond_ref, w_ref, b_ref, o_ref):
    c = cond_ref[...]
    o_ref[0] = _dot(_silu(c).astype(BF16), w_ref[...].astype(BF16)) + b_ref[...]


def _ada_modulation(cond, w_ada, b_ada):
    return pl.pallas_call(
        _ada_kernel,
        out_shape=jax.ShapeDtypeStruct((N_MOD, MOD_ROWS, D_MODEL), F32),
        grid=(N_MOD,),
        in_specs=[
            pl.BlockSpec((MOD_ROWS, D_MODEL), lambda j: (0, 0)),
            pl.BlockSpec((D_MODEL, D_MODEL), lambda j: (0, j)),
            pl.BlockSpec((1, D_MODEL), lambda j: (0, j)),
        ],
        out_specs=pl.BlockSpec((1, MOD_ROWS, D_MODEL), lambda j: (j, 0, 0)),
        compiler_params=_params(1),
        name="ada_modulation",
    )(cond, w_ada, b_ada)


class _Mod:
    def __init__(self, mod_ref, tiles_per_seq):
        self.ref = mod_ref
        self.row = 0 if tiles_per_seq is None else 1 + pl.program_id(0) // tiles_per_seq

    def __getitem__(self, m):
        return self.ref[m, pl.ds(self.row, 1), :]


def _modulated(x, mod, npre_ref, i):
    return _rms(x, npre_ref[i:i + 1, :]) * (1.0 + mod[3 * i + 1]) + mod[3 * i]


def _residual(x, out, mod, npost_ref, i, weight):
    return x + (weight * mod[3 * i + 2]) * _rms(out, npost_ref[i:i + 1, :])


def _ffn_sublayer(x, i, mod, npre_ref, npost_ref, w1_ref, w3_ref, w2_ref):
    h = _modulated(x, mod, npre_ref, i).astype(BF16)
    acc = None
    for lo, hi in zip(FF_SPLITS[:-1], FF_SPLITS[1:]):
        a = _dot(h, w1_ref[:, lo:hi])
        g = _dot(h, w3_ref[:, lo:hi])
        part = _dot((_silu(a) * g).astype(BF16), w2_ref[lo:hi, :])
        acc = part if acc is None else acc + part
    return _residual(x, acc, mod, npost_ref, i, 0.5)


def _ffn_first_kernel(x_ref, mod_ref, npre_ref, npost_ref, w1_ref, w3_ref, w2_ref, o_ref,
                      *, tiles_per_seq):
    mod = _Mod(mod_ref, tiles_per_seq)
    o_ref[...] = _ffn_sublayer(x_ref[...], 0, mod, npre_ref, npost_ref, w1_ref, w3_ref, w2_ref)


def _mix_ffn_kernel(x_ref, att_ref, gla_ref, wo_ref, mod_ref, npre_ref, npost_ref,
                    w1_ref, w3_ref, w2_ref, o_ref, *, tiles_per_seq):
    mod = _Mod(mod_ref, tiles_per_seq)
    mix = (_dot(att_ref[...].astype(BF16), wo_ref[0:ATT_Q, :])
           + _dot(gla_ref[...].astype(BF16), wo_ref[ATT_Q:, :]))
    x = _residual(x_ref[...], mix, mod, npost_ref, 1, 1.0)
    o_ref[...] = _ffn_sublayer(x, 2, mod, npre_ref, npost_ref, w1_ref, w3_ref, w2_ref)


def _rope_tile(x, cos, sin_up, sin_dn):
    up = pltpu.roll(x, LANES - 16, axis=1)
    dn = pltpu.roll(x, 16, axis=1)
    return x * cos + up * sin_up + dn * sin_dn


def _proj_kernel(*refs, tiles_per_seq, cache_seq):
    x_ref, mod_ref, npre_ref, win_ref, wup_ref, bup_ref = refs[:6]
    rope_refs = refs[6:9] if tiles_per_seq is not None else None
    outs = refs[6 + (3 if rope_refs else 0):]
    q_ref, k_ref, v_ref, gq_ref, gk_ref, gv_ref, gg_ref, la_ref = outs[:8]
    h = _modulated(x_ref[...], _Mod(mod_ref, tiles_per_seq), npre_ref, 1).astype(BF16)
    q = _dot(h, win_ref[:, OFF_Q:OFF_K])
    kv = _dot(h, win_ref[:, OFF_K:OFF_GQ])
    k, v = kv[:, :ATT_KV], kv[:, ATT_KV:]
    if rope_refs:
        cos, sup, sdn = (r[...] for r in rope_refs)
        for j in range(ATT_Q // LANES):
            q_ref[:, j * LANES:(j + 1) * LANES] = _rope_tile(
                q[:, j * LANES:(j + 1) * LANES], cos, sup, sdn)
        k_ref[...] = _rope_tile(k, cos, sup, sdn)
    else:
        q_ref[...] = q
        k_ref[...] = k
    v_ref[...] = v
    if cache_seq is not None:
        for src, dst in ((k, outs[8]), (v, outs[9])):
            t = src.T
            for b in range(TOKEN_TILE // cache_seq):
                dst[b, 0] = t[:, b * cache_seq:(b + 1) * cache_seq].reshape(
                    N_KV_HEADS, HEAD_DIM, cache_seq)
    gqk = _dot(h, win_ref[:, OFF_GQ:OFF_GV])
    gq_ref[...] = gqk[:, :GLA_QK]
    gk_ref[...] = gqk[:, GLA_QK:]
    gv_ref[...] = _dot(h, win_ref[:, OFF_GV:OFF_GG])
    gg_ref[...] = _dot(h, win_ref[:, OFF_GG:OFF_LR])
    lr = _dot(h, win_ref[:, OFF_LR:IN_WIDTH])
    z = _dot(lr.astype(BF16), wup_ref[...]) + bup_ref[...]
    log_sig = jnp.minimum(z, 0.0) - jnp.log1p(jnp.exp(-jnp.abs(z)))
    la_ref[...] = log_sig * (1.0 / GLA_TAU)


def _token_spec(width):
    return pl.BlockSpec((TOKEN_TILE, width), lambda i: (i, 0))


def _ffn_weight_spec(shape, j):
    return pl.BlockSpec((None,) + tuple(shape[1:]), lambda i: (j, 0, 0),
                        pipeline_mode=pl.Buffered(1))


def _tiles_per_seq(latent_len):
    return None if latent_len is None else latent_len // TOKEN_TILE


def _ffn_first(x, mod, latent_len, npre, npost, w1, w3, w2):
    t = x.shape[0]
    return pl.pallas_call(
        functools.partial(_ffn_first_kernel, tiles_per_seq=_tiles_per_seq(latent_len)),
        out_shape=jax.ShapeDtypeStruct((t, D_MODEL), F32),
        grid=(t // TOKEN_TILE,),
        in_specs=[_token_spec(D_MODEL), _resident(mod.shape), _resident(npre.shape),
                  _resident(npost.shape), _ffn_weight_spec(w1.shape, 0),
                  _ffn_weight_spec(w3.shape, 0), _ffn_weight_spec(w2.shape, 0)],
        out_specs=_token_spec(D_MODEL),
        compiler_params=_params(1),
        name="ffn_first",
    )(x, mod, npre, npost, w1, w3, w2)


def _mix_ffn(x, att, gla, w_out, mod, latent_len, npre, npost, w1, w3, w2):
    t = x.shape[0]
    return pl.pallas_call(
        functools.partial(_mix_ffn_kernel, tiles_per_seq=_tiles_per_seq(latent_len)),
        out_shape=jax.ShapeDtypeStruct((t, D_MODEL), F32),
        grid=(t // TOKEN_TILE,),
        in_specs=[_token_spec(D_MODEL), _token_spec(ATT_Q), _token_spec(GLA_V),
                  _resident(w_out.shape), _resident(mod.shape), _resident(npre.shape),
                  _resident(npost.shape), _ffn_weight_spec(w1.shape, 1),
                  _ffn_weight_spec(w3.shape, 1), _ffn_weight_spec(w2.shape, 1)],
        out_specs=_token_spec(D_MODEL),
        compiler_params=_params(1),
        name="mix_ffn",
    )(x, att, gla, w_out, mod, npre, npost, w1, w3, w2)


def _project(x, mod, latent_len, cache_seq, npre, w_in, w_up, b_up):
    t = x.shape[0]
    tiles = _tiles_per_seq(latent_len)
    in_specs = [_token_spec(D_MODEL), _resident(mod.shape), _resident(npre.shape),
                _resident(w_in.shape), _resident(w_up.shape), _resident(b_up.shape)]
    args = [x, mod, npre, w_in, w_up, b_up]
    if tiles is not None:
        in_specs += [pl.BlockSpec((TOKEN_TILE, LANES), lambda i: (i % tiles, 0))] * 3
        args += [jnp.asarray(tab) for tab in _rope_tables(latent_len)]
    widths = (ATT_Q, ATT_KV, ATT_KV, GLA_QK, GLA_QK, GLA_V, GLA_V, 2 * GLA_QK)
    out_shape = [jax.ShapeDtypeStruct((t, w), F32) for w in widths]
    out_specs = [_token_spec(w) for w in widths]
    if cache_seq is not None:
        seqs = TOKEN_TILE // cache_seq
        cache = (t // cache_seq, 1, N_KV_HEADS, HEAD_DIM, cache_seq)
        out_shape += [jax.ShapeDtypeStruct(cache, F32)] * 2
        out_specs += [pl.BlockSpec((seqs,) + cache[1:], lambda i: (i, 0, 0, 0, 0))] * 2
    return pl.pallas_call(
        functools.partial(_proj_kernel, tiles_per_seq=tiles, cache_seq=cache_seq),
        out_shape=out_shape,
        grid=(t // TOKEN_TILE,),
        in_specs=in_specs,
        out_specs=out_specs,
        compiler_params=_params(1),
        name="project",
    )(*args)


def _rope_tables(seq_len):
    half = HEAD_DIM // 2
    inv_freq = np.float32(ROPE_BASE) ** (-np.arange(0, half, 2, dtype=np.float32) / half)
    pos = np.arange(seq_len)
    row = (pos // GRID_W).astype(np.float32)
    col = (pos % GRID_W).astype(np.float32)
    within = np.arange(LANES) % HEAD_DIM
    idx = within % half
    freq = inv_freq[idx % (half // 2)].astype(np.float32)
    p = np.where((within // half == 0)[None, :], row[:, None], col[:, None])
    ang = (p * freq[None, :]).astype(np.float32)
    cos, sin = np.cos(ang).astype(np.float32), np.sin(ang).astype(np.float32)
    first = (idx < half // 2)[None, :]
    zero = np.float32(0.0)
    return cos, np.where(first, -sin, zero), np.where(first, zero, sin)


def _attn_kernel(*refs, windowed, seq_len):
    if windowed:
        sink_ref, q_ref, k_ref, v_ref, kc_ref, vc_ref, o_ref = refs
    else:
        sink_ref, q_ref, k_ref, v_ref, o_ref = refs
    nq = q_ref.shape[0]
    lane = lax.broadcasted_iota(jnp.int32, (1, LANES), 1)
    half_of = [lane < HALF, lane >= HALF]

    if windowed:
        i = pl.program_id(1)
        span = 3 * BLOCK
        start = pl.multiple_of(jnp.clip((i - 1) * BLOCK, 0, seq_len - span), BLOCK)
        kpos = start + lax.broadcasted_iota(jnp.int32, (1, span), 1)
        qpos = i * BLOCK + lax.broadcasted_iota(jnp.int32, (nq, 1), 0)
        valid = jnp.abs(kpos - qpos) <= WINDOW
        valid2 = jnp.concatenate([valid, valid], axis=0)
        segs = [(k_ref[pl.ds(start, span), :], v_ref[pl.ds(start, span), :], valid2),
                (kc_ref[0], vc_ref[0], None)]
    else:
        segs = [(k_ref[...], v_ref[...], None)]
    segs = [(k, pltpu.roll(k, HALF, axis=1), v, pltpu.roll(v, HALF, axis=1), m)
            for k, v, m in segs]

    scale = HEAD_DIM ** -0.5
    rows = lax.broadcasted_iota(jnp.int32, (2 * nq, 1), 0)
    out = [None] * (ATT_Q // LANES)
    for g in range(N_KV_HEADS):
        tiles = (2 * g, 2 * g + 1)
        for e in range(2):
            qm = jnp.concatenate(
                [jnp.where(half_of[e], q_ref[:, j * LANES:(j + 1) * LANES] * scale, 0.0)
                 for j in tiles], axis=0).astype(BF16)
            sink = jnp.where(rows < nq, sink_ref[2 * tiles[0] + e], sink_ref[2 * tiles[1] + e])
            scores = []
            for k, k_sw, _, _, m in segs:
                s = _dot_t(qm, (k if e == g else k_sw).astype(BF16))
                scores.append(s if m is None else jnp.where(m, s, NEG))
            mx = sink
            for s in scores:
                mx = jnp.maximum(mx, jnp.max(s, axis=-1, keepdims=True))
            probs = [jnp.exp(s - mx) for s in scores]
            den = jnp.exp(sink - mx)
            for p in probs:
                den = den + jnp.sum(p, axis=-1, keepdims=True)
            inv = 1.0 / den
            o = None
            for p, (_, _, v, v_sw, _) in zip(probs, segs):
                vv = jnp.where(half_of[e], v if e == g else v_sw, 0.0).astype(BF16)
                part = _dot((p * inv).astype(BF16), vv)
                o = part if o is None else o + part
            for r, j in enumerate(tiles):
                blk = o[r * nq:(r + 1) * nq, :]
                out[j] = blk if out[j] is None else out[j] + blk
    for j, blk in enumerate(out):
        o_ref[:, j * LANES:(j + 1) * LANES] = blk


def _smem_spec():
    return pl.BlockSpec(memory_space=pltpu.SMEM)


def _context_attention(sink, q, k, v, batch, seq_len):
    def seq(width):
        return pl.BlockSpec((seq_len, width), lambda b: (b, 0))
    return pl.pallas_call(
        functools.partial(_attn_kernel, windowed=False, seq_len=seq_len),
        out_shape=jax.ShapeDtypeStruct(q.shape, F32),
        grid=(batch,),
        in_specs=[_smem_spec(), seq(ATT_Q), seq(ATT_KV), seq(ATT_KV)],
        out_specs=seq(ATT_Q),
        compiler_params=_params(1),
        name="context_attention",
    )(sink, q, k, v)


def _latent_attention(sink, q, k, v, k_ctx, v_ctx, batch, seq_len):
    nb = seq_len // BLOCK
    past = k_ctx.shape[1]
    return pl.pallas_call(
        functools.partial(_attn_kernel, windowed=True, seq_len=seq_len),
        out_shape=jax.ShapeDtypeStruct(q.shape, F32),
        grid=(batch, nb),
        in_specs=[_smem_spec(),
                  pl.BlockSpec((BLOCK, ATT_Q), lambda b, i: (b * nb + i, 0)),
                  pl.BlockSpec((seq_len, ATT_KV), lambda b, i: (b, 0)),
                  pl.BlockSpec((seq_len, ATT_KV), lambda b, i: (b, 0)),
                  pl.BlockSpec((1, past, ATT_KV), lambda b, i: (b, 0, 0)),
                  pl.BlockSpec((1, past, ATT_KV), lambda b, i: (b, 0, 0))],
        out_specs=pl.BlockSpec((BLOCK, ATT_Q), lambda b, i: (b * nb + i, 0)),
        compiler_params=_params(2),
        name="latent_attention",
    )(sink, q, k, v, k_ctx, v_ctx)


def _split2(x):
    hi = x.astype(BF16)
    lo = (x - hi.astype(F32)).astype(BF16)
    return hi, lo


def _gla_kernel(*refs, n_chunks, has_s0, emit_state):
    refs = list(refs)
    gq_ref, gk_ref, gv_ref, gg_ref, la_ref, gn_ref = refs[:6]
    pos = 6
    s0_refs = refs[pos:pos + 2] if has_s0 else None
    pos += 2 if has_s0 else 0
    o_ref = refs[pos]
    pos += 1
    sfin_refs = refs[pos:pos + 2] if emit_state else None
    pos += 2 if emit_state else 0
    cum_ref, sent_ref, st_ref = refs[pos:]

    C = GLA_CHUNK
    n_pairs = GLA_QK // LANES
    lane = lax.broadcasted_iota(jnp.int32, (1, LANES), 1)
    half_of = [lane < HALF, lane >= HALF]
    r_i = lax.broadcasted_iota(jnp.int32, (C, C), 0)
    c_i = lax.broadcasted_iota(jnp.int32, (C, C), 1)
    lower = c_i <= r_i
    upper = c_i >= r_i
    tri = [jnp.where(lower, 1.0, 0.0).astype(BF16), jnp.where(upper, 1.0, 0.0).astype(BF16)]

    for d in range(2):
        for p in range(n_pairs):
            if has_s0:
                s0 = s0_refs[d][0, 0, 2 * p:2 * p + 2, :, :].reshape(2 * GLA_DK, GLA_DV)
                st_ref[d, p] = s0.T
            else:
                st_ref[d, p] = jnp.zeros((GLA_DV, 2 * GLA_DK), F32)

    def scan_step(i, carry):
        for d, n in ((0, i), (1, n_chunks - 1 - i)):
            rows = pl.ds(pl.multiple_of(n * C, C), C)
            la = la_ref[rows, d * GLA_QK:(d + 1) * GLA_QK]
            cum = sum(_dot(tri[d], part) for part in _split2(la))
            cum_ref[d, rows, :] = cum
            tot = cum[C - 1:C, :] if d == 0 else cum[0:1, :]
            k_in = gk_ref[rows, :] * jnp.exp(tot - cum)
            decay = jnp.exp(tot)
            for p in range(n_pairs):
                k2 = k_in[:, p * LANES:(p + 1) * LANES]
                kv_t = None
                for e in range(2):
                    h = 2 * p + e
                    v_t = gv_ref[rows, h * GLA_DV:(h + 1) * GLA_DV].T.astype(BF16)
                    part = _dot(v_t, jnp.where(half_of[e], k2, 0.0).astype(BF16))
                    kv_t = part if kv_t is None else kv_t + part
                st = st_ref[d, p]
                sent_ref[d, n, p] = st.astype(BF16)
                st_ref[d, p] = decay[:, p * LANES:(p + 1) * LANES] * st + kv_t
        return carry

    lax.fori_loop(0, n_chunks, scan_step, 0, unroll=GLA_UNROLL)

    gnorm = gn_ref[...]
    qscale = GLA_DK ** -0.5

    def out_step(n, carry):
        rows = pl.ds(pl.multiple_of(n * C, C), C)
        q = gq_ref[rows, :] * qscale
        k = gk_ref[rows, :]
        qs, ks, qin = [], [], []
        for d in range(2):
            cum = cum_ref[d, rows, :]
            ref = cum[C // 2:C // 2 + 1, :]
            qs.append(q * jnp.exp(cum - ref))
            ks.append((k * jnp.exp(ref - cum)).astype(BF16))
            qin.append(q * jnp.exp(cum))
        for p in range(n_pairs):
            sl = slice(p * LANES, (p + 1) * LANES)
            for e in range(2):
                h = 2 * p + e
                hs = slice(h * GLA_DV, (h + 1) * GLA_DV)
                m = half_of[e]
                s_f = _dot_t(jnp.where(m, qs[0][:, sl], 0.0).astype(BF16), ks[0][:, sl])
                s_b = _dot_t(jnp.where(m, qs[1][:, sl], 0.0).astype(BF16), ks[1][:, sl])
                prob = jnp.where(lower, s_f, 0.0) + jnp.where(upper, s_b, 0.0)
                o = _dot(prob.astype(BF16), gv_ref[rows, hs].astype(BF16))
                for d in range(2):
                    o = o + _dot_t(jnp.where(m, qin[d][:, sl], 0.0).astype(BF16), sent_ref[d, n, p])
                o = o * lax.rsqrt(jnp.mean(o * o, axis=-1, keepdims=True) + EPS) * gnorm
                o_ref[rows, hs] = o * _silu(gg_ref[rows, hs])
        return carry

    lax.fori_loop(0, n_chunks, out_step, 0, unroll=GLA_UNROLL)

    if emit_state:
        for d in range(2):
            for p in range(n_pairs):
                sfin_refs[d][0, 0, 2 * p:2 * p + 2, :, :] = st_ref[d, p].T.reshape(2, GLA_DK, GLA_DV)


def _gla(gq, gk, gv, gg, la, gnorm, batch, seq_len, s0=None, emit_state=False):
    n_chunks = seq_len // GLA_CHUNK
    has_s0 = s0 is not None

    def seq(width):
        return pl.BlockSpec((seq_len, width), lambda b: (b, 0))
    state_spec = pl.BlockSpec((1, 1, GLA_HEADS, GLA_DK, GLA_DV), lambda b: (b, 0, 0, 0, 0))
    in_specs = [seq(GLA_QK), seq(GLA_QK), seq(GLA_V), seq(GLA_V), seq(2 * GLA_QK),
                pl.BlockSpec((1, GLA_DV), lambda b: (0, 0))]
    args = [gq, gk, gv, gg, la, gnorm]
    if has_s0:
        in_specs += [state_spec, state_spec]
        args += list(s0)
    out_shape = [jax.ShapeDtypeStruct((batch * seq_len, GLA_V), F32)]
    out_specs = [seq(GLA_V)]
    if emit_state:
        out_shape += [jax.ShapeDtypeStruct((batch, 1, GLA_HEADS, GLA_DK, GLA_DV), F32)] * 2
        out_specs += [state_spec, state_spec]
    n_pairs = GLA_QK // LANES
    return pl.pallas_call(
        functools.partial(_gla_kernel, n_chunks=n_chunks, has_s0=has_s0, emit_state=emit_state),
        out_shape=out_shape,
        grid=(batch,),
        in_specs=in_specs,
        out_specs=out_specs,
        scratch_shapes=[pltpu.VMEM((2, seq_len, GLA_QK), F32),
                        pltpu.VMEM((2, n_chunks, n_pairs, GLA_DV, LANES), BF16),
                        pltpu.VMEM((2, n_pairs, GLA_DV, LANES), F32)],
        compiler_params=_params(1),
        name="gla",
    )(*args)


def kernel(x_prompt, x_sample, cache_k, cache_v, state_gla_fwd, state_gla_bwd, c, c_ctx,
           w_ada, b_ada, norm_pre, norm_post, ffn_w1, ffn_w3, ffn_w2, w_in,
           gla_w_up, gla_b_up, gla_norm, attn_sink, w_out):
    depth = w_in.shape[0]
    assert depth == 1, "single trunk layer"
    batch, seq = x_prompt.shape[0], x_prompt.shape[1]
    dec_batch, dec_seq = x_sample.shape[0], x_sample.shape[1]
    past = cache_k.shape[2]
    l = 0

    cond = jnp.concatenate(
        [c_ctx[None, :], c, jnp.zeros((MOD_ROWS - 1 - dec_batch, D_MODEL), F32)], axis=0)
    mod = _ada_modulation(cond, w_ada[l], b_ada[l][None, :])

    npre, npost = norm_pre[l], norm_post[l]
    w1 = ffn_w1[l].astype(BF16)
    w3 = ffn_w3[l].astype(BF16)
    w2 = ffn_w2[l].astype(BF16)
    w_in_b = w_in[l].astype(BF16)
    w_out_b = w_out[l].astype(BF16)
    zeros = jnp.zeros((GLA_LOW_RANK, GLA_QK), F32)
    w_up = jnp.concatenate(
        [jnp.concatenate([gla_w_up[l, 0], zeros], axis=1),
         jnp.concatenate([zeros, gla_w_up[l, 1]], axis=1)], axis=0).astype(BF16)
    b_up = gla_b_up[l].reshape(1, 2 * GLA_QK)
    gnorm = gla_norm[l][None, :]
    sink = attn_sink[l]

    def trunk(x, latent):
        n_batch, n_seq = (dec_batch, dec_seq) if latent else (batch, seq)
        latent_len = n_seq if latent else None
        x1 = _ffn_first(x, mod, latent_len, npre, npost, w1, w3, w2)
        q, k, v, gq, gk, gv, gg, la, *cache_t = _project(
            x1, mod, latent_len, None if latent else n_seq, npre, w_in_b, w_up, b_up)
        if latent:
            att = _latent_attention(sink, q, k, v, cache_k[:, l].reshape(dec_batch, past, ATT_KV),
                                    cache_v[:, l].reshape(dec_batch, past, ATT_KV), n_batch, n_seq)
            (gla,) = _gla(gq, gk, gv, gg, la, gnorm, n_batch, n_seq,
                          s0=(state_gla_fwd[:, l:l + 1], state_gla_bwd[:, l:l + 1]))
            extras = ()
        else:
            att = _context_attention(sink, q, k, v, n_batch, n_seq)
            gla, s_f, s_b = _gla(gq, gk, gv, gg, la, gnorm, n_batch, n_seq, emit_state=True)
            k_new, v_new = (jnp.transpose(c_t, (0, 1, 4, 2, 3)) for c_t in cache_t)
            extras = (k_new, v_new, s_f, s_b)
        y = _mix_ffn(x1, att, gla, w_out_b, mod, latent_len, npre, npost, w1, w3, w2)
        return y.reshape(n_batch, n_seq, D_MODEL), extras

    y_prompt, (k_new, v_new, s_f, s_b) = trunk(x_prompt.reshape(batch * seq, D_MODEL), False)
    y_sample, _ = trunk(x_sample.reshape(dec_batch * dec_seq, D_MODEL), True)
    return (y_prompt, y_sample, k_new, v_new, s_f, s_b)
```

```python
import functools

import numpy as np
import jax
import jax.numpy as jnp
from jax import lax
from jax.experimental import pallas as pl
from jax.experimental.pallas import tpu as pltpu

F32 = jnp.float32
BF16 = jnp.bfloat16

D_MODEL = 1024
GRID_W = 64
N_Q_HEADS = 8
N_KV_HEADS = 2
HEAD_DIM = 64
WINDOW = 128
BLOCK = 128
ROPE_BASE = 10000.0
GLA_HEADS = 4
GLA_DK = 64
GLA_DV = 128
GLA_LOW_RANK = 16
GLA_TAU = 16.0
D_FF = 2816
N_MOD = 9
EPS = 1e-6
NEG = -1e30

ATT_Q = N_Q_HEADS * HEAD_DIM
ATT_KV = N_KV_HEADS * HEAD_DIM
GLA_QK = GLA_HEADS * GLA_DK
GLA_V = GLA_HEADS * GLA_DV
OFF_Q = 0
OFF_K = OFF_Q + ATT_Q
OFF_V = OFF_K + ATT_KV
OFF_GQ = OFF_V + ATT_KV
OFF_GK = OFF_GQ + GLA_QK
OFF_GV = OFF_GK + GLA_QK
OFF_GG = OFF_GV + GLA_V
OFF_LR = OFF_GG + GLA_V
IN_WIDTH = OFF_LR + 2 * GLA_LOW_RANK
LOG2_E = 1.4426950408889634
ATT_Q_SCALE = HEAD_DIM ** -0.5 * LOG2_E

LANES = 128
SUBLANES = 8
HALF = LANES // 2
VMEM_LIMIT = 56 * 1024 * 1024

TOKEN_TILE = 512
FF_SPLITS = (0, 1536, D_FF)
GLA_CHUNK = 128
GLA_UNROLL = 4
MOD_ROWS = 8


def _params(n_axes):
    return pltpu.CompilerParams(
        dimension_semantics=("arbitrary",) * n_axes, vmem_limit_bytes=VMEM_LIMIT)


def _resident(shape):
    zeros = (0,) * len(shape)
    return pl.BlockSpec(shape, lambda *_: zeros, pipeline_mode=pl.Buffered(1))


def _sigmoid(x):
    return 1.0 / (1.0 + jnp.exp(-x))


def _silu(x):
    return x * _sigmoid(x)


def _rms(x, g):
    return x * lax.rsqrt(jnp.mean(x * x, axis=-1, keepdims=True) + EPS) * g


def _dot(a, b):
    return jnp.dot(a, b, preferred_element_type=F32)


def _dot_t(a, b):
    return lax.dot_general(a, b, (((1,), (1,)), ((), ())), preferred_element_type=F32)


def _ada_kernel(cond_ref, w_ref, b_ref, o_ref):
    c = cond_ref[...]
    o_ref[0] = _dot(_silu(c).astype(BF16), w_ref[...].astype(BF16)) + b_ref[...]


def _ada_modulation(cond, w_ada, b_ada):
    return pl.pallas_call(
        _ada_kernel,
        out_shape=jax.ShapeDtypeStruct((N_MOD, MOD_ROWS, D_MODEL), F32),
        grid=(N_MOD,),
        in_specs=[
            pl.BlockSpec((MOD_ROWS, D_MODEL), lambda j: (0, 0)),
            pl.BlockSpec((D_MODEL, D_MODEL), lambda j: (0, j)),
            pl.BlockSpec((1, D_MODEL), lambda j: (0, j)),
        ],
        out_specs=pl.BlockSpec((1, MOD_ROWS, D_MODEL), lambda j: (j, 0, 0)),
        compiler_params=_params(1),
        name="ada_modulation",
    )(cond, w_ada, b_ada)


class _Mod:
    def __init__(self, mod_ref, tiles_per_seq):
        self.ref = mod_ref
        self.row = 0 if tiles_per_seq is None else 1 + pl.program_id(0) // tiles_per_seq

    def __getitem__(self, m):
        return self.ref[m, pl.ds(self.row, 1), :]


def _modulated(x, mod, npre_ref, i):
    return _rms(x, npre_ref[i:i + 1, :]) * (1.0 + mod[3 * i + 1]) + mod[3 * i]


def _residual(x, out, mod, npost_ref, i, weight):
    return x + (weight * mod[3 * i + 2]) * _rms(out, npost_ref[i:i + 1, :])


def _ffn_sublayer(x, i, mod, npre_ref, npost_ref, w1_ref, w3_ref, w2_ref):
    h = _modulated(x, mod, npre_ref, i).astype(BF16)
    acc = None
    for lo, hi in zip(FF_SPLITS[:-1], FF_SPLITS[1:]):
        a = _dot(h, w1_ref[:, lo:hi])
        g = _dot(h, w3_ref[:, lo:hi])
        part = _dot((_silu(a) * g).astype(BF16), w2_ref[lo:hi, :])
        acc = part if acc is None else acc + part
    return _residual(x, acc, mod, npost_ref, i, 0.5)


def _ffn_first_kernel(x_ref, mod_ref, npre_ref, npost_ref, w1_ref, w3_ref, w2_ref, o_ref,
                      *, tiles_per_seq):
    mod = _Mod(mod_ref, tiles_per_seq)
    o_ref[...] = _ffn_sublayer(x_ref[...], 0, mod, npre_ref, npost_ref, w1_ref, w3_ref, w2_ref)


def _mix_ffn_kernel(x_ref, att_ref, gla_ref, wo_ref, mod_ref, npre_ref, npost_ref,
                    w1_ref, w3_ref, w2_ref, o_ref, *, tiles_per_seq):
    mod = _Mod(mod_ref, tiles_per_seq)
    mix = _dot(att_ref[...], wo_ref[0:ATT_Q, :]) + _dot(gla_ref[...], wo_ref[ATT_Q:, :])
    x = _residual(x_ref[...], mix, mod, npost_ref, 1, 1.0)
    o_ref[...] = _ffn_sublayer(x, 2, mod, npre_ref, npost_ref, w1_ref, w3_ref, w2_ref)


def _rope_tile(x, cos, sin_up, sin_dn):
    up = pltpu.roll(x, LANES - 16, axis=1)
    dn = pltpu.roll(x, 16, axis=1)
    return x * cos + up * sin_up + dn * sin_dn


def _proj_kernel(*refs, tiles_per_seq, cache_seq):
    x_ref, mod_ref, npre_ref, win_ref, wup_ref, bup_ref = refs[:6]
    rope_refs = refs[6:9] if tiles_per_seq is not None else None
    outs = refs[6 + (3 if rope_refs else 0):]
    q_ref, k_ref, v_ref, gq_ref, gk_ref, gv_ref, gg_ref, la_ref = outs[:8]
    h = _modulated(x_ref[...], _Mod(mod_ref, tiles_per_seq), npre_ref, 1).astype(BF16)
    q = _dot(h, win_ref[:, OFF_Q:OFF_K])
    kv = _dot(h, win_ref[:, OFF_K:OFF_GQ])
    k, v = kv[:, :ATT_KV], kv[:, ATT_KV:]
    if rope_refs:
        cos, sup, sdn = (r[...] for r in rope_refs)
        q = jnp.concatenate([_rope_tile(q[:, j * LANES:(j + 1) * LANES], cos, sup, sdn)
                             for j in range(ATT_Q // LANES)], axis=1)
        k_att = _rope_tile(k, cos, sup, sdn)
    else:
        k_att = k
    q_ref[...] = (q * ATT_Q_SCALE).astype(BF16)
    k_ref[:, :ATT_KV] = k_att.astype(BF16)
    k_ref[:, ATT_KV:] = pltpu.roll(k_att, HALF, axis=1).astype(BF16)
    v_ref[:, :ATT_KV] = v.astype(BF16)
    v_ref[:, ATT_KV:] = pltpu.roll(v, HALF, axis=1).astype(BF16)
    if cache_seq is not None:
        for src, dst in ((k, outs[8]), (v, outs[9])):
            t = src.T
            for b in range(TOKEN_TILE // cache_seq):
                dst[b, 0] = t[:, b * cache_seq:(b + 1) * cache_seq].reshape(
                    N_KV_HEADS, HEAD_DIM, cache_seq)
    gqk = _dot(h, win_ref[:, OFF_GQ:OFF_GV])
    gq_ref[...] = gqk[:, :GLA_QK]
    gk_ref[...] = gqk[:, GLA_QK:]
    gv_ref[...] = _dot(h, win_ref[:, OFF_GV:OFF_GG]).astype(BF16)
    gg_ref[...] = _dot(h, win_ref[:, OFF_GG:OFF_LR])
    lr = _dot(h, win_ref[:, OFF_LR:IN_WIDTH])
    z = _dot(lr.astype(BF16), wup_ref[...]) + bup_ref[...]
    log_sig = jnp.minimum(z, 0.0) - jnp.log1p(jnp.exp(-jnp.abs(z)))
    la_ref[...] = log_sig * (1.0 / GLA_TAU)


def _token_spec(width):
    return pl.BlockSpec((TOKEN_TILE, width), lambda i: (i, 0))


def _ffn_weight_spec(shape, j):
    return pl.BlockSpec((None,) + tuple(shape[1:]), lambda i: (j, 0, 0),
                        pipeline_mode=pl.Buffered(1))


def _tiles_per_seq(latent_len):
    return None if latent_len is None else latent_len // TOKEN_TILE


def _ffn_first(x, mod, latent_len, npre, npost, w1, w3, w2):
    t = x.shape[0]
    return pl.pallas_call(
        functools.partial(_ffn_first_kernel, tiles_per_seq=_tiles_per_seq(latent_len)),
        out_shape=jax.ShapeDtypeStruct((t, D_MODEL), F32),
        grid=(t // TOKEN_TILE,),
        in_specs=[_token_spec(D_MODEL), _resident(mod.shape), _resident(npre.shape),
                  _resident(npost.shape), _ffn_weight_spec(w1.shape, 0),
                  _ffn_weight_spec(w3.shape, 0), _ffn_weight_spec(w2.shape, 0)],
        out_specs=_token_spec(D_MODEL),
        compiler_params=_params(1),
        name="ffn_first",
    )(x, mod, npre, npost, w1, w3, w2)


def _mix_ffn(x, att, gla, w_out, mod, latent_len, npre, npost, w1, w3, w2):
    t = x.shape[0]
    return pl.pallas_call(
        functools.partial(_mix_ffn_kernel, tiles_per_seq=_tiles_per_seq(latent_len)),
        out_shape=jax.ShapeDtypeStruct((t, D_MODEL), F32),
        grid=(t // TOKEN_TILE,),
        in_specs=[_token_spec(D_MODEL), _token_spec(ATT_Q), _token_spec(GLA_V),
                  _resident(w_out.shape), _resident(mod.shape), _resident(npre.shape),
                  _resident(npost.shape), _ffn_weight_spec(w1.shape, 1),
                  _ffn_weight_spec(w3.shape, 1), _ffn_weight_spec(w2.shape, 1)],
        out_specs=_token_spec(D_MODEL),
        compiler_params=_params(1),
        name="mix_ffn",
    )(x, att, gla, w_out, mod, npre, npost, w1, w3, w2)


def _project(x, mod, latent_len, cache_seq, npre, w_in, w_up, b_up):
    t = x.shape[0]
    tiles = _tiles_per_seq(latent_len)
    in_specs = [_token_spec(D_MODEL), _resident(mod.shape), _resident(npre.shape),
                _resident(w_in.shape), _resident(w_up.shape), _resident(b_up.shape)]
    args = [x, mod, npre, w_in, w_up, b_up]
    if tiles is not None:
        in_specs += [pl.BlockSpec((TOKEN_TILE, LANES), lambda i: (i % tiles, 0))] * 3
        args += [jnp.asarray(tab) for tab in _rope_tables(latent_len)]
    outs = ((ATT_Q, BF16), (2 * ATT_KV, BF16), (2 * ATT_KV, BF16), (GLA_QK, F32), (GLA_QK, F32),
            (GLA_V, BF16), (GLA_V, F32), (2 * GLA_QK, F32))
    out_shape = [jax.ShapeDtypeStruct((t, w), dt) for w, dt in outs]
    out_specs = [_token_spec(w) for w, _ in outs]
    if cache_seq is not None:
        seqs = TOKEN_TILE // cache_seq
        cache = (t // cache_seq, 1, N_KV_HEADS, HEAD_DIM, cache_seq)
        out_shape += [jax.ShapeDtypeStruct(cache, F32)] * 2
        out_specs += [pl.BlockSpec((seqs,) + cache[1:], lambda i: (i, 0, 0, 0, 0))] * 2
    return pl.pallas_call(
        functools.partial(_proj_kernel, tiles_per_seq=tiles, cache_seq=cache_seq),
        out_shape=out_shape,
        grid=(t // TOKEN_TILE,),
        in_specs=in_specs,
        out_specs=out_specs,
        compiler_params=_params(1),
        name="project",
    )(*args)


def _rope_tables(seq_len):
    half = HEAD_DIM // 2
    inv_freq = np.float32(ROPE_BASE) ** (-np.arange(0, half, 2, dtype=np.float32) / half)
    pos = np.arange(seq_len)
    row = (pos // GRID_W).astype(np.float32)
    col = (pos % GRID_W).astype(np.float32)
    within = np.arange(LANES) % HEAD_DIM
    idx = within % half
    freq = inv_freq[idx % (half // 2)].astype(np.float32)
    p = np.where((within // half == 0)[None, :], row[:, None], col[:, None])
    ang = (p * freq[None, :]).astype(np.float32)
    cos, sin = np.cos(ang).astype(np.float32), np.sin(ang).astype(np.float32)
    first = (idx < half // 2)[None, :]
    zero = np.float32(0.0)
    return cos, np.where(first, -sin, zero), np.where(first, zero, sin)


def _attn_kernel(*refs, windowed, seq_len):
    if windowed:
        sink_ref, q_ref, k_ref, v_ref, kc_ref, vc_ref, o_ref = refs
    else:
        sink_ref, q_ref, k_ref, v_ref, o_ref = refs
    nq = q_ref.shape[0]
    lane = lax.broadcasted_iota(jnp.int32, (1, LANES), 1)
    half_of = [lane < HALF, lane >= HALF]

    if windowed:
        i = pl.program_id(1)
        span = 3 * BLOCK
        start = pl.multiple_of(jnp.clip((i - 1) * BLOCK, 0, seq_len - span), BLOCK)
        kpos = start + lax.broadcasted_iota(jnp.int32, (1, span), 1)
        qpos = i * BLOCK + lax.broadcasted_iota(jnp.int32, (nq, 1), 0)
        valid = jnp.abs(kpos - qpos) <= WINDOW
        valid2 = jnp.concatenate([valid, valid], axis=0)
        k_loc = k_ref[pl.ds(start, span), :]
        v_loc = v_ref[pl.ds(start, span), :]
        k_ctx, v_ctx = kc_ref[0], vc_ref[0]
        segs = [(k_loc[:, :ATT_KV], k_loc[:, ATT_KV:], v_loc[:, :ATT_KV], v_loc[:, ATT_KV:], valid2),
                (k_ctx.astype(BF16), pltpu.roll(k_ctx, HALF, axis=1).astype(BF16),
                 v_ctx.astype(BF16), pltpu.roll(v_ctx, HALF, axis=1).astype(BF16), None)]
    else:
        segs = [(k_ref[:, :ATT_KV], k_ref[:, ATT_KV:], v_ref[:, :ATT_KV], v_ref[:, ATT_KV:], None)]

    rows = lax.broadcasted_iota(jnp.int32, (2 * nq, 1), 0)
    zero = jnp.zeros((), BF16)
    out = [None] * (ATT_Q // LANES)
    for g in range(N_KV_HEADS):
        tiles = (2 * g, 2 * g + 1)
        for e in range(2):
            qm = jnp.concatenate(
                [jnp.where(half_of[e], q_ref[:, j * LANES:(j + 1) * LANES], zero) for j in tiles],
                axis=0)
            sink = jnp.where(rows < nq, sink_ref[2 * tiles[0] + e],
                             sink_ref[2 * tiles[1] + e]) * LOG2_E
            scores = []
            for k, k_sw, _, _, m in segs:
                s = _dot_t(qm, k if e == g else k_sw)
                scores.append(s if m is None else jnp.where(m, s, NEG))
            mx = sink
            for s in scores:
                mx = jnp.maximum(mx, jnp.max(s, axis=-1, keepdims=True))
            probs = [jnp.exp2(s - mx) for s in scores]
            den = jnp.exp2(sink - mx)
            for p in probs:
                den = den + jnp.sum(p, axis=-1, keepdims=True)
            o = None
            for p, (_, _, v, v_sw, _) in zip(probs, segs):
                part = _dot(p.astype(BF16), jnp.where(half_of[e], v if e == g else v_sw, zero))
                o = part if o is None else o + part
            o = o * (1.0 / den)
            for r, j in enumerate(tiles):
                blk = o[r * nq:(r + 1) * nq, :]
                out[j] = blk if out[j] is None else out[j] + blk
    for j, blk in enumerate(out):
        o_ref[:, j * LANES:(j + 1) * LANES] = blk.astype(o_ref.dtype)


def _smem_spec():
    return pl.BlockSpec(memory_space=pltpu.SMEM)


def _context_attention(sink, q, k, v, batch, seq_len):
    def seq(width):
        return pl.BlockSpec((seq_len, width), lambda b: (b, 0))
    return pl.pallas_call(
        functools.partial(_attn_kernel, windowed=False, seq_len=seq_len),
        out_shape=jax.ShapeDtypeStruct(q.shape, BF16),
        grid=(batch,),
        in_specs=[_smem_spec(), seq(ATT_Q), seq(2 * ATT_KV), seq(2 * ATT_KV)],
        out_specs=seq(ATT_Q),
        compiler_params=_params(1),
        name="context_attention",
    )(sink, q, k, v)


def _latent_attention(sink, q, k, v, k_ctx, v_ctx, batch, seq_len):
    nb = seq_len // BLOCK
    past = k_ctx.shape[1]
    return pl.pallas_call(
        functools.partial(_attn_kernel, windowed=True, seq_len=seq_len),
        out_shape=jax.ShapeDtypeStruct(q.shape, BF16),
        grid=(batch, nb),
        in_specs=[_smem_spec(),
                  pl.BlockSpec((BLOCK, ATT_Q), lambda b, i: (b * nb + i, 0)),
                  pl.BlockSpec((seq_len, 2 * ATT_KV), lambda b, i: (b, 0)),
                  pl.BlockSpec((seq_len, 2 * ATT_KV), lambda b, i: (b, 0)),
                  pl.BlockSpec((1, past, ATT_KV), lambda b, i: (b, 0, 0)),
                  pl.BlockSpec((1, past, ATT_KV), lambda b, i: (b, 0, 0))],
        out_specs=pl.BlockSpec((BLOCK, ATT_Q), lambda b, i: (b * nb + i, 0)),
        compiler_params=_params(2),
        name="latent_attention",
    )(sink, q, k, v, k_ctx, v_ctx)


def _split2(x):
    hi = x.astype(BF16)
    lo = (x - hi.astype(F32)).astype(BF16)
    return hi, lo


def _gla_kernel(*refs, n_chunks, has_s0, emit_state):
    refs = list(refs)
    gq_ref, gk_ref, gv_ref, gg_ref, la_ref, gn_ref = refs[:6]
    pos = 6
    s0_refs = refs[pos:pos + 2] if has_s0 else None
    pos += 2 if has_s0 else 0
    o_ref = refs[pos]
    pos += 1
    sfin_refs = refs[pos:pos + 2] if emit_state else None
    pos += 2 if emit_state else 0
    cum_ref, sent_ref, st_ref = refs[pos:]

    C = GLA_CHUNK
    n_pairs = GLA_QK // LANES
    lane = lax.broadcasted_iota(jnp.int32, (1, LANES), 1)
    half_of = [lane < HALF, lane >= HALF]
    r_i = lax.broadcasted_iota(jnp.int32, (C, C), 0)
    c_i = lax.broadcasted_iota(jnp.int32, (C, C), 1)
    lower = c_i <= r_i
    upper = c_i >= r_i
    tri = [jnp.where(lower, 1.0, 0.0).astype(BF16), jnp.where(upper, 1.0, 0.0).astype(BF16)]

    for d in range(2):
        for p in range(n_pairs):
            if has_s0:
                s0 = s0_refs[d][0, 0, 2 * p:2 * p + 2, :, :].reshape(2 * GLA_DK, GLA_DV)
                st_ref[d, p] = s0.T
            else:
                st_ref[d, p] = jnp.zeros((GLA_DV, 2 * GLA_DK), F32)

    def scan_step(i, carry):
        for d, n in ((0, i), (1, n_chunks - 1 - i)):
            rows = pl.ds(pl.multiple_of(n * C, C), C)
            la = la_ref[rows, d * GLA_QK:(d + 1) * GLA_QK]
            cum = sum(_dot(tri[d], part) for part in _split2(la))
            cum_ref[d, rows, :] = cum
            tot = cum[C - 1:C, :] if d == 0 else cum[0:1, :]
            k_in = gk_ref[rows, :] * jnp.exp(tot - cum)
            decay = jnp.exp(tot)
            for p in range(n_pairs):
                k2 = k_in[:, p * LANES:(p + 1) * LANES]
                kv_t = None
                for e in range(2):
                    h = 2 * p + e
                    v_t = gv_ref[rows, h * GLA_DV:(h + 1) * GLA_DV].T
                    part = _dot(v_t, jnp.where(half_of[e], k2, 0.0).astype(BF16))
                    kv_t = part if kv_t is None else kv_t + part
                st = st_ref[d, p]
                sent_ref[d, n, p] = st.astype(BF16)
                st_ref[d, p] = decay[:, p * LANES:(p + 1) * LANES] * st + kv_t
        return carry

    lax.fori_loop(0, n_chunks, scan_step, 0, unroll=GLA_UNROLL)

    gnorm = gn_ref[...]
    qscale = GLA_DK ** -0.5

    def out_step(n, carry):
        rows = pl.ds(pl.multiple_of(n * C, C), C)
        q = gq_ref[rows, :] * qscale
        k = gk_ref[rows, :]
        qs, ks, qin = [], [], []
        for d in range(2):
            cum = cum_ref[d, rows, :]
            ref = cum[C // 2:C // 2 + 1, :]
            qs.append(q * jnp.exp(cum - ref))
            ks.append((k * jnp.exp(ref - cum)).astype(BF16))
            qin.append(q * jnp.exp(cum))
        for p in range(n_pairs):
            sl = slice(p * LANES, (p + 1) * LANES)
            for e in range(2):
                h = 2 * p + e
                hs = slice(h * GLA_DV, (h + 1) * GLA_DV)
                m = half_of[e]
                s_f = _dot_t(jnp.where(m, qs[0][:, sl], 0.0).astype(BF16), ks[0][:, sl])
                s_b = _dot_t(jnp.where(m, qs[1][:, sl], 0.0).astype(BF16), ks[1][:, sl])
                prob = jnp.where(lower, s_f, 0.0) + jnp.where(upper, s_b, 0.0)
                o = _dot(prob.astype(BF16), gv_ref[rows, hs])
                for d in range(2):
                    o = o + _dot_t(jnp.where(m, qin[d][:, sl], 0.0).astype(BF16), sent_ref[d, n, p])
                o = o * lax.rsqrt(jnp.mean(o * o, axis=-1, keepdims=True) + EPS) * gnorm
                o_ref[rows, hs] = (o * _silu(gg_ref[rows, hs])).astype(o_ref.dtype)
        return carry

    lax.fori_loop(0, n_chunks, out_step, 0, unroll=GLA_UNROLL)

    if emit_state:
        for d in range(2):
            for p in range(n_pairs):
                sfin_refs[d][0, 0, 2 * p:2 * p + 2, :, :] = st_ref[d, p].T.reshape(2, GLA_DK, GLA_DV)


def _gla(gq, gk, gv, gg, la, gnorm, batch, seq_len, s0=None, emit_state=False):
    n_chunks = seq_len // GLA_CHUNK
    has_s0 = s0 is not None

    def seq(width):
        return pl.BlockSpec((seq_len, width), lambda b: (b, 0))
    state_spec = pl.BlockSpec((1, 1, GLA_HEADS, GLA_DK, GLA_DV), lambda b: (b, 0, 0, 0, 0))
    in_specs = [seq(GLA_QK), seq(GLA_QK), seq(GLA_V), seq(GLA_V), seq(2 * GLA_QK),
                pl.BlockSpec((1, GLA_DV), lambda b: (0, 0))]
    args = [gq, gk, gv, gg, la, gnorm]
    if has_s0:
        in_specs += [state_spec, state_spec]
        args += list(s0)
    out_shape = [jax.ShapeDtypeStruct((batch * seq_len, GLA_V), BF16)]
    out_specs = [seq(GLA_V)]
    if emit_state:
        out_shape += [jax.ShapeDtypeStruct((batch, 1, GLA_HEADS, GLA_DK, GLA_DV), F32)] * 2
        out_specs += [state_spec, state_spec]
    n_pairs = GLA_QK // LANES
    return pl.pallas_call(
        functools.partial(_gla_kernel, n_chunks=n_chunks, has_s0=has_s0, emit_state=emit_state),
        out_shape=out_shape,
        grid=(batch,),
        in_specs=in_specs,
        out_specs=out_specs,
        scratch_shapes=[pltpu.VMEM((2, seq_len, GLA_QK), F32),
                        pltpu.VMEM((2, n_chunks, n_pairs, GLA_DV, LANES), BF16),
                        pltpu.VMEM((2, n_pairs, GLA_DV, LANES), F32)],
        compiler_params=_params(1),
        name="gla",
    )(*args)


def kernel(x_prompt, x_sample, cache_k, cache_v, state_gla_fwd, state_gla_bwd, c, c_ctx,
           w_ada, b_ada, norm_pre, norm_post, ffn_w1, ffn_w3, ffn_w2, w_in,
           gla_w_up, gla_b_up, gla_norm, attn_sink, w_out):
    depth = w_in.shape[0]
    assert depth == 1, "single trunk layer"
    batch, seq = x_prompt.shape[0], x_prompt.shape[1]
    dec_batch, dec_seq = x_sample.shape[0], x_sample.shape[1]
    past = cache_k.shape[2]
    l = 0

    cond = jnp.concatenate(
        [c_ctx[None, :], c, jnp.zeros((MOD_ROWS - 1 - dec_batch, D_MODEL), F32)], axis=0)
    mod = _ada_modulation(cond, w_ada[l], b_ada[l][None, :])

    npre, npost = norm_pre[l], norm_post[l]
    w1 = ffn_w1[l].astype(BF16)
    w3 = ffn_w3[l].astype(BF16)
    w2 = ffn_w2[l].astype(BF16)
    w_in_b = w_in[l].astype(BF16)
    w_out_b = w_out[l].astype(BF16)
    zeros = jnp.zeros((GLA_LOW_RANK, GLA_QK), F32)
    w_up = jnp.concatenate(
        [jnp.concatenate([gla_w_up[l, 0], zeros], axis=1),
         jnp.concatenate([zeros, gla_w_up[l, 1]], axis=1)], axis=0).astype(BF16)
    b_up = gla_b_up[l].reshape(1, 2 * GLA_QK)
    gnorm = gla_norm[l][None, :]
    sink = attn_sink[l]

    def trunk(x, latent):
        n_batch, n_seq = (dec_batch, dec_seq) if latent else (batch, seq)
        latent_len = n_seq if latent else None
        x1 = _ffn_first(x, mod, latent_len, npre, npost, w1, w3, w2)
        q, k, v, gq, gk, gv, gg, la, *cache_t = _project(
            x1, mod, latent_len, None if latent else n_seq, npre, w_in_b, w_up, b_up)
        if latent:
            att = _latent_attention(sink, q, k, v, cache_k[:, l].reshape(dec_batch, past, ATT_KV),
                                    cache_v[:, l].reshape(dec_batch, past, ATT_KV), n_batch, n_seq)
            (gla,) = _gla(gq, gk, gv, gg, la, gnorm, n_batch, n_seq,
                          s0=(state_gla_fwd[:, l:l + 1], state_gla_bwd[:, l:l + 1]))
            extras = ()
        else:
            att = _context_attention(sink, q, k, v, n_batch, n_seq)
            gla, s_f, s_b = _gla(gq, gk, gv, gg, la, gnorm, n_batch, n_seq, emit_state=True)
            k_new, v_new = (jnp.transpose(c_t, (0, 1, 4, 2, 3)) for c_t in cache_t)
            extras = (k_new, v_new, s_f, s_b)
        y = _mix_ffn(x1, att, gla, w_out_b, mod, latent_len, npre, npost, w1, w3, w2)
        return y.reshape(n_batch, n_seq, D_MODEL), extras

    y_prompt, (k_new, v_new, s_f, s_b) = trunk(x_prompt.reshape(batch * seq, D_MODEL), False)
    y_sample, _ = trunk(x_sample.reshape(dec_batch * dec_seq, D_MODEL), True)
    return (y_prompt, y_sample, k_new, v_new, s_f, s_b)
```

```python
import functools

import numpy as np
import jax
import jax.numpy as jnp
from jax import lax
from jax.experimental import pallas as pl
from jax.experimental.pallas import tpu as pltpu

F32 = jnp.float32
BF16 = jnp.bfloat16

D_MODEL = 1024
GRID_W = 64
N_Q_HEADS = 8
N_KV_HEADS = 2
HEAD_DIM = 64
WINDOW = 128
BLOCK = 128
ROPE_BASE = 10000.0
GLA_HEADS = 4
GLA_DK = 64
GLA_DV = 128
GLA_LOW_RANK = 16
GLA_TAU = 16.0
D_FF = 2816
N_MOD = 9
EPS = 1e-6
NEG = -1e30

ATT_Q = N_Q_HEADS * HEAD_DIM
ATT_KV = N_KV_HEADS * HEAD_DIM
GLA_QK = GLA_HEADS * GLA_DK
GLA_V = GLA_HEADS * GLA_DV
OFF_Q = 0
OFF_K = OFF_Q + ATT_Q
OFF_V = OFF_K + ATT_KV
OFF_GQ = OFF_V + ATT_KV
OFF_GK = OFF_GQ + GLA_QK
OFF_GV = OFF_GK + GLA_QK
OFF_GG = OFF_GV + GLA_V
OFF_LR = OFF_GG + GLA_V
IN_WIDTH = OFF_LR + 2 * GLA_LOW_RANK
LOG2_E = 1.4426950408889634
ATT_Q_SCALE = HEAD_DIM ** -0.5 * LOG2_E

LANES = 128
SUBLANES = 8
HALF = LANES // 2
VMEM_LIMIT = 56 * 1024 * 1024

TOKEN_TILE = 512
FF_SPLITS = (0, 1536, D_FF)
CTX_SEQS_PER_STEP = 2
LAT_BLOCKS_PER_STEP = 2
GLA_CHUNK = 128
GLA_UNROLL = 4
MOD_ROWS = 8


def _params(n_axes):
    return pltpu.CompilerParams(
        dimension_semantics=("arbitrary",) * n_axes, vmem_limit_bytes=VMEM_LIMIT)


def _resident(shape):
    zeros = (0,) * len(shape)
    return pl.BlockSpec(shape, lambda *_: zeros, pipeline_mode=pl.Buffered(1))


def _sigmoid(x):
    return 1.0 / (1.0 + jnp.exp(-x))


def _silu(x):
    return x * _sigmoid(x)


def _rms(x, g):
    return x * lax.rsqrt(jnp.mean(x * x, axis=-1, keepdims=True) + EPS) * g


def _dot(a, b):
    return jnp.dot(a, b, preferred_element_type=F32)


def _dot_t(a, b):
    return lax.dot_general(a, b, (((1,), (1,)), ((), ())), preferred_element_type=F32)


def _ada_kernel(cond_ref, w_ref, b_ref, o_ref):
    c = cond_ref[...]
    o_ref[0] = _dot(_silu(c).astype(BF16), w_ref[...].astype(BF16)) + b_ref[...]


def _ada_modulation(cond, w_ada, b_ada):
    return pl.pallas_call(
        _ada_kernel,
        out_shape=jax.ShapeDtypeStruct((N_MOD, MOD_ROWS, D_MODEL), F32),
        grid=(N_MOD,),
        in_specs=[
            pl.BlockSpec((MOD_ROWS, D_MODEL), lambda j: (0, 0)),
            pl.BlockSpec((D_MODEL, D_MODEL), lambda j: (0, j)),
            pl.BlockSpec((1, D_MODEL), lambda j: (0, j)),
        ],
        out_specs=pl.BlockSpec((1, MOD_ROWS, D_MODEL), lambda j: (j, 0, 0)),
        compiler_params=_params(1),
        name="ada_modulation",
    )(cond, w_ada, b_ada)


class _Mod:
    def __init__(self, mod_ref, tiles_per_seq):
        self.ref = mod_ref
        self.row = 0 if tiles_per_seq is None else 1 + pl.program_id(0) // tiles_per_seq

    def __getitem__(self, m):
        return self.ref[m, pl.ds(self.row, 1), :]


def _modulated(x, mod, npre_ref, i):
    return _rms(x, npre_ref[i:i + 1, :]) * (1.0 + mod[3 * i + 1]) + mod[3 * i]


def _residual(x, out, mod, npost_ref, i, weight):
    return x + (weight * mod[3 * i + 2]) * _rms(out, npost_ref[i:i + 1, :])


def _ffn_sublayer(x, i, mod, npre_ref, npost_ref, w1_ref, w3_ref, w2_ref):
    h = _modulated(x, mod, npre_ref, i).astype(BF16)
    acc = None
    for lo, hi in zip(FF_SPLITS[:-1], FF_SPLITS[1:]):
        a = _dot(h, w1_ref[:, lo:hi])
        g = _dot(h, w3_ref[:, lo:hi])
        part = _dot((_silu(a) * g).astype(BF16), w2_ref[lo:hi, :])
        acc = part if acc is None else acc + part
    return _residual(x, acc, mod, npost_ref, i, 0.5)


def _ffn_first_kernel(x_ref, mod_ref, npre_ref, npost_ref, w1_ref, w3_ref, w2_ref, o_ref,
                      *, tiles_per_seq):
    mod = _Mod(mod_ref, tiles_per_seq)
    o_ref[...] = _ffn_sublayer(x_ref[...], 0, mod, npre_ref, npost_ref, w1_ref, w3_ref, w2_ref)


def _mix_ffn_kernel(x_ref, att_ref, gla_ref, wo_ref, mod_ref, npre_ref, npost_ref,
                    w1_ref, w3_ref, w2_ref, o_ref, *, tiles_per_seq):
    mod = _Mod(mod_ref, tiles_per_seq)
    mix = _dot(att_ref[...], wo_ref[0:ATT_Q, :]) + _dot(gla_ref[...], wo_ref[ATT_Q:, :])
    x = _residual(x_ref[...], mix, mod, npost_ref, 1, 1.0)
    o_ref[...] = _ffn_sublayer(x, 2, mod, npre_ref, npost_ref, w1_ref, w3_ref, w2_ref)


def _rope_tile(x, cos, sin_up, sin_dn):
    up = pltpu.roll(x, LANES - 16, axis=1)
    dn = pltpu.roll(x, 16, axis=1)
    return x * cos + up * sin_up + dn * sin_dn


def _proj_kernel(*refs, tiles_per_seq, cache_seq):
    x_ref, mod_ref, npre_ref, win_ref, wup_ref, bup_ref = refs[:6]
    rope_refs = refs[6:9] if tiles_per_seq is not None else None
    outs = refs[6 + (3 if rope_refs else 0):]
    q_ref, k_ref, v_ref, gq_ref, gk_ref, gv_ref, gg_ref, la_ref = outs[:8]
    h = _modulated(x_ref[...], _Mod(mod_ref, tiles_per_seq), npre_ref, 1).astype(BF16)
    q = _dot(h, win_ref[:, OFF_Q:OFF_K])
    kv = _dot(h, win_ref[:, OFF_K:OFF_GQ])
    k, v = kv[:, :ATT_KV], kv[:, ATT_KV:]
    if rope_refs:
        cos, sup, sdn = (r[...] for r in rope_refs)
        q = jnp.concatenate([_rope_tile(q[:, j * LANES:(j + 1) * LANES], cos, sup, sdn)
                             for j in range(ATT_Q // LANES)], axis=1)
        k_att = _rope_tile(k, cos, sup, sdn)
    else:
        k_att = k
    q_ref[...] = (q * ATT_Q_SCALE).astype(BF16)
    k_ref[:, :ATT_KV] = k_att.astype(BF16)
    k_ref[:, ATT_KV:] = pltpu.roll(k_att, HALF, axis=1).astype(BF16)
    v_ref[:, :ATT_KV] = v.astype(BF16)
    v_ref[:, ATT_KV:] = pltpu.roll(v, HALF, axis=1).astype(BF16)
    if cache_seq is not None:
        for t, dst in ((k.T, outs[8]), (v.T, outs[9])):
            for b in range(TOKEN_TILE // cache_seq):
                dst[b, 0] = t[:, b * cache_seq:(b + 1) * cache_seq].reshape(
                    N_KV_HEADS, HEAD_DIM, cache_seq)
    gqk = _dot(h, win_ref[:, OFF_GQ:OFF_GV])
    gq_ref[...] = gqk[:, :GLA_QK]
    gk_ref[...] = gqk[:, GLA_QK:]
    gv_ref[...] = _dot(h, win_ref[:, OFF_GV:OFF_GG]).astype(BF16)
    gg_ref[...] = _dot(h, win_ref[:, OFF_GG:OFF_LR])
    lr = _dot(h, win_ref[:, OFF_LR:IN_WIDTH])
    z = _dot(lr.astype(BF16), wup_ref[...]) + bup_ref[...]
    log_sig = jnp.minimum(z, 0.0) - jnp.log1p(jnp.exp(-jnp.abs(z)))
    la_ref[...] = log_sig * (1.0 / GLA_TAU)


def _token_spec(width):
    return pl.BlockSpec((TOKEN_TILE, width), lambda i: (i, 0))


def _ffn_weight_spec(shape, j):
    return pl.BlockSpec((None,) + tuple(shape[1:]), lambda i: (j, 0, 0),
                        pipeline_mode=pl.Buffered(1))


def _tiles_per_seq(latent_len):
    return None if latent_len is None else latent_len // TOKEN_TILE


def _ffn_first(x, mod, latent_len, npre, npost, w1, w3, w2):
    t = x.shape[0]
    return pl.pallas_call(
        functools.partial(_ffn_first_kernel, tiles_per_seq=_tiles_per_seq(latent_len)),
        out_shape=jax.ShapeDtypeStruct((t, D_MODEL), F32),
        grid=(t // TOKEN_TILE,),
        in_specs=[_token_spec(D_MODEL), _resident(mod.shape), _resident(npre.shape),
                  _resident(npost.shape), _ffn_weight_spec(w1.shape, 0),
                  _ffn_weight_spec(w3.shape, 0), _ffn_weight_spec(w2.shape, 0)],
        out_specs=_token_spec(D_MODEL),
        compiler_params=_params(1),
        name="ffn_first",
    )(x, mod, npre, npost, w1, w3, w2)


def _mix_ffn(x, att, gla, w_out, mod, latent_len, npre, npost, w1, w3, w2):
    t = x.shape[0]
    return pl.pallas_call(
        functools.partial(_mix_ffn_kernel, tiles_per_seq=_tiles_per_seq(latent_len)),
        out_shape=jax.ShapeDtypeStruct((t, D_MODEL), F32),
        grid=(t // TOKEN_TILE,),
        in_specs=[_token_spec(D_MODEL), _token_spec(ATT_Q), _token_spec(GLA_V),
                  _resident(w_out.shape), _resident(mod.shape), _resident(npre.shape),
                  _resident(npost.shape), _ffn_weight_spec(w1.shape, 1),
                  _ffn_weight_spec(w3.shape, 1), _ffn_weight_spec(w2.shape, 1)],
        out_specs=_token_spec(D_MODEL),
        compiler_params=_params(1),
        name="mix_ffn",
    )(x, att, gla, w_out, mod, npre, npost, w1, w3, w2)


def _project(x, mod, latent_len, cache_seq, npre, w_in, w_up, b_up):
    t = x.shape[0]
    tiles = _tiles_per_seq(latent_len)
    in_specs = [_token_spec(D_MODEL), _resident(mod.shape), _resident(npre.shape),
                _resident(w_in.shape), _resident(w_up.shape), _resident(b_up.shape)]
    args = [x, mod, npre, w_in, w_up, b_up]
    if tiles is not None:
        in_specs += [pl.BlockSpec((TOKEN_TILE, LANES), lambda i: (i % tiles, 0))] * 3
        args += [jnp.asarray(tab) for tab in _rope_tables(latent_len)]
    outs = ((ATT_Q, BF16), (2 * ATT_KV, BF16), (2 * ATT_KV, BF16), (GLA_QK, F32), (GLA_QK, F32),
            (GLA_V, BF16), (GLA_V, F32), (2 * GLA_QK, F32))
    out_shape = [jax.ShapeDtypeStruct((t, w), dt) for w, dt in outs]
    out_specs = [_token_spec(w) for w, _ in outs]
    if cache_seq is not None:
        seqs = TOKEN_TILE // cache_seq
        cache = (t // cache_seq, 1, N_KV_HEADS, HEAD_DIM, cache_seq)
        out_shape += [jax.ShapeDtypeStruct(cache, F32)] * 2
        out_specs += [pl.BlockSpec((seqs,) + cache[1:], lambda i: (i, 0, 0, 0, 0))] * 2
    return pl.pallas_call(
        functools.partial(_proj_kernel, tiles_per_seq=tiles, cache_seq=cache_seq),
        out_shape=out_shape,
        grid=(t // TOKEN_TILE,),
        in_specs=in_specs,
        out_specs=out_specs,
        compiler_params=_params(1),
        name="project",
    )(*args)


def _rope_tables(seq_len):
    half = HEAD_DIM // 2
    inv_freq = np.float32(ROPE_BASE) ** (-np.arange(0, half, 2, dtype=np.float32) / half)
    pos = np.arange(seq_len)
    row = (pos // GRID_W).astype(np.float32)
    col = (pos % GRID_W).astype(np.float32)
    within = np.arange(LANES) % HEAD_DIM
    idx = within % half
    freq = inv_freq[idx % (half // 2)].astype(np.float32)
    p = np.where((within // half == 0)[None, :], row[:, None], col[:, None])
    ang = (p * freq[None, :]).astype(np.float32)
    cos, sin = np.cos(ang).astype(np.float32), np.sin(ang).astype(np.float32)
    first = (idx < half // 2)[None, :]
    zero = np.float32(0.0)
    return cos, np.where(first, -sin, zero), np.where(first, zero, sin)


def _attend(q_ref, q_rows, nq, segs, sink_ref):
    lane = lax.broadcasted_iota(jnp.int32, (1, LANES), 1)
    half_of = [lane < HALF, lane >= HALF]
    rows = lax.broadcasted_iota(jnp.int32, (2 * nq, 1), 0)
    zero = jnp.zeros((), BF16)
    out = [None] * (ATT_Q // LANES)
    for g in range(N_KV_HEADS):
        tiles = (2 * g, 2 * g + 1)
        for e in range(2):
            qm = jnp.concatenate(
                [jnp.where(half_of[e], q_ref[q_rows, j * LANES:(j + 1) * LANES], zero)
                 for j in tiles], axis=0)
            sink = jnp.where(rows < nq, sink_ref[2 * tiles[0] + e],
                             sink_ref[2 * tiles[1] + e]) * LOG2_E
            scores = []
            for k, k_sw, _, _, m in segs:
                s = _dot_t(qm, k if e == g else k_sw)
                scores.append(s if m is None else jnp.where(m, s, NEG))
            mx = sink
            for s in scores:
                mx = jnp.maximum(mx, jnp.max(s, axis=-1, keepdims=True))
            probs = [jnp.exp2(s - mx) for s in scores]
            den = jnp.exp2(sink - mx)
            for p in probs:
                den = den + jnp.sum(p, axis=-1, keepdims=True)
            o = None
            for p, (_, _, v, v_sw, _) in zip(probs, segs):
                part = _dot(p.astype(BF16), jnp.where(half_of[e], v if e == g else v_sw, zero))
                o = part if o is None else o + part
            o = o * (1.0 / den)
            for r, j in enumerate(tiles):
                blk = o[r * nq:(r + 1) * nq, :]
                out[j] = blk if out[j] is None else out[j] + blk
    return out


def _context_attn_kernel(sink_ref, q_ref, k_ref, v_ref, o_ref, *, seq_len):
    for s in range(q_ref.shape[0] // seq_len):
        rows = slice(s * seq_len, (s + 1) * seq_len)
        segs = [(k_ref[rows, :ATT_KV], k_ref[rows, ATT_KV:],
                 v_ref[rows, :ATT_KV], v_ref[rows, ATT_KV:], None)]
        for j, t in enumerate(_attend(q_ref, rows, seq_len, segs, sink_ref)):
            o_ref[rows, j * LANES:(j + 1) * LANES] = t.astype(o_ref.dtype)


def _latent_attn_kernel(sink_ref, q_ref, k_ref, v_ref, kc_ref, vc_ref, o_ref, *, seq_len):
    k_ctx, v_ctx = kc_ref[0], vc_ref[0]
    ctx = (k_ctx.astype(BF16), pltpu.roll(k_ctx, HALF, axis=1).astype(BF16),
           v_ctx.astype(BF16), pltpu.roll(v_ctx, HALF, axis=1).astype(BF16), None)
    span = 3 * BLOCK
    per_step = q_ref.shape[0] // BLOCK
    for s in range(per_step):
        i = pl.program_id(1) * per_step + s
        start = pl.multiple_of(jnp.clip((i - 1) * BLOCK, 0, seq_len - span), BLOCK)
        kpos = start + lax.broadcasted_iota(jnp.int32, (1, span), 1)
        qpos = i * BLOCK + lax.broadcasted_iota(jnp.int32, (BLOCK, 1), 0)
        valid = jnp.abs(kpos - qpos) <= WINDOW
        valid2 = jnp.concatenate([valid, valid], axis=0)
        k_loc = k_ref[pl.ds(start, span), :]
        v_loc = v_ref[pl.ds(start, span), :]
        segs = [(k_loc[:, :ATT_KV], k_loc[:, ATT_KV:], v_loc[:, :ATT_KV], v_loc[:, ATT_KV:], valid2),
                ctx]
        rows = slice(s * BLOCK, (s + 1) * BLOCK)
        for j, t in enumerate(_attend(q_ref, rows, BLOCK, segs, sink_ref)):
            o_ref[rows, j * LANES:(j + 1) * LANES] = t.astype(o_ref.dtype)


def _smem_spec():
    return pl.BlockSpec(memory_space=pltpu.SMEM)


def _context_attention(sink, q, k, v, batch, seq_len):
    rows = CTX_SEQS_PER_STEP * seq_len

    def seq(width):
        return pl.BlockSpec((rows, width), lambda b: (b, 0))
    return pl.pallas_call(
        functools.partial(_context_attn_kernel, seq_len=seq_len),
        out_shape=jax.ShapeDtypeStruct(q.shape, BF16),
        grid=(batch // CTX_SEQS_PER_STEP,),
        in_specs=[_smem_spec(), seq(ATT_Q), seq(2 * ATT_KV), seq(2 * ATT_KV)],
        out_specs=seq(ATT_Q),
        compiler_params=_params(1),
        name="context_attention",
    )(sink, q, k, v)


def _latent_attention(sink, q, k, v, k_ctx, v_ctx, batch, seq_len):
    steps = seq_len // (LAT_BLOCKS_PER_STEP * BLOCK)
    rows = LAT_BLOCKS_PER_STEP * BLOCK
    past = k_ctx.shape[1]
    return pl.pallas_call(
        functools.partial(_latent_attn_kernel, seq_len=seq_len),
        out_shape=jax.ShapeDtypeStruct(q.shape, BF16),
        grid=(batch, steps),
        in_specs=[_smem_spec(),
                  pl.BlockSpec((rows, ATT_Q), lambda b, i: (b * steps + i, 0)),
                  pl.BlockSpec((seq_len, 2 * ATT_KV), lambda b, i: (b, 0)),
                  pl.BlockSpec((seq_len, 2 * ATT_KV), lambda b, i: (b, 0)),
                  pl.BlockSpec((1, past, ATT_KV), lambda b, i: (b, 0, 0)),
                  pl.BlockSpec((1, past, ATT_KV), lambda b, i: (b, 0, 0))],
        out_specs=pl.BlockSpec((rows, ATT_Q), lambda b, i: (b * steps + i, 0)),
        compiler_params=_params(2),
        name="latent_attention",
    )(sink, q, k, v, k_ctx, v_ctx)


def _split2(x):
    hi = x.astype(BF16)
    lo = (x - hi.astype(F32)).astype(BF16)
    return hi, lo


def _gla_kernel(*refs, n_chunks, has_s0, emit_state):
    refs = list(refs)
    gq_ref, gk_ref, gv_ref, gg_ref, la_ref, gn_ref = refs[:6]
    pos = 6
    s0_refs = refs[pos:pos + 2] if has_s0 else None
    pos += 2 if has_s0 else 0
    o_ref = refs[pos]
    pos += 1
    sfin_refs = refs[pos:pos + 2] if emit_state else None
    pos += 2 if emit_state else 0
    cum_ref, sent_ref, st_ref = refs[pos:]

    C = GLA_CHUNK
    n_pairs = GLA_QK // LANES
    lane = lax.broadcasted_iota(jnp.int32, (1, LANES), 1)
    half_of = [lane < HALF, lane >= HALF]
    r_i = lax.broadcasted_iota(jnp.int32, (C, C), 0)
    c_i = lax.broadcasted_iota(jnp.int32, (C, C), 1)
    lower = c_i <= r_i
    upper = c_i >= r_i
    tri = [jnp.where(lower, 1.0, 0.0).astype(BF16), jnp.where(upper, 1.0, 0.0).astype(BF16)]

    gnorm = gn_ref[...]
    qscale = GLA_DK ** -0.5

    def one_sequence(s):
        base = s * n_chunks * C

        def chunk_rows(n):
            return pl.ds(pl.multiple_of(base + n * C, C), C)

        for d in range(2):
            for p in range(n_pairs):
                if has_s0:
                    s0 = s0_refs[d][s, 0, 2 * p:2 * p + 2, :, :].reshape(2 * GLA_DK, GLA_DV)
                    st_ref[s, d, p] = s0.T
                else:
                    st_ref[s, d, p] = jnp.zeros((GLA_DV, 2 * GLA_DK), F32)

        def scan_step(i, carry):
            for d, n in ((0, i), (1, n_chunks - 1 - i)):
                rows = chunk_rows(n)
                la = la_ref[rows, d * GLA_QK:(d + 1) * GLA_QK]
                cum = sum(_dot(tri[d], part) for part in _split2(la))
                cum_ref[d, rows, :] = cum
                tot = cum[C - 1:C, :] if d == 0 else cum[0:1, :]
                k_in = gk_ref[rows, :] * jnp.exp(tot - cum)
                decay = jnp.exp(tot)
                for p in range(n_pairs):
                    k2 = k_in[:, p * LANES:(p + 1) * LANES]
                    kv_t = None
                    for e in range(2):
                        h = 2 * p + e
                        v_t = gv_ref[rows, h * GLA_DV:(h + 1) * GLA_DV].T
                        part = _dot(v_t, jnp.where(half_of[e], k2, 0.0).astype(BF16))
                        kv_t = part if kv_t is None else kv_t + part
                    st = st_ref[s, d, p]
                    sent_ref[s, d, n, p] = st.astype(BF16)
                    st_ref[s, d, p] = decay[:, p * LANES:(p + 1) * LANES] * st + kv_t
            return carry

        lax.fori_loop(0, n_chunks, scan_step, 0, unroll=GLA_UNROLL)

        def out_step(n, carry):
            rows = chunk_rows(n)
            q = gq_ref[rows, :] * qscale
            k = gk_ref[rows, :]
            qs, ks, qin = [], [], []
            for d in range(2):
                cum = cum_ref[d, rows, :]
                ref = cum[C // 2:C // 2 + 1, :]
                qs.append(q * jnp.exp(cum - ref))
                ks.append((k * jnp.exp(ref - cum)).astype(BF16))
                qin.append(q * jnp.exp(cum))
            for p in range(n_pairs):
                sl = slice(p * LANES, (p + 1) * LANES)
                for e in range(2):
                    h = 2 * p + e
                    hs = slice(h * GLA_DV, (h + 1) * GLA_DV)
                    m = half_of[e]
                    s_f = _dot_t(jnp.where(m, qs[0][:, sl], 0.0).astype(BF16), ks[0][:, sl])
                    s_b = _dot_t(jnp.where(m, qs[1][:, sl], 0.0).astype(BF16), ks[1][:, sl])
                    prob = jnp.where(lower, s_f, 0.0) + jnp.where(upper, s_b, 0.0)
                    o = _dot(prob.astype(BF16), gv_ref[rows, hs])
                    for d in range(2):
                        o = o + _dot_t(jnp.where(m, qin[d][:, sl], 0.0).astype(BF16),
                                       sent_ref[s, d, n, p])
                    o = o * lax.rsqrt(jnp.mean(o * o, axis=-1, keepdims=True) + EPS) * gnorm
                    o_ref[rows, hs] = (o * _silu(gg_ref[rows, hs])).astype(o_ref.dtype)
            return carry

        lax.fori_loop(0, n_chunks, out_step, 0, unroll=GLA_UNROLL)

        if emit_state:
            for d in range(2):
                for p in range(n_pairs):
                    sfin_refs[d][s, 0, 2 * p:2 * p + 2, :, :] = (
                        st_ref[s, d, p].T.reshape(2, GLA_DK, GLA_DV))

    for s in range(st_ref.shape[0]):
        one_sequence(s)


def _gla(gq, gk, gv, gg, la, gnorm, batch, seq_len, seqs_per_step, s0=None, emit_state=False):
    n_chunks = seq_len // GLA_CHUNK
    has_s0 = s0 is not None
    rows = seqs_per_step * seq_len

    def seq(width):
        return pl.BlockSpec((rows, width), lambda b: (b, 0))
    state_spec = pl.BlockSpec((seqs_per_step, 1, GLA_HEADS, GLA_DK, GLA_DV),
                              lambda b: (b, 0, 0, 0, 0))
    in_specs = [seq(GLA_QK), seq(GLA_QK), seq(GLA_V), seq(GLA_V), seq(2 * GLA_QK),
                pl.BlockSpec((1, GLA_DV), lambda b: (0, 0))]
    args = [gq, gk, gv, gg, la, gnorm]
    if has_s0:
        in_specs += [state_spec, state_spec]
        args += list(s0)
    out_shape = [jax.ShapeDtypeStruct((batch * seq_len, GLA_V), BF16)]
    out_specs = [seq(GLA_V)]
    if emit_state:
        out_shape += [jax.ShapeDtypeStruct((batch, 1, GLA_HEADS, GLA_DK, GLA_DV), F32)] * 2
        out_specs += [state_spec, state_spec]
    n_pairs = GLA_QK // LANES
    return pl.pallas_call(
        functools.partial(_gla_kernel, n_chunks=n_chunks, has_s0=has_s0, emit_state=emit_state),
        out_shape=out_shape,
        grid=(batch // seqs_per_step,),
        in_specs=in_specs,
        out_specs=out_specs,
        scratch_shapes=[pltpu.VMEM((2, rows, GLA_QK), F32),
                        pltpu.VMEM((seqs_per_step, 2, n_chunks, n_pairs, GLA_DV, LANES), BF16),
                        pltpu.VMEM((seqs_per_step, 2, n_pairs, GLA_DV, LANES), F32)],
        compiler_params=_params(1),
        name="gla",
    )(*args)


def kernel(x_prompt, x_sample, cache_k, cache_v, state_gla_fwd, state_gla_bwd, c, c_ctx,
           w_ada, b_ada, norm_pre, norm_post, ffn_w1, ffn_w3, ffn_w2, w_in,
           gla_w_up, gla_b_up, gla_norm, attn_sink, w_out):
    depth = w_in.shape[0]
    assert depth == 1, "single trunk layer"
    batch, seq = x_prompt.shape[0], x_prompt.shape[1]
    dec_batch, dec_seq = x_sample.shape[0], x_sample.shape[1]
    past = cache_k.shape[2]
    l = 0

    cond = jnp.concatenate(
        [c_ctx[None, :], c, jnp.zeros((MOD_ROWS - 1 - dec_batch, D_MODEL), F32)], axis=0)
    mod = _ada_modulation(cond, w_ada[l], b_ada[l][None, :])

    npre, npost = norm_pre[l], norm_post[l]
    w1 = ffn_w1[l].astype(BF16)
    w3 = ffn_w3[l].astype(BF16)
    w2 = ffn_w2[l].astype(BF16)
    w_in_b = w_in[l].astype(BF16)
    w_out_b = w_out[l].astype(BF16)
    zeros = jnp.zeros((GLA_LOW_RANK, GLA_QK), F32)
    w_up = jnp.concatenate(
        [jnp.concatenate([gla_w_up[l, 0], zeros], axis=1),
         jnp.concatenate([zeros, gla_w_up[l, 1]], axis=1)], axis=0).astype(BF16)
    b_up = gla_b_up[l].reshape(1, 2 * GLA_QK)
    gnorm = gla_norm[l][None, :]
    sink = attn_sink[l]

    def trunk(x, latent):
        n_batch, n_seq = (dec_batch, dec_seq) if latent else (batch, seq)
        latent_len = n_seq if latent else None
        x1 = _ffn_first(x, mod, latent_len, npre, npost, w1, w3, w2)
        q, k, v, gq, gk, gv, gg, la, *cache_t = _project(
            x1, mod, latent_len, None if latent else n_seq, npre, w_in_b, w_up, b_up)
        if latent:
            att = _latent_attention(sink, q, k, v, cache_k[:, l].reshape(dec_batch, past, ATT_KV),
                                    cache_v[:, l].reshape(dec_batch, past, ATT_KV), n_batch, n_seq)
            (gla,) = _gla(gq, gk, gv, gg, la, gnorm, n_batch, n_seq, 1,
                          s0=(state_gla_fwd[:, l:l + 1], state_gla_bwd[:, l:l + 1]))
            extras = ()
        else:
            att = _context_attention(sink, q, k, v, n_batch, n_seq)
            gla, s_f, s_b = _gla(gq, gk, gv, gg, la, gnorm, n_batch, n_seq, CTX_SEQS_PER_STEP,
                                 emit_state=True)
            k_new, v_new = (jnp.transpose(c_t, (0, 1, 4, 2, 3)) for c_t in cache_t)
            extras = (k_new, v_new, s_f, s_b)
        y = _mix_ffn(x1, att, gla, w_out_b, mod, latent_len, npre, npost, w1, w3, w2)
        return y.reshape(n_batch, n_seq, D_MODEL), extras

    y_prompt, (k_new, v_new, s_f, s_b) = trunk(x_prompt.reshape(batch * seq, D_MODEL), False)
    y_sample, _ = trunk(x_sample.reshape(dec_batch * dec_seq, D_MODEL), True)
    return (y_prompt, y_sample, k_new, v_new, s_f, s_b)
```

```python
import functools

import numpy as np
import jax
import jax.numpy as jnp
from jax import lax
from jax.experimental import pallas as pl
from jax.experimental.pallas import tpu as pltpu

F32 = jnp.float32
BF16 = jnp.bfloat16

D_MODEL = 1024
GRID_W = 64
N_Q_HEADS = 8
N_KV_HEADS = 2
HEAD_DIM = 64
WINDOW = 128
BLOCK = 128
ROPE_BASE = 10000.0
GLA_HEADS = 4
GLA_DK = 64
GLA_DV = 128
GLA_LOW_RANK = 16
GLA_TAU = 16.0
D_FF = 2816
N_MOD = 9
EPS = 1e-6
NEG = -1e30

ATT_Q = N_Q_HEADS * HEAD_DIM
ATT_KV = N_KV_HEADS * HEAD_DIM
GLA_QK = GLA_HEADS * GLA_DK
GLA_V = GLA_HEADS * GLA_DV
OFF_Q = 0
OFF_K = OFF_Q + ATT_Q
OFF_V = OFF_K + ATT_KV
OFF_GQ = OFF_V + ATT_KV
OFF_GK = OFF_GQ + GLA_QK
OFF_GV = OFF_GK + GLA_QK
OFF_GG = OFF_GV + GLA_V
OFF_LR = OFF_GG + GLA_V
IN_WIDTH = OFF_LR + 2 * GLA_LOW_RANK
LOG2_E = 1.4426950408889634
ATT_Q_SCALE = HEAD_DIM ** -0.5 * LOG2_E

LANES = 128
SUBLANES = 8
HALF = LANES // 2
VMEM_LIMIT = 56 * 1024 * 1024

TOKEN_TILE = 512
FFN_TILE = 1024
FFN_SUB_TILES = 2
FF_SPLITS = (0, 768, 1536, 2304, D_FF)
CTX_SEQS_PER_STEP = 2
LAT_BLOCKS_PER_STEP = 2
GLA_CHUNK = 128
GLA_UNROLL = 4
MOD_ROWS = 8


def _params(n_axes):
    return pltpu.CompilerParams(
        dimension_semantics=("arbitrary",) * n_axes, vmem_limit_bytes=VMEM_LIMIT)


def _resident(shape):
    zeros = (0,) * len(shape)
    return pl.BlockSpec(shape, lambda *_: zeros, pipeline_mode=pl.Buffered(1))


def _sigmoid(x):
    return 1.0 / (1.0 + jnp.exp(-x))


def _silu(x):
    return x * _sigmoid(x)


def _rms(x, g):
    return x * lax.rsqrt(jnp.mean(x * x, axis=-1, keepdims=True) + EPS) * g


def _dot(a, b):
    return jnp.dot(a, b, preferred_element_type=F32)


def _dot_t(a, b):
    return lax.dot_general(a, b, (((1,), (1,)), ((), ())), preferred_element_type=F32)


def _ada_kernel(cond_ref, w_ref, b_ref, o_ref):
    c = cond_ref[...]
    o_ref[0] = _dot(_silu(c).astype(BF16), w_ref[...].astype(BF16)) + b_ref[...]


def _ada_modulation(cond, w_ada, b_ada):
    return pl.pallas_call(
        _ada_kernel,
        out_shape=jax.ShapeDtypeStruct((N_MOD, MOD_ROWS, D_MODEL), F32),
        grid=(N_MOD,),
        in_specs=[
            pl.BlockSpec((MOD_ROWS, D_MODEL), lambda j: (0, 0)),
            pl.BlockSpec((D_MODEL, D_MODEL), lambda j: (0, j)),
            pl.BlockSpec((1, D_MODEL), lambda j: (0, j)),
        ],
        out_specs=pl.BlockSpec((1, MOD_ROWS, D_MODEL), lambda j: (j, 0, 0)),
        compiler_params=_params(1),
        name="ada_modulation",
    )(cond, w_ada, b_ada)


class _Mod:
    def __init__(self, mod_ref, tiles_per_seq):
        self.ref = mod_ref
        self.row = 0 if tiles_per_seq is None else 1 + pl.program_id(0) // tiles_per_seq

    def __getitem__(self, m):
        return self.ref[m, pl.ds(self.row, 1), :]


def _modulated(x, mod, npre_ref, i):
    return _rms(x, npre_ref[i:i + 1, :]) * (1.0 + mod[3 * i + 1]) + mod[3 * i]


def _residual(x, out, mod, npost_ref, i, weight):
    return x + (weight * mod[3 * i + 2]) * _rms(out, npost_ref[i:i + 1, :])


def _ffn_sublayer(x, i, mod, npre_ref, npost_ref, w1_ref, w3_ref, w2_ref):
    h = _modulated(x, mod, npre_ref, i).astype(BF16)
    acc = None
    for lo, hi in zip(FF_SPLITS[:-1], FF_SPLITS[1:]):
        a = _dot(h, w1_ref[:, lo:hi])
        g = _dot(h, w3_ref[:, lo:hi])
        part = _dot((_silu(a) * g).astype(BF16), w2_ref[lo:hi, :])
        acc = part if acc is None else acc + part
    return _residual(x, acc, mod, npost_ref, i, 0.5)


def _sub_tiles():
    rows = FFN_TILE // FFN_SUB_TILES
    return [slice(r * rows, (r + 1) * rows) for r in range(FFN_SUB_TILES)]


def _ffn_first_kernel(x_ref, mod_ref, npre_ref, npost_ref, w1_ref, w3_ref, w2_ref, o_ref,
                      *, tiles_per_seq):
    mod = _Mod(mod_ref, tiles_per_seq)
    for rows in _sub_tiles():
        o_ref[rows, :] = _ffn_sublayer(
            x_ref[rows, :], 0, mod, npre_ref, npost_ref, w1_ref, w3_ref, w2_ref)


def _mix_ffn_kernel(x_ref, att_ref, gla_ref, wo_ref, mod_ref, npre_ref, npost_ref,
                    w1_ref, w3_ref, w2_ref, o_ref, *, tiles_per_seq):
    mod = _Mod(mod_ref, tiles_per_seq)
    for rows in _sub_tiles():
        mix = (_dot(att_ref[rows, :], wo_ref[0:ATT_Q, :])
               + _dot(gla_ref[rows, :], wo_ref[ATT_Q:, :]))
        x = _residual(x_ref[rows, :], mix, mod, npost_ref, 1, 1.0)
        o_ref[rows, :] = _ffn_sublayer(x, 2, mod, npre_ref, npost_ref, w1_ref, w3_ref, w2_ref)


def _rope_tile(x, cos, sin_up, sin_dn):
    up = pltpu.roll(x, LANES - 16, axis=1)
    dn = pltpu.roll(x, 16, axis=1)
    return x * cos + up * sin_up + dn * sin_dn


def _proj_kernel(*refs, tiles_per_seq, cache_seq):
    x_ref, mod_ref, npre_ref, win_ref, wup_ref, bup_ref = refs[:6]
    rope_refs = refs[6:9] if tiles_per_seq is not None else None
    outs = refs[6 + (3 if rope_refs else 0):]
    q_ref, k_ref, v_ref, gq_ref, gk_ref, gv_ref, gg_ref, la_ref = outs[:8]
    h = _modulated(x_ref[...], _Mod(mod_ref, tiles_per_seq), npre_ref, 1).astype(BF16)
    q = _dot(h, win_ref[:, OFF_Q:OFF_K])
    kv = _dot(h, win_ref[:, OFF_K:OFF_GQ])
    k, v = kv[:, :ATT_KV], kv[:, ATT_KV:]
    if rope_refs:
        cos, sup, sdn = (r[...] for r in rope_refs)
        q = jnp.concatenate([_rope_tile(q[:, j * LANES:(j + 1) * LANES], cos, sup, sdn)
                             for j in range(ATT_Q // LANES)], axis=1)
        k_att = _rope_tile(k, cos, sup, sdn)
    else:
        k_att = k
    q_ref[...] = (q * ATT_Q_SCALE).astype(BF16)
    k_ref[:, :ATT_KV] = k_att.astype(BF16)
    k_ref[:, ATT_KV:] = pltpu.roll(k_att, HALF, axis=1).astype(BF16)
    v_ref[:, :ATT_KV] = v.astype(BF16)
    v_ref[:, ATT_KV:] = pltpu.roll(v, HALF, axis=1).astype(BF16)
    if cache_seq is not None:
        for t, dst in ((k.T, outs[8]), (v.T, outs[9])):
            for b in range(TOKEN_TILE // cache_seq):
                dst[b, 0] = t[:, b * cache_seq:(b + 1) * cache_seq].reshape(
                    N_KV_HEADS, HEAD_DIM, cache_seq)
    gqk = _dot(h, win_ref[:, OFF_GQ:OFF_GV])
    gq_ref[...] = gqk[:, :GLA_QK]
    gk_ref[...] = gqk[:, GLA_QK:]
    gv_ref[...] = _dot(h, win_ref[:, OFF_GV:OFF_GG]).astype(BF16)
    gg_ref[...] = _dot(h, win_ref[:, OFF_GG:OFF_LR])
    lr = _dot(h, win_ref[:, OFF_LR:IN_WIDTH])
    z = _dot(lr.astype(BF16), wup_ref[...]) + bup_ref[...]
    log_sig = jnp.minimum(z, 0.0) - jnp.log1p(jnp.exp(-jnp.abs(z)))
    la_ref[...] = log_sig * (1.0 / GLA_TAU)


def _token_spec(width, tile=TOKEN_TILE):
    return pl.BlockSpec((tile, width), lambda i: (i, 0))


def _ffn_weight_spec(shape, j):
    return pl.BlockSpec((None,) + tuple(shape[1:]), lambda i: (j, 0, 0),
                        pipeline_mode=pl.Buffered(1))


def _tiles_per_seq(latent_len, tile=TOKEN_TILE):
    return None if latent_len is None else latent_len // tile


def _ffn_first(x, mod, latent_len, npre, npost, w1, w3, w2):
    t = x.shape[0]
    return pl.pallas_call(
        functools.partial(_ffn_first_kernel, tiles_per_seq=_tiles_per_seq(latent_len, FFN_TILE)),
        out_shape=jax.ShapeDtypeStruct((t, D_MODEL), F32),
        grid=(t // FFN_TILE,),
        in_specs=[_token_spec(D_MODEL, FFN_TILE), _resident(mod.shape), _resident(npre.shape),
                  _resident(npost.shape), _ffn_weight_spec(w1.shape, 0),
                  _ffn_weight_spec(w3.shape, 0), _ffn_weight_spec(w2.shape, 0)],
        out_specs=_token_spec(D_MODEL, FFN_TILE),
        compiler_params=_params(1),
        name="ffn_first",
    )(x, mod, npre, npost, w1, w3, w2)


def _mix_ffn(x, att, gla, w_out, mod, latent_len, npre, npost, w1, w3, w2):
    t = x.shape[0]
    return pl.pallas_call(
        functools.partial(_mix_ffn_kernel, tiles_per_seq=_tiles_per_seq(latent_len, FFN_TILE)),
        out_shape=jax.ShapeDtypeStruct((t, D_MODEL), F32),
        grid=(t // FFN_TILE,),
        in_specs=[_token_spec(D_MODEL, FFN_TILE), _token_spec(ATT_Q, FFN_TILE),
                  _token_spec(GLA_V, FFN_TILE),
                  _resident(w_out.shape), _resident(mod.shape), _resident(npre.shape),
                  _resident(npost.shape), _ffn_weight_spec(w1.shape, 1),
                  _ffn_weight_spec(w3.shape, 1), _ffn_weight_spec(w2.shape, 1)],
        out_specs=_token_spec(D_MODEL, FFN_TILE),
        compiler_params=_params(1),
        name="mix_ffn",
    )(x, att, gla, w_out, mod, npre, npost, w1, w3, w2)


def _project(x, mod, latent_len, cache_seq, npre, w_in, w_up, b_up):
    t = x.shape[0]
    tiles = _tiles_per_seq(latent_len)
    in_specs = [_token_spec(D_MODEL), _resident(mod.shape), _resident(npre.shape),
                _resident(w_in.shape), _resident(w_up.shape), _resident(b_up.shape)]
    args = [x, mod, npre, w_in, w_up, b_up]
    if tiles is not None:
        in_specs += [pl.BlockSpec((TOKEN_TILE, LANES), lambda i: (i % tiles, 0))] * 3
        args += [jnp.asarray(tab) for tab in _rope_tables(latent_len)]
    outs = ((ATT_Q, BF16), (2 * ATT_KV, BF16), (2 * ATT_KV, BF16), (GLA_QK, F32), (GLA_QK, F32),
            (GLA_V, BF16), (GLA_V, F32), (2 * GLA_QK, F32))
    out_shape = [jax.ShapeDtypeStruct((t, w), dt) for w, dt in outs]
    out_specs = [_token_spec(w) for w, _ in outs]
    if cache_seq is not None:
        seqs = TOKEN_TILE // cache_seq
        cache = (t // cache_seq, 1, N_KV_HEADS, HEAD_DIM, cache_seq)
        out_shape += [jax.ShapeDtypeStruct(cache, F32)] * 2
        out_specs += [pl.BlockSpec((seqs,) + cache[1:], lambda i: (i, 0, 0, 0, 0))] * 2
    return pl.pallas_call(
        functools.partial(_proj_kernel, tiles_per_seq=tiles, cache_seq=cache_seq),
        out_shape=out_shape,
        grid=(t // TOKEN_TILE,),
        in_specs=in_specs,
        out_specs=out_specs,
        compiler_params=_params(1),
        name="project",
    )(*args)


def _rope_tables(seq_len):
    half = HEAD_DIM // 2
    inv_freq = np.float32(ROPE_BASE) ** (-np.arange(0, half, 2, dtype=np.float32) / half)
    pos = np.arange(seq_len)
    row = (pos // GRID_W).astype(np.float32)
    col = (pos % GRID_W).astype(np.float32)
    within = np.arange(LANES) % HEAD_DIM
    idx = within % half
    freq = inv_freq[idx % (half // 2)].astype(np.float32)
    p = np.where((within // half == 0)[None, :], row[:, None], col[:, None])
    ang = (p * freq[None, :]).astype(np.float32)
    cos, sin = np.cos(ang).astype(np.float32), np.sin(ang).astype(np.float32)
    first = (idx < half // 2)[None, :]
    zero = np.float32(0.0)
    return cos, np.where(first, -sin, zero), np.where(first, zero, sin)


def _attend(q_ref, q_rows, nq, segs, sink_ref):
    lane = lax.broadcasted_iota(jnp.int32, (1, LANES), 1)
    half_of = [lane < HALF, lane >= HALF]
    rows = lax.broadcasted_iota(jnp.int32, (2 * nq, 1), 0)
    zero = jnp.zeros((), BF16)
    out = [None] * (ATT_Q // LANES)
    for g in range(N_KV_HEADS):
        tiles = (2 * g, 2 * g + 1)
        for e in range(2):
            qm = jnp.concatenate(
                [jnp.where(half_of[e], q_ref[q_rows, j * LANES:(j + 1) * LANES], zero)
                 for j in tiles], axis=0)
            sink = jnp.where(rows < nq, sink_ref[2 * tiles[0] + e],
                             sink_ref[2 * tiles[1] + e]) * LOG2_E
            scores = []
            for k, k_sw, _, _, m in segs:
                s = _dot_t(qm, k if e == g else k_sw)
                scores.append(s if m is None else jnp.where(m, s, NEG))
            mx = sink
            for s in scores:
                mx = jnp.maximum(mx, jnp.max(s, axis=-1, keepdims=True))
            probs = [jnp.exp2(s - mx) for s in scores]
            den = jnp.exp2(sink - mx)
            for p in probs:
                den = den + jnp.sum(p, axis=-1, keepdims=True)
            o = None
            for p, (_, _, v, v_sw, _) in zip(probs, segs):
                part = _dot(p.astype(BF16), jnp.where(half_of[e], v if e == g else v_sw, zero))
                o = part if o is None else o + part
            o = o * (1.0 / den)
            for r, j in enumerate(tiles):
                blk = o[r * nq:(r + 1) * nq, :]
                out[j] = blk if out[j] is None else out[j] + blk
    return out


def _context_attn_kernel(sink_ref, q_ref, k_ref, v_ref, o_ref, *, seq_len):
    for s in range(q_ref.shape[0] // seq_len):
        rows = slice(s * seq_len, (s + 1) * seq_len)
        segs = [(k_ref[rows, :ATT_KV], k_ref[rows, ATT_KV:],
                 v_ref[rows, :ATT_KV], v_ref[rows, ATT_KV:], None)]
        for j, t in enumerate(_attend(q_ref, rows, seq_len, segs, sink_ref)):
            o_ref[rows, j * LANES:(j + 1) * LANES] = t.astype(o_ref.dtype)


def _latent_attn_kernel(sink_ref, q_ref, k_ref, v_ref, kc_ref, vc_ref, o_ref, *, seq_len):
    k_ctx, v_ctx = kc_ref[0], vc_ref[0]
    ctx = (k_ctx.astype(BF16), pltpu.roll(k_ctx, HALF, axis=1).astype(BF16),
           v_ctx.astype(BF16), pltpu.roll(v_ctx, HALF, axis=1).astype(BF16), None)
    span = 3 * BLOCK
    per_step = q_ref.shape[0] // BLOCK
    for s in range(per_step):
        i = pl.program_id(1) * per_step + s
        start = pl.multiple_of(jnp.clip((i - 1) * BLOCK, 0, seq_len - span), BLOCK)
        kpos = start + lax.broadcasted_iota(jnp.int32, (1, span), 1)
        qpos = i * BLOCK + lax.broadcasted_iota(jnp.int32, (BLOCK, 1), 0)
        valid = jnp.abs(kpos - qpos) <= WINDOW
        valid2 = jnp.concatenate([valid, valid], axis=0)
        k_loc = k_ref[pl.ds(start, span), :]
        v_loc = v_ref[pl.ds(start, span), :]
        segs = [(k_loc[:, :ATT_KV], k_loc[:, ATT_KV:], v_loc[:, :ATT_KV], v_loc[:, ATT_KV:], valid2),
                ctx]
        rows = slice(s * BLOCK, (s + 1) * BLOCK)
        for j, t in enumerate(_attend(q_ref, rows, BLOCK, segs, sink_ref)):
            o_ref[rows, j * LANES:(j + 1) * LANES] = t.astype(o_ref.dtype)


def _smem_spec():
    return pl.BlockSpec(memory_space=pltpu.SMEM)


def _context_attention(sink, q, k, v, batch, seq_len):
    rows = CTX_SEQS_PER_STEP * seq_len

    def seq(width):
        return pl.BlockSpec((rows, width), lambda b: (b, 0))
    return pl.pallas_call(
        functools.partial(_context_attn_kernel, seq_len=seq_len),
        out_shape=jax.ShapeDtypeStruct(q.shape, BF16),
        grid=(batch // CTX_SEQS_PER_STEP,),
        in_specs=[_smem_spec(), seq(ATT_Q), seq(2 * ATT_KV), seq(2 * ATT_KV)],
        out_specs=seq(ATT_Q),
        compiler_params=_params(1),
        name="context_attention",
    )(sink, q, k, v)


def _latent_attention(sink, q, k, v, k_ctx, v_ctx, batch, seq_len):
    steps = seq_len // (LAT_BLOCKS_PER_STEP * BLOCK)
    rows = LAT_BLOCKS_PER_STEP * BLOCK
    past = k_ctx.shape[1]
    return pl.pallas_call(
        functools.partial(_latent_attn_kernel, seq_len=seq_len),
        out_shape=jax.ShapeDtypeStruct(q.shape, BF16),
        grid=(batch, steps),
        in_specs=[_smem_spec(),
                  pl.BlockSpec((rows, ATT_Q), lambda b, i: (b * steps + i, 0)),
                  pl.BlockSpec((seq_len, 2 * ATT_KV), lambda b, i: (b, 0)),
                  pl.BlockSpec((seq_len, 2 * ATT_KV), lambda b, i: (b, 0)),
                  pl.BlockSpec((1, past, ATT_KV), lambda b, i: (b, 0, 0)),
                  pl.BlockSpec((1, past, ATT_KV), lambda b, i: (b, 0, 0))],
        out_specs=pl.BlockSpec((rows, ATT_Q), lambda b, i: (b * steps + i, 0)),
        compiler_params=_params(2),
        name="latent_attention",
    )(sink, q, k, v, k_ctx, v_ctx)


def _split2(x):
    hi = x.astype(BF16)
    lo = (x - hi.astype(F32)).astype(BF16)
    return hi, lo


def _gla_kernel(*refs, n_chunks, has_s0, emit_state):
    refs = list(refs)
    gq_ref, gk_ref, gv_ref, gg_ref, la_ref, gn_ref = refs[:6]
    pos = 6
    s0_refs = refs[pos:pos + 2] if has_s0 else None
    pos += 2 if has_s0 else 0
    o_ref = refs[pos]
    pos += 1
    sfin_refs = refs[pos:pos + 2] if emit_state else None
    pos += 2 if emit_state else 0
    cum_ref, sent_ref, st_ref = refs[pos:]

    C = GLA_CHUNK
    n_pairs = GLA_QK // LANES
    lane = lax.broadcasted_iota(jnp.int32, (1, LANES), 1)
    half_of = [lane < HALF, lane >= HALF]
    r_i = lax.broadcasted_iota(jnp.int32, (C, C), 0)
    c_i = lax.broadcasted_iota(jnp.int32, (C, C), 1)
    lower = c_i <= r_i
    upper = c_i >= r_i
    tri = [jnp.where(lower, 1.0, 0.0).astype(BF16), jnp.where(upper, 1.0, 0.0).astype(BF16)]

    gnorm = gn_ref[...]
    qscale = GLA_DK ** -0.5

    def one_sequence(s):
        base = s * n_chunks * C

        def chunk_rows(n):
            return pl.ds(pl.multiple_of(base + n * C, C), C)

        for d in range(2):
            for p in range(n_pairs):
                if has_s0:
                    s0 = s0_refs[d][s, 0, 2 * p:2 * p + 2, :, :].reshape(2 * GLA_DK, GLA_DV)
                    st_ref[s, d, p] = s0.T
                else:
                    st_ref[s, d, p] = jnp.zeros((GLA_DV, 2 * GLA_DK), F32)

        def scan_step(i, carry):
            for d, n in ((0, i), (1, n_chunks - 1 - i)):
                rows = chunk_rows(n)
                la = la_ref[rows, d * GLA_QK:(d + 1) * GLA_QK]
                cum = sum(_dot(tri[d], part) for part in _split2(la))
                cum_ref[d, rows, :] = cum
                tot = cum[C - 1:C, :] if d == 0 else cum[0:1, :]
                k_in = gk_ref[rows, :] * jnp.exp(tot - cum)
                decay = jnp.exp(tot)
                for p in range(n_pairs):
                    k2 = k_in[:, p * LANES:(p + 1) * LANES]
                    kv_t = None
                    for e in range(2):
                        h = 2 * p + e
                        v_t = gv_ref[rows, h * GLA_DV:(h + 1) * GLA_DV].T
                        part = _dot(v_t, jnp.where(half_of[e], k2, 0.0).astype(BF16))
                        kv_t = part if kv_t is None else kv_t + part
                    st = st_ref[s, d, p]
                    sent_ref[s, d, n, p] = st.astype(BF16)
                    st_ref[s, d, p] = decay[:, p * LANES:(p + 1) * LANES] * st + kv_t
            return carry

        lax.fori_loop(0, n_chunks, scan_step, 0, unroll=GLA_UNROLL)

        def out_step(n, carry):
            rows = chunk_rows(n)
            q = gq_ref[rows, :] * qscale
            k = gk_ref[rows, :]
            qs, ks, qin = [], [], []
            for d in range(2):
                cum = cum_ref[d, rows, :]
                ref = cum[C // 2:C // 2 + 1, :]
                qs.append(q * jnp.exp(cum - ref))
                ks.append((k * jnp.exp(ref - cum)).astype(BF16))
                qin.append(q * jnp.exp(cum))
            for p in range(n_pairs):
                sl = slice(p * LANES, (p + 1) * LANES)
                for e in range(2):
                    h = 2 * p + e
                    hs = slice(h * GLA_DV, (h + 1) * GLA_DV)
                    m = half_of[e]
                    s_f = _dot_t(jnp.where(m, qs[0][:, sl], 0.0).astype(BF16), ks[0][:, sl])
                    s_b = _dot_t(jnp.where(m, qs[1][:, sl], 0.0).astype(BF16), ks[1][:, sl])
                    prob = jnp.where(lower, s_f, 0.0) + jnp.where(upper, s_b, 0.0)
                    o = _dot(prob.astype(BF16), gv_ref[rows, hs])
                    for d in range(2):
                        o = o + _dot_t(jnp.where(m, qin[d][:, sl], 0.0).astype(BF16),
                                       sent_ref[s, d, n, p])
                    o = o * lax.rsqrt(jnp.mean(o * o, axis=-1, keepdims=True) + EPS) * gnorm
                    o_ref[rows, hs] = (o * _silu(gg_ref[rows, hs])).astype(o_ref.dtype)
            return carry

        lax.fori_loop(0, n_chunks, out_step, 0, unroll=GLA_UNROLL)

        if emit_state:
            for d in range(2):
                for p in range(n_pairs):
                    sfin_refs[d][s, 0, 2 * p:2 * p + 2, :, :] = (
                        st_ref[s, d, p].T.reshape(2, GLA_DK, GLA_DV))

    for s in range(st_ref.shape[0]):
        one_sequence(s)


def _gla(gq, gk, gv, gg, la, gnorm, batch, seq_len, seqs_per_step, s0=None, emit_state=False):
    n_chunks = seq_len // GLA_CHUNK
    has_s0 = s0 is not None
    rows = seqs_per_step * seq_len

    def seq(width):
        return pl.BlockSpec((rows, width), lambda b: (b, 0))
    state_spec = pl.BlockSpec((seqs_per_step, 1, GLA_HEADS, GLA_DK, GLA_DV),
                              lambda b: (b, 0, 0, 0, 0))
    in_specs = [seq(GLA_QK), seq(GLA_QK), seq(GLA_V), seq(GLA_V), seq(2 * GLA_QK),
                pl.BlockSpec((1, GLA_DV), lambda b: (0, 0))]
    args = [gq, gk, gv, gg, la, gnorm]
    if has_s0:
        in_specs += [state_spec, state_spec]
        args += list(s0)
    out_shape = [jax.ShapeDtypeStruct((batch * seq_len, GLA_V), BF16)]
    out_specs = [seq(GLA_V)]
    if emit_state:
        out_shape += [jax.ShapeDtypeStruct((batch, 1, GLA_HEADS, GLA_DK, GLA_DV), F32)] * 2
        out_specs += [state_spec, state_spec]
    n_pairs = GLA_QK // LANES
    return pl.pallas_call(
        functools.partial(_gla_kernel, n_chunks=n_chunks, has_s0=has_s0, emit_state=emit_state),
        out_shape=out_shape,
        grid=(batch // seqs_per_step,),
        in_specs=in_specs,
        out_specs=out_specs,
        scratch_shapes=[pltpu.VMEM((2, rows, GLA_QK), F32),
                        pltpu.VMEM((seqs_per_step, 2, n_chunks, n_pairs, GLA_DV, LANES), BF16),
                        pltpu.VMEM((seqs_per_step, 2, n_pairs, GLA_DV, LANES), F32)],
        compiler_params=_params(1),
        name="gla",
    )(*args)


def kernel(x_prompt, x_sample, cache_k, cache_v, state_gla_fwd, state_gla_bwd, c, c_ctx,
           w_ada, b_ada, norm_pre, norm_post, ffn_w1, ffn_w3, ffn_w2, w_in,
           gla_w_up, gla_b_up, gla_norm, attn_sink, w_out):
    depth = w_in.shape[0]
    assert depth == 1, "single trunk layer"
    batch, seq = x_prompt.shape[0], x_prompt.shape[1]
    dec_batch, dec_seq = x_sample.shape[0], x_sample.shape[1]
    past = cache_k.shape[2]
    l = 0

    cond = jnp.concatenate(
        [c_ctx[None, :], c, jnp.zeros((MOD_ROWS - 1 - dec_batch, D_MODEL), F32)], axis=0)
    mod = _ada_modulation(cond, w_ada[l], b_ada[l][None, :])

    npre, npost = norm_pre[l], norm_post[l]
    w1 = ffn_w1[l].astype(BF16)
    w3 = ffn_w3[l].astype(BF16)
    w2 = ffn_w2[l].astype(BF16)
    w_in_b = w_in[l].astype(BF16)
    w_out_b = w_out[l].astype(BF16)
    zeros = jnp.zeros((GLA_LOW_RANK, GLA_QK), F32)
    w_up = jnp.concatenate(
        [jnp.concatenate([gla_w_up[l, 0], zeros], axis=1),
         jnp.concatenate([zeros, gla_w_up[l, 1]], axis=1)], axis=0).astype(BF16)
    b_up = gla_b_up[l].reshape(1, 2 * GLA_QK)
    gnorm = gla_norm[l][None, :]
    sink = attn_sink[l]

    def trunk(x, latent):
        n_batch, n_seq = (dec_batch, dec_seq) if latent else (batch, seq)
        latent_len = n_seq if latent else None
        x1 = _ffn_first(x, mod, latent_len, npre, npost, w1, w3, w2)
        q, k, v, gq, gk, gv, gg, la, *cache_t = _project(
            x1, mod, latent_len, None if latent else n_seq, npre, w_in_b, w_up, b_up)
        if latent:
            att = _latent_attention(sink, q, k, v, cache_k[:, l].reshape(dec_batch, past, ATT_KV),
                                    cache_v[:, l].reshape(dec_batch, past, ATT_KV), n_batch, n_seq)
            (gla,) = _gla(gq, gk, gv, gg, la, gnorm, n_batch, n_seq, 1,
                          s0=(state_gla_fwd[:, l:l + 1], state_gla_bwd[:, l:l + 1]))
            extras = ()
        else:
            att = _context_attention(sink, q, k, v, n_batch, n_seq)
            gla, s_f, s_b = _gla(gq, gk, gv, gg, la, gnorm, n_batch, n_seq, CTX_SEQS_PER_STEP,
                                 emit_state=True)
            k_new, v_new = (jnp.transpose(c_t, (0, 1, 4, 2, 3)) for c_t in cache_t)
            extras = (k_new, v_new, s_f, s_b)
        y = _mix_ffn(x1, att, gla, w_out_b, mod, latent_len, npre, npost, w1, w3, w2)
        return y.reshape(n_batch, n_seq, D_MODEL), extras

    y_prompt, (k_new, v_new, s_f, s_b) = trunk(x_prompt.reshape(batch * seq, D_MODEL), False)
    y_sample, _ = trunk(x_sample.reshape(dec_batch * dec_seq, D_MODEL), True)
    return (y_prompt, y_sample, k_new, v_new, s_f, s_b)
```

```python
import functools

import numpy as np
import jax
import jax.numpy as jnp
from jax import lax
from jax.experimental import pallas as pl
from jax.experimental.pallas import tpu as pltpu

F32 = jnp.float32
BF16 = jnp.bfloat16

D_MODEL = 1024
GRID_W = 64
N_Q_HEADS = 8
N_KV_HEADS = 2
HEAD_DIM = 64
WINDOW = 128
BLOCK = 128
ROPE_BASE = 10000.0
GLA_HEADS = 4
GLA_DK = 64
GLA_DV = 128
GLA_LOW_RANK = 16
GLA_TAU = 16.0
D_FF = 2816
N_MOD = 9
EPS = 1e-6
NEG = -1e30

ATT_Q = N_Q_HEADS * HEAD_DIM
ATT_KV = N_KV_HEADS * HEAD_DIM
GLA_QK = GLA_HEADS * GLA_DK
GLA_V = GLA_HEADS * GLA_DV
OFF_Q = 0
OFF_K = OFF_Q + ATT_Q
OFF_V = OFF_K + ATT_KV
OFF_GQ = OFF_V + ATT_KV
OFF_GK = OFF_GQ + GLA_QK
OFF_GV = OFF_GK + GLA_QK
OFF_GG = OFF_GV + GLA_V
OFF_LR = OFF_GG + GLA_V
IN_WIDTH = OFF_LR + 2 * GLA_LOW_RANK
LOG2_E = 1.4426950408889634
ATT_Q_SCALE = HEAD_DIM ** -0.5 * LOG2_E

LANES = 128
SUBLANES = 8
HALF = LANES // 2
VMEM_LIMIT = 56 * 1024 * 1024

TOKEN_TILE = 512
FFN_TILE = 512
FFN_SUB_TILES = 1
FF_SPLITS = (0, 1536, D_FF)
CTX_SEQS_PER_STEP = 2
LAT_BLOCKS_PER_STEP = 2
GLA_CHUNK = 128
GLA_UNROLL = 4
MOD_ROWS = 8


def _params(n_axes):
    return pltpu.CompilerParams(
        dimension_semantics=("arbitrary",) * n_axes, vmem_limit_bytes=VMEM_LIMIT)


def _resident(shape):
    zeros = (0,) * len(shape)
    return pl.BlockSpec(shape, lambda *_: zeros, pipeline_mode=pl.Buffered(1))


def _sigmoid(x):
    return 1.0 / (1.0 + jnp.exp(-x))


def _silu(x):
    return x * _sigmoid(x)


def _rms(x, g):
    return x * lax.rsqrt(jnp.mean(x * x, axis=-1, keepdims=True) + EPS) * g


def _dot(a, b):
    return jnp.dot(a, b, preferred_element_type=F32)


def _dot_t(a, b):
    return lax.dot_general(a, b, (((1,), (1,)), ((), ())), preferred_element_type=F32)


def _ada_kernel(cond_ref, w_ref, b_ref, o_ref):
    c = cond_ref[...]
    o_ref[0] = _dot(_silu(c).astype(BF16), w_ref[...].astype(BF16)) + b_ref[...]


def _ada_modulation(cond, w_ada, b_ada):
    return pl.pallas_call(
        _ada_kernel,
        out_shape=jax.ShapeDtypeStruct((N_MOD, MOD_ROWS, D_MODEL), F32),
        grid=(N_MOD,),
        in_specs=[
            pl.BlockSpec((MOD_ROWS, D_MODEL), lambda j: (0, 0)),
            pl.BlockSpec((D_MODEL, D_MODEL), lambda j: (0, j)),
            pl.BlockSpec((1, D_MODEL), lambda j: (0, j)),
        ],
        out_specs=pl.BlockSpec((1, MOD_ROWS, D_MODEL), lambda j: (j, 0, 0)),
        compiler_params=_params(1),
        name="ada_modulation",
    )(cond, w_ada, b_ada)


class _Mod:
    def __init__(self, mod_ref, tiles_per_seq):
        self.ref = mod_ref
        self.row = 0 if tiles_per_seq is None else 1 + pl.program_id(0) // tiles_per_seq

    def __getitem__(self, m):
        return self.ref[m, pl.ds(self.row, 1), :]


def _modulated(x, mod, npre_ref, i):
    return _rms(x, npre_ref[i:i + 1, :]) * (1.0 + mod[3 * i + 1]) + mod[3 * i]


def _residual(x, out, mod, npost_ref, i, weight):
    return x + (weight * mod[3 * i + 2]) * _rms(out, npost_ref[i:i + 1, :])


def _ffn_sublayer(x, i, mod, npre_ref, npost_ref, w1_ref, w3_ref, w2_ref):
    h = _modulated(x, mod, npre_ref, i).astype(BF16)
    acc = None
    for lo, hi in zip(FF_SPLITS[:-1], FF_SPLITS[1:]):
        a = _dot(h, w1_ref[:, lo:hi])
        g = _dot(h, w3_ref[:, lo:hi])
        part = _dot((_silu(a) * g).astype(BF16), w2_ref[lo:hi, :])
        acc = part if acc is None else acc + part
    return _residual(x, acc, mod, npost_ref, i, 0.5)


def _sub_tiles():
    rows = FFN_TILE // FFN_SUB_TILES
    return [slice(r * rows, (r + 1) * rows) for r in range(FFN_SUB_TILES)]


def _ffn_first_kernel(*refs, tiles_per_seq, n_cast):
    x_ref, mod_ref, npre_ref, npost_ref, w1_ref, w3_ref, w2_ref = refs[:7]
    cast_src = refs[7:7 + n_cast]
    o_ref = refs[7 + n_cast]
    cast_dst = refs[8 + n_cast:]
    mod = _Mod(mod_ref, tiles_per_seq)
    for rows in _sub_tiles():
        o_ref[rows, :] = _ffn_sublayer(
            x_ref[rows, :], 0, mod, npre_ref, npost_ref, w1_ref, w3_ref, w2_ref)
    for src, dst in zip(cast_src, cast_dst):
        dst[...] = src[...].astype(BF16)


def _mix_ffn_kernel(x_ref, att_ref, gla_ref, wo_ref, mod_ref, npre_ref, npost_ref,
                    w1_ref, w3_ref, w2_ref, o_ref, *, tiles_per_seq):
    mod = _Mod(mod_ref, tiles_per_seq)
    for rows in _sub_tiles():
        mix = (_dot(att_ref[rows, :], wo_ref[0:ATT_Q, :])
               + _dot(gla_ref[rows, :], wo_ref[ATT_Q:, :]))
        x = _residual(x_ref[rows, :], mix, mod, npost_ref, 1, 1.0)
        o_ref[rows, :] = _ffn_sublayer(x, 2, mod, npre_ref, npost_ref, w1_ref, w3_ref, w2_ref)


def _rope_tile(x, cos, sin_up, sin_dn):
    up = pltpu.roll(x, LANES - 16, axis=1)
    dn = pltpu.roll(x, 16, axis=1)
    return x * cos + up * sin_up + dn * sin_dn


def _proj_kernel(*refs, tiles_per_seq, cache_seq):
    x_ref, mod_ref, npre_ref, win_ref, wup_ref, bup_ref = refs[:6]
    rope_refs = refs[6:9] if tiles_per_seq is not None else None
    outs = refs[6 + (3 if rope_refs else 0):]
    q_ref, k_ref, v_ref, gq_ref, gk_ref, gv_ref, gg_ref, la_ref = outs[:8]
    h = _modulated(x_ref[...], _Mod(mod_ref, tiles_per_seq), npre_ref, 1).astype(BF16)
    q = _dot(h, win_ref[:, OFF_Q:OFF_K])
    kv = _dot(h, win_ref[:, OFF_K:OFF_GQ])
    k, v = kv[:, :ATT_KV], kv[:, ATT_KV:]
    if rope_refs:
        cos, sup, sdn = (r[...] for r in rope_refs)
        q = jnp.concatenate([_rope_tile(q[:, j * LANES:(j + 1) * LANES], cos, sup, sdn)
                             for j in range(ATT_Q // LANES)], axis=1)
        k_att = _rope_tile(k, cos, sup, sdn)
    else:
        k_att = k
    q_ref[...] = (q * ATT_Q_SCALE).astype(BF16)
    k_ref[:, :ATT_KV] = k_att.astype(BF16)
    k_ref[:, ATT_KV:] = pltpu.roll(k_att, HALF, axis=1).astype(BF16)
    v_ref[:, :ATT_KV] = v.astype(BF16)
    v_ref[:, ATT_KV:] = pltpu.roll(v, HALF, axis=1).astype(BF16)
    if cache_seq is not None:
        for t, dst in ((k.T, outs[8]), (v.T, outs[9])):
            for b in range(TOKEN_TILE // cache_seq):
                dst[b, 0] = t[:, b * cache_seq:(b + 1) * cache_seq].reshape(
                    N_KV_HEADS, HEAD_DIM, cache_seq)
    gqk = _dot(h, win_ref[:, OFF_GQ:OFF_GV])
    gq_ref[...] = gqk[:, :GLA_QK]
    gk_ref[...] = gqk[:, GLA_QK:]
    gv_ref[...] = _dot(h, win_ref[:, OFF_GV:OFF_GG]).astype(BF16)
    gg_ref[...] = _dot(h, win_ref[:, OFF_GG:OFF_LR])
    lr = _dot(h, win_ref[:, OFF_LR:IN_WIDTH])
    z = _dot(lr.astype(BF16), wup_ref[...]) + bup_ref[...]
    log_sig = jnp.minimum(z, 0.0) - jnp.log1p(jnp.exp(-jnp.abs(z)))
    la_ref[...] = log_sig * (1.0 / GLA_TAU)


def _token_spec(width, tile=TOKEN_TILE):
    return pl.BlockSpec((tile, width), lambda i: (i, 0))


def _tiles_per_seq(latent_len, tile=TOKEN_TILE):
    return None if latent_len is None else latent_len // tile


def _ffn_first(x, mod, latent_len, npre, npost, w1, w3, w2, cast=()):
    t = x.shape[0]
    steps = t // FFN_TILE
    in_specs = [_token_spec(D_MODEL, FFN_TILE), _resident(mod.shape), _resident(npre.shape),
                _resident(npost.shape), _resident(w1.shape), _resident(w3.shape),
                _resident(w2.shape)]
    out_shape = [jax.ShapeDtypeStruct((t, D_MODEL), F32)]
    out_specs = [_token_spec(D_MODEL, FFN_TILE)]
    for stack, j in cast:
        _, rows, cols = stack.shape
        slab = rows // steps
        assert slab * steps == rows and slab % 16 == 0, "row slabs must be whole bf16 tiles"
        in_specs.append(pl.BlockSpec((None, slab, cols), functools.partial(
            lambda i, j: (j, i, 0), j=j)))
        out_shape.append(jax.ShapeDtypeStruct((rows, cols), BF16))
        out_specs.append(pl.BlockSpec((slab, cols), lambda i: (i, 0)))
    return pl.pallas_call(
        functools.partial(_ffn_first_kernel, tiles_per_seq=_tiles_per_seq(latent_len, FFN_TILE),
                          n_cast=len(cast)),
        out_shape=out_shape,
        grid=(steps,),
        in_specs=in_specs,
        out_specs=out_specs,
        compiler_params=_params(1),
        name="ffn_first",
    )(x, mod, npre, npost, w1, w3, w2, *[stack for stack, _ in cast])


def _mix_ffn(x, att, gla, w_out, mod, latent_len, npre, npost, w1, w3, w2):
    t = x.shape[0]
    return pl.pallas_call(
        functools.partial(_mix_ffn_kernel, tiles_per_seq=_tiles_per_seq(latent_len, FFN_TILE)),
        out_shape=jax.ShapeDtypeStruct((t, D_MODEL), F32),
        grid=(t // FFN_TILE,),
        in_specs=[_token_spec(D_MODEL, FFN_TILE), _token_spec(ATT_Q, FFN_TILE),
                  _token_spec(GLA_V, FFN_TILE),
                  _resident(w_out.shape), _resident(mod.shape), _resident(npre.shape),
                  _resident(npost.shape), _resident(w1.shape), _resident(w3.shape),
                  _resident(w2.shape)],
        out_specs=_token_spec(D_MODEL, FFN_TILE),
        compiler_params=_params(1),
        name="mix_ffn",
    )(x, att, gla, w_out, mod, npre, npost, w1, w3, w2)


def _project(x, mod, latent_len, cache_seq, npre, w_in, w_up, b_up):
    t = x.shape[0]
    tiles = _tiles_per_seq(latent_len)
    in_specs = [_token_spec(D_MODEL), _resident(mod.shape), _resident(npre.shape),
                _resident(w_in.shape), _resident(w_up.shape), _resident(b_up.shape)]
    args = [x, mod, npre, w_in, w_up, b_up]
    if tiles is not None:
        in_specs += [pl.BlockSpec((TOKEN_TILE, LANES), lambda i: (i % tiles, 0))] * 3
        args += [jnp.asarray(tab) for tab in _rope_tables(latent_len)]
    outs = ((ATT_Q, BF16), (2 * ATT_KV, BF16), (2 * ATT_KV, BF16), (GLA_QK, F32), (GLA_QK, F32),
            (GLA_V, BF16), (GLA_V, F32), (2 * GLA_QK, F32))
    out_shape = [jax.ShapeDtypeStruct((t, w), dt) for w, dt in outs]
    out_specs = [_token_spec(w) for w, _ in outs]
    if cache_seq is not None:
        seqs = TOKEN_TILE // cache_seq
        cache = (t // cache_seq, 1, N_KV_HEADS, HEAD_DIM, cache_seq)
        out_shape += [jax.ShapeDtypeStruct(cache, F32)] * 2
        out_specs += [pl.BlockSpec((seqs,) + cache[1:], lambda i: (i, 0, 0, 0, 0))] * 2
    return pl.pallas_call(
        functools.partial(_proj_kernel, tiles_per_seq=tiles, cache_seq=cache_seq),
        out_shape=out_shape,
        grid=(t // TOKEN_TILE,),
        in_specs=in_specs,
        out_specs=out_specs,
        compiler_params=_params(1),
        name="project",
    )(*args)


def _rope_tables(seq_len):
    half = HEAD_DIM // 2
    inv_freq = np.float32(ROPE_BASE) ** (-np.arange(0, half, 2, dtype=np.float32) / half)
    pos = np.arange(seq_len)
    row = (pos // GRID_W).astype(np.float32)
    col = (pos % GRID_W).astype(np.float32)
    within = np.arange(LANES) % HEAD_DIM
    idx = within % half
    freq = inv_freq[idx % (half // 2)].astype(np.float32)
    p = np.where((within // half == 0)[None, :], row[:, None], col[:, None])
    ang = (p * freq[None, :]).astype(np.float32)
    cos, sin = np.cos(ang).astype(np.float32), np.sin(ang).astype(np.float32)
    first = (idx < half // 2)[None, :]
    zero = np.float32(0.0)
    return cos, np.where(first, -sin, zero), np.where(first, zero, sin)


def _attend(q_ref, q_rows, nq, segs, sink_ref):
    lane = lax.broadcasted_iota(jnp.int32, (1, LANES), 1)
    half_of = [lane < HALF, lane >= HALF]
    rows = lax.broadcasted_iota(jnp.int32, (2 * nq, 1), 0)
    zero = jnp.zeros((), BF16)
    out = [None] * (ATT_Q // LANES)
    for g in range(N_KV_HEADS):
        tiles = (2 * g, 2 * g + 1)
        for e in range(2):
            qm = jnp.concatenate(
                [jnp.where(half_of[e], q_ref[q_rows, j * LANES:(j + 1) * LANES], zero)
                 for j in tiles], axis=0)
            sink = jnp.where(rows < nq, sink_ref[2 * tiles[0] + e],
                             sink_ref[2 * tiles[1] + e]) * LOG2_E
            scores = []
            for k, k_sw, _, _, m in segs:
                s = _dot_t(qm, k if e == g else k_sw)
                scores.append(s if m is None else jnp.where(m, s, NEG))
            mx = sink
            for s in scores:
                mx = jnp.maximum(mx, jnp.max(s, axis=-1, keepdims=True))
            probs = [jnp.exp2(s - mx) for s in scores]
            den = jnp.exp2(sink - mx)
            for p in probs:
                den = den + jnp.sum(p, axis=-1, keepdims=True)
            o = None
            for p, (_, _, v, v_sw, _) in zip(probs, segs):
                part = _dot(p.astype(BF16), jnp.where(half_of[e], v if e == g else v_sw, zero))
                o = part if o is None else o + part
            o = o * (1.0 / den)
            for r, j in enumerate(tiles):
                blk = o[r * nq:(r + 1) * nq, :]
                out[j] = blk if out[j] is None else out[j] + blk
    return out


def _context_attn_kernel(sink_ref, q_ref, k_ref, v_ref, o_ref, *, seq_len):
    for s in range(q_ref.shape[0] // seq_len):
        rows = slice(s * seq_len, (s + 1) * seq_len)
        segs = [(k_ref[rows, :ATT_KV], k_ref[rows, ATT_KV:],
                 v_ref[rows, :ATT_KV], v_ref[rows, ATT_KV:], None)]
        for j, t in enumerate(_attend(q_ref, rows, seq_len, segs, sink_ref)):
            o_ref[rows, j * LANES:(j + 1) * LANES] = t.astype(o_ref.dtype)


def _latent_attn_kernel(sink_ref, q_ref, k_ref, v_ref, kc_ref, vc_ref, o_ref, *, seq_len):
    k_ctx, v_ctx = kc_ref[0], vc_ref[0]
    ctx = (k_ctx.astype(BF16), pltpu.roll(k_ctx, HALF, axis=1).astype(BF16),
           v_ctx.astype(BF16), pltpu.roll(v_ctx, HALF, axis=1).astype(BF16), None)
    span = 3 * BLOCK
    per_step = q_ref.shape[0] // BLOCK
    for s in range(per_step):
        i = pl.program_id(1) * per_step + s
        start = pl.multiple_of(jnp.clip((i - 1) * BLOCK, 0, seq_len - span), BLOCK)
        kpos = start + lax.broadcasted_iota(jnp.int32, (1, span), 1)
        qpos = i * BLOCK + lax.broadcasted_iota(jnp.int32, (BLOCK, 1), 0)
        valid = jnp.abs(kpos - qpos) <= WINDOW
        valid2 = jnp.concatenate([valid, valid], axis=0)
        k_loc = k_ref[pl.ds(start, span), :]
        v_loc = v_ref[pl.ds(start, span), :]
        segs = [(k_loc[:, :ATT_KV], k_loc[:, ATT_KV:], v_loc[:, :ATT_KV], v_loc[:, ATT_KV:], valid2),
                ctx]
        rows = slice(s * BLOCK, (s + 1) * BLOCK)
        for j, t in enumerate(_attend(q_ref, rows, BLOCK, segs, sink_ref)):
            o_ref[rows, j * LANES:(j + 1) * LANES] = t.astype(o_ref.dtype)


def _smem_spec():
    return pl.BlockSpec(memory_space=pltpu.SMEM)


def _context_attention(sink, q, k, v, batch, seq_len):
    rows = CTX_SEQS_PER_STEP * seq_len

    def seq(width):
        return pl.BlockSpec((rows, width), lambda b: (b, 0))
    return pl.pallas_call(
        functools.partial(_context_attn_kernel, seq_len=seq_len),
        out_shape=jax.ShapeDtypeStruct(q.shape, BF16),
        grid=(batch // CTX_SEQS_PER_STEP,),
        in_specs=[_smem_spec(), seq(ATT_Q), seq(2 * ATT_KV), seq(2 * ATT_KV)],
        out_specs=seq(ATT_Q),
        compiler_params=_params(1),
        name="context_attention",
    )(sink, q, k, v)


def _latent_attention(sink, q, k, v, k_ctx, v_ctx, batch, seq_len):
    steps = seq_len // (LAT_BLOCKS_PER_STEP * BLOCK)
    rows = LAT_BLOCKS_PER_STEP * BLOCK
    past = k_ctx.shape[1]
    return pl.pallas_call(
        functools.partial(_latent_attn_kernel, seq_len=seq_len),
        out_shape=jax.ShapeDtypeStruct(q.shape, BF16),
        grid=(batch, steps),
        in_specs=[_smem_spec(),
                  pl.BlockSpec((rows, ATT_Q), lambda b, i: (b * steps + i, 0)),
                  pl.BlockSpec((seq_len, 2 * ATT_KV), lambda b, i: (b, 0)),
                  pl.BlockSpec((seq_len, 2 * ATT_KV), lambda b, i: (b, 0)),
                  pl.BlockSpec((1, past, ATT_KV), lambda b, i: (b, 0, 0)),
                  pl.BlockSpec((1, past, ATT_KV), lambda b, i: (b, 0, 0))],
        out_specs=pl.BlockSpec((rows, ATT_Q), lambda b, i: (b * steps + i, 0)),
        compiler_params=_params(2),
        name="latent_attention",
    )(sink, q, k, v, k_ctx, v_ctx)


def _split2(x):
    hi = x.astype(BF16)
    lo = (x - hi.astype(F32)).astype(BF16)
    return hi, lo


def _gla_kernel(*refs, n_chunks, has_s0, emit_state):
    refs = list(refs)
    gq_ref, gk_ref, gv_ref, gg_ref, la_ref, gn_ref = refs[:6]
    pos = 6
    s0_refs = refs[pos:pos + 2] if has_s0 else None
    pos += 2 if has_s0 else 0
    o_ref = refs[pos]
    pos += 1
    sfin_refs = refs[pos:pos + 2] if emit_state else None
    pos += 2 if emit_state else 0
    cum_ref, sent_ref, st_ref = refs[pos:]

    C = GLA_CHUNK
    n_pairs = GLA_QK // LANES
    lane = lax.broadcasted_iota(jnp.int32, (1, LANES), 1)
    half_of = [lane < HALF, lane >= HALF]
    r_i = lax.broadcasted_iota(jnp.int32, (C, C), 0)
    c_i = lax.broadcasted_iota(jnp.int32, (C, C), 1)
    lower = c_i <= r_i
    upper = c_i >= r_i
    tri = [jnp.where(lower, 1.0, 0.0).astype(BF16), jnp.where(upper, 1.0, 0.0).astype(BF16)]

    gnorm = gn_ref[...]
    qscale = GLA_DK ** -0.5

    def one_sequence(s):
        base = s * n_chunks * C

        def chunk_rows(n):
            return pl.ds(pl.multiple_of(base + n * C, C), C)

        for d in range(2):
            for p in range(n_pairs):
                if has_s0:
                    s0 = s0_refs[d][s, 0, 2 * p:2 * p + 2, :, :].reshape(2 * GLA_DK, GLA_DV)
                    st_ref[s, d, p] = s0.T
                else:
                    st_ref[s, d, p] = jnp.zeros((GLA_DV, 2 * GLA_DK), F32)

        def scan_step(i, carry):
            for d, n in ((0, i), (1, n_chunks - 1 - i)):
                rows = chunk_rows(n)
                la = la_ref[rows, d * GLA_QK:(d + 1) * GLA_QK]
                cum = sum(_dot(tri[d], part) for part in _split2(la))
                cum_ref[d, rows, :] = cum
                tot = cum[C - 1:C, :] if d == 0 else cum[0:1, :]
                k_in = gk_ref[rows, :] * jnp.exp(tot - cum)
                decay = jnp.exp(tot)
                for p in range(n_pairs):
                    k2 = k_in[:, p * LANES:(p + 1) * LANES]
                    kv_t = None
                    for e in range(2):
                        h = 2 * p + e
                        v_t = gv_ref[rows, h * GLA_DV:(h + 1) * GLA_DV].T
                        part = _dot(v_t, jnp.where(half_of[e], k2, 0.0).astype(BF16))
                        kv_t = part if kv_t is None else kv_t + part
                    st = st_ref[s, d, p]
                    sent_ref[s, d, n, p] = st.astype(BF16)
                    st_ref[s, d, p] = decay[:, p * LANES:(p + 1) * LANES] * st + kv_t
            return carry

        lax.fori_loop(0, n_chunks, scan_step, 0, unroll=GLA_UNROLL)

        def out_step(n, carry):
            rows = chunk_rows(n)
            q = gq_ref[rows, :] * qscale
            k = gk_ref[rows, :]
            qs, ks, qin = [], [], []
            for d in range(2):
                cum = cum_ref[d, rows, :]
                ref = cum[C // 2:C // 2 + 1, :]
                qs.append(q * jnp.exp(cum - ref))
                ks.append((k * jnp.exp(ref - cum)).astype(BF16))
                qin.append(q * jnp.exp(cum))
            for p in range(n_pairs):
                sl = slice(p * LANES, (p + 1) * LANES)
                for e in range(2):
                    h = 2 * p + e
                    hs = slice(h * GLA_DV, (h + 1) * GLA_DV)
                    m = half_of[e]
                    s_f = _dot_t(jnp.where(m, qs[0][:, sl], 0.0).astype(BF16), ks[0][:, sl])
                    s_b = _dot_t(jnp.where(m, qs[1][:, sl], 0.0).astype(BF16), ks[1][:, sl])
                    prob = jnp.where(lower, s_f, 0.0) + jnp.where(upper, s_b, 0.0)
                    o = _dot(prob.astype(BF16), gv_ref[rows, hs])
                    for d in range(2):
                        o = o + _dot_t(jnp.where(m, qin[d][:, sl], 0.0).astype(BF16),
                                       sent_ref[s, d, n, p])
                    o = o * lax.rsqrt(jnp.mean(o * o, axis=-1, keepdims=True) + EPS) * gnorm
                    o_ref[rows, hs] = (o * _silu(gg_ref[rows, hs])).astype(o_ref.dtype)
            return carry

        lax.fori_loop(0, n_chunks, out_step, 0, unroll=GLA_UNROLL)

        if emit_state:
            for d in range(2):
                for p in range(n_pairs):
                    sfin_refs[d][s, 0, 2 * p:2 * p + 2, :, :] = (
                        st_ref[s, d, p].T.reshape(2, GLA_DK, GLA_DV))

    for s in range(st_ref.shape[0]):
        one_sequence(s)


def _gla(gq, gk, gv, gg, la, gnorm, batch, seq_len, seqs_per_step, s0=None, emit_state=False):
    n_chunks = seq_len // GLA_CHUNK
    has_s0 = s0 is not None
    rows = seqs_per_step * seq_len

    def seq(width):
        return pl.BlockSpec((rows, width), lambda b: (b, 0))
    state_spec = pl.BlockSpec((seqs_per_step, 1, GLA_HEADS, GLA_DK, GLA_DV),
                              lambda b: (b, 0, 0, 0, 0))
    in_specs = [seq(GLA_QK), seq(GLA_QK), seq(GLA_V), seq(GLA_V), seq(2 * GLA_QK),
                pl.BlockSpec((1, GLA_DV), lambda b: (0, 0))]
    args = [gq, gk, gv, gg, la, gnorm]
    if has_s0:
        in_specs += [state_spec, state_spec]
        args += list(s0)
    out_shape = [jax.ShapeDtypeStruct((batch * seq_len, GLA_V), BF16)]
    out_specs = [seq(GLA_V)]
    if emit_state:
        out_shape += [jax.ShapeDtypeStruct((batch, 1, GLA_HEADS, GLA_DK, GLA_DV), F32)] * 2
        out_specs += [state_spec, state_spec]
    n_pairs = GLA_QK // LANES
    return pl.pallas_call(
        functools.partial(_gla_kernel, n_chunks=n_chunks, has_s0=has_s0, emit_state=emit_state),
        out_shape=out_shape,
        grid=(batch // seqs_per_step,),
        in_specs=in_specs,
        out_specs=out_specs,
        scratch_shapes=[pltpu.VMEM((2, rows, GLA_QK), F32),
                        pltpu.VMEM((seqs_per_step, 2, n_chunks, n_pairs, GLA_DV, LANES), BF16),
                        pltpu.VMEM((seqs_per_step, 2, n_pairs, GLA_DV, LANES), F32)],
        compiler_params=_params(1),
        name="gla",
    )(*args)


def kernel(x_prompt, x_sample, cache_k, cache_v, state_gla_fwd, state_gla_bwd, c, c_ctx,
           w_ada, b_ada, norm_pre, norm_post, ffn_w1, ffn_w3, ffn_w2, w_in,
           gla_w_up, gla_b_up, gla_norm, attn_sink, w_out):
    depth = w_in.shape[0]
    assert depth == 1, "single trunk layer"
    batch, seq = x_prompt.shape[0], x_prompt.shape[1]
    dec_batch, dec_seq = x_sample.shape[0], x_sample.shape[1]
    past = cache_k.shape[2]
    l = 0

    cond = jnp.concatenate(
        [c_ctx[None, :], c, jnp.zeros((MOD_ROWS - 1 - dec_batch, D_MODEL), F32)], axis=0)
    mod = _ada_modulation(cond, w_ada[l], b_ada[l][None, :])

    npre, npost = norm_pre[l], norm_post[l]
    first = [w[l, 0].astype(BF16) for w in (ffn_w1, ffn_w3, ffn_w2)]
    late_f32 = [(ffn_w1[l], 1), (ffn_w3[l], 1), (ffn_w2[l], 1), (w_out, l)]
    late = []
    w_in_b = w_in[l].astype(BF16)
    zeros = jnp.zeros((GLA_LOW_RANK, GLA_QK), F32)
    w_up = jnp.concatenate(
        [jnp.concatenate([gla_w_up[l, 0], zeros], axis=1),
         jnp.concatenate([zeros, gla_w_up[l, 1]], axis=1)], axis=0).astype(BF16)
    b_up = gla_b_up[l].reshape(1, 2 * GLA_QK)
    gnorm = gla_norm[l][None, :]
    sink = attn_sink[l]

    def trunk(x, latent):
        n_batch, n_seq = (dec_batch, dec_seq) if latent else (batch, seq)
        latent_len = n_seq if latent else None
        x1, *cast = _ffn_first(x, mod, latent_len, npre, npost, *first,
                               cast=() if late else late_f32)
        late.extend(cast)
        w1_b, w3_b, w2_b, w_out_b = late
        q, k, v, gq, gk, gv, gg, la, *cache_t = _project(
            x1, mod, latent_len, None if latent else n_seq, npre, w_in_b, w_up, b_up)
        if latent:
            att = _latent_attention(sink, q, k, v, cache_k[:, l].reshape(dec_batch, past, ATT_KV),
                                    cache_v[:, l].reshape(dec_batch, past, ATT_KV), n_batch, n_seq)
            (gla,) = _gla(gq, gk, gv, gg, la, gnorm, n_batch, n_seq, 1,
                          s0=(state_gla_fwd[:, l:l + 1], state_gla_bwd[:, l:l + 1]))
            extras = ()
        else:
            att = _context_attention(sink, q, k, v, n_batch, n_seq)
            gla, s_f, s_b = _gla(gq, gk, gv, gg, la, gnorm, n_batch, n_seq, CTX_SEQS_PER_STEP,
                                 emit_state=True)
            k_new, v_new = (jnp.transpose(c_t, (0, 1, 4, 2, 3)) for c_t in cache_t)
            extras = (k_new, v_new, s_f, s_b)
        y = _mix_ffn(x1, att, gla, w_out_b, mod, latent_len, npre, npost, w1_b, w3_b, w2_b)
        return y.reshape(n_batch, n_seq, D_MODEL), extras

    y_prompt, (k_new, v_new, s_f, s_b) = trunk(x_prompt.reshape(batch * seq, D_MODEL), False)
    y_sample, _ = trunk(x_sample.reshape(dec_batch * dec_seq, D_MODEL), True)
    return (y_prompt, y_sample, k_new, v_new, s_f, s_b)
```

```python
import functools

import numpy as np
import jax
import jax.numpy as jnp
from jax import lax
from jax.experimental import pallas as pl
from jax.experimental.pallas import tpu as pltpu

F32 = jnp.float32
BF16 = jnp.bfloat16

D_MODEL = 1024
GRID_W = 64
N_Q_HEADS = 8
N_KV_HEADS = 2
HEAD_DIM = 64
WINDOW = 128
BLOCK = 128
ROPE_BASE = 10000.0
GLA_HEADS = 4
GLA_DK = 64
GLA_DV = 128
GLA_LOW_RANK = 16
GLA_TAU = 16.0
D_FF = 2816
N_MOD = 9
EPS = 1e-6
NEG = -1e30

ATT_Q = N_Q_HEADS * HEAD_DIM
ATT_KV = N_KV_HEADS * HEAD_DIM
GLA_QK = GLA_HEADS * GLA_DK
GLA_V = GLA_HEADS * GLA_DV
OFF_Q = 0
OFF_K = OFF_Q + ATT_Q
OFF_V = OFF_K + ATT_KV
OFF_GQ = OFF_V + ATT_KV
OFF_GK = OFF_GQ + GLA_QK
OFF_GV = OFF_GK + GLA_QK
OFF_GG = OFF_GV + GLA_V
OFF_LR = OFF_GG + GLA_V
IN_WIDTH = OFF_LR + 2 * GLA_LOW_RANK
A_Q = 0
A_K = A_Q + ATT_Q
A_KS = A_K + ATT_KV
A_V = A_KS + ATT_KV
A_VS = A_V + ATT_KV
ATT_IN_W = A_VS + ATT_KV
G_Q = 0
G_K = G_Q + GLA_QK
G_G = G_K + GLA_QK
G_LA = G_G + GLA_V
GLA_F_W = G_LA + 2 * GLA_QK
LOG2_E = 1.4426950408889634
ATT_Q_SCALE = HEAD_DIM ** -0.5 * LOG2_E

LANES = 128
SUBLANES = 8
HALF = LANES // 2
VMEM_LIMIT = 56 * 1024 * 1024

TOKEN_TILE = 512
FF_SPLITS = (0, 1536, D_FF)
ADA_K_TILE = 128
CTX_SEQS_PER_STEP = 2
LAT_BLOCKS_PER_STEP = 2
GLA_CHUNK = 128
GLA_UNROLL = 4
MOD_ROWS = 8


def _params(n_axes):
    return pltpu.CompilerParams(
        dimension_semantics=("arbitrary",) * n_axes, vmem_limit_bytes=VMEM_LIMIT)


def _resident(shape):
    zeros = (0,) * len(shape)
    return pl.BlockSpec(shape, lambda *_: zeros, pipeline_mode=pl.Buffered(1))


def _sigmoid(x):
    return 1.0 / (1.0 + jnp.exp(-x))


def _silu(x):
    return x * _sigmoid(x)


def _rms(x, g):
    return x * lax.rsqrt(jnp.mean(x * x, axis=-1, keepdims=True) + EPS) * g


def _dot(a, b):
    return jnp.dot(a, b, preferred_element_type=F32)


def _dot_t(a, b):
    return lax.dot_general(a, b, (((1,), (1,)), ((), ())), preferred_element_type=F32)


def _ada_kernel(cond_ref, w_ref, b_ref, o_ref):
    @pl.when(pl.program_id(0) == 0)
    def _():
        for m in range(N_MOD):
            o_ref[m] = jnp.broadcast_to(b_ref[:, m * D_MODEL:(m + 1) * D_MODEL],
                                        (MOD_ROWS, D_MODEL))
    part = _dot(_silu(cond_ref[...]).astype(BF16), w_ref[...].astype(BF16))
    for m in range(N_MOD):
        o_ref[m] += part[:, m * D_MODEL:(m + 1) * D_MODEL]


def _ada_modulation(cond, w_ada, b_ada):
    k_dim, n = w_ada.shape
    return pl.pallas_call(
        _ada_kernel,
        out_shape=jax.ShapeDtypeStruct((N_MOD, MOD_ROWS, D_MODEL), F32),
        grid=(k_dim // ADA_K_TILE,),
        in_specs=[
            pl.BlockSpec((MOD_ROWS, ADA_K_TILE), lambda k: (0, k)),
            pl.BlockSpec((ADA_K_TILE, n), lambda k: (k, 0)),
            pl.BlockSpec((1, n), lambda k: (0, 0)),
        ],
        out_specs=pl.BlockSpec((N_MOD, MOD_ROWS, D_MODEL), lambda k: (0, 0, 0)),
        compiler_params=_params(1),
        name="ada_modulation",
    )(cond, w_ada, b_ada)


class _Mod:
    def __init__(self, mod_ref, tiles_per_seq):
        self.ref = mod_ref
        self.row = 0 if tiles_per_seq is None else 1 + pl.program_id(0) // tiles_per_seq

    def __getitem__(self, m):
        return self.ref[m, pl.ds(self.row, 1), :]


def _modulated(x, mod, npre_ref, i):
    return _rms(x, npre_ref[i:i + 1, :]) * (1.0 + mod[3 * i + 1]) + mod[3 * i]


def _residual(x, out, mod, npost_ref, i, weight):
    return x + (weight * mod[3 * i + 2]) * _rms(out, npost_ref[i:i + 1, :])


def _ffn_sublayer(x, i, mod, npre_ref, npost_ref, w1_ref, w3_ref, w2_ref):
    h = _modulated(x, mod, npre_ref, i).astype(BF16)
    acc = None
    for lo, hi in zip(FF_SPLITS[:-1], FF_SPLITS[1:]):
        a = _dot(h, w1_ref[:, lo:hi])
        g = _dot(h, w3_ref[:, lo:hi])
        part = _dot((_silu(a) * g).astype(BF16), w2_ref[lo:hi, :])
        acc = part if acc is None else acc + part
    return _residual(x, acc, mod, npost_ref, i, 0.5)


def _ffn_first_kernel(x_ref, mod_ref, npre_ref, npost_ref, w1_ref, w3_ref, w2_ref, o_ref,
                      *, tiles_per_seq):
    mod = _Mod(mod_ref, tiles_per_seq)
    o_ref[...] = _ffn_sublayer(x_ref[...], 0, mod, npre_ref, npost_ref, w1_ref, w3_ref, w2_ref)


def _mix_ffn_kernel(x_ref, att_ref, gla_ref, wo_ref, mod_ref, npre_ref, npost_ref,
                    w1_ref, w3_ref, w2_ref, o_ref, *, tiles_per_seq):
    mod = _Mod(mod_ref, tiles_per_seq)
    mix = _dot(att_ref[...], wo_ref[0:ATT_Q, :]) + _dot(gla_ref[...], wo_ref[ATT_Q:, :])
    x = _residual(x_ref[...], mix, mod, npost_ref, 1, 1.0)
    o_ref[...] = _ffn_sublayer(x, 2, mod, npre_ref, npost_ref, w1_ref, w3_ref, w2_ref)


def _rope_tile(x, cos, sin_up, sin_dn):
    up = pltpu.roll(x, LANES - 16, axis=1)
    dn = pltpu.roll(x, 16, axis=1)
    return x * cos + up * sin_up + dn * sin_dn


def _proj_kernel(*refs, tiles_per_seq, cache_seq):
    x_ref, mod_ref, npre_ref, win_ref, wup_ref, bup_ref = refs[:6]
    rope_refs = refs[6:9] if tiles_per_seq is not None else None
    outs = refs[6 + (3 if rope_refs else 0):]
    att_ref, glaf_ref, gv_ref = outs[:3]
    h = _modulated(x_ref[...], _Mod(mod_ref, tiles_per_seq), npre_ref, 1).astype(BF16)
    q = _dot(h, win_ref[:, OFF_Q:OFF_K])
    kv = _dot(h, win_ref[:, OFF_K:OFF_GQ])
    k, v = kv[:, :ATT_KV], kv[:, ATT_KV:]
    if rope_refs:
        cos, sup, sdn = (r[...] for r in rope_refs)
        q = jnp.concatenate([_rope_tile(q[:, j * LANES:(j + 1) * LANES], cos, sup, sdn)
                             for j in range(ATT_Q // LANES)], axis=1)
        k_att = _rope_tile(k, cos, sup, sdn)
    else:
        k_att = k
    att_ref[:, A_Q:A_K] = (q * ATT_Q_SCALE).astype(BF16)
    att_ref[:, A_K:A_KS] = k_att.astype(BF16)
    att_ref[:, A_KS:A_V] = pltpu.roll(k_att, HALF, axis=1).astype(BF16)
    att_ref[:, A_V:A_VS] = v.astype(BF16)
    att_ref[:, A_VS:ATT_IN_W] = pltpu.roll(v, HALF, axis=1).astype(BF16)
    if cache_seq is not None:
        for t, dst in ((k.T, outs[3]), (v.T, outs[4])):
            for b in range(TOKEN_TILE // cache_seq):
                dst[b, 0] = t[:, b * cache_seq:(b + 1) * cache_seq].reshape(
                    N_KV_HEADS, HEAD_DIM, cache_seq)
    glaf_ref[:, G_Q:G_G] = _dot(h, win_ref[:, OFF_GQ:OFF_GV])
    gv_ref[...] = _dot(h, win_ref[:, OFF_GV:OFF_GG]).astype(BF16)
    glaf_ref[:, G_G:G_LA] = _dot(h, win_ref[:, OFF_GG:OFF_LR])
    lr = _dot(h, win_ref[:, OFF_LR:IN_WIDTH])
    z = _dot(lr.astype(BF16), wup_ref[...]) + bup_ref[...]
    log_sig = jnp.minimum(z, 0.0) - jnp.log1p(jnp.exp(-jnp.abs(z)))
    glaf_ref[:, G_LA:GLA_F_W] = log_sig * (1.0 / GLA_TAU)


def _token_spec(width):
    return pl.BlockSpec((TOKEN_TILE, width), lambda i: (i, 0))


def _tiles_per_seq(latent_len):
    return None if latent_len is None else latent_len // TOKEN_TILE


def _ffn_first(x, mod, latent_len, npre, npost, w1, w3, w2):
    t = x.shape[0]
    return pl.pallas_call(
        functools.partial(_ffn_first_kernel, tiles_per_seq=_tiles_per_seq(latent_len)),
        out_shape=jax.ShapeDtypeStruct((t, D_MODEL), F32),
        grid=(t // TOKEN_TILE,),
        in_specs=[_token_spec(D_MODEL), _resident(mod.shape), _resident(npre.shape),
                  _resident(npost.shape), _resident(w1.shape), _resident(w3.shape),
                  _resident(w2.shape)],
        out_specs=_token_spec(D_MODEL),
        compiler_params=_params(1),
        name="ffn_first",
    )(x, mod, npre, npost, w1, w3, w2)


def _mix_ffn(x, att, gla, w_out, mod, latent_len, npre, npost, w1, w3, w2):
    t = x.shape[0]
    return pl.pallas_call(
        functools.partial(_mix_ffn_kernel, tiles_per_seq=_tiles_per_seq(latent_len)),
        out_shape=jax.ShapeDtypeStruct((t, D_MODEL), F32),
        grid=(t // TOKEN_TILE,),
        in_specs=[_token_spec(D_MODEL), _token_spec(ATT_Q), _token_spec(GLA_V),
                  _resident(w_out.shape), _resident(mod.shape), _resident(npre.shape),
                  _resident(npost.shape), _resident(w1.shape), _resident(w3.shape),
                  _resident(w2.shape)],
        out_specs=_token_spec(D_MODEL),
        compiler_params=_params(1),
        name="mix_ffn",
    )(x, att, gla, w_out, mod, npre, npost, w1, w3, w2)


def _project(x, mod, latent_len, cache_seq, npre, w_in, w_up, b_up):
    t = x.shape[0]
    tiles = _tiles_per_seq(latent_len)
    in_specs = [_token_spec(D_MODEL), _resident(mod.shape), _resident(npre.shape),
                _resident(w_in.shape), _resident(w_up.shape), _resident(b_up.shape)]
    args = [x, mod, npre, w_in, w_up, b_up]
    if tiles is not None:
        in_specs += [pl.BlockSpec((TOKEN_TILE, LANES), lambda i: (i % tiles, 0))] * 3
        args += [jnp.asarray(tab) for tab in _rope_tables(latent_len)]
    outs = ((ATT_IN_W, BF16), (GLA_F_W, F32), (GLA_V, BF16))
    out_shape = [jax.ShapeDtypeStruct((t, w), dt) for w, dt in outs]
    out_specs = [_token_spec(w) for w, _ in outs]
    if cache_seq is not None:
        seqs = TOKEN_TILE // cache_seq
        cache = (t // cache_seq, 1, N_KV_HEADS, HEAD_DIM, cache_seq)
        out_shape += [jax.ShapeDtypeStruct(cache, F32)] * 2
        out_specs += [pl.BlockSpec((seqs,) + cache[1:], lambda i: (i, 0, 0, 0, 0))] * 2
    return pl.pallas_call(
        functools.partial(_proj_kernel, tiles_per_seq=tiles, cache_seq=cache_seq),
        out_shape=out_shape,
        grid=(t // TOKEN_TILE,),
        in_specs=in_specs,
        out_specs=out_specs,
        compiler_params=_params(1),
        name="project",
    )(*args)


def _rope_tables(seq_len):
    half = HEAD_DIM // 2
    inv_freq = np.float32(ROPE_BASE) ** (-np.arange(0, half, 2, dtype=np.float32) / half)
    pos = np.arange(seq_len)
    row = (pos // GRID_W).astype(np.float32)
    col = (pos % GRID_W).astype(np.float32)
    within = np.arange(LANES) % HEAD_DIM
    idx = within % half
    freq = inv_freq[idx % (half // 2)].astype(np.float32)
    p = np.where((within // half == 0)[None, :], row[:, None], col[:, None])
    ang = (p * freq[None, :]).astype(np.float32)
    cos, sin = np.cos(ang).astype(np.float32), np.sin(ang).astype(np.float32)
    first = (idx < half // 2)[None, :]
    zero = np.float32(0.0)
    return cos, np.where(first, -sin, zero), np.where(first, zero, sin)


def _attend(q_ref, q_rows, nq, segs, sink_ref):
    lane = lax.broadcasted_iota(jnp.int32, (1, LANES), 1)
    half_of = [lane < HALF, lane >= HALF]
    rows = lax.broadcasted_iota(jnp.int32, (2 * nq, 1), 0)
    zero = jnp.zeros((), BF16)
    out = [None] * (ATT_Q // LANES)
    for g in range(N_KV_HEADS):
        tiles = (2 * g, 2 * g + 1)
        for e in range(2):
            qm = jnp.concatenate(
                [jnp.where(half_of[e], q_ref[q_rows, j * LANES:(j + 1) * LANES], zero)
                 for j in tiles], axis=0)
            sink = jnp.where(rows < nq, sink_ref[2 * tiles[0] + e],
                             sink_ref[2 * tiles[1] + e]) * LOG2_E
            scores = []
            for k, k_sw, _, _, m in segs:
                s = _dot_t(qm, k if e == g else k_sw)
                scores.append(s if m is None else jnp.where(m, s, NEG))
            mx = sink
            for s in scores:
                mx = jnp.maximum(mx, jnp.max(s, axis=-1, keepdims=True))
            probs = [jnp.exp2(s - mx) for s in scores]
            den = jnp.exp2(sink - mx)
            for p in probs:
                den = den + jnp.sum(p, axis=-1, keepdims=True)
            o = None
            for p, (_, _, v, v_sw, _) in zip(probs, segs):
                part = _dot(p.astype(BF16), jnp.where(half_of[e], v if e == g else v_sw, zero))
                o = part if o is None else o + part
            o = o * (1.0 / den)
            for r, j in enumerate(tiles):
                blk = o[r * nq:(r + 1) * nq, :]
                out[j] = blk if out[j] is None else out[j] + blk
    return out


def _context_attn_kernel(sink_ref, a_ref, o_ref, *, seq_len):
    for s in range(a_ref.shape[0] // seq_len):
        rows = slice(s * seq_len, (s + 1) * seq_len)
        segs = [(a_ref[rows, A_K:A_KS], a_ref[rows, A_KS:A_V],
                 a_ref[rows, A_V:A_VS], a_ref[rows, A_VS:ATT_IN_W], None)]
        for j, t in enumerate(_attend(a_ref, rows, seq_len, segs, sink_ref)):
            o_ref[rows, j * LANES:(j + 1) * LANES] = t.astype(o_ref.dtype)


def _latent_attn_kernel(sink_ref, q_ref, kv_ref, kc_ref, vc_ref, o_ref, *, seq_len):
    k_ctx, v_ctx = kc_ref[0], vc_ref[0]
    ctx = (k_ctx.astype(BF16), pltpu.roll(k_ctx, HALF, axis=1).astype(BF16),
           v_ctx.astype(BF16), pltpu.roll(v_ctx, HALF, axis=1).astype(BF16), None)
    span = 3 * BLOCK
    per_step = q_ref.shape[0] // BLOCK
    for s in range(per_step):
        i = pl.program_id(1) * per_step + s
        start = pl.multiple_of(jnp.clip((i - 1) * BLOCK, 0, seq_len - span), BLOCK)
        kpos = start + lax.broadcasted_iota(jnp.int32, (1, span), 1)
        qpos = i * BLOCK + lax.broadcasted_iota(jnp.int32, (BLOCK, 1), 0)
        valid = jnp.abs(kpos - qpos) <= WINDOW
        valid2 = jnp.concatenate([valid, valid], axis=0)
        kv = kv_ref[pl.ds(start, span), :]
        segs = [tuple(kv[:, c * ATT_KV:(c + 1) * ATT_KV] for c in range(4)) + (valid2,), ctx]
        rows = slice(s * BLOCK, (s + 1) * BLOCK)
        for j, t in enumerate(_attend(q_ref, rows, BLOCK, segs, sink_ref)):
            o_ref[rows, j * LANES:(j + 1) * LANES] = t.astype(o_ref.dtype)


def _smem_spec():
    return pl.BlockSpec(memory_space=pltpu.SMEM)


def _context_attention(sink, att_in, batch, seq_len):
    rows = CTX_SEQS_PER_STEP * seq_len

    def seq(width):
        return pl.BlockSpec((rows, width), lambda b: (b, 0))
    return pl.pallas_call(
        functools.partial(_context_attn_kernel, seq_len=seq_len),
        out_shape=jax.ShapeDtypeStruct((batch * seq_len, ATT_Q), BF16),
        grid=(batch // CTX_SEQS_PER_STEP,),
        in_specs=[_smem_spec(), seq(ATT_IN_W)],
        out_specs=seq(ATT_Q),
        compiler_params=_params(1),
        name="context_attention",
    )(sink, att_in)


def _latent_attention(sink, att_in, k_ctx, v_ctx, batch, seq_len):
    steps = seq_len // (LAT_BLOCKS_PER_STEP * BLOCK)
    rows = LAT_BLOCKS_PER_STEP * BLOCK
    past = k_ctx.shape[1]
    assert A_K == ATT_IN_W - A_K, "q and the key / value columns are the two halves of att_in"
    return pl.pallas_call(
        functools.partial(_latent_attn_kernel, seq_len=seq_len),
        out_shape=jax.ShapeDtypeStruct((batch * seq_len, ATT_Q), BF16),
        grid=(batch, steps),
        in_specs=[_smem_spec(),
                  pl.BlockSpec((rows, ATT_Q), lambda b, i: (b * steps + i, 0)),
                  pl.BlockSpec((seq_len, ATT_IN_W - A_K), lambda b, i: (b, 1)),
                  pl.BlockSpec((1, past, ATT_KV), lambda b, i: (b, 0, 0)),
                  pl.BlockSpec((1, past, ATT_KV), lambda b, i: (b, 0, 0))],
        out_specs=pl.BlockSpec((rows, ATT_Q), lambda b, i: (b * steps + i, 0)),
        compiler_params=_params(2),
        name="latent_attention",
    )(sink, att_in, att_in, k_ctx, v_ctx)


def _split2(x):
    hi = x.astype(BF16)
    lo = (x - hi.astype(F32)).astype(BF16)
    return hi, lo


def _gla_kernel(*refs, n_chunks, has_s0, emit_state):
    refs = list(refs)
    gf_ref, gv_ref, gn_ref = refs[:3]
    pos = 3
    s0_refs = refs[pos:pos + 2] if has_s0 else None
    pos += 2 if has_s0 else 0
    o_ref = refs[pos]
    pos += 1
    sfin_refs = refs[pos:pos + 2] if emit_state else None
    pos += 2 if emit_state else 0
    cum_ref, sent_ref, st_ref = refs[pos:]

    C = GLA_CHUNK
    n_pairs = GLA_QK // LANES
    lane = lax.broadcasted_iota(jnp.int32, (1, LANES), 1)
    half_of = [lane < HALF, lane >= HALF]
    r_i = lax.broadcasted_iota(jnp.int32, (C, C), 0)
    c_i = lax.broadcasted_iota(jnp.int32, (C, C), 1)
    lower = c_i <= r_i
    upper = c_i >= r_i
    tri = [jnp.where(lower, 1.0, 0.0).astype(BF16), jnp.where(upper, 1.0, 0.0).astype(BF16)]

    gnorm = gn_ref[...]
    qscale = GLA_DK ** -0.5

    def one_sequence(s):
        base = s * n_chunks * C

        def chunk_rows(n):
            return pl.ds(pl.multiple_of(base + n * C, C), C)

        for d in range(2):
            for p in range(n_pairs):
                if has_s0:
                    s0 = s0_refs[d][s, 0, 2 * p:2 * p + 2, :, :].reshape(2 * GLA_DK, GLA_DV)
                    st_ref[s, d, p] = s0.T
                else:
                    st_ref[s, d, p] = jnp.zeros((GLA_DV, 2 * GLA_DK), F32)

        def scan_step(i, carry):
            for d, n in ((0, i), (1, n_chunks - 1 - i)):
                rows = chunk_rows(n)
                la = gf_ref[rows, G_LA + d * GLA_QK:G_LA + (d + 1) * GLA_QK]
                cum = sum(_dot(tri[d], part) for part in _split2(la))
                cum_ref[d, rows, :] = cum
                tot = cum[C - 1:C, :] if d == 0 else cum[0:1, :]
                k_in = gf_ref[rows, G_K:G_G] * jnp.exp(tot - cum)
                decay = jnp.exp(tot)
                for p in range(n_pairs):
                    k2 = k_in[:, p * LANES:(p + 1) * LANES]
                    kv_t = None
                    for e in range(2):
                        h = 2 * p + e
                        v_t = gv_ref[rows, h * GLA_DV:(h + 1) * GLA_DV].T
                        part = _dot(v_t, jnp.where(half_of[e], k2, 0.0).astype(BF16))
                        kv_t = part if kv_t is None else kv_t + part
                    st = st_ref[s, d, p]
                    sent_ref[s, d, n, p] = st.astype(BF16)
                    st_ref[s, d, p] = decay[:, p * LANES:(p + 1) * LANES] * st + kv_t
            return carry

        lax.fori_loop(0, n_chunks, scan_step, 0, unroll=GLA_UNROLL)

        def out_step(n, carry):
            rows = chunk_rows(n)
            q = gf_ref[rows, G_Q:G_K] * qscale
            k = gf_ref[rows, G_K:G_G]
            qs, ks, qin = [], [], []
            for d in range(2):
                cum = cum_ref[d, rows, :]
                ref = cum[C // 2:C // 2 + 1, :]
                qs.append(q * jnp.exp(cum - ref))
                ks.append((k * jnp.exp(ref - cum)).astype(BF16))
                qin.append(q * jnp.exp(cum))
            for p in range(n_pairs):
                sl = slice(p * LANES, (p + 1) * LANES)
                for e in range(2):
                    h = 2 * p + e
                    hs = slice(h * GLA_DV, (h + 1) * GLA_DV)
                    m = half_of[e]
                    s_f = _dot_t(jnp.where(m, qs[0][:, sl], 0.0).astype(BF16), ks[0][:, sl])
                    s_b = _dot_t(jnp.where(m, qs[1][:, sl], 0.0).astype(BF16), ks[1][:, sl])
                    prob = jnp.where(lower, s_f, 0.0) + jnp.where(upper, s_b, 0.0)
                    o = _dot(prob.astype(BF16), gv_ref[rows, hs])
                    for d in range(2):
                        o = o + _dot_t(jnp.where(m, qin[d][:, sl], 0.0).astype(BF16),
                                       sent_ref[s, d, n, p])
                    o = o * lax.rsqrt(jnp.mean(o * o, axis=-1, keepdims=True) + EPS) * gnorm
                    gate = gf_ref[rows, G_G + h * GLA_DV:G_G + (h + 1) * GLA_DV]
                    o_ref[rows, hs] = (o * _silu(gate)).astype(o_ref.dtype)
            return carry

        lax.fori_loop(0, n_chunks, out_step, 0, unroll=GLA_UNROLL)

        if emit_state:
            for d in range(2):
                for p in range(n_pairs):
                    sfin_refs[d][s, 0, 2 * p:2 * p + 2, :, :] = (
                        st_ref[s, d, p].T.reshape(2, GLA_DK, GLA_DV))

    for s in range(st_ref.shape[0]):
        one_sequence(s)


def _gla(gla_f, gv, gnorm, batch, seq_len, seqs_per_step, s0=None, emit_state=False):
    n_chunks = seq_len // GLA_CHUNK
    has_s0 = s0 is not None
    rows = seqs_per_step * seq_len

    def seq(width):
        return pl.BlockSpec((rows, width), lambda b: (b, 0))
    state_spec = pl.BlockSpec((seqs_per_step, 1, GLA_HEADS, GLA_DK, GLA_DV),
                              lambda b: (b, 0, 0, 0, 0))
    in_specs = [seq(GLA_F_W), seq(GLA_V), pl.BlockSpec((1, GLA_DV), lambda b: (0, 0))]
    args = [gla_f, gv, gnorm]
    if has_s0:
        in_specs += [state_spec, state_spec]
        args += list(s0)
    out_shape = [jax.ShapeDtypeStruct((batch * seq_len, GLA_V), BF16)]
    out_specs = [seq(GLA_V)]
    if emit_state:
        out_shape += [jax.ShapeDtypeStruct((batch, 1, GLA_HEADS, GLA_DK, GLA_DV), F32)] * 2
        out_specs += [state_spec, state_spec]
    n_pairs = GLA_QK // LANES
    return pl.pallas_call(
        functools.partial(_gla_kernel, n_chunks=n_chunks, has_s0=has_s0, emit_state=emit_state),
        out_shape=out_shape,
        grid=(batch // seqs_per_step,),
        in_specs=in_specs,
        out_specs=out_specs,
        scratch_shapes=[pltpu.VMEM((2, rows, GLA_QK), F32),
                        pltpu.VMEM((seqs_per_step, 2, n_chunks, n_pairs, GLA_DV, LANES), BF16),
                        pltpu.VMEM((seqs_per_step, 2, n_pairs, GLA_DV, LANES), F32)],
        compiler_params=_params(1),
        name="gla",
    )(*args)


def kernel(x_prompt, x_sample, cache_k, cache_v, state_gla_fwd, state_gla_bwd, c, c_ctx,
           w_ada, b_ada, norm_pre, norm_post, ffn_w1, ffn_w3, ffn_w2, w_in,
           gla_w_up, gla_b_up, gla_norm, attn_sink, w_out):
    depth = w_in.shape[0]
    assert depth == 1, "single trunk layer"
    batch, seq = x_prompt.shape[0], x_prompt.shape[1]
    dec_batch, dec_seq = x_sample.shape[0], x_sample.shape[1]
    past = cache_k.shape[2]
    l = 0

    cond = jnp.concatenate(
        [c_ctx[None, :], c, jnp.zeros((MOD_ROWS - 1 - dec_batch, D_MODEL), F32)], axis=0)
    mod = _ada_modulation(cond, w_ada[l], b_ada[l][None, :])

    npre, npost = norm_pre[l], norm_post[l]
    ffn_b = [[w[l, j].astype(BF16) for w in (ffn_w1, ffn_w3, ffn_w2)] for j in range(2)]
    w_in_b = w_in[l].astype(BF16)
    w_out_b = w_out[l].astype(BF16)
    zeros = jnp.zeros((GLA_LOW_RANK, GLA_QK), F32)
    w_up = jnp.concatenate(
        [jnp.concatenate([gla_w_up[l, 0], zeros], axis=1),
         jnp.concatenate([zeros, gla_w_up[l, 1]], axis=1)], axis=0).astype(BF16)
    b_up = gla_b_up[l].reshape(1, 2 * GLA_QK)
    gnorm = gla_norm[l][None, :]
    sink = attn_sink[l]

    def trunk(x, latent):
        n_batch, n_seq = (dec_batch, dec_seq) if latent else (batch, seq)
        latent_len = n_seq if latent else None
        x1 = _ffn_first(x, mod, latent_len, npre, npost, *ffn_b[0])
        att_in, gla_f, gv, *cache_t = _project(
            x1, mod, latent_len, None if latent else n_seq, npre, w_in_b, w_up, b_up)
        if latent:
            att = _latent_attention(sink, att_in, cache_k[:, l].reshape(dec_batch, past, ATT_KV),
                                    cache_v[:, l].reshape(dec_batch, past, ATT_KV), n_batch, n_seq)
            (gla,) = _gla(gla_f, gv, gnorm, n_batch, n_seq, 1,
                          s0=(state_gla_fwd[:, l:l + 1], state_gla_bwd[:, l:l + 1]))
            extras = ()
        else:
            att = _context_attention(sink, att_in, n_batch, n_seq)
            gla, s_f, s_b = _gla(gla_f, gv, gnorm, n_batch, n_seq, CTX_SEQS_PER_STEP,
                                 emit_state=True)
            k_new, v_new = (jnp.transpose(c_t, (0, 1, 4, 2, 3)) for c_t in cache_t)
            extras = (k_new, v_new, s_f, s_b)
        y = _mix_ffn(x1, att, gla, w_out_b, mod, latent_len, npre, npost, *ffn_b[1])
        return y.reshape(n_batch, n_seq, D_MODEL), extras

    y_prompt, (k_new, v_new, s_f, s_b) = trunk(x_prompt.reshape(batch * seq, D_MODEL), False)
    y_sample, _ = trunk(x_sample.reshape(dec_batch * dec_seq, D_MODEL), True)
    return (y_prompt, y_sample, k_new, v_new, s_f, s_b)
```

```python
import functools

import numpy as np
import jax
import jax.numpy as jnp
from jax import lax
from jax.experimental import pallas as pl
from jax.experimental.pallas import tpu as pltpu

F32 = jnp.float32
BF16 = jnp.bfloat16

D_MODEL = 1024
GRID_W = 64
N_Q_HEADS = 8
N_KV_HEADS = 2
HEAD_DIM = 64
WINDOW = 128
BLOCK = 128
ROPE_BASE = 10000.0
GLA_HEADS = 4
GLA_DK = 64
GLA_DV = 128
GLA_LOW_RANK = 16
GLA_TAU = 16.0
D_FF = 2816
N_MOD = 9
EPS = 1e-6
NEG = -1e30

ATT_Q = N_Q_HEADS * HEAD_DIM
ATT_KV = N_KV_HEADS * HEAD_DIM
GLA_QK = GLA_HEADS * GLA_DK
GLA_V = GLA_HEADS * GLA_DV
OFF_Q = 0
OFF_K = OFF_Q + ATT_Q
OFF_V = OFF_K + ATT_KV
OFF_GQ = OFF_V + ATT_KV
OFF_GK = OFF_GQ + GLA_QK
OFF_GV = OFF_GK + GLA_QK
OFF_GG = OFF_GV + GLA_V
OFF_LR = OFF_GG + GLA_V
IN_WIDTH = OFF_LR + 2 * GLA_LOW_RANK
A_Q = 0
A_K = A_Q + ATT_Q
A_KS = A_K + ATT_KV
A_V = A_KS + ATT_KV
A_VS = A_V + ATT_KV
ATT_IN_W = A_VS + ATT_KV
G_Q = 0
G_K = G_Q + GLA_QK
G_G = G_K + GLA_QK
G_LA = G_G + GLA_V
GLA_F_W = G_LA + 2 * GLA_QK
LOG2_E = 1.4426950408889634
ATT_Q_SCALE = HEAD_DIM ** -0.5 * LOG2_E

LANES = 128
SUBLANES = 8
HALF = LANES // 2
VMEM_LIMIT = 56 * 1024 * 1024

TOKEN_TILE = 512
FF_SPLITS = (0, 1536, D_FF)
ADA_K_TILE = 128
CTX_SEQS_PER_STEP = 2
LAT_BLOCKS_PER_STEP = 2
GLA_CHUNK = 128
GLA_UNROLL = 4
MOD_ROWS = 8


def _params(n_axes):
    return pltpu.CompilerParams(
        dimension_semantics=("arbitrary",) * n_axes, vmem_limit_bytes=VMEM_LIMIT)


def _resident(shape):
    zeros = (0,) * len(shape)
    return pl.BlockSpec(shape, lambda *_: zeros, pipeline_mode=pl.Buffered(1))


def _sigmoid(x):
    return 1.0 / (1.0 + jnp.exp(-x))


def _silu(x):
    return x * _sigmoid(x)


def _rms(x, g):
    return x * lax.rsqrt(jnp.mean(x * x, axis=-1, keepdims=True) + EPS) * g


def _dot(a, b):
    return jnp.dot(a, b, preferred_element_type=F32)


def _dot_t(a, b):
    return lax.dot_general(a, b, (((1,), (1,)), ((), ())), preferred_element_type=F32)


def _ada_kernel(cond_ref, w_ref, b_ref, o_ref):
    @pl.when(pl.program_id(0) == 0)
    def _():
        for m in range(N_MOD):
            o_ref[m] = jnp.broadcast_to(b_ref[:, m * D_MODEL:(m + 1) * D_MODEL],
                                        (MOD_ROWS, D_MODEL))
    part = _dot(_silu(cond_ref[...]).astype(BF16), w_ref[...].astype(BF16))
    for m in range(N_MOD):
        o_ref[m] += part[:, m * D_MODEL:(m + 1) * D_MODEL]


def _ada_modulation(cond, w_ada, b_ada):
    k_dim, n = w_ada.shape
    return pl.pallas_call(
        _ada_kernel,
        out_shape=jax.ShapeDtypeStruct((N_MOD, MOD_ROWS, D_MODEL), F32),
        grid=(k_dim // ADA_K_TILE,),
        in_specs=[
            pl.BlockSpec((MOD_ROWS, ADA_K_TILE), lambda k: (0, k)),
            pl.BlockSpec((ADA_K_TILE, n), lambda k: (k, 0)),
            pl.BlockSpec((1, n), lambda k: (0, 0)),
        ],
        out_specs=pl.BlockSpec((N_MOD, MOD_ROWS, D_MODEL), lambda k: (0, 0, 0)),
        compiler_params=_params(1),
        name="ada_modulation",
    )(cond, w_ada, b_ada)


class _Mod:
    def __init__(self, mod_ref, tiles_per_seq):
        self.ref = mod_ref
        self.row = 0 if tiles_per_seq is None else 1 + pl.program_id(0) // tiles_per_seq

    def __getitem__(self, m):
        return self.ref[m, pl.ds(self.row, 1), :]


def _modulated(x, mod, npre_ref, i):
    return _rms(x, npre_ref[i:i + 1, :]) * (1.0 + mod[3 * i + 1]) + mod[3 * i]


def _residual(x, out, mod, npost_ref, i, weight):
    return x + (weight * mod[3 * i + 2]) * _rms(out, npost_ref[i:i + 1, :])


def _ffn_sublayer(x, i, mod, npre_ref, npost_ref, w1_ref, w3_ref, w2_ref):
    h = _modulated(x, mod, npre_ref, i).astype(BF16)
    acc = None
    for lo, hi in zip(FF_SPLITS[:-1], FF_SPLITS[1:]):
        a = _dot(h, w1_ref[:, lo:hi])
        g = _dot(h, w3_ref[:, lo:hi])
        part = _dot((_silu(a) * g).astype(BF16), w2_ref[lo:hi, :])
        acc = part if acc is None else acc + part
    return _residual(x, acc, mod, npost_ref, i, 0.5)


def _ffn_first_kernel(x_ref, mod_ref, npre_ref, npost_ref, w1_ref, w3_ref, w2_ref, o_ref,
                      *, tiles_per_seq):
    mod = _Mod(mod_ref, tiles_per_seq)
    o_ref[...] = _ffn_sublayer(x_ref[...], 0, mod, npre_ref, npost_ref, w1_ref, w3_ref, w2_ref)


def _mix_ffn_kernel(x_ref, att_ref, gla_ref, wo_ref, mod_ref, npre_ref, npost_ref,
                    w1_ref, w3_ref, w2_ref, o_ref, *, tiles_per_seq):
    mod = _Mod(mod_ref, tiles_per_seq)
    mix = _dot(att_ref[...], wo_ref[0:ATT_Q, :]) + _dot(gla_ref[...], wo_ref[ATT_Q:, :])
    x = _residual(x_ref[...], mix, mod, npost_ref, 1, 1.0)
    o_ref[...] = _ffn_sublayer(x, 2, mod, npre_ref, npost_ref, w1_ref, w3_ref, w2_ref)


def _rope_tile(x, cos, sin_up, sin_dn):
    up = pltpu.roll(x, LANES - 16, axis=1)
    dn = pltpu.roll(x, 16, axis=1)
    return x * cos + up * sin_up + dn * sin_dn


def _proj_kernel(*refs, tiles_per_seq, cache_seq):
    x_ref, mod_ref, npre_ref, win_ref, wup_ref, bup_ref = refs[:6]
    rope_refs = refs[6:9] if tiles_per_seq is not None else None
    outs = refs[6 + (3 if rope_refs else 0):]
    att_ref, glaf_ref, gv_ref = outs[:3]
    h = _modulated(x_ref[...], _Mod(mod_ref, tiles_per_seq), npre_ref, 1).astype(BF16)
    q = _dot(h, win_ref[:, OFF_Q:OFF_K])
    kv = _dot(h, win_ref[:, OFF_K:OFF_GQ])
    k, v = kv[:, :ATT_KV], kv[:, ATT_KV:]
    if rope_refs:
        cos, sup, sdn = (r[...] for r in rope_refs)
        q = jnp.concatenate([_rope_tile(q[:, j * LANES:(j + 1) * LANES], cos, sup, sdn)
                             for j in range(ATT_Q // LANES)], axis=1)
        k_att = _rope_tile(k, cos, sup, sdn)
    else:
        k_att = k
    att_ref[:, A_Q:A_K] = (q * ATT_Q_SCALE).astype(BF16)
    att_ref[:, A_K:A_KS] = k_att.astype(BF16)
    att_ref[:, A_KS:A_V] = pltpu.roll(k_att, HALF, axis=1).astype(BF16)
    att_ref[:, A_V:A_VS] = v.astype(BF16)
    att_ref[:, A_VS:ATT_IN_W] = pltpu.roll(v, HALF, axis=1).astype(BF16)
    if cache_seq is not None:
        for t, dst in ((k.T, outs[3]), (v.T, outs[4])):
            for b in range(TOKEN_TILE // cache_seq):
                dst[b, 0] = t[:, b * cache_seq:(b + 1) * cache_seq].reshape(
                    N_KV_HEADS, HEAD_DIM, cache_seq)
    glaf_ref[:, G_Q:G_G] = _dot(h, win_ref[:, OFF_GQ:OFF_GV])
    gv_ref[...] = _dot(h, win_ref[:, OFF_GV:OFF_GG]).astype(BF16)
    glaf_ref[:, G_G:G_LA] = _dot(h, win_ref[:, OFF_GG:OFF_LR])
    lr = _dot(h, win_ref[:, OFF_LR:IN_WIDTH])
    z = _dot(lr.astype(BF16), wup_ref[...]) + bup_ref[...]
    log_sig = jnp.minimum(z, 0.0) - jnp.log1p(jnp.exp(-jnp.abs(z)))
    glaf_ref[:, G_LA:GLA_F_W] = log_sig * (1.0 / GLA_TAU)


def _token_spec(width):
    return pl.BlockSpec((TOKEN_TILE, width), lambda i: (i, 0))


def _ffn_weight_spec(shape, j):
    return pl.BlockSpec((None,) + tuple(shape[1:]), lambda i: (j, 0, 0),
                        pipeline_mode=pl.Buffered(1))


def _tiles_per_seq(latent_len):
    return None if latent_len is None else latent_len // TOKEN_TILE


def _ffn_first(x, mod, latent_len, npre, npost, w1, w3, w2):
    t = x.shape[0]
    return pl.pallas_call(
        functools.partial(_ffn_first_kernel, tiles_per_seq=_tiles_per_seq(latent_len)),
        out_shape=jax.ShapeDtypeStruct((t, D_MODEL), F32),
        grid=(t // TOKEN_TILE,),
        in_specs=[_token_spec(D_MODEL), _resident(mod.shape), _resident(npre.shape),
                  _resident(npost.shape), _ffn_weight_spec(w1.shape, 0),
                  _ffn_weight_spec(w3.shape, 0), _ffn_weight_spec(w2.shape, 0)],
        out_specs=_token_spec(D_MODEL),
        compiler_params=_params(1),
        name="ffn_first",
    )(x, mod, npre, npost, w1, w3, w2)


def _mix_ffn(x, att, gla, w_out, mod, latent_len, npre, npost, w1, w3, w2):
    t = x.shape[0]
    return pl.pallas_call(
        functools.partial(_mix_ffn_kernel, tiles_per_seq=_tiles_per_seq(latent_len)),
        out_shape=jax.ShapeDtypeStruct((t, D_MODEL), F32),
        grid=(t // TOKEN_TILE,),
        in_specs=[_token_spec(D_MODEL), _token_spec(ATT_Q), _token_spec(GLA_V),
                  _resident(w_out.shape), _resident(mod.shape), _resident(npre.shape),
                  _resident(npost.shape), _ffn_weight_spec(w1.shape, 1),
                  _ffn_weight_spec(w3.shape, 1), _ffn_weight_spec(w2.shape, 1)],
        out_specs=_token_spec(D_MODEL),
        compiler_params=_params(1),
        name="mix_ffn",
    )(x, att, gla, w_out, mod, npre, npost, w1, w3, w2)


def _project(x, mod, latent_len, cache_seq, npre, w_in, w_up, b_up):
    t = x.shape[0]
    tiles = _tiles_per_seq(latent_len)
    in_specs = [_token_spec(D_MODEL), _resident(mod.shape), _resident(npre.shape),
                _resident(w_in.shape), _resident(w_up.shape), _resident(b_up.shape)]
    args = [x, mod, npre, w_in, w_up, b_up]
    if tiles is not None:
        in_specs += [pl.BlockSpec((TOKEN_TILE, LANES), lambda i: (i % tiles, 0))] * 3
        args += [jnp.asarray(tab) for tab in _rope_tables(latent_len)]
    outs = ((ATT_IN_W, BF16), (GLA_F_W, F32), (GLA_V, BF16))
    out_shape = [jax.ShapeDtypeStruct((t, w), dt) for w, dt in outs]
    out_specs = [_token_spec(w) for w, _ in outs]
    if cache_seq is not None:
        seqs = TOKEN_TILE // cache_seq
        cache = (t // cache_seq, 1, N_KV_HEADS, HEAD_DIM, cache_seq)
        out_shape += [jax.ShapeDtypeStruct(cache, F32)] * 2
        out_specs += [pl.BlockSpec((seqs,) + cache[1:], lambda i: (i, 0, 0, 0, 0))] * 2
    return pl.pallas_call(
        functools.partial(_proj_kernel, tiles_per_seq=tiles, cache_seq=cache_seq),
        out_shape=out_shape,
        grid=(t // TOKEN_TILE,),
        in_specs=in_specs,
        out_specs=out_specs,
        compiler_params=_params(1),
        name="project",
    )(*args)


def _rope_tables(seq_len):
    half = HEAD_DIM // 2
    inv_freq = np.float32(ROPE_BASE) ** (-np.arange(0, half, 2, dtype=np.float32) / half)
    pos = np.arange(seq_len)
    row = (pos // GRID_W).astype(np.float32)
    col = (pos % GRID_W).astype(np.float32)
    within = np.arange(LANES) % HEAD_DIM
    idx = within % half
    freq = inv_freq[idx % (half // 2)].astype(np.float32)
    p = np.where((within // half == 0)[None, :], row[:, None], col[:, None])
    ang = (p * freq[None, :]).astype(np.float32)
    cos, sin = np.cos(ang).astype(np.float32), np.sin(ang).astype(np.float32)
    first = (idx < half // 2)[None, :]
    zero = np.float32(0.0)
    return cos, np.where(first, -sin, zero), np.where(first, zero, sin)


def _attend(q_ref, q_rows, nq, segs, sink_ref):
    lane = lax.broadcasted_iota(jnp.int32, (1, LANES), 1)
    half_of = [lane < HALF, lane >= HALF]
    rows = lax.broadcasted_iota(jnp.int32, (2 * nq, 1), 0)
    zero = jnp.zeros((), BF16)
    out = [None] * (ATT_Q // LANES)
    for g in range(N_KV_HEADS):
        tiles = (2 * g, 2 * g + 1)
        for e in range(2):
            qm = jnp.concatenate(
                [jnp.where(half_of[e], q_ref[q_rows, j * LANES:(j + 1) * LANES], zero)
                 for j in tiles], axis=0)
            sink = jnp.where(rows < nq, sink_ref[2 * tiles[0] + e],
                             sink_ref[2 * tiles[1] + e]) * LOG2_E
            scores = []
            for k, k_sw, _, _, m in segs:
                s = _dot_t(qm, k if e == g else k_sw)
                scores.append(s if m is None else jnp.where(m, s, NEG))
            mx = sink
            for s in scores:
                mx = jnp.maximum(mx, jnp.max(s, axis=-1, keepdims=True))
            probs = [jnp.exp2(s - mx) for s in scores]
            den = jnp.exp2(sink - mx)
            for p in probs:
                den = den + jnp.sum(p, axis=-1, keepdims=True)
            o = None
            for p, (_, _, v, v_sw, _) in zip(probs, segs):
                part = _dot(p.astype(BF16), jnp.where(half_of[e], v if e == g else v_sw, zero))
                o = part if o is None else o + part
            o = o * (1.0 / den)
            for r, j in enumerate(tiles):
                blk = o[r * nq:(r + 1) * nq, :]
                out[j] = blk if out[j] is None else out[j] + blk
    return out


def _context_attn_kernel(sink_ref, a_ref, o_ref, *, seq_len):
    for s in range(a_ref.shape[0] // seq_len):
        rows = slice(s * seq_len, (s + 1) * seq_len)
        segs = [(a_ref[rows, A_K:A_KS], a_ref[rows, A_KS:A_V],
                 a_ref[rows, A_V:A_VS], a_ref[rows, A_VS:ATT_IN_W], None)]
        for j, t in enumerate(_attend(a_ref, rows, seq_len, segs, sink_ref)):
            o_ref[rows, j * LANES:(j + 1) * LANES] = t.astype(o_ref.dtype)


def _latent_attn_kernel(sink_ref, q_ref, kv_ref, kc_ref, vc_ref, o_ref, *, seq_len):
    k_ctx, v_ctx = kc_ref[0], vc_ref[0]
    ctx = (k_ctx.astype(BF16), pltpu.roll(k_ctx, HALF, axis=1).astype(BF16),
           v_ctx.astype(BF16), pltpu.roll(v_ctx, HALF, axis=1).astype(BF16), None)
    span = 3 * BLOCK
    per_step = q_ref.shape[0] // BLOCK
    for s in range(per_step):
        i = pl.program_id(1) * per_step + s
        start = pl.multiple_of(jnp.clip((i - 1) * BLOCK, 0, seq_len - span), BLOCK)
        kpos = start + lax.broadcasted_iota(jnp.int32, (1, span), 1)
        qpos = i * BLOCK + lax.broadcasted_iota(jnp.int32, (BLOCK, 1), 0)
        valid = jnp.abs(kpos - qpos) <= WINDOW
        valid2 = jnp.concatenate([valid, valid], axis=0)
        kv = kv_ref[pl.ds(start, span), :]
        segs = [tuple(kv[:, c * ATT_KV:(c + 1) * ATT_KV] for c in range(4)) + (valid2,), ctx]
        rows = slice(s * BLOCK, (s + 1) * BLOCK)
        for j, t in enumerate(_attend(q_ref, rows, BLOCK, segs, sink_ref)):
            o_ref[rows, j * LANES:(j + 1) * LANES] = t.astype(o_ref.dtype)


def _smem_spec():
    return pl.BlockSpec(memory_space=pltpu.SMEM)


def _context_attention(sink, att_in, batch, seq_len):
    rows = CTX_SEQS_PER_STEP * seq_len

    def seq(width):
        return pl.BlockSpec((rows, width), lambda b: (b, 0))
    return pl.pallas_call(
        functools.partial(_context_attn_kernel, seq_len=seq_len),
        out_shape=jax.ShapeDtypeStruct((batch * seq_len, ATT_Q), BF16),
        grid=(batch // CTX_SEQS_PER_STEP,),
        in_specs=[_smem_spec(), seq(ATT_IN_W)],
        out_specs=seq(ATT_Q),
        compiler_params=_params(1),
        name="context_attention",
    )(sink, att_in)


def _latent_attention(sink, att_in, k_ctx, v_ctx, batch, seq_len):
    steps = seq_len // (LAT_BLOCKS_PER_STEP * BLOCK)
    rows = LAT_BLOCKS_PER_STEP * BLOCK
    past = k_ctx.shape[1]
    assert A_K == ATT_IN_W - A_K, "q and the key / value columns are the two halves of att_in"
    return pl.pallas_call(
        functools.partial(_latent_attn_kernel, seq_len=seq_len),
        out_shape=jax.ShapeDtypeStruct((batch * seq_len, ATT_Q), BF16),
        grid=(batch, steps),
        in_specs=[_smem_spec(),
                  pl.BlockSpec((rows, ATT_Q), lambda b, i: (b * steps + i, 0)),
                  pl.BlockSpec((seq_len, ATT_IN_W - A_K), lambda b, i: (b, 1)),
                  pl.BlockSpec((1, past, ATT_KV), lambda b, i: (b, 0, 0)),
                  pl.BlockSpec((1, past, ATT_KV), lambda b, i: (b, 0, 0))],
        out_specs=pl.BlockSpec((rows, ATT_Q), lambda b, i: (b * steps + i, 0)),
        compiler_params=_params(2),
        name="latent_attention",
    )(sink, att_in, att_in, k_ctx, v_ctx)


def _split2(x):
    hi = x.astype(BF16)
    lo = (x - hi.astype(F32)).astype(BF16)
    return hi, lo


def _gla_kernel(*refs, n_chunks, has_s0, emit_state):
    refs = list(refs)
    gf_ref, gv_ref, gn_ref = refs[:3]
    pos = 3
    s0_refs = refs[pos:pos + 2] if has_s0 else None
    pos += 2 if has_s0 else 0
    o_ref = refs[pos]
    pos += 1
    sfin_refs = refs[pos:pos + 2] if emit_state else None
    pos += 2 if emit_state else 0
    cum_ref, sent_ref, st_ref = refs[pos:]

    C = GLA_CHUNK
    n_pairs = GLA_QK // LANES
    lane = lax.broadcasted_iota(jnp.int32, (1, LANES), 1)
    half_of = [lane < HALF, lane >= HALF]
    r_i = lax.broadcasted_iota(jnp.int32, (C, C), 0)
    c_i = lax.broadcasted_iota(jnp.int32, (C, C), 1)
    lower = c_i <= r_i
    upper = c_i >= r_i
    tri = [jnp.where(lower, 1.0, 0.0).astype(BF16), jnp.where(upper, 1.0, 0.0).astype(BF16)]

    gnorm = gn_ref[...]
    qscale = GLA_DK ** -0.5

    def one_sequence(s):
        base = s * n_chunks * C

        def chunk_rows(n):
            return pl.ds(pl.multiple_of(base + n * C, C), C)

        for d in range(2):
            for p in range(n_pairs):
                if has_s0:
                    s0 = s0_refs[d][s, 0, 2 * p:2 * p + 2, :, :].reshape(2 * GLA_DK, GLA_DV)
                    st_ref[s, d, p] = s0.T
                else:
                    st_ref[s, d, p] = jnp.zeros((GLA_DV, 2 * GLA_DK), F32)

        def scan_step(i, carry):
            for d, n in ((0, i), (1, n_chunks - 1 - i)):
                rows = chunk_rows(n)
                la = gf_ref[rows, G_LA + d * GLA_QK:G_LA + (d + 1) * GLA_QK]
                cum = sum(_dot(tri[d], part) for part in _split2(la))
                cum_ref[d, rows, :] = cum
                tot = cum[C - 1:C, :] if d == 0 else cum[0:1, :]
                k_in = gf_ref[rows, G_K:G_G] * jnp.exp(tot - cum)
                decay = jnp.exp(tot)
                for p in range(n_pairs):
                    k2 = k_in[:, p * LANES:(p + 1) * LANES]
                    kv_t = None
                    for e in range(2):
                        h = 2 * p + e
                        v_t = gv_ref[rows, h * GLA_DV:(h + 1) * GLA_DV].T
                        part = _dot(v_t, jnp.where(half_of[e], k2, 0.0).astype(BF16))
                        kv_t = part if kv_t is None else kv_t + part
                    st = st_ref[s, d, p]
                    sent_ref[s, d, n, p] = st.astype(BF16)
                    st_ref[s, d, p] = decay[:, p * LANES:(p + 1) * LANES] * st + kv_t
            return carry

        lax.fori_loop(0, n_chunks, scan_step, 0, unroll=GLA_UNROLL)

        def out_step(n, carry):
            rows = chunk_rows(n)
            q = gf_ref[rows, G_Q:G_K] * qscale
            k = gf_ref[rows, G_K:G_G]
            qs, ks, qin = [], [], []
            for d in range(2):
                cum = cum_ref[d, rows, :]
                ref = cum[C // 2:C // 2 + 1, :]
                qs.append(q * jnp.exp(cum - ref))
                ks.append((k * jnp.exp(ref - cum)).astype(BF16))
                qin.append(q * jnp.exp(cum))
            for p in range(n_pairs):
                sl = slice(p * LANES, (p + 1) * LANES)
                for e in range(2):
                    h = 2 * p + e
                    hs = slice(h * GLA_DV, (h + 1) * GLA_DV)
                    m = half_of[e]
                    s_f = _dot_t(jnp.where(m, qs[0][:, sl], 0.0).astype(BF16), ks[0][:, sl])
                    s_b = _dot_t(jnp.where(m, qs[1][:, sl], 0.0).astype(BF16), ks[1][:, sl])
                    prob = jnp.where(lower, s_f, 0.0) + jnp.where(upper, s_b, 0.0)
                    o = _dot(prob.astype(BF16), gv_ref[rows, hs])
                    for d in range(2):
                        o = o + _dot_t(jnp.where(m, qin[d][:, sl], 0.0).astype(BF16),
                                       sent_ref[s, d, n, p])
                    o = o * lax.rsqrt(jnp.mean(o * o, axis=-1, keepdims=True) + EPS) * gnorm
                    gate = gf_ref[rows, G_G + h * GLA_DV:G_G + (h + 1) * GLA_DV]
                    o_ref[rows, hs] = (o * _silu(gate)).astype(o_ref.dtype)
            return carry

        lax.fori_loop(0, n_chunks, out_step, 0, unroll=GLA_UNROLL)

        if emit_state:
            for d in range(2):
                for p in range(n_pairs):
                    sfin_refs[d][s, 0, 2 * p:2 * p + 2, :, :] = (
                        st_ref[s, d, p].T.reshape(2, GLA_DK, GLA_DV))

    for s in range(st_ref.shape[0]):
        one_sequence(s)


def _gla(gla_f, gv, gnorm, batch, seq_len, seqs_per_step, s0=None, emit_state=False):
    n_chunks = seq_len // GLA_CHUNK
    has_s0 = s0 is not None
    rows = seqs_per_step * seq_len

    def seq(width):
        return pl.BlockSpec((rows, width), lambda b: (b, 0))
    state_spec = pl.BlockSpec((seqs_per_step, 1, GLA_HEADS, GLA_DK, GLA_DV),
                              lambda b: (b, 0, 0, 0, 0))
    in_specs = [seq(GLA_F_W), seq(GLA_V), pl.BlockSpec((1, GLA_DV), lambda b: (0, 0))]
    args = [gla_f, gv, gnorm]
    if has_s0:
        in_specs += [state_spec, state_spec]
        args += list(s0)
    out_shape = [jax.ShapeDtypeStruct((batch * seq_len, GLA_V), BF16)]
    out_specs = [seq(GLA_V)]
    if emit_state:
        out_shape += [jax.ShapeDtypeStruct((batch, 1, GLA_HEADS, GLA_DK, GLA_DV), F32)] * 2
        out_specs += [state_spec, state_spec]
    n_pairs = GLA_QK // LANES
    return pl.pallas_call(
        functools.partial(_gla_kernel, n_chunks=n_chunks, has_s0=has_s0, emit_state=emit_state),
        out_shape=out_shape,
        grid=(batch // seqs_per_step,),
        in_specs=in_specs,
        out_specs=out_specs,
        scratch_shapes=[pltpu.VMEM((2, rows, GLA_QK), F32),
                        pltpu.VMEM((seqs_per_step, 2, n_chunks, n_pairs, GLA_DV, LANES), BF16),
                        pltpu.VMEM((seqs_per_step, 2, n_pairs, GLA_DV, LANES), F32)],
        compiler_params=_params(1),
        name="gla",
    )(*args)


def kernel(x_prompt, x_sample, cache_k, cache_v, state_gla_fwd, state_gla_bwd, c, c_ctx,
           w_ada, b_ada, norm_pre, norm_post, ffn_w1, ffn_w3, ffn_w2, w_in,
           gla_w_up, gla_b_up, gla_norm, attn_sink, w_out):
    depth = w_in.shape[0]
    assert depth == 1, "single trunk layer"
    batch, seq = x_prompt.shape[0], x_prompt.shape[1]
    dec_batch, dec_seq = x_sample.shape[0], x_sample.shape[1]
    past = cache_k.shape[2]
    l = 0

    cond = jnp.concatenate(
        [c_ctx[None, :], c, jnp.zeros((MOD_ROWS - 1 - dec_batch, D_MODEL), F32)], axis=0)
    mod = _ada_modulation(cond, w_ada[l], b_ada[l][None, :])

    npre, npost = norm_pre[l], norm_post[l]
    ffn_b = [w[l].astype(BF16) for w in (ffn_w1, ffn_w3, ffn_w2)]
    w_in_b = w_in[l].astype(BF16)
    w_out_b = w_out[l].astype(BF16)
    zeros = jnp.zeros((GLA_LOW_RANK, GLA_QK), F32)
    w_up = jnp.concatenate(
        [jnp.concatenate([gla_w_up[l, 0], zeros], axis=1),
         jnp.concatenate([zeros, gla_w_up[l, 1]], axis=1)], axis=0).astype(BF16)
    b_up = gla_b_up[l].reshape(1, 2 * GLA_QK)
    gnorm = gla_norm[l][None, :]
    sink = attn_sink[l]

    def trunk(x, latent):
        n_batch, n_seq = (dec_batch, dec_seq) if latent else (batch, seq)
        latent_len = n_seq if latent else None
        x1 = _ffn_first(x, mod, latent_len, npre, npost, *ffn_b)
        att_in, gla_f, gv, *cache_t = _project(
            x1, mod, latent_len, None if latent else n_seq, npre, w_in_b, w_up, b_up)
        if latent:
            att = _latent_attention(sink, att_in, cache_k[:, l].reshape(dec_batch, past, ATT_KV),
                                    cache_v[:, l].reshape(dec_batch, past, ATT_KV), n_batch, n_seq)
            (gla,) = _gla(gla_f, gv, gnorm, n_batch, n_seq, 1,
                          s0=(state_gla_fwd[:, l:l + 1], state_gla_bwd[:, l:l + 1]))
            extras = ()
        else:
            att = _context_attention(sink, att_in, n_batch, n_seq)
            gla, s_f, s_b = _gla(gla_f, gv, gnorm, n_batch, n_seq, CTX_SEQS_PER_STEP,
                                 emit_state=True)
            k_new, v_new = (jnp.transpose(c_t, (0, 1, 4, 2, 3)) for c_t in cache_t)
            extras = (k_new, v_new, s_f, s_b)
        y = _mix_ffn(x1, att, gla, w_out_b, mod, latent_len, npre, npost, *ffn_b)
        return y.reshape(n_batch, n_seq, D_MODEL), extras

    y_prompt, (k_new, v_new, s_f, s_b) = trunk(x_prompt.reshape(batch * seq, D_MODEL), False)
    y_sample, _ = trunk(x_sample.reshape(dec_batch * dec_seq, D_MODEL), True)
    return (y_prompt, y_sample, k_new, v_new, s_f, s_b)
```

```python
import functools

import numpy as np
import jax
import jax.numpy as jnp
from jax import lax
from jax.experimental import pallas as pl
from jax.experimental.pallas import tpu as pltpu

F32 = jnp.float32
BF16 = jnp.bfloat16

D_MODEL = 1024
GRID_W = 64
N_Q_HEADS = 8
N_KV_HEADS = 2
HEAD_DIM = 64
WINDOW = 128
BLOCK = 128
ROPE_BASE = 10000.0
GLA_HEADS = 4
GLA_DK = 64
GLA_DV = 128
GLA_LOW_RANK = 16
GLA_TAU = 16.0
D_FF = 2816
N_MOD = 9
EPS = 1e-6
NEG = -1e30

ATT_Q = N_Q_HEADS * HEAD_DIM
ATT_KV = N_KV_HEADS * HEAD_DIM
GLA_QK = GLA_HEADS * GLA_DK
GLA_V = GLA_HEADS * GLA_DV
OFF_Q = 0
OFF_K = OFF_Q + ATT_Q
OFF_V = OFF_K + ATT_KV
OFF_GQ = OFF_V + ATT_KV
OFF_GK = OFF_GQ + GLA_QK
OFF_GV = OFF_GK + GLA_QK
OFF_GG = OFF_GV + GLA_V
OFF_LR = OFF_GG + GLA_V
IN_WIDTH = OFF_LR + 2 * GLA_LOW_RANK
A_Q = 0
A_K = A_Q + ATT_Q
A_KS = A_K + ATT_KV
A_V = A_KS + ATT_KV
A_VS = A_V + ATT_KV
ATT_IN_W = A_VS + ATT_KV
G_Q = 0
G_K = G_Q + GLA_QK
G_G = G_K + GLA_QK
G_LA = G_G + GLA_V
GLA_F_W = G_LA + 2 * GLA_QK
LOG2_E = 1.4426950408889634
ATT_Q_SCALE = HEAD_DIM ** -0.5 * LOG2_E

LANES = 128
SUBLANES = 8
HALF = LANES // 2
VMEM_LIMIT = 56 * 1024 * 1024

TOKEN_TILE = 512
FF_SPLITS = (0, 1536, D_FF)
ADA_K_TILE = 128
CTX_SEQS_PER_STEP = 4
LAT_BLOCKS_PER_STEP = 4
GLA_CHUNK = 128
MOD_ROWS = 8


def _params(n_axes):
    return pltpu.CompilerParams(
        dimension_semantics=("arbitrary",) * n_axes, vmem_limit_bytes=VMEM_LIMIT)


def _resident(shape):
    zeros = (0,) * len(shape)
    return pl.BlockSpec(shape, lambda *_: zeros, pipeline_mode=pl.Buffered(1))


def _sigmoid(x):
    return 1.0 / (1.0 + jnp.exp(-x))


def _silu(x):
    return x * _sigmoid(x)


def _rms(x, g):
    return x * lax.rsqrt(jnp.mean(x * x, axis=-1, keepdims=True) + EPS) * g


def _dot(a, b):
    return jnp.dot(a, b, preferred_element_type=F32)


def _dot_t(a, b):
    return lax.dot_general(a, b, (((1,), (1,)), ((), ())), preferred_element_type=F32)


def _ada_kernel(cond_ref, w_ref, b_ref, o_ref):
    @pl.when(pl.program_id(0) == 0)
    def _():
        for m in range(N_MOD):
            o_ref[m] = jnp.broadcast_to(b_ref[:, m * D_MODEL:(m + 1) * D_MODEL],
                                        (MOD_ROWS, D_MODEL))
    part = _dot(_silu(cond_ref[...]).astype(BF16), w_ref[...].astype(BF16))
    for m in range(N_MOD):
        o_ref[m] += part[:, m * D_MODEL:(m + 1) * D_MODEL]


def _ada_modulation(cond, w_ada, b_ada):
    k_dim, n = w_ada.shape
    return pl.pallas_call(
        _ada_kernel,
        out_shape=jax.ShapeDtypeStruct((N_MOD, MOD_ROWS, D_MODEL), F32),
        grid=(k_dim // ADA_K_TILE,),
        in_specs=[
            pl.BlockSpec((MOD_ROWS, ADA_K_TILE), lambda k: (0, k)),
            pl.BlockSpec((ADA_K_TILE, n), lambda k: (k, 0)),
            pl.BlockSpec((1, n), lambda k: (0, 0)),
        ],
        out_specs=pl.BlockSpec((N_MOD, MOD_ROWS, D_MODEL), lambda k: (0, 0, 0)),
        compiler_params=_params(1),
        name="ada_modulation",
    )(cond, w_ada, b_ada)


class _Mod:
    def __init__(self, mod_ref, tiles_per_seq):
        self.ref = mod_ref
        self.row = 0 if tiles_per_seq is None else 1 + pl.program_id(0) // tiles_per_seq

    def __getitem__(self, m):
        return self.ref[m, pl.ds(self.row, 1), :]


def _modulated(x, mod, npre_ref, i):
    return _rms(x, npre_ref[i:i + 1, :]) * (1.0 + mod[3 * i + 1]) + mod[3 * i]


def _residual(x, out, mod, npost_ref, i, weight):
    return x + (weight * mod[3 * i + 2]) * _rms(out, npost_ref[i:i + 1, :])


def _ffn_sublayer(x, i, mod, npre_ref, npost_ref, w1_ref, w3_ref, w2_ref):
    h = _modulated(x, mod, npre_ref, i).astype(BF16)
    acc = None
    for lo, hi in zip(FF_SPLITS[:-1], FF_SPLITS[1:]):
        a = _dot(h, w1_ref[:, lo:hi])
        g = _dot(h, w3_ref[:, lo:hi])
        part = _dot((_silu(a) * g).astype(BF16), w2_ref[lo:hi, :])
        acc = part if acc is None else acc + part
    return _residual(x, acc, mod, npost_ref, i, 0.5)


def _ffn_first_kernel(x_ref, mod_ref, npre_ref, npost_ref, w1_ref, w3_ref, w2_ref, o_ref,
                      *, tiles_per_seq):
    mod = _Mod(mod_ref, tiles_per_seq)
    o_ref[...] = _ffn_sublayer(x_ref[...], 0, mod, npre_ref, npost_ref, w1_ref, w3_ref, w2_ref)


def _mix_ffn_kernel(x_ref, att_ref, gla_ref, wo_ref, mod_ref, npre_ref, npost_ref,
                    w1_ref, w3_ref, w2_ref, o_ref, *, tiles_per_seq):
    mod = _Mod(mod_ref, tiles_per_seq)
    mix = _dot(att_ref[...], wo_ref[0:ATT_Q, :]) + _dot(gla_ref[...], wo_ref[ATT_Q:, :])
    x = _residual(x_ref[...], mix, mod, npost_ref, 1, 1.0)
    o_ref[...] = _ffn_sublayer(x, 2, mod, npre_ref, npost_ref, w1_ref, w3_ref, w2_ref)


def _rope_tile(x, cos, sin_up, sin_dn):
    up = pltpu.roll(x, LANES - 16, axis=1)
    dn = pltpu.roll(x, 16, axis=1)
    return x * cos + up * sin_up + dn * sin_dn


def _proj_kernel(*refs, tiles_per_seq, cache_seq):
    x_ref, mod_ref, npre_ref, win_ref, wup_ref, bup_ref = refs[:6]
    rope_refs = refs[6:9] if tiles_per_seq is not None else None
    outs = refs[6 + (3 if rope_refs else 0):]
    att_ref, glaf_ref, gv_ref = outs[:3]
    h = _modulated(x_ref[...], _Mod(mod_ref, tiles_per_seq), npre_ref, 1).astype(BF16)
    q = _dot(h, win_ref[:, OFF_Q:OFF_K])
    kv = _dot(h, win_ref[:, OFF_K:OFF_GQ])
    k, v = kv[:, :ATT_KV], kv[:, ATT_KV:]
    if rope_refs:
        cos, sup, sdn = (r[...] for r in rope_refs)
        q = jnp.concatenate([_rope_tile(q[:, j * LANES:(j + 1) * LANES], cos, sup, sdn)
                             for j in range(ATT_Q // LANES)], axis=1)
        k_att = _rope_tile(k, cos, sup, sdn)
    else:
        k_att = k
    att_ref[:, A_Q:A_K] = (q * ATT_Q_SCALE).astype(BF16)
    att_ref[:, A_K:A_KS] = k_att.astype(BF16)
    att_ref[:, A_KS:A_V] = pltpu.roll(k_att, HALF, axis=1).astype(BF16)
    att_ref[:, A_V:A_VS] = v.astype(BF16)
    att_ref[:, A_VS:ATT_IN_W] = pltpu.roll(v, HALF, axis=1).astype(BF16)
    if cache_seq is not None:
        for t, dst in ((k.T, outs[3]), (v.T, outs[4])):
            for b in range(TOKEN_TILE // cache_seq):
                dst[b, 0] = t[:, b * cache_seq:(b + 1) * cache_seq].reshape(
                    N_KV_HEADS, HEAD_DIM, cache_seq)
    glaf_ref[:, G_Q:G_G] = _dot(h, win_ref[:, OFF_GQ:OFF_GV])
    gv_ref[...] = _dot(h, win_ref[:, OFF_GV:OFF_GG]).astype(BF16)
    glaf_ref[:, G_G:G_LA] = _dot(h, win_ref[:, OFF_GG:OFF_LR])
    lr = _dot(h, win_ref[:, OFF_LR:IN_WIDTH])
    z = _dot(lr.astype(BF16), wup_ref[...]) + bup_ref[...]
    log_sig = jnp.minimum(z, 0.0) - jnp.log1p(jnp.exp(-jnp.abs(z)))
    glaf_ref[:, G_LA:GLA_F_W] = log_sig * (1.0 / GLA_TAU)


def _token_spec(width):
    return pl.BlockSpec((TOKEN_TILE, width), lambda i: (i, 0))


def _ffn_weight_spec(shape, j):
    return pl.BlockSpec((None,) + tuple(shape[1:]), lambda i: (j, 0, 0),
                        pipeline_mode=pl.Buffered(1))


def _tiles_per_seq(latent_len):
    return None if latent_len is None else latent_len // TOKEN_TILE


def _ffn_first(x, mod, latent_len, npre, npost, w1, w3, w2):
    t = x.shape[0]
    return pl.pallas_call(
        functools.partial(_ffn_first_kernel, tiles_per_seq=_tiles_per_seq(latent_len)),
        out_shape=jax.ShapeDtypeStruct((t, D_MODEL), F32),
        grid=(t // TOKEN_TILE,),
        in_specs=[_token_spec(D_MODEL), _resident(mod.shape), _resident(npre.shape),
                  _resident(npost.shape), _ffn_weight_spec(w1.shape, 0),
                  _ffn_weight_spec(w3.shape, 0), _ffn_weight_spec(w2.shape, 0)],
        out_specs=_token_spec(D_MODEL),
        compiler_params=_params(1),
        name="ffn_first",
    )(x, mod, npre, npost, w1, w3, w2)


def _mix_ffn(x, att, gla, w_out, mod, latent_len, npre, npost, w1, w3, w2):
    t = x.shape[0]
    return pl.pallas_call(
        functools.partial(_mix_ffn_kernel, tiles_per_seq=_tiles_per_seq(latent_len)),
        out_shape=jax.ShapeDtypeStruct((t, D_MODEL), F32),
        grid=(t // TOKEN_TILE,),
        in_specs=[_token_spec(D_MODEL), _token_spec(ATT_Q), _token_spec(GLA_V),
                  _resident(w_out.shape), _resident(mod.shape), _resident(npre.shape),
                  _resident(npost.shape), _ffn_weight_spec(w1.shape, 1),
                  _ffn_weight_spec(w3.shape, 1), _ffn_weight_spec(w2.shape, 1)],
        out_specs=_token_spec(D_MODEL),
        compiler_params=_params(1),
        name="mix_ffn",
    )(x, att, gla, w_out, mod, npre, npost, w1, w3, w2)


def _project(x, mod, latent_len, cache_seq, npre, w_in, w_up, b_up):
    t = x.shape[0]
    tiles = _tiles_per_seq(latent_len)
    in_specs = [_token_spec(D_MODEL), _resident(mod.shape), _resident(npre.shape),
                _resident(w_in.shape), _resident(w_up.shape), _resident(b_up.shape)]
    args = [x, mod, npre, w_in, w_up, b_up]
    if tiles is not None:
        in_specs += [pl.BlockSpec((TOKEN_TILE, LANES), lambda i: (i % tiles, 0))] * 3
        args += [jnp.asarray(tab) for tab in _rope_tables(latent_len)]
    outs = ((ATT_IN_W, BF16), (GLA_F_W, F32), (GLA_V, BF16))
    out_shape = [jax.ShapeDtypeStruct((t, w), dt) for w, dt in outs]
    out_specs = [_token_spec(w) for w, _ in outs]
    if cache_seq is not None:
        seqs = TOKEN_TILE // cache_seq
        cache = (t // cache_seq, 1, N_KV_HEADS, HEAD_DIM, cache_seq)
        out_shape += [jax.ShapeDtypeStruct(cache, F32)] * 2
        out_specs += [pl.BlockSpec((seqs,) + cache[1:], lambda i: (i, 0, 0, 0, 0))] * 2
    return pl.pallas_call(
        functools.partial(_proj_kernel, tiles_per_seq=tiles, cache_seq=cache_seq),
        out_shape=out_shape,
        grid=(t // TOKEN_TILE,),
        in_specs=in_specs,
        out_specs=out_specs,
        compiler_params=_params(1),
        name="project",
    )(*args)


def _rope_tables(seq_len):
    half = HEAD_DIM // 2
    inv_freq = np.float32(ROPE_BASE) ** (-np.arange(0, half, 2, dtype=np.float32) / half)
    pos = np.arange(seq_len)
    row = (pos // GRID_W).astype(np.float32)
    col = (pos % GRID_W).astype(np.float32)
    within = np.arange(LANES) % HEAD_DIM
    idx = within % half
    freq = inv_freq[idx % (half // 2)].astype(np.float32)
    p = np.where((within // half == 0)[None, :], row[:, None], col[:, None])
    ang = (p * freq[None, :]).astype(np.float32)
    cos, sin = np.cos(ang).astype(np.float32), np.sin(ang).astype(np.float32)
    first = (idx < half // 2)[None, :]
    zero = np.float32(0.0)
    return cos, np.where(first, -sin, zero), np.where(first, zero, sin)


def _attend(q_ref, q_rows, nq, segs, sink_ref):
    lane = lax.broadcasted_iota(jnp.int32, (1, LANES), 1)
    half_of = [lane < HALF, lane >= HALF]
    rows = lax.broadcasted_iota(jnp.int32, (2 * nq, 1), 0)
    zero = jnp.zeros((), BF16)
    out = [None] * (ATT_Q // LANES)
    for g in range(N_KV_HEADS):
        tiles = (2 * g, 2 * g + 1)
        for e in range(2):
            qm = jnp.concatenate(
                [jnp.where(half_of[e], q_ref[q_rows, j * LANES:(j + 1) * LANES], zero)
                 for j in tiles], axis=0)
            sink = jnp.where(rows < nq, sink_ref[2 * tiles[0] + e],
                             sink_ref[2 * tiles[1] + e]) * LOG2_E
            scores = []
            for k, k_sw, _, _, m in segs:
                s = _dot_t(qm, k if e == g else k_sw)
                scores.append(s if m is None else jnp.where(m, s, NEG))
            mx = sink
            for s in scores:
                mx = jnp.maximum(mx, jnp.max(s, axis=-1, keepdims=True))
            probs = [jnp.exp2(s - mx) for s in scores]
            den = jnp.exp2(sink - mx)
            for p in probs:
                den = den + jnp.sum(p, axis=-1, keepdims=True)
            o = None
            for p, (_, _, v, v_sw, _) in zip(probs, segs):
                part = _dot(p.astype(BF16), jnp.where(half_of[e], v if e == g else v_sw, zero))
                o = part if o is None else o + part
            o = o * (1.0 / den)
            for r, j in enumerate(tiles):
                blk = o[r * nq:(r + 1) * nq, :]
                out[j] = blk if out[j] is None else out[j] + blk
    return out


def _context_attn_kernel(sink_ref, a_ref, o_ref, *, seq_len):
    for s in range(a_ref.shape[0] // seq_len):
        rows = slice(s * seq_len, (s + 1) * seq_len)
        segs = [(a_ref[rows, A_K:A_KS], a_ref[rows, A_KS:A_V],
                 a_ref[rows, A_V:A_VS], a_ref[rows, A_VS:ATT_IN_W], None)]
        for j, t in enumerate(_attend(a_ref, rows, seq_len, segs, sink_ref)):
            o_ref[rows, j * LANES:(j + 1) * LANES] = t.astype(o_ref.dtype)


def _latent_attn_kernel(sink_ref, q_ref, kv_ref, kc_ref, vc_ref, o_ref, *, seq_len):
    k_ctx, v_ctx = kc_ref[0], vc_ref[0]
    ctx = (k_ctx.astype(BF16), pltpu.roll(k_ctx, HALF, axis=1).astype(BF16),
           v_ctx.astype(BF16), pltpu.roll(v_ctx, HALF, axis=1).astype(BF16), None)
    span = 3 * BLOCK
    per_step = q_ref.shape[0] // BLOCK
    for s in range(per_step):
        i = pl.program_id(1) * per_step + s
        start = pl.multiple_of(jnp.clip((i - 1) * BLOCK, 0, seq_len - span), BLOCK)
        kpos = start + lax.broadcasted_iota(jnp.int32, (1, span), 1)
        qpos = i * BLOCK + lax.broadcasted_iota(jnp.int32, (BLOCK, 1), 0)
        valid = jnp.abs(kpos - qpos) <= WINDOW
        valid2 = jnp.concatenate([valid, valid], axis=0)
        kv = kv_ref[pl.ds(start, span), :]
        segs = [tuple(kv[:, c * ATT_KV:(c + 1) * ATT_KV] for c in range(4)) + (valid2,), ctx]
        rows = slice(s * BLOCK, (s + 1) * BLOCK)
        for j, t in enumerate(_attend(q_ref, rows, BLOCK, segs, sink_ref)):
            o_ref[rows, j * LANES:(j + 1) * LANES] = t.astype(o_ref.dtype)


def _smem_spec():
    return pl.BlockSpec(memory_space=pltpu.SMEM)


def _context_attention(sink, att_in, batch, seq_len):
    rows = CTX_SEQS_PER_STEP * seq_len

    def seq(width):
        return pl.BlockSpec((rows, width), lambda b: (b, 0))
    return pl.pallas_call(
        functools.partial(_context_attn_kernel, seq_len=seq_len),
        out_shape=jax.ShapeDtypeStruct((batch * seq_len, ATT_Q), BF16),
        grid=(batch // CTX_SEQS_PER_STEP,),
        in_specs=[_smem_spec(), seq(ATT_IN_W)],
        out_specs=seq(ATT_Q),
        compiler_params=_params(1),
        name="context_attention",
    )(sink, att_in)


def _latent_attention(sink, att_in, k_ctx, v_ctx, batch, seq_len):
    steps = seq_len // (LAT_BLOCKS_PER_STEP * BLOCK)
    rows = LAT_BLOCKS_PER_STEP * BLOCK
    past = k_ctx.shape[1]
    assert A_K == ATT_IN_W - A_K, "q and the key / value columns are the two halves of att_in"
    return pl.pallas_call(
        functools.partial(_latent_attn_kernel, seq_len=seq_len),
        out_shape=jax.ShapeDtypeStruct((batch * seq_len, ATT_Q), BF16),
        grid=(batch, steps),
        in_specs=[_smem_spec(),
                  pl.BlockSpec((rows, ATT_Q), lambda b, i: (b * steps + i, 0)),
                  pl.BlockSpec((seq_len, ATT_IN_W - A_K), lambda b, i: (b, 1)),
                  pl.BlockSpec((1, past, ATT_KV), lambda b, i: (b, 0, 0)),
                  pl.BlockSpec((1, past, ATT_KV), lambda b, i: (b, 0, 0))],
        out_specs=pl.BlockSpec((rows, ATT_Q), lambda b, i: (b * steps + i, 0)),
        compiler_params=_params(2),
        name="latent_attention",
    )(sink, att_in, att_in, k_ctx, v_ctx)


def _split2(x):
    hi = x.astype(BF16)
    lo = (x - hi.astype(F32)).astype(BF16)
    return hi, lo


def _gla_kernel(*refs, n_chunks, has_s0, emit_state):
    refs = list(refs)
    gf_ref, gv_ref, gn_ref = refs[:3]
    pos = 3
    s0_refs = refs[pos:pos + 2] if has_s0 else None
    pos += 2 if has_s0 else 0
    o_ref = refs[pos]
    pos += 1
    sfin_refs = refs[pos:pos + 2] if emit_state else None
    pos += 2 if emit_state else 0
    cum_ref, kv_ref, sent_ref = refs[pos:]

    C = GLA_CHUNK
    n_pairs = GLA_QK // LANES
    lane = lax.broadcasted_iota(jnp.int32, (1, LANES), 1)
    half_of = [lane < HALF, lane >= HALF]
    r_i = lax.broadcasted_iota(jnp.int32, (C, C), 0)
    c_i = lax.broadcasted_iota(jnp.int32, (C, C), 1)
    lower = c_i <= r_i
    upper = c_i >= r_i
    tri = jnp.concatenate([jnp.where(lower, 1.0, 0.0), jnp.where(upper, 1.0, 0.0)],
                          axis=0).astype(BF16)
    zeros_cc = jnp.zeros((C, C), F32)
    gnorm = gn_ref[...]
    qscale = GLA_DK ** -0.5

    def one_sequence(s):
        def chunk_rows(n):
            return slice((s * n_chunks + n) * C, (s * n_chunks + n + 1) * C)

        for n in range(n_chunks):
            rows = chunk_rows(n)
            hi, lo = _split2(gf_ref[rows, G_LA:GLA_F_W])
            sums = _dot(tri, jnp.concatenate([hi, lo], axis=1))
            cum_f = sums[:C, 0:GLA_QK] + sums[:C, 2 * GLA_QK:3 * GLA_QK]
            cum_b = sums[C:, GLA_QK:2 * GLA_QK] + sums[C:, 3 * GLA_QK:]
            cum_ref[0, rows, :] = cum_f
            cum_ref[1, rows, :] = cum_b
            k = gf_ref[rows, G_K:G_G]
            k_in = (k * jnp.exp(cum_f[C - 1:C, :] - cum_f), k * jnp.exp(cum_b[0:1, :] - cum_b))
            for p in range(n_pairs):
                sl = slice(p * LANES, (p + 1) * LANES)
                kv_t = None
                for e in range(2):
                    h = 2 * p + e
                    v_t = gv_ref[rows, h * GLA_DV:(h + 1) * GLA_DV].T
                    k_e = jnp.concatenate(
                        [jnp.where(half_of[e], k_in[d][:, sl], 0.0) for d in range(2)], axis=1)
                    part = _dot(v_t, k_e.astype(BF16))
                    kv_t = part if kv_t is None else kv_t + part
                kv_ref[s, n, p] = kv_t

        for p in range(n_pairs):
            sl = slice(p * LANES, (p + 1) * LANES)
            st = []
            for d in range(2):
                if has_s0:
                    s0 = s0_refs[d][s, 0, 2 * p:2 * p + 2, :, :].reshape(2 * GLA_DK, GLA_DV)
                    st.append(s0.T)
                else:
                    st.append(jnp.zeros((GLA_DV, 2 * GLA_DK), F32))
            for i in range(n_chunks):
                for d, n in ((0, i), (1, n_chunks - 1 - i)):
                    tot_row = (s * n_chunks + n) * C + (C - 1 if d == 0 else 0)
                    decay = jnp.exp(cum_ref[d, tot_row:tot_row + 1, sl])
                    sent_ref[s, n, p, :, d * LANES:(d + 1) * LANES] = st[d].astype(BF16)
                    st[d] = decay * st[d] + kv_ref[s, n, p, :, d * LANES:(d + 1) * LANES]
            if emit_state:
                for d in range(2):
                    sfin_refs[d][s, 0, 2 * p:2 * p + 2, :, :] = st[d].T.reshape(2, GLA_DK, GLA_DV)

        for n in range(n_chunks):
            rows = chunk_rows(n)
            q = gf_ref[rows, G_Q:G_K] * qscale
            k = gf_ref[rows, G_K:G_G]
            qs, ks, qin = [], [], []
            for d in range(2):
                cum = cum_ref[d, rows, :]
                ref = cum[C // 2:C // 2 + 1, :]
                qs.append(q * jnp.exp(cum - ref))
                ks.append((k * jnp.exp(ref - cum)).astype(BF16))
                qin.append(q * jnp.exp(cum))
            for p in range(n_pairs):
                sl = slice(p * LANES, (p + 1) * LANES)
                lhs = jnp.concatenate(
                    [jnp.where(half_of[e], qs[d][:, sl], 0.0) for e in range(2) for d in range(2)],
                    axis=0).astype(BF16)
                sc = _dot_t(lhs, jnp.concatenate([ks[0][:, sl], ks[1][:, sl]], axis=0))
                prob = []
                for e in range(2):
                    s_f = sc[(2 * e) * C:(2 * e + 1) * C, :C]
                    s_b = sc[(2 * e + 1) * C:(2 * e + 2) * C, C:]
                    prob.append(jnp.where(lower, s_f, 0.0) + jnp.where(upper, s_b, 0.0))
                p_blk = jnp.concatenate(
                    [jnp.concatenate([prob[0], zeros_cc], axis=1),
                     jnp.concatenate([zeros_cc, prob[1]], axis=1)], axis=0).astype(BF16)
                v2 = jnp.concatenate(
                    [gv_ref[rows, (2 * p + e) * GLA_DV:(2 * p + e + 1) * GLA_DV] for e in range(2)],
                    axis=0)
                q_in = jnp.concatenate(
                    [jnp.concatenate([jnp.where(half_of[e], qin[d][:, sl], 0.0) for d in range(2)],
                                     axis=1) for e in range(2)], axis=0).astype(BF16)
                o2 = _dot(p_blk, v2) + _dot_t(q_in, sent_ref[s, n, p])
                for e in range(2):
                    h = 2 * p + e
                    o = o2[e * C:(e + 1) * C, :]
                    o = o * lax.rsqrt(jnp.mean(o * o, axis=-1, keepdims=True) + EPS) * gnorm
                    gate = gf_ref[rows, G_G + h * GLA_DV:G_G + (h + 1) * GLA_DV]
                    o_ref[rows, h * GLA_DV:(h + 1) * GLA_DV] = (o * _silu(gate)).astype(o_ref.dtype)

    for s in range(kv_ref.shape[0]):
        one_sequence(s)


def _gla(gla_f, gv, gnorm, batch, seq_len, seqs_per_step, s0=None, emit_state=False):
    n_chunks = seq_len // GLA_CHUNK
    has_s0 = s0 is not None
    rows = seqs_per_step * seq_len

    def seq(width):
        return pl.BlockSpec((rows, width), lambda b: (b, 0))
    state_spec = pl.BlockSpec((seqs_per_step, 1, GLA_HEADS, GLA_DK, GLA_DV),
                              lambda b: (b, 0, 0, 0, 0))
    in_specs = [seq(GLA_F_W), seq(GLA_V), pl.BlockSpec((1, GLA_DV), lambda b: (0, 0))]
    args = [gla_f, gv, gnorm]
    if has_s0:
        in_specs += [state_spec, state_spec]
        args += list(s0)
    out_shape = [jax.ShapeDtypeStruct((batch * seq_len, GLA_V), BF16)]
    out_specs = [seq(GLA_V)]
    if emit_state:
        out_shape += [jax.ShapeDtypeStruct((batch, 1, GLA_HEADS, GLA_DK, GLA_DV), F32)] * 2
        out_specs += [state_spec, state_spec]
    n_pairs = GLA_QK // LANES
    return pl.pallas_call(
        functools.partial(_gla_kernel, n_chunks=n_chunks, has_s0=has_s0, emit_state=emit_state),
        out_shape=out_shape,
        grid=(batch // seqs_per_step,),
        in_specs=in_specs,
        out_specs=out_specs,
        scratch_shapes=[pltpu.VMEM((2, rows, GLA_QK), F32),
                        pltpu.VMEM((seqs_per_step, n_chunks, n_pairs, GLA_DV, 2 * LANES), F32),
                        pltpu.VMEM((seqs_per_step, n_chunks, n_pairs, GLA_DV, 2 * LANES), BF16)],
        compiler_params=_params(1),
        name="gla",
    )(*args)


def kernel(x_prompt, x_sample, cache_k, cache_v, state_gla_fwd, state_gla_bwd, c, c_ctx,
           w_ada, b_ada, norm_pre, norm_post, ffn_w1, ffn_w3, ffn_w2, w_in,
           gla_w_up, gla_b_up, gla_norm, attn_sink, w_out):
    depth = w_in.shape[0]
    assert depth == 1, "single trunk layer"
    batch, seq = x_prompt.shape[0], x_prompt.shape[1]
    dec_batch, dec_seq = x_sample.shape[0], x_sample.shape[1]
    past = cache_k.shape[2]
    l = 0

    cond = jnp.concatenate(
        [c_ctx[None, :], c, jnp.zeros((MOD_ROWS - 1 - dec_batch, D_MODEL), F32)], axis=0)
    mod = _ada_modulation(cond, w_ada[l], b_ada[l][None, :])

    npre, npost = norm_pre[l], norm_post[l]
    ffn_b = [w[l].astype(BF16) for w in (ffn_w1, ffn_w3, ffn_w2)]
    w_in_b = w_in[l].astype(BF16)
    w_out_b = w_out[l].astype(BF16)
    zeros = jnp.zeros((GLA_LOW_RANK, GLA_QK), F32)
    w_up = jnp.concatenate(
        [jnp.concatenate([gla_w_up[l, 0], zeros], axis=1),
         jnp.concatenate([zeros, gla_w_up[l, 1]], axis=1)], axis=0).astype(BF16)
    b_up = gla_b_up[l].reshape(1, 2 * GLA_QK)
    gnorm = gla_norm[l][None, :]
    sink = attn_sink[l]

    def trunk(x, latent):
        n_batch, n_seq = (dec_batch, dec_seq) if latent else (batch, seq)
        latent_len = n_seq if latent else None
        x1 = _ffn_first(x, mod, latent_len, npre, npost, *ffn_b)
        att_in, gla_f, gv, *cache_t = _project(
            x1, mod, latent_len, None if latent else n_seq, npre, w_in_b, w_up, b_up)
        if latent:
            att = _latent_attention(sink, att_in, cache_k[:, l].reshape(dec_batch, past, ATT_KV),
                                    cache_v[:, l].reshape(dec_batch, past, ATT_KV), n_batch, n_seq)
            (gla,) = _gla(gla_f, gv, gnorm, n_batch, n_seq, 1,
                          s0=(state_gla_fwd[:, l:l + 1], state_gla_bwd[:, l:l + 1]))
            extras = ()
        else:
            att = _context_attention(sink, att_in, n_batch, n_seq)
            gla, s_f, s_b = _gla(gla_f, gv, gnorm, n_batch, n_seq, CTX_SEQS_PER_STEP,
                                 emit_state=True)
            k_new, v_new = (jnp.transpose(c_t, (0, 1, 4, 2, 3)) for c_t in cache_t)
            extras = (k_new, v_new, s_f, s_b)
        y = _mix_ffn(x1, att, gla, w_out_b, mod, latent_len, npre, npost, *ffn_b)
        return y.reshape(n_batch, n_seq, D_MODEL), extras

    y_prompt, (k_new, v_new, s_f, s_b) = trunk(x_prompt.reshape(batch * seq, D_MODEL), False)
    y_sample, _ = trunk(x_sample.reshape(dec_batch * dec_seq, D_MODEL), True)
    return (y_prompt, y_sample, k_new, v_new, s_f, s_b)
```

```python
import functools

import numpy as np
import jax
import jax.numpy as jnp
from jax import lax
from jax.experimental import pallas as pl
from jax.experimental.pallas import tpu as pltpu

F32 = jnp.float32
BF16 = jnp.bfloat16

D_MODEL = 1024
GRID_W = 64
N_Q_HEADS = 8
N_KV_HEADS = 2
HEAD_DIM = 64
WINDOW = 128
BLOCK = 128
ROPE_BASE = 10000.0
GLA_HEADS = 4
GLA_DK = 64
GLA_DV = 128
GLA_LOW_RANK = 16
GLA_TAU = 16.0
D_FF = 2816
N_MOD = 9
EPS = 1e-6
NEG = -1e30

ATT_Q = N_Q_HEADS * HEAD_DIM
ATT_KV = N_KV_HEADS * HEAD_DIM
GLA_QK = GLA_HEADS * GLA_DK
GLA_V = GLA_HEADS * GLA_DV
OFF_Q = 0
OFF_K = OFF_Q + ATT_Q
OFF_V = OFF_K + ATT_KV
OFF_GQ = OFF_V + ATT_KV
OFF_GK = OFF_GQ + GLA_QK
OFF_GV = OFF_GK + GLA_QK
OFF_GG = OFF_GV + GLA_V
OFF_LR = OFF_GG + GLA_V
IN_WIDTH = OFF_LR + 2 * GLA_LOW_RANK
A_Q = 0
A_K = A_Q + ATT_Q
A_KS = A_K + ATT_KV
A_V = A_KS + ATT_KV
A_VS = A_V + ATT_KV
ATT_IN_W = A_VS + ATT_KV
G_Q = 0
G_K = G_Q + GLA_QK
G_G = G_K + GLA_QK
G_LA = G_G + GLA_V
GLA_F_W = G_LA + 2 * GLA_QK
LOG2_E = 1.4426950408889634
ATT_Q_SCALE = HEAD_DIM ** -0.5 * LOG2_E

LANES = 128
SUBLANES = 8
HALF = LANES // 2
VMEM_LIMIT = 56 * 1024 * 1024

TOKEN_TILE = 512
FF_SPLITS = (0, 1536, D_FF)
ADA_K_TILE = 128
CTX_SEQS_PER_STEP = 4
LAT_BLOCKS_PER_STEP = 4
GLA_CHUNK = 128
MOD_ROWS = 8


def _params(n_axes):
    return pltpu.CompilerParams(
        dimension_semantics=("arbitrary",) * n_axes, vmem_limit_bytes=VMEM_LIMIT)


def _resident(shape):
    zeros = (0,) * len(shape)
    return pl.BlockSpec(shape, lambda *_: zeros, pipeline_mode=pl.Buffered(1))


def _sigmoid(x):
    return 1.0 / (1.0 + jnp.exp(-x))


def _silu(x):
    return x * _sigmoid(x)


def _rms(x, g):
    return x * lax.rsqrt(jnp.mean(x * x, axis=-1, keepdims=True) + EPS) * g


def _dot(a, b):
    return jnp.dot(a, b, preferred_element_type=F32)


def _dot_t(a, b):
    return lax.dot_general(a, b, (((1,), (1,)), ((), ())), preferred_element_type=F32)


def _ada_kernel(cond_ref, w_ref, b_ref, o_ref):
    @pl.when(pl.program_id(0) == 0)
    def _():
        for m in range(N_MOD):
            o_ref[m] = jnp.broadcast_to(b_ref[:, m * D_MODEL:(m + 1) * D_MODEL],
                                        (MOD_ROWS, D_MODEL))
    part = _dot(_silu(cond_ref[...]).astype(BF16), w_ref[...].astype(BF16))
    for m in range(N_MOD):
        o_ref[m] += part[:, m * D_MODEL:(m + 1) * D_MODEL]


def _ada_modulation(cond, w_ada, b_ada):
    k_dim, n = w_ada.shape
    return pl.pallas_call(
        _ada_kernel,
        out_shape=jax.ShapeDtypeStruct((N_MOD, MOD_ROWS, D_MODEL), F32),
        grid=(k_dim // ADA_K_TILE,),
        in_specs=[
            pl.BlockSpec((MOD_ROWS, ADA_K_TILE), lambda k: (0, k)),
            pl.BlockSpec((ADA_K_TILE, n), lambda k: (k, 0)),
            pl.BlockSpec((1, n), lambda k: (0, 0)),
        ],
        out_specs=pl.BlockSpec((N_MOD, MOD_ROWS, D_MODEL), lambda k: (0, 0, 0)),
        compiler_params=_params(1),
        name="ada_modulation",
    )(cond, w_ada, b_ada)


class _Mod:
    def __init__(self, mod_ref, tiles_per_seq):
        self.ref = mod_ref
        self.row = 0 if tiles_per_seq is None else 1 + pl.program_id(0) // tiles_per_seq

    def __getitem__(self, m):
        return self.ref[m, pl.ds(self.row, 1), :]


def _modulated(x, mod, npre_ref, i):
    return _rms(x, npre_ref[i:i + 1, :]) * (1.0 + mod[3 * i + 1]) + mod[3 * i]


def _residual(x, out, mod, npost_ref, i, weight):
    return x + (weight * mod[3 * i + 2]) * _rms(out, npost_ref[i:i + 1, :])


def _ffn_sublayer(x, i, mod, npre_ref, npost_ref, w1_ref, w3_ref, w2_ref):
    h = _modulated(x, mod, npre_ref, i).astype(BF16)
    acc = None
    for lo, hi in zip(FF_SPLITS[:-1], FF_SPLITS[1:]):
        a = _dot(h, w1_ref[:, lo:hi])
        g = _dot(h, w3_ref[:, lo:hi])
        part = _dot((_silu(a) * g).astype(BF16), w2_ref[lo:hi, :])
        acc = part if acc is None else acc + part
    return _residual(x, acc, mod, npost_ref, i, 0.5)


def _ffn_first_kernel(x_ref, mod_ref, npre_ref, npost_ref, w1_ref, w3_ref, w2_ref, o_ref,
                      *, tiles_per_seq):
    mod = _Mod(mod_ref, tiles_per_seq)
    o_ref[...] = _ffn_sublayer(x_ref[...], 0, mod, npre_ref, npost_ref, w1_ref, w3_ref, w2_ref)


def _mix_ffn_kernel(x_ref, att_ref, gla_ref, wo_ref, mod_ref, npre_ref, npost_ref,
                    w1_ref, w3_ref, w2_ref, o_ref, *, tiles_per_seq):
    mod = _Mod(mod_ref, tiles_per_seq)
    mix = _dot(att_ref[...], wo_ref[0:ATT_Q, :]) + _dot(gla_ref[...], wo_ref[ATT_Q:, :])
    x = _residual(x_ref[...], mix, mod, npost_ref, 1, 1.0)
    o_ref[...] = _ffn_sublayer(x, 2, mod, npre_ref, npost_ref, w1_ref, w3_ref, w2_ref)


def _rope_tile(x, cos, sin_up, sin_dn):
    up = pltpu.roll(x, LANES - 16, axis=1)
    dn = pltpu.roll(x, 16, axis=1)
    return x * cos + up * sin_up + dn * sin_dn


def _proj_kernel(*refs, tiles_per_seq, cache_seq):
    x_ref, mod_ref, npre_ref, win_ref, wup_ref, bup_ref = refs[:6]
    rope_refs = refs[6:9] if tiles_per_seq is not None else None
    outs = refs[6 + (3 if rope_refs else 0):]
    att_ref, glaf_ref, gv_ref = outs[:3]
    h = _modulated(x_ref[...], _Mod(mod_ref, tiles_per_seq), npre_ref, 1).astype(BF16)
    q = _dot(h, win_ref[:, OFF_Q:OFF_K])
    kv = _dot(h, win_ref[:, OFF_K:OFF_GQ])
    k, v = kv[:, :ATT_KV], kv[:, ATT_KV:]
    if rope_refs:
        cos, sup, sdn = (r[...] for r in rope_refs)
        q = jnp.concatenate([_rope_tile(q[:, j * LANES:(j + 1) * LANES], cos, sup, sdn)
                             for j in range(ATT_Q // LANES)], axis=1)
        k_att = _rope_tile(k, cos, sup, sdn)
    else:
        k_att = k
    att_ref[:, A_Q:A_K] = (q * ATT_Q_SCALE).astype(BF16)
    att_ref[:, A_K:A_KS] = k_att.astype(BF16)
    att_ref[:, A_KS:A_V] = pltpu.roll(k_att, HALF, axis=1).astype(BF16)
    att_ref[:, A_V:A_VS] = v.astype(BF16)
    att_ref[:, A_VS:ATT_IN_W] = pltpu.roll(v, HALF, axis=1).astype(BF16)
    if cache_seq is not None:
        for t, dst in ((k.T, outs[3]), (v.T, outs[4])):
            for b in range(TOKEN_TILE // cache_seq):
                dst[b, 0] = t[:, b * cache_seq:(b + 1) * cache_seq].reshape(
                    N_KV_HEADS, HEAD_DIM, cache_seq)
    glaf_ref[:, G_Q:G_G] = _dot(h, win_ref[:, OFF_GQ:OFF_GV])
    gv_ref[...] = _dot(h, win_ref[:, OFF_GV:OFF_GG]).astype(BF16)
    glaf_ref[:, G_G:G_LA] = _dot(h, win_ref[:, OFF_GG:OFF_LR])
    lr = _dot(h, win_ref[:, OFF_LR:IN_WIDTH])
    z = _dot(lr.astype(BF16), wup_ref[...]) + bup_ref[...]
    log_sig = jnp.minimum(z, 0.0) - jnp.log1p(jnp.exp(-jnp.abs(z)))
    glaf_ref[:, G_LA:GLA_F_W] = log_sig * (1.0 / GLA_TAU)


def _token_spec(width):
    return pl.BlockSpec((TOKEN_TILE, width), lambda i: (i, 0))


def _ffn_weight_spec(shape, j):
    return pl.BlockSpec((None,) + tuple(shape[1:]), lambda i: (j, 0, 0),
                        pipeline_mode=pl.Buffered(1))


def _tiles_per_seq(latent_len):
    return None if latent_len is None else latent_len // TOKEN_TILE


def _ffn_first(x, mod, latent_len, npre, npost, w1, w3, w2):
    t = x.shape[0]
    return pl.pallas_call(
        functools.partial(_ffn_first_kernel, tiles_per_seq=_tiles_per_seq(latent_len)),
        out_shape=jax.ShapeDtypeStruct((t, D_MODEL), F32),
        grid=(t // TOKEN_TILE,),
        in_specs=[_token_spec(D_MODEL), _resident(mod.shape), _resident(npre.shape),
                  _resident(npost.shape), _ffn_weight_spec(w1.shape, 0),
                  _ffn_weight_spec(w3.shape, 0), _ffn_weight_spec(w2.shape, 0)],
        out_specs=_token_spec(D_MODEL),
        compiler_params=_params(1),
        name="ffn_first",
    )(x, mod, npre, npost, w1, w3, w2)


def _mix_ffn(x, att, gla, w_out, mod, latent_len, npre, npost, w1, w3, w2):
    t = x.shape[0]
    return pl.pallas_call(
        functools.partial(_mix_ffn_kernel, tiles_per_seq=_tiles_per_seq(latent_len)),
        out_shape=jax.ShapeDtypeStruct((t, D_MODEL), F32),
        grid=(t // TOKEN_TILE,),
        in_specs=[_token_spec(D_MODEL), _token_spec(ATT_Q), _token_spec(GLA_V),
                  _resident(w_out.shape), _resident(mod.shape), _resident(npre.shape),
                  _resident(npost.shape), _ffn_weight_spec(w1.shape, 1),
                  _ffn_weight_spec(w3.shape, 1), _ffn_weight_spec(w2.shape, 1)],
        out_specs=_token_spec(D_MODEL),
        compiler_params=_params(1),
        name="mix_ffn",
    )(x, att, gla, w_out, mod, npre, npost, w1, w3, w2)


def _project(x, mod, latent_len, cache_seq, npre, w_in, w_up, b_up):
    t = x.shape[0]
    tiles = _tiles_per_seq(latent_len)
    in_specs = [_token_spec(D_MODEL), _resident(mod.shape), _resident(npre.shape),
                _resident(w_in.shape), _resident(w_up.shape), _resident(b_up.shape)]
    args = [x, mod, npre, w_in, w_up, b_up]
    if tiles is not None:
        in_specs += [pl.BlockSpec((TOKEN_TILE, LANES), lambda i: (i % tiles, 0))] * 3
        args += [jnp.asarray(tab) for tab in _rope_tables(latent_len)]
    outs = ((ATT_IN_W, BF16), (GLA_F_W, F32), (GLA_V, BF16))
    out_shape = [jax.ShapeDtypeStruct((t, w), dt) for w, dt in outs]
    out_specs = [_token_spec(w) for w, _ in outs]
    if cache_seq is not None:
        seqs = TOKEN_TILE // cache_seq
        cache = (t // cache_seq, 1, N_KV_HEADS, HEAD_DIM, cache_seq)
        out_shape += [jax.ShapeDtypeStruct(cache, F32)] * 2
        out_specs += [pl.BlockSpec((seqs,) + cache[1:], lambda i: (i, 0, 0, 0, 0))] * 2
    return pl.pallas_call(
        functools.partial(_proj_kernel, tiles_per_seq=tiles, cache_seq=cache_seq),
        out_shape=out_shape,
        grid=(t // TOKEN_TILE,),
        in_specs=in_specs,
        out_specs=out_specs,
        compiler_params=_params(1),
        name="project",
    )(*args)


def _rope_tables(seq_len):
    half = HEAD_DIM // 2
    inv_freq = np.float32(ROPE_BASE) ** (-np.arange(0, half, 2, dtype=np.float32) / half)
    pos = np.arange(seq_len)
    row = (pos // GRID_W).astype(np.float32)
    col = (pos % GRID_W).astype(np.float32)
    within = np.arange(LANES) % HEAD_DIM
    idx = within % half
    freq = inv_freq[idx % (half // 2)].astype(np.float32)
    p = np.where((within // half == 0)[None, :], row[:, None], col[:, None])
    ang = (p * freq[None, :]).astype(np.float32)
    cos, sin = np.cos(ang).astype(np.float32), np.sin(ang).astype(np.float32)
    first = (idx < half // 2)[None, :]
    zero = np.float32(0.0)
    return cos, np.where(first, -sin, zero), np.where(first, zero, sin)


def _attend(q_ref, q_rows, nq, segs, sink_ref):
    lane = lax.broadcasted_iota(jnp.int32, (1, LANES), 1)
    half_of = [lane < HALF, lane >= HALF]
    rows = lax.broadcasted_iota(jnp.int32, (2 * nq, 1), 0)
    zero = jnp.zeros((), BF16)
    out = [None] * (ATT_Q // LANES)
    for g in range(N_KV_HEADS):
        tiles = (2 * g, 2 * g + 1)
        for e in range(2):
            qm = jnp.concatenate(
                [jnp.where(half_of[e], q_ref[q_rows, j * LANES:(j + 1) * LANES], zero)
                 for j in tiles], axis=0)
            sink = jnp.where(rows < nq, sink_ref[2 * tiles[0] + e],
                             sink_ref[2 * tiles[1] + e]) * LOG2_E
            scores = []
            for k, k_sw, _, _, m in segs:
                s = _dot_t(qm, k if e == g else k_sw)
                scores.append(s if m is None else jnp.where(m, s, NEG))
            mx = sink
            for s in scores:
                mx = jnp.maximum(mx, jnp.max(s, axis=-1, keepdims=True))
            probs = [jnp.exp2(s - mx) for s in scores]
            den = jnp.exp2(sink - mx)
            for p in probs:
                den = den + jnp.sum(p, axis=-1, keepdims=True)
            o = None
            for p, (_, _, v, v_sw, _) in zip(probs, segs):
                part = _dot(p.astype(BF16), jnp.where(half_of[e], v if e == g else v_sw, zero))
                o = part if o is None else o + part
            o = o * (1.0 / den)
            for r, j in enumerate(tiles):
                blk = o[r * nq:(r + 1) * nq, :]
                out[j] = blk if out[j] is None else out[j] + blk
    return out


def _context_attn_kernel(sink_ref, a_ref, o_ref, *, seq_len):
    for s in range(a_ref.shape[0] // seq_len):
        rows = slice(s * seq_len, (s + 1) * seq_len)
        segs = [(a_ref[rows, A_K:A_KS], a_ref[rows, A_KS:A_V],
                 a_ref[rows, A_V:A_VS], a_ref[rows, A_VS:ATT_IN_W], None)]
        for j, t in enumerate(_attend(a_ref, rows, seq_len, segs, sink_ref)):
            o_ref[rows, j * LANES:(j + 1) * LANES] = t.astype(o_ref.dtype)


def _latent_attn_kernel(sink_ref, q_ref, kv_ref, kc_ref, vc_ref, o_ref, *, seq_len):
    k_ctx, v_ctx = kc_ref[0], vc_ref[0]
    ctx = (k_ctx.astype(BF16), pltpu.roll(k_ctx, HALF, axis=1).astype(BF16),
           v_ctx.astype(BF16), pltpu.roll(v_ctx, HALF, axis=1).astype(BF16), None)
    span = 3 * BLOCK
    per_step = q_ref.shape[0] // BLOCK
    for s in range(per_step):
        i = pl.program_id(1) * per_step + s
        start = pl.multiple_of(jnp.clip((i - 1) * BLOCK, 0, seq_len - span), BLOCK)
        kpos = start + lax.broadcasted_iota(jnp.int32, (1, span), 1)
        qpos = i * BLOCK + lax.broadcasted_iota(jnp.int32, (BLOCK, 1), 0)
        valid = jnp.abs(kpos - qpos) <= WINDOW
        valid2 = jnp.concatenate([valid, valid], axis=0)
        kv = kv_ref[pl.ds(start, span), :]
        segs = [tuple(kv[:, c * ATT_KV:(c + 1) * ATT_KV] for c in range(4)) + (valid2,), ctx]
        rows = slice(s * BLOCK, (s + 1) * BLOCK)
        for j, t in enumerate(_attend(q_ref, rows, BLOCK, segs, sink_ref)):
            o_ref[rows, j * LANES:(j + 1) * LANES] = t.astype(o_ref.dtype)


def _smem_spec():
    return pl.BlockSpec(memory_space=pltpu.SMEM)


def _context_attention(sink, att_in, batch, seq_len):
    rows = CTX_SEQS_PER_STEP * seq_len

    def seq(width):
        return pl.BlockSpec((rows, width), lambda b: (b, 0))
    return pl.pallas_call(
        functools.partial(_context_attn_kernel, seq_len=seq_len),
        out_shape=jax.ShapeDtypeStruct((batch * seq_len, ATT_Q), BF16),
        grid=(batch // CTX_SEQS_PER_STEP,),
        in_specs=[_smem_spec(), seq(ATT_IN_W)],
        out_specs=seq(ATT_Q),
        compiler_params=_params(1),
        name="context_attention",
    )(sink, att_in)


def _latent_attention(sink, att_in, k_ctx, v_ctx, batch, seq_len):
    steps = seq_len // (LAT_BLOCKS_PER_STEP * BLOCK)
    rows = LAT_BLOCKS_PER_STEP * BLOCK
    past = k_ctx.shape[1]
    assert A_K == ATT_IN_W - A_K, "q and the key / value columns are the two halves of att_in"
    return pl.pallas_call(
        functools.partial(_latent_attn_kernel, seq_len=seq_len),
        out_shape=jax.ShapeDtypeStruct((batch * seq_len, ATT_Q), BF16),
        grid=(batch, steps),
        in_specs=[_smem_spec(),
                  pl.BlockSpec((rows, ATT_Q), lambda b, i: (b * steps + i, 0)),
                  pl.BlockSpec((seq_len, ATT_IN_W - A_K), lambda b, i: (b, 1)),
                  pl.BlockSpec((1, past, ATT_KV), lambda b, i: (b, 0, 0)),
                  pl.BlockSpec((1, past, ATT_KV), lambda b, i: (b, 0, 0))],
        out_specs=pl.BlockSpec((rows, ATT_Q), lambda b, i: (b * steps + i, 0)),
        compiler_params=_params(2),
        name="latent_attention",
    )(sink, att_in, att_in, k_ctx, v_ctx)


def _split2(x):
    hi = x.astype(BF16)
    lo = (x - hi.astype(F32)).astype(BF16)
    return hi, lo


def _gla_kernel(*refs, n_chunks, has_s0, emit_state):
    refs = list(refs)
    gf_ref, gv_ref, gn_ref = refs[:3]
    pos = 3
    s0_refs = refs[pos:pos + 2] if has_s0 else None
    pos += 2 if has_s0 else 0
    o_ref = refs[pos]
    pos += 1
    sfin_refs = refs[pos:pos + 2] if emit_state else None
    pos += 2 if emit_state else 0
    cum_ref, kv_ref, sent_ref = refs[pos:]

    C = GLA_CHUNK
    n_pairs = GLA_QK // LANES
    lane = lax.broadcasted_iota(jnp.int32, (1, LANES), 1)
    half_of = [lane < HALF, lane >= HALF]
    r_i = lax.broadcasted_iota(jnp.int32, (C, C), 0)
    c_i = lax.broadcasted_iota(jnp.int32, (C, C), 1)
    lower = c_i <= r_i
    upper = c_i >= r_i
    tri = jnp.concatenate([jnp.where(lower, 1.0, 0.0), jnp.where(upper, 1.0, 0.0)],
                          axis=0).astype(BF16)
    zeros_cc = jnp.zeros((C, C), F32)
    head0_lanes = jnp.concatenate([half_of[0], half_of[0]], axis=1)
    gnorm = gn_ref[...]
    qscale = GLA_DK ** -0.5

    def one_sequence(s):
        def chunk_rows(n):
            return slice((s * n_chunks + n) * C, (s * n_chunks + n + 1) * C)

        for n in range(n_chunks):
            rows = chunk_rows(n)
            hi, lo = _split2(gf_ref[rows, G_LA:GLA_F_W])
            sums = _dot(tri, jnp.concatenate([hi, lo], axis=1))
            cum_f = sums[:C, 0:GLA_QK] + sums[:C, 2 * GLA_QK:3 * GLA_QK]
            cum_b = sums[C:, GLA_QK:2 * GLA_QK] + sums[C:, 3 * GLA_QK:]
            cum_ref[0, rows, :] = cum_f
            cum_ref[1, rows, :] = cum_b
            k = gf_ref[rows, G_K:G_G]
            k_in = (k * jnp.exp(cum_f[C - 1:C, :] - cum_f), k * jnp.exp(cum_b[0:1, :] - cum_b))
            for p in range(n_pairs):
                sl = slice(p * LANES, (p + 1) * LANES)
                v_t = jnp.concatenate(
                    [gv_ref[rows, h * GLA_DV:(h + 1) * GLA_DV].T for h in (2 * p, 2 * p + 1)],
                    axis=0)
                k_both = jnp.concatenate([k_in[0][:, sl], k_in[1][:, sl]], axis=1).astype(BF16)
                prod = _dot(v_t, k_both)
                kv_ref[s, n, p] = jnp.where(head0_lanes, prod[:GLA_DV], prod[GLA_DV:])

        for p in range(n_pairs):
            sl = slice(p * LANES, (p + 1) * LANES)
            st = []
            for d in range(2):
                if has_s0:
                    s0 = s0_refs[d][s, 0, 2 * p:2 * p + 2, :, :].reshape(2 * GLA_DK, GLA_DV)
                    st.append(s0.T)
                else:
                    st.append(jnp.zeros((GLA_DV, 2 * GLA_DK), F32))
            for i in range(n_chunks):
                for d, n in ((0, i), (1, n_chunks - 1 - i)):
                    tot_row = (s * n_chunks + n) * C + (C - 1 if d == 0 else 0)
                    decay = jnp.exp(cum_ref[d, tot_row:tot_row + 1, sl])
                    sent_ref[s, n, p, :, d * LANES:(d + 1) * LANES] = st[d].astype(BF16)
                    st[d] = decay * st[d] + kv_ref[s, n, p, :, d * LANES:(d + 1) * LANES]
            if emit_state:
                for d in range(2):
                    sfin_refs[d][s, 0, 2 * p:2 * p + 2, :, :] = st[d].T.reshape(2, GLA_DK, GLA_DV)

        for n in range(n_chunks):
            rows = chunk_rows(n)
            q = gf_ref[rows, G_Q:G_K] * qscale
            k = gf_ref[rows, G_K:G_G]
            qs, ks, qin = [], [], []
            for d in range(2):
                cum = cum_ref[d, rows, :]
                ref = cum[C // 2:C // 2 + 1, :]
                qs.append(q * jnp.exp(cum - ref))
                ks.append((k * jnp.exp(ref - cum)).astype(BF16))
                qin.append(q * jnp.exp(cum))
            for p in range(n_pairs):
                sl = slice(p * LANES, (p + 1) * LANES)
                lhs = jnp.concatenate(
                    [jnp.where(half_of[e], qs[d][:, sl], 0.0) for e in range(2) for d in range(2)],
                    axis=0).astype(BF16)
                sc = _dot_t(lhs, jnp.concatenate([ks[0][:, sl], ks[1][:, sl]], axis=0))
                prob = []
                for e in range(2):
                    s_f = sc[(2 * e) * C:(2 * e + 1) * C, :C]
                    s_b = sc[(2 * e + 1) * C:(2 * e + 2) * C, C:]
                    prob.append(jnp.where(lower, s_f, 0.0) + jnp.where(upper, s_b, 0.0))
                p_blk = jnp.concatenate(
                    [jnp.concatenate([prob[0], zeros_cc], axis=1),
                     jnp.concatenate([zeros_cc, prob[1]], axis=1)], axis=0).astype(BF16)
                v2 = jnp.concatenate(
                    [gv_ref[rows, (2 * p + e) * GLA_DV:(2 * p + e + 1) * GLA_DV] for e in range(2)],
                    axis=0)
                q_in = jnp.concatenate(
                    [jnp.concatenate([jnp.where(half_of[e], qin[d][:, sl], 0.0) for d in range(2)],
                                     axis=1) for e in range(2)], axis=0).astype(BF16)
                o2 = _dot(p_blk, v2) + _dot_t(q_in, sent_ref[s, n, p])
                for e in range(2):
                    h = 2 * p + e
                    o = o2[e * C:(e + 1) * C, :]
                    o = o * lax.rsqrt(jnp.mean(o * o, axis=-1, keepdims=True) + EPS) * gnorm
                    gate = gf_ref[rows, G_G + h * GLA_DV:G_G + (h + 1) * GLA_DV]
                    o_ref[rows, h * GLA_DV:(h + 1) * GLA_DV] = (o * _silu(gate)).astype(o_ref.dtype)

    for s in range(kv_ref.shape[0]):
        one_sequence(s)


def _gla(gla_f, gv, gnorm, batch, seq_len, seqs_per_step, s0=None, emit_state=False):
    n_chunks = seq_len // GLA_CHUNK
    has_s0 = s0 is not None
    rows = seqs_per_step * seq_len

    def seq(width):
        return pl.BlockSpec((rows, width), lambda b: (b, 0))
    state_spec = pl.BlockSpec((seqs_per_step, 1, GLA_HEADS, GLA_DK, GLA_DV),
                              lambda b: (b, 0, 0, 0, 0))
    in_specs = [seq(GLA_F_W), seq(GLA_V), pl.BlockSpec((1, GLA_DV), lambda b: (0, 0))]
    args = [gla_f, gv, gnorm]
    if has_s0:
        in_specs += [state_spec, state_spec]
        args += list(s0)
    out_shape = [jax.ShapeDtypeStruct((batch * seq_len, GLA_V), BF16)]
    out_specs = [seq(GLA_V)]
    if emit_state:
        out_shape += [jax.ShapeDtypeStruct((batch, 1, GLA_HEADS, GLA_DK, GLA_DV), F32)] * 2
        out_specs += [state_spec, state_spec]
    n_pairs = GLA_QK // LANES
    return pl.pallas_call(
        functools.partial(_gla_kernel, n_chunks=n_chunks, has_s0=has_s0, emit_state=emit_state),
        out_shape=out_shape,
        grid=(batch // seqs_per_step,),
        in_specs=in_specs,
        out_specs=out_specs,
        scratch_shapes=[pltpu.VMEM((2, rows, GLA_QK), F32),
                        pltpu.VMEM((seqs_per_step, n_chunks, n_pairs, GLA_DV, 2 * LANES), F32),
                        pltpu.VMEM((seqs_per_step, n_chunks, n_pairs, GLA_DV, 2 * LANES), BF16)],
        compiler_params=_params(1),
        name="gla",
    )(*args)


def kernel(x_prompt, x_sample, cache_k, cache_v, state_gla_fwd, state_gla_bwd, c, c_ctx,
           w_ada, b_ada, norm_pre, norm_post, ffn_w1, ffn_w3, ffn_w2, w_in,
           gla_w_up, gla_b_up, gla_norm, attn_sink, w_out):
    depth = w_in.shape[0]
    assert depth == 1, "single trunk layer"
    batch, seq = x_prompt.shape[0], x_prompt.shape[1]
    dec_batch, dec_seq = x_sample.shape[0], x_sample.shape[1]
    past = cache_k.shape[2]
    l = 0

    cond = jnp.concatenate(
        [c_ctx[None, :], c, jnp.zeros((MOD_ROWS - 1 - dec_batch, D_MODEL), F32)], axis=0)
    mod = _ada_modulation(cond, w_ada[l], b_ada[l][None, :])

    npre, npost = norm_pre[l], norm_post[l]
    ffn_b = [w[l].astype(BF16) for w in (ffn_w1, ffn_w3, ffn_w2)]
    w_in_b = w_in[l].astype(BF16)
    w_out_b = w_out[l].astype(BF16)
    zeros = jnp.zeros((GLA_LOW_RANK, GLA_QK), F32)
    w_up = jnp.concatenate(
        [jnp.concatenate([gla_w_up[l, 0], zeros], axis=1),
         jnp.concatenate([zeros, gla_w_up[l, 1]], axis=1)], axis=0).astype(BF16)
    b_up = gla_b_up[l].reshape(1, 2 * GLA_QK)
    gnorm = gla_norm[l][None, :]
    sink = attn_sink[l]

    def trunk(x, latent):
        n_batch, n_seq = (dec_batch, dec_seq) if latent else (batch, seq)
        latent_len = n_seq if latent else None
        x1 = _ffn_first(x, mod, latent_len, npre, npost, *ffn_b)
        att_in, gla_f, gv, *cache_t = _project(
            x1, mod, latent_len, None if latent else n_seq, npre, w_in_b, w_up, b_up)
        if latent:
            att = _latent_attention(sink, att_in, cache_k[:, l].reshape(dec_batch, past, ATT_KV),
                                    cache_v[:, l].reshape(dec_batch, past, ATT_KV), n_batch, n_seq)
            (gla,) = _gla(gla_f, gv, gnorm, n_batch, n_seq, 1,
                          s0=(state_gla_fwd[:, l:l + 1], state_gla_bwd[:, l:l + 1]))
            extras = ()
        else:
            att = _context_attention(sink, att_in, n_batch, n_seq)
            gla, s_f, s_b = _gla(gla_f, gv, gnorm, n_batch, n_seq, CTX_SEQS_PER_STEP,
                                 emit_state=True)
            k_new, v_new = (jnp.transpose(c_t, (0, 1, 4, 2, 3)) for c_t in cache_t)
            extras = (k_new, v_new, s_f, s_b)
        y = _mix_ffn(x1, att, gla, w_out_b, mod, latent_len, npre, npost, *ffn_b)
        return y.reshape(n_batch, n_seq, D_MODEL), extras

    y_prompt, (k_new, v_new, s_f, s_b) = trunk(x_prompt.reshape(batch * seq, D_MODEL), False)
    y_sample, _ = trunk(x_sample.reshape(dec_batch * dec_seq, D_MODEL), True)
    return (y_prompt, y_sample, k_new, v_new, s_f, s_b)
```

```python
import functools

import numpy as np
import jax
import jax.numpy as jnp
from jax import lax
from jax.experimental import pallas as pl
from jax.experimental.pallas import tpu as pltpu

F32 = jnp.float32
BF16 = jnp.bfloat16

D_MODEL = 1024
GRID_W = 64
N_Q_HEADS = 8
N_KV_HEADS = 2
HEAD_DIM = 64
WINDOW = 128
BLOCK = 128
ROPE_BASE = 10000.0
GLA_HEADS = 4
GLA_DK = 64
GLA_DV = 128
GLA_LOW_RANK = 16
GLA_TAU = 16.0
D_FF = 2816
N_MOD = 9
EPS = 1e-6
NEG = -1e30

ATT_Q = N_Q_HEADS * HEAD_DIM
ATT_KV = N_KV_HEADS * HEAD_DIM
GLA_QK = GLA_HEADS * GLA_DK
GLA_V = GLA_HEADS * GLA_DV
OFF_Q = 0
OFF_K = OFF_Q + ATT_Q
OFF_V = OFF_K + ATT_KV
OFF_GQ = OFF_V + ATT_KV
OFF_GK = OFF_GQ + GLA_QK
OFF_GV = OFF_GK + GLA_QK
OFF_GG = OFF_GV + GLA_V
OFF_LR = OFF_GG + GLA_V
IN_WIDTH = OFF_LR + 2 * GLA_LOW_RANK
A_Q = 0
A_K = A_Q + ATT_Q
A_KS = A_K + ATT_KV
A_V = A_KS + ATT_KV
A_VS = A_V + ATT_KV
ATT_IN_W = A_VS + ATT_KV
G_Q = 0
G_K = G_Q + GLA_QK
G_G = G_K + GLA_QK
G_LA = G_G + GLA_V
GLA_F_W = G_LA + 2 * GLA_QK
LOG2_E = 1.4426950408889634
ATT_Q_SCALE = HEAD_DIM ** -0.5 * LOG2_E

LANES = 128
SUBLANES = 8
HALF = LANES // 2
VMEM_LIMIT = 56 * 1024 * 1024

TOKEN_TILE = 512
FF_SPLITS = (0, 1536, D_FF)
ADA_K_TILE = 128
CTX_SEQS_PER_STEP = 4
LAT_BLOCKS_PER_STEP = 4
GLA_CHUNK = 128
MOD_ROWS = 8


def _params(n_axes):
    return pltpu.CompilerParams(
        dimension_semantics=("arbitrary",) * n_axes, vmem_limit_bytes=VMEM_LIMIT)


def _resident(shape):
    zeros = (0,) * len(shape)
    return pl.BlockSpec(shape, lambda *_: zeros, pipeline_mode=pl.Buffered(1))


def _sigmoid(x):
    return 1.0 / (1.0 + jnp.exp(-x))


def _silu(x):
    return x * _sigmoid(x)


def _rms(x, g):
    return x * lax.rsqrt(jnp.mean(x * x, axis=-1, keepdims=True) + EPS) * g


def _dot(a, b):
    return jnp.dot(a, b, preferred_element_type=F32)


def _dot_t(a, b):
    return lax.dot_general(a, b, (((1,), (1,)), ((), ())), preferred_element_type=F32)


def _ada_kernel(cond_ref, w_ref, b_ref, o_ref):
    @pl.when(pl.program_id(0) == 0)
    def _():
        for m in range(N_MOD):
            o_ref[m] = jnp.broadcast_to(b_ref[:, m * D_MODEL:(m + 1) * D_MODEL],
                                        (MOD_ROWS, D_MODEL))
    part = _dot(_silu(cond_ref[...]).astype(BF16), w_ref[...].astype(BF16))
    for m in range(N_MOD):
        o_ref[m] += part[:, m * D_MODEL:(m + 1) * D_MODEL]


def _ada_modulation(cond, w_ada, b_ada):
    k_dim, n = w_ada.shape
    return pl.pallas_call(
        _ada_kernel,
        out_shape=jax.ShapeDtypeStruct((N_MOD, MOD_ROWS, D_MODEL), F32),
        grid=(k_dim // ADA_K_TILE,),
        in_specs=[
            pl.BlockSpec((MOD_ROWS, ADA_K_TILE), lambda k: (0, k)),
            pl.BlockSpec((ADA_K_TILE, n), lambda k: (k, 0)),
            pl.BlockSpec((1, n), lambda k: (0, 0)),
        ],
        out_specs=pl.BlockSpec((N_MOD, MOD_ROWS, D_MODEL), lambda k: (0, 0, 0)),
        compiler_params=_params(1),
        name="ada_modulation",
    )(cond, w_ada, b_ada)


class _Mod:
    def __init__(self, mod_ref, tiles_per_seq):
        self.ref = mod_ref
        self.row = 0 if tiles_per_seq is None else 1 + pl.program_id(0) // tiles_per_seq

    def __getitem__(self, m):
        return self.ref[m, pl.ds(self.row, 1), :]


def _modulated(x, mod, npre_ref, i):
    return _rms(x, npre_ref[i:i + 1, :]) * (1.0 + mod[3 * i + 1]) + mod[3 * i]


def _residual(x, out, mod, npost_ref, i, weight):
    return x + (weight * mod[3 * i + 2]) * _rms(out, npost_ref[i:i + 1, :])


def _ffn_sublayer(x, i, mod, npre_ref, npost_ref, w1_ref, w3_ref, w2_ref):
    h = _modulated(x, mod, npre_ref, i).astype(BF16)
    acc = None
    for lo, hi in zip(FF_SPLITS[:-1], FF_SPLITS[1:]):
        a = _dot(h, w1_ref[:, lo:hi])
        g = _dot(h, w3_ref[:, lo:hi])
        part = _dot((_silu(a) * g).astype(BF16), w2_ref[lo:hi, :])
        acc = part if acc is None else acc + part
    return _residual(x, acc, mod, npost_ref, i, 0.5)


def _ffn_first_kernel(x_ref, mod_ref, npre_ref, npost_ref, w1_ref, w3_ref, w2_ref, o_ref,
                      *, tiles_per_seq):
    mod = _Mod(mod_ref, tiles_per_seq)
    o_ref[...] = _ffn_sublayer(x_ref[...], 0, mod, npre_ref, npost_ref, w1_ref, w3_ref, w2_ref)


def _mix_ffn_kernel(x_ref, att_ref, gla_ref, wo_ref, mod_ref, npre_ref, npost_ref,
                    w1_ref, w3_ref, w2_ref, o_ref, *, tiles_per_seq):
    mod = _Mod(mod_ref, tiles_per_seq)
    mix = _dot(att_ref[...], wo_ref[0:ATT_Q, :]) + _dot(gla_ref[...], wo_ref[ATT_Q:, :])
    x = _residual(x_ref[...], mix, mod, npost_ref, 1, 1.0)
    o_ref[...] = _ffn_sublayer(x, 2, mod, npre_ref, npost_ref, w1_ref, w3_ref, w2_ref)


def _rope_tile(x, cos, sin_up, sin_dn):
    up = pltpu.roll(x, LANES - 16, axis=1)
    dn = pltpu.roll(x, 16, axis=1)
    return x * cos + up * sin_up + dn * sin_dn


def _proj_kernel(*refs, tiles_per_seq, cache_seq):
    x_ref, mod_ref, npre_ref, win_ref, wup_ref, bup_ref = refs[:6]
    rope_refs = refs[6:9] if tiles_per_seq is not None else None
    outs = refs[6 + (3 if rope_refs else 0):]
    att_ref, glaf_ref, gv_ref = outs[:3]
    h = _modulated(x_ref[...], _Mod(mod_ref, tiles_per_seq), npre_ref, 1).astype(BF16)
    q = _dot(h, win_ref[:, OFF_Q:OFF_K])
    kv = _dot(h, win_ref[:, OFF_K:OFF_GQ])
    k, v = kv[:, :ATT_KV], kv[:, ATT_KV:]
    lr = _dot(h, win_ref[:, OFF_LR:IN_WIDTH])
    z = _dot(lr.astype(BF16), wup_ref[...]) + bup_ref[...]
    if rope_refs:
        cos, sup, sdn = (r[...] for r in rope_refs)
        q = jnp.concatenate([_rope_tile(q[:, j * LANES:(j + 1) * LANES], cos, sup, sdn)
                             for j in range(ATT_Q // LANES)], axis=1)
        k_att = _rope_tile(k, cos, sup, sdn)
    else:
        k_att = k
    att_ref[:, A_Q:A_K] = (q * ATT_Q_SCALE).astype(BF16)
    att_ref[:, A_K:A_KS] = k_att.astype(BF16)
    att_ref[:, A_KS:A_V] = pltpu.roll(k_att, HALF, axis=1).astype(BF16)
    att_ref[:, A_V:A_VS] = v.astype(BF16)
    att_ref[:, A_VS:ATT_IN_W] = pltpu.roll(v, HALF, axis=1).astype(BF16)
    if cache_seq is not None:
        for t, dst in ((k.T, outs[3]), (v.T, outs[4])):
            for b in range(TOKEN_TILE // cache_seq):
                dst[b, 0] = t[:, b * cache_seq:(b + 1) * cache_seq].reshape(
                    N_KV_HEADS, HEAD_DIM, cache_seq)
    glaf_ref[:, G_Q:G_G] = _dot(h, win_ref[:, OFF_GQ:OFF_GV])
    log_sig = jnp.minimum(z, 0.0) - jnp.log(1.0 + jnp.exp(-jnp.abs(z)))
    glaf_ref[:, G_LA:GLA_F_W] = log_sig * (1.0 / GLA_TAU)
    gv_ref[...] = _dot(h, win_ref[:, OFF_GV:OFF_GG]).astype(BF16)
    glaf_ref[:, G_G:G_LA] = _dot(h, win_ref[:, OFF_GG:OFF_LR])


def _token_spec(width):
    return pl.BlockSpec((TOKEN_TILE, width), lambda i: (i, 0))


def _ffn_weight_spec(shape, j):
    return pl.BlockSpec((None,) + tuple(shape[1:]), lambda i: (j, 0, 0),
                        pipeline_mode=pl.Buffered(1))


def _tiles_per_seq(latent_len):
    return None if latent_len is None else latent_len // TOKEN_TILE


def _ffn_first(x, mod, latent_len, npre, npost, w1, w3, w2):
    t = x.shape[0]
    return pl.pallas_call(
        functools.partial(_ffn_first_kernel, tiles_per_seq=_tiles_per_seq(latent_len)),
        out_shape=jax.ShapeDtypeStruct((t, D_MODEL), F32),
        grid=(t // TOKEN_TILE,),
        in_specs=[_token_spec(D_MODEL), _resident(mod.shape), _resident(npre.shape),
                  _resident(npost.shape), _ffn_weight_spec(w1.shape, 0),
                  _ffn_weight_spec(w3.shape, 0), _ffn_weight_spec(w2.shape, 0)],
        out_specs=_token_spec(D_MODEL),
        compiler_params=_params(1),
        name="ffn_first",
    )(x, mod, npre, npost, w1, w3, w2)


def _mix_ffn(x, att, gla, w_out, mod, latent_len, npre, npost, w1, w3, w2):
    t = x.shape[0]
    return pl.pallas_call(
        functools.partial(_mix_ffn_kernel, tiles_per_seq=_tiles_per_seq(latent_len)),
        out_shape=jax.ShapeDtypeStruct((t, D_MODEL), F32),
        grid=(t // TOKEN_TILE,),
        in_specs=[_token_spec(D_MODEL), _token_spec(ATT_Q), _token_spec(GLA_V),
                  _resident(w_out.shape), _resident(mod.shape), _resident(npre.shape),
                  _resident(npost.shape), _ffn_weight_spec(w1.shape, 1),
                  _ffn_weight_spec(w3.shape, 1), _ffn_weight_spec(w2.shape, 1)],
        out_specs=_token_spec(D_MODEL),
        compiler_params=_params(1),
        name="mix_ffn",
    )(x, att, gla, w_out, mod, npre, npost, w1, w3, w2)


def _project(x, mod, latent_len, cache_seq, npre, w_in, w_up, b_up):
    t = x.shape[0]
    tiles = _tiles_per_seq(latent_len)
    in_specs = [_token_spec(D_MODEL), _resident(mod.shape), _resident(npre.shape),
                _resident(w_in.shape), _resident(w_up.shape), _resident(b_up.shape)]
    args = [x, mod, npre, w_in, w_up, b_up]
    if tiles is not None:
        in_specs += [pl.BlockSpec((TOKEN_TILE, LANES), lambda i: (i % tiles, 0))] * 3
        args += [jnp.asarray(tab) for tab in _rope_tables(latent_len)]
    outs = ((ATT_IN_W, BF16), (GLA_F_W, F32), (GLA_V, BF16))
    out_shape = [jax.ShapeDtypeStruct((t, w), dt) for w, dt in outs]
    out_specs = [_token_spec(w) for w, _ in outs]
    if cache_seq is not None:
        seqs = TOKEN_TILE // cache_seq
        cache = (t // cache_seq, 1, N_KV_HEADS, HEAD_DIM, cache_seq)
        out_shape += [jax.ShapeDtypeStruct(cache, F32)] * 2
        out_specs += [pl.BlockSpec((seqs,) + cache[1:], lambda i: (i, 0, 0, 0, 0))] * 2
    return pl.pallas_call(
        functools.partial(_proj_kernel, tiles_per_seq=tiles, cache_seq=cache_seq),
        out_shape=out_shape,
        grid=(t // TOKEN_TILE,),
        in_specs=in_specs,
        out_specs=out_specs,
        compiler_params=_params(1),
        name="project",
    )(*args)


def _rope_tables(seq_len):
    half = HEAD_DIM // 2
    inv_freq = np.float32(ROPE_BASE) ** (-np.arange(0, half, 2, dtype=np.float32) / half)
    pos = np.arange(seq_len)
    row = (pos // GRID_W).astype(np.float32)
    col = (pos % GRID_W).astype(np.float32)
    within = np.arange(LANES) % HEAD_DIM
    idx = within % half
    freq = inv_freq[idx % (half // 2)].astype(np.float32)
    p = np.where((within // half == 0)[None, :], row[:, None], col[:, None])
    ang = (p * freq[None, :]).astype(np.float32)
    cos, sin = np.cos(ang).astype(np.float32), np.sin(ang).astype(np.float32)
    first = (idx < half // 2)[None, :]
    zero = np.float32(0.0)
    return cos, np.where(first, -sin, zero), np.where(first, zero, sin)


def _attend(q_ref, q_rows, nq, segs, sink_ref):
    lane = lax.broadcasted_iota(jnp.int32, (1, LANES), 1)
    half_of = [lane < HALF, lane >= HALF]
    rows = lax.broadcasted_iota(jnp.int32, (2 * nq, 1), 0)
    zero = jnp.zeros((), BF16)
    out = [None] * (ATT_Q // LANES)
    for g in range(N_KV_HEADS):
        tiles = (2 * g, 2 * g + 1)
        for e in range(2):
            qm = jnp.concatenate(
                [jnp.where(half_of[e], q_ref[q_rows, j * LANES:(j + 1) * LANES], zero)
                 for j in tiles], axis=0)
            sink = jnp.where(rows < nq, sink_ref[2 * tiles[0] + e],
                             sink_ref[2 * tiles[1] + e]) * LOG2_E
            scores = []
            for k, k_sw, _, _, m in segs:
                s = _dot_t(qm, k if e == g else k_sw)
                scores.append(s if m is None else jnp.where(m, s, NEG))
            mx = sink
            for s in scores:
                mx = jnp.maximum(mx, jnp.max(s, axis=-1, keepdims=True))
            probs = [jnp.exp2(s - mx) for s in scores]
            den = jnp.exp2(sink - mx)
            for p in probs:
                den = den + jnp.sum(p, axis=-1, keepdims=True)
            o = None
            for p, (_, _, v, v_sw, _) in zip(probs, segs):
                part = _dot(p.astype(BF16), jnp.where(half_of[e], v if e == g else v_sw, zero))
                o = part if o is None else o + part
            o = o * (1.0 / den)
            for r, j in enumerate(tiles):
                blk = o[r * nq:(r + 1) * nq, :]
                out[j] = blk if out[j] is None else out[j] + blk
    return out


def _context_attn_kernel(sink_ref, a_ref, o_ref, *, seq_len):
    for s in range(a_ref.shape[0] // seq_len):
        rows = slice(s * seq_len, (s + 1) * seq_len)
        segs = [(a_ref[rows, A_K:A_KS], a_ref[rows, A_KS:A_V],
                 a_ref[rows, A_V:A_VS], a_ref[rows, A_VS:ATT_IN_W], None)]
        for j, t in enumerate(_attend(a_ref, rows, seq_len, segs, sink_ref)):
            o_ref[rows, j * LANES:(j + 1) * LANES] = t.astype(o_ref.dtype)


def _latent_attn_kernel(sink_ref, q_ref, kv_ref, kc_ref, vc_ref, o_ref, *, seq_len):
    k_ctx, v_ctx = kc_ref[0], vc_ref[0]
    ctx = (k_ctx.astype(BF16), pltpu.roll(k_ctx, HALF, axis=1).astype(BF16),
           v_ctx.astype(BF16), pltpu.roll(v_ctx, HALF, axis=1).astype(BF16), None)
    span = 3 * BLOCK
    per_step = q_ref.shape[0] // BLOCK
    for s in range(per_step):
        i = pl.program_id(1) * per_step + s
        start = pl.multiple_of(jnp.clip((i - 1) * BLOCK, 0, seq_len - span), BLOCK)
        kpos = start + lax.broadcasted_iota(jnp.int32, (1, span), 1)
        qpos = i * BLOCK + lax.broadcasted_iota(jnp.int32, (BLOCK, 1), 0)
        valid = jnp.abs(kpos - qpos) <= WINDOW
        valid2 = jnp.concatenate([valid, valid], axis=0)
        kv = kv_ref[pl.ds(start, span), :]
        segs = [tuple(kv[:, c * ATT_KV:(c + 1) * ATT_KV] for c in range(4)) + (valid2,), ctx]
        rows = slice(s * BLOCK, (s + 1) * BLOCK)
        for j, t in enumerate(_attend(q_ref, rows, BLOCK, segs, sink_ref)):
            o_ref[rows, j * LANES:(j + 1) * LANES] = t.astype(o_ref.dtype)


def _smem_spec():
    return pl.BlockSpec(memory_space=pltpu.SMEM)


def _context_attention(sink, att_in, batch, seq_len):
    rows = CTX_SEQS_PER_STEP * seq_len

    def seq(width):
        return pl.BlockSpec((rows, width), lambda b: (b, 0))
    return pl.pallas_call(
        functools.partial(_context_attn_kernel, seq_len=seq_len),
        out_shape=jax.ShapeDtypeStruct((batch * seq_len, ATT_Q), BF16),
        grid=(batch // CTX_SEQS_PER_STEP,),
        in_specs=[_smem_spec(), seq(ATT_IN_W)],
        out_specs=seq(ATT_Q),
        compiler_params=_params(1),
        name="context_attention",
    )(sink, att_in)


def _latent_attention(sink, att_in, k_ctx, v_ctx, batch, seq_len):
    steps = seq_len // (LAT_BLOCKS_PER_STEP * BLOCK)
    rows = LAT_BLOCKS_PER_STEP * BLOCK
    past = k_ctx.shape[1]
    assert A_K == ATT_IN_W - A_K, "q and the key / value columns are the two halves of att_in"
    return pl.pallas_call(
        functools.partial(_latent_attn_kernel, seq_len=seq_len),
        out_shape=jax.ShapeDtypeStruct((batch * seq_len, ATT_Q), BF16),
        grid=(batch, steps),
        in_specs=[_smem_spec(),
                  pl.BlockSpec((rows, ATT_Q), lambda b, i: (b * steps + i, 0)),
                  pl.BlockSpec((seq_len, ATT_IN_W - A_K), lambda b, i: (b, 1)),
                  pl.BlockSpec((1, past, ATT_KV), lambda b, i: (b, 0, 0)),
                  pl.BlockSpec((1, past, ATT_KV), lambda b, i: (b, 0, 0))],
        out_specs=pl.BlockSpec((rows, ATT_Q), lambda b, i: (b * steps + i, 0)),
        compiler_params=_params(2),
        name="latent_attention",
    )(sink, att_in, att_in, k_ctx, v_ctx)


def _split2(x):
    hi = x.astype(BF16)
    lo = (x - hi.astype(F32)).astype(BF16)
    return hi, lo


def _gla_kernel(*refs, n_chunks, has_s0, emit_state):
    refs = list(refs)
    gf_ref, gv_ref, gn_ref = refs[:3]
    pos = 3
    s0_refs = refs[pos:pos + 2] if has_s0 else None
    pos += 2 if has_s0 else 0
    o_ref = refs[pos]
    pos += 1
    sfin_refs = refs[pos:pos + 2] if emit_state else None
    pos += 2 if emit_state else 0
    cum_ref, kv_ref, sent_ref = refs[pos:]

    C = GLA_CHUNK
    n_pairs = GLA_QK // LANES
    lane = lax.broadcasted_iota(jnp.int32, (1, LANES), 1)
    half_of = [lane < HALF, lane >= HALF]
    r_i = lax.broadcasted_iota(jnp.int32, (C, C), 0)
    c_i = lax.broadcasted_iota(jnp.int32, (C, C), 1)
    lower = c_i <= r_i
    upper = c_i >= r_i
    tri = jnp.concatenate([jnp.where(lower, 1.0, 0.0), jnp.where(upper, 1.0, 0.0)],
                          axis=0).astype(BF16)
    zeros_cc = jnp.zeros((C, C), F32)
    gnorm = gn_ref[...]
    qscale = GLA_DK ** -0.5

    def one_sequence(s):
        def chunk_rows(n):
            return slice((s * n_chunks + n) * C, (s * n_chunks + n + 1) * C)

        for n in range(n_chunks):
            rows = chunk_rows(n)
            hi, lo = _split2(gf_ref[rows, G_LA:GLA_F_W])
            sums = _dot(tri, jnp.concatenate([hi, lo], axis=1))
            cum_f = sums[:C, 0:GLA_QK] + sums[:C, 2 * GLA_QK:3 * GLA_QK]
            cum_b = sums[C:, GLA_QK:2 * GLA_QK] + sums[C:, 3 * GLA_QK:]
            cum_ref[0, rows, :] = cum_f
            cum_ref[1, rows, :] = cum_b
            k = gf_ref[rows, G_K:G_G]
            k_in = (k * jnp.exp(cum_f[C - 1:C, :] - cum_f), k * jnp.exp(cum_b[0:1, :] - cum_b))
            for p in range(n_pairs):
                sl = slice(p * LANES, (p + 1) * LANES)
                kv_t = None
                for e in range(2):
                    h = 2 * p + e
                    v_t = gv_ref[rows, h * GLA_DV:(h + 1) * GLA_DV].T
                    k_e = jnp.concatenate(
                        [jnp.where(half_of[e], k_in[d][:, sl], 0.0) for d in range(2)], axis=1)
                    part = _dot(v_t, k_e.astype(BF16))
                    kv_t = part if kv_t is None else kv_t + part
                kv_ref[s, n, p] = kv_t

        for p in range(n_pairs):
            sl = slice(p * LANES, (p + 1) * LANES)
            st = []
            for d in range(2):
                if has_s0:
                    s0 = s0_refs[d][s, 0, 2 * p:2 * p + 2, :, :].reshape(2 * GLA_DK, GLA_DV)
                    st.append(s0.T)
                else:
                    st.append(jnp.zeros((GLA_DV, 2 * GLA_DK), F32))
            for i in range(n_chunks):
                for d, n in ((0, i), (1, n_chunks - 1 - i)):
                    tot_row = (s * n_chunks + n) * C + (C - 1 if d == 0 else 0)
                    decay = jnp.exp(cum_ref[d, tot_row:tot_row + 1, sl])
                    sent_ref[s, n, p, :, d * LANES:(d + 1) * LANES] = st[d].astype(BF16)
                    st[d] = decay * st[d] + kv_ref[s, n, p, :, d * LANES:(d + 1) * LANES]
            if emit_state:
                for d in range(2):
                    sfin_refs[d][s, 0, 2 * p:2 * p + 2, :, :] = st[d].T.reshape(2, GLA_DK, GLA_DV)

        for n in range(n_chunks):
            rows = chunk_rows(n)
            q = gf_ref[rows, G_Q:G_K] * qscale
            k = gf_ref[rows, G_K:G_G]
            qs, ks, qin = [], [], []
            for d in range(2):
                cum = cum_ref[d, rows, :]
                ref = cum[C // 2:C // 2 + 1, :]
                qs.append(q * jnp.exp(cum - ref))
                ks.append((k * jnp.exp(ref - cum)).astype(BF16))
                qin.append(q * jnp.exp(cum))
            for p in range(n_pairs):
                sl = slice(p * LANES, (p + 1) * LANES)
                lhs = jnp.concatenate(
                    [jnp.where(half_of[e], qs[d][:, sl], 0.0) for e in range(2) for d in range(2)],
                    axis=0).astype(BF16)
                sc = _dot_t(lhs, jnp.concatenate([ks[0][:, sl], ks[1][:, sl]], axis=0))
                prob = []
                for e in range(2):
                    s_f = sc[(2 * e) * C:(2 * e + 1) * C, :C]
                    s_b = sc[(2 * e + 1) * C:(2 * e + 2) * C, C:]
                    prob.append(jnp.where(lower, s_f, 0.0) + jnp.where(upper, s_b, 0.0))
                p_blk = jnp.concatenate(
                    [jnp.concatenate([prob[0], zeros_cc], axis=1),
                     jnp.concatenate([zeros_cc, prob[1]], axis=1)], axis=0).astype(BF16)
                v2 = jnp.concatenate(
                    [gv_ref[rows, (2 * p + e) * GLA_DV:(2 * p + e + 1) * GLA_DV] for e in range(2)],
                    axis=0)
                q_in = jnp.concatenate(
                    [jnp.concatenate([jnp.where(half_of[e], qin[d][:, sl], 0.0) for d in range(2)],
                                     axis=1) for e in range(2)], axis=0).astype(BF16)
                o2 = _dot(p_blk, v2) + _dot_t(q_in, sent_ref[s, n, p])
                for e in range(2):
                    h = 2 * p + e
                    o = o2[e * C:(e + 1) * C, :]
                    o = o * lax.rsqrt(jnp.mean(o * o, axis=-1, keepdims=True) + EPS) * gnorm
                    gate = gf_ref[rows, G_G + h * GLA_DV:G_G + (h + 1) * GLA_DV]
                    o_ref[rows, h * GLA_DV:(h + 1) * GLA_DV] = (o * _silu(gate)).astype(o_ref.dtype)

    for s in range(kv_ref.shape[0]):
        one_sequence(s)


def _gla(gla_f, gv, gnorm, batch, seq_len, seqs_per_step, s0=None, emit_state=False):
    n_chunks = seq_len // GLA_CHUNK
    has_s0 = s0 is not None
    rows = seqs_per_step * seq_len

    def seq(width):
        return pl.BlockSpec((rows, width), lambda b: (b, 0))
    state_spec = pl.BlockSpec((seqs_per_step, 1, GLA_HEADS, GLA_DK, GLA_DV),
                              lambda b: (b, 0, 0, 0, 0))
    in_specs = [seq(GLA_F_W), seq(GLA_V), pl.BlockSpec((1, GLA_DV), lambda b: (0, 0))]
    args = [gla_f, gv, gnorm]
    if has_s0:
        in_specs += [state_spec, state_spec]
        args += list(s0)
    out_shape = [jax.ShapeDtypeStruct((batch * seq_len, GLA_V), BF16)]
    out_specs = [seq(GLA_V)]
    if emit_state:
        out_shape += [jax.ShapeDtypeStruct((batch, 1, GLA_HEADS, GLA_DK, GLA_DV), F32)] * 2
        out_specs += [state_spec, state_spec]
    n_pairs = GLA_QK // LANES
    return pl.pallas_call(
        functools.partial(_gla_kernel, n_chunks=n_chunks, has_s0=has_s0, emit_state=emit_state),
        out_shape=out_shape,
        grid=(batch // seqs_per_step,),
        in_specs=in_specs,
        out_specs=out_specs,
        scratch_shapes=[pltpu.VMEM((2, rows, GLA_QK), F32),
                        pltpu.VMEM((seqs_per_step, n_chunks, n_pairs, GLA_DV, 2 * LANES), F32),
                        pltpu.VMEM((seqs_per_step, n_chunks, n_pairs, GLA_DV, 2 * LANES), BF16)],
        compiler_params=_params(1),
        name="gla",
    )(*args)


def kernel(x_prompt, x_sample, cache_k, cache_v, state_gla_fwd, state_gla_bwd, c, c_ctx,
           w_ada, b_ada, norm_pre, norm_post, ffn_w1, ffn_w3, ffn_w2, w_in,
           gla_w_up, gla_b_up, gla_norm, attn_sink, w_out):
    depth = w_in.shape[0]
    assert depth == 1, "single trunk layer"
    batch, seq = x_prompt.shape[0], x_prompt.shape[1]
    dec_batch, dec_seq = x_sample.shape[0], x_sample.shape[1]
    past = cache_k.shape[2]
    l = 0

    cond = jnp.concatenate(
        [c_ctx[None, :], c, jnp.zeros((MOD_ROWS - 1 - dec_batch, D_MODEL), F32)], axis=0)
    mod = _ada_modulation(cond, w_ada[l], b_ada[l][None, :])

    npre, npost = norm_pre[l], norm_post[l]
    ffn_b = [w[l].astype(BF16) for w in (ffn_w1, ffn_w3, ffn_w2)]
    w_in_b = w_in[l].astype(BF16)
    w_out_b = w_out[l].astype(BF16)
    zeros = jnp.zeros((GLA_LOW_RANK, GLA_QK), F32)
    w_up = jnp.concatenate(
        [jnp.concatenate([gla_w_up[l, 0], zeros], axis=1),
         jnp.concatenate([zeros, gla_w_up[l, 1]], axis=1)], axis=0).astype(BF16)
    b_up = gla_b_up[l].reshape(1, 2 * GLA_QK)
    gnorm = gla_norm[l][None, :]
    sink = attn_sink[l]

    def trunk(x, latent):
        n_batch, n_seq = (dec_batch, dec_seq) if latent else (batch, seq)
        latent_len = n_seq if latent else None
        x1 = _ffn_first(x, mod, latent_len, npre, npost, *ffn_b)
        att_in, gla_f, gv, *cache_t = _project(
            x1, mod, latent_len, None if latent else n_seq, npre, w_in_b, w_up, b_up)
        if latent:
            att = _latent_attention(sink, att_in, cache_k[:, l].reshape(dec_batch, past, ATT_KV),
                                    cache_v[:, l].reshape(dec_batch, past, ATT_KV), n_batch, n_seq)
            (gla,) = _gla(gla_f, gv, gnorm, n_batch, n_seq, 1,
                          s0=(state_gla_fwd[:, l:l + 1], state_gla_bwd[:, l:l + 1]))
            extras = ()
        else:
            att = _context_attention(sink, att_in, n_batch, n_seq)
            gla, s_f, s_b = _gla(gla_f, gv, gnorm, n_batch, n_seq, CTX_SEQS_PER_STEP,
                                 emit_state=True)
            k_new, v_new = (jnp.transpose(c_t, (0, 1, 4, 2, 3)) for c_t in cache_t)
            extras = (k_new, v_new, s_f, s_b)
        y = _mix_ffn(x1, att, gla, w_out_b, mod, latent_len, npre, npost, *ffn_b)
        return y.reshape(n_batch, n_seq, D_MODEL), extras

    y_prompt, (k_new, v_new, s_f, s_b) = trunk(x_prompt.reshape(batch * seq, D_MODEL), False)
    y_sample, _ = trunk(x_sample.reshape(dec_batch * dec_seq, D_MODEL), True)
    return (y_prompt, y_sample, k_new, v_new, s_f, s_b)
```

```python
import functools

import numpy as np
import jax
import jax.numpy as jnp
from jax import lax
from jax.experimental import pallas as pl
from jax.experimental.pallas import tpu as pltpu

F32 = jnp.float32
BF16 = jnp.bfloat16

D_MODEL = 1024
GRID_W = 64
N_Q_HEADS = 8
N_KV_HEADS = 2
HEAD_DIM = 64
WINDOW = 128
BLOCK = 128
ROPE_BASE = 10000.0
GLA_HEADS = 4
GLA_DK = 64
GLA_DV = 128
GLA_LOW_RANK = 16
GLA_TAU = 16.0
D_FF = 2816
N_MOD = 9
EPS = 1e-6
NEG = -1e30

ATT_Q = N_Q_HEADS * HEAD_DIM
ATT_KV = N_KV_HEADS * HEAD_DIM
GLA_QK = GLA_HEADS * GLA_DK
GLA_V = GLA_HEADS * GLA_DV
OFF_Q = 0
OFF_K = OFF_Q + ATT_Q
OFF_V = OFF_K + ATT_KV
OFF_GQ = OFF_V + ATT_KV
OFF_GK = OFF_GQ + GLA_QK
OFF_GV = OFF_GK + GLA_QK
OFF_GG = OFF_GV + GLA_V
OFF_LR = OFF_GG + GLA_V
IN_WIDTH = OFF_LR + 2 * GLA_LOW_RANK
A_Q = 0
A_K = A_Q + ATT_Q
A_KS = A_K + ATT_KV
A_V = A_KS + ATT_KV
A_VS = A_V + ATT_KV
ATT_IN_W = A_VS + ATT_KV
G_Q = 0
G_K = G_Q + GLA_QK
G_G = G_K + GLA_QK
G_LA = G_G + GLA_V
GLA_F_W = G_LA + 2 * GLA_QK
LOG2_E = 1.4426950408889634
ATT_Q_SCALE = HEAD_DIM ** -0.5 * LOG2_E

LANES = 128
SUBLANES = 8
HALF = LANES // 2
VMEM_LIMIT = 56 * 1024 * 1024

TOKEN_TILE = 512
FF_SPLITS = (0, 1536, D_FF)
ADA_K_TILE = 128
CTX_SEQS_PER_STEP = 4
LAT_BLOCKS_PER_STEP = 4
GLA_CHUNK = 128
MOD_ROWS = 8


def _params(n_axes):
    return pltpu.CompilerParams(
        dimension_semantics=("arbitrary",) * n_axes, vmem_limit_bytes=VMEM_LIMIT)


def _resident(shape):
    zeros = (0,) * len(shape)
    return pl.BlockSpec(shape, lambda *_: zeros, pipeline_mode=pl.Buffered(1))


def _sigmoid(x):
    return 1.0 / (1.0 + jnp.exp(-x))


def _silu(x):
    return x * _sigmoid(x)


def _rms(x, g):
    return x * lax.rsqrt(jnp.mean(x * x, axis=-1, keepdims=True) + EPS) * g


def _dot(a, b):
    return jnp.dot(a, b, preferred_element_type=F32)


def _dot_t(a, b):
    return lax.dot_general(a, b, (((1,), (1,)), ((), ())), preferred_element_type=F32)


def _ada_kernel(cond_ref, w_ref, b_ref, o_ref):
    @pl.when(pl.program_id(0) == 0)
    def _():
        for m in range(N_MOD):
            o_ref[m] = jnp.broadcast_to(b_ref[:, m * D_MODEL:(m + 1) * D_MODEL],
                                        (MOD_ROWS, D_MODEL))
    part = _dot(_silu(cond_ref[...]).astype(BF16), w_ref[...].astype(BF16))
    for m in range(N_MOD):
        o_ref[m] += part[:, m * D_MODEL:(m + 1) * D_MODEL]


def _ada_modulation(cond, w_ada, b_ada):
    k_dim, n = w_ada.shape
    return pl.pallas_call(
        _ada_kernel,
        out_shape=jax.ShapeDtypeStruct((N_MOD, MOD_ROWS, D_MODEL), F32),
        grid=(k_dim // ADA_K_TILE,),
        in_specs=[
            pl.BlockSpec((MOD_ROWS, ADA_K_TILE), lambda k: (0, k)),
            pl.BlockSpec((ADA_K_TILE, n), lambda k: (k, 0)),
            pl.BlockSpec((1, n), lambda k: (0, 0)),
        ],
        out_specs=pl.BlockSpec((N_MOD, MOD_ROWS, D_MODEL), lambda k: (0, 0, 0)),
        compiler_params=_params(1),
        name="ada_modulation",
    )(cond, w_ada, b_ada)


class _Mod:
    def __init__(self, mod_ref, tiles_per_seq):
        self.ref = mod_ref
        self.row = 0 if tiles_per_seq is None else 1 + pl.program_id(0) // tiles_per_seq

    def __getitem__(self, m):
        return self.ref[m, pl.ds(self.row, 1), :]


def _modulated(x, mod, npre_ref, i):
    return _rms(x, npre_ref[i:i + 1, :]) * (1.0 + mod[3 * i + 1]) + mod[3 * i]


def _residual(x, out, mod, npost_ref, i, weight):
    return x + (weight * mod[3 * i + 2]) * _rms(out, npost_ref[i:i + 1, :])


def _ffn_sublayer(x, i, mod, npre_ref, npost_ref, w1_ref, w3_ref, w2_ref):
    h = _modulated(x, mod, npre_ref, i).astype(BF16)
    acc = None
    for lo, hi in zip(FF_SPLITS[:-1], FF_SPLITS[1:]):
        a = _dot(h, w1_ref[:, lo:hi])
        g = _dot(h, w3_ref[:, lo:hi])
        part = _dot((_silu(a) * g).astype(BF16), w2_ref[lo:hi, :])
        acc = part if acc is None else acc + part
    return _residual(x, acc, mod, npost_ref, i, 0.5)


def _ffn_first_kernel(x_ref, mod_ref, npre_ref, npost_ref, w1_ref, w3_ref, w2_ref, o_ref,
                      *, tiles_per_seq):
    mod = _Mod(mod_ref, tiles_per_seq)
    o_ref[...] = _ffn_sublayer(x_ref[...], 0, mod, npre_ref, npost_ref, w1_ref, w3_ref, w2_ref)


def _mix_ffn_kernel(x_ref, att_ref, gla_ref, wo_ref, mod_ref, npre_ref, npost_ref,
                    w1_ref, w3_ref, w2_ref, o_ref, *, tiles_per_seq):
    mod = _Mod(mod_ref, tiles_per_seq)
    mix = _dot(att_ref[...], wo_ref[0:ATT_Q, :]) + _dot(gla_ref[...], wo_ref[ATT_Q:, :])
    x = _residual(x_ref[...], mix, mod, npost_ref, 1, 1.0)
    o_ref[...] = _ffn_sublayer(x, 2, mod, npre_ref, npost_ref, w1_ref, w3_ref, w2_ref)


def _rope_tile(x, cos, sin_up, sin_dn):
    up = pltpu.roll(x, LANES - 16, axis=1)
    dn = pltpu.roll(x, 16, axis=1)
    return x * cos + up * sin_up + dn * sin_dn


def _gla_project(h, win_ref, wup_ref, bup_ref, gf_ref, gv_ref):
    lr = _dot(h, win_ref[:, OFF_LR:IN_WIDTH])
    z = _dot(lr.astype(BF16), wup_ref[...]) + bup_ref[...]
    gf_ref[:, G_Q:G_G] = _dot(h, win_ref[:, OFF_GQ:OFF_GV])
    log_sig = jnp.minimum(z, 0.0) - jnp.log(1.0 + jnp.exp(-jnp.abs(z)))
    gf_ref[:, G_LA:GLA_F_W] = log_sig * (1.0 / GLA_TAU)
    gv_ref[...] = _dot(h, win_ref[:, OFF_GV:OFF_GG]).astype(BF16)
    gf_ref[:, G_G:G_LA] = _dot(h, win_ref[:, OFF_GG:OFF_LR])


def _proj_kernel(*refs, tiles_per_seq, cache_seq):
    x_ref, mod_ref, npre_ref, win_ref = refs[:4]
    rope_refs = refs[4:7] if tiles_per_seq is not None else None
    outs = refs[4 + (3 if rope_refs else 0):]
    att_ref = outs[0]
    h = _modulated(x_ref[...], _Mod(mod_ref, tiles_per_seq), npre_ref, 1).astype(BF16)
    q = _dot(h, win_ref[:, OFF_Q:OFF_K])
    kv = _dot(h, win_ref[:, OFF_K:OFF_GQ])
    k, v = kv[:, :ATT_KV], kv[:, ATT_KV:]
    if rope_refs:
        cos, sup, sdn = (r[...] for r in rope_refs)
        q = jnp.concatenate([_rope_tile(q[:, j * LANES:(j + 1) * LANES], cos, sup, sdn)
                             for j in range(ATT_Q // LANES)], axis=1)
        k_att = _rope_tile(k, cos, sup, sdn)
    else:
        k_att = k
    att_ref[:, A_Q:A_K] = (q * ATT_Q_SCALE).astype(BF16)
    att_ref[:, A_K:A_KS] = k_att.astype(BF16)
    att_ref[:, A_KS:A_V] = pltpu.roll(k_att, HALF, axis=1).astype(BF16)
    att_ref[:, A_V:A_VS] = v.astype(BF16)
    att_ref[:, A_VS:ATT_IN_W] = pltpu.roll(v, HALF, axis=1).astype(BF16)
    if cache_seq is not None:
        for t, dst in ((k.T, outs[1]), (v.T, outs[2])):
            for b in range(TOKEN_TILE // cache_seq):
                dst[b, 0] = t[:, b * cache_seq:(b + 1) * cache_seq].reshape(
                    N_KV_HEADS, HEAD_DIM, cache_seq)


def _token_spec(width):
    return pl.BlockSpec((TOKEN_TILE, width), lambda i: (i, 0))


def _ffn_weight_spec(shape, j):
    return pl.BlockSpec((None,) + tuple(shape[1:]), lambda i: (j, 0, 0),
                        pipeline_mode=pl.Buffered(1))


def _tiles_per_seq(latent_len):
    return None if latent_len is None else latent_len // TOKEN_TILE


def _ffn_first(x, mod, latent_len, npre, npost, w1, w3, w2):
    t = x.shape[0]
    return pl.pallas_call(
        functools.partial(_ffn_first_kernel, tiles_per_seq=_tiles_per_seq(latent_len)),
        out_shape=jax.ShapeDtypeStruct((t, D_MODEL), F32),
        grid=(t // TOKEN_TILE,),
        in_specs=[_token_spec(D_MODEL), _resident(mod.shape), _resident(npre.shape),
                  _resident(npost.shape), _ffn_weight_spec(w1.shape, 0),
                  _ffn_weight_spec(w3.shape, 0), _ffn_weight_spec(w2.shape, 0)],
        out_specs=_token_spec(D_MODEL),
        compiler_params=_params(1),
        name="ffn_first",
    )(x, mod, npre, npost, w1, w3, w2)


def _mix_ffn(x, att, gla, w_out, mod, latent_len, npre, npost, w1, w3, w2):
    t = x.shape[0]
    return pl.pallas_call(
        functools.partial(_mix_ffn_kernel, tiles_per_seq=_tiles_per_seq(latent_len)),
        out_shape=jax.ShapeDtypeStruct((t, D_MODEL), F32),
        grid=(t // TOKEN_TILE,),
        in_specs=[_token_spec(D_MODEL), _token_spec(ATT_Q), _token_spec(GLA_V),
                  _resident(w_out.shape), _resident(mod.shape), _resident(npre.shape),
                  _resident(npost.shape), _ffn_weight_spec(w1.shape, 1),
                  _ffn_weight_spec(w3.shape, 1), _ffn_weight_spec(w2.shape, 1)],
        out_specs=_token_spec(D_MODEL),
        compiler_params=_params(1),
        name="mix_ffn",
    )(x, att, gla, w_out, mod, npre, npost, w1, w3, w2)


def _project(x, mod, latent_len, cache_seq, npre, w_in):
    t = x.shape[0]
    tiles = _tiles_per_seq(latent_len)
    in_specs = [_token_spec(D_MODEL), _resident(mod.shape), _resident(npre.shape),
                _resident(w_in.shape)]
    args = [x, mod, npre, w_in]
    if tiles is not None:
        in_specs += [pl.BlockSpec((TOKEN_TILE, LANES), lambda i: (i % tiles, 0))] * 3
        args += [jnp.asarray(tab) for tab in _rope_tables(latent_len)]
    out_shape = [jax.ShapeDtypeStruct((t, ATT_IN_W), BF16)]
    out_specs = [_token_spec(ATT_IN_W)]
    if cache_seq is not None:
        seqs = TOKEN_TILE // cache_seq
        cache = (t // cache_seq, 1, N_KV_HEADS, HEAD_DIM, cache_seq)
        out_shape += [jax.ShapeDtypeStruct(cache, F32)] * 2
        out_specs += [pl.BlockSpec((seqs,) + cache[1:], lambda i: (i, 0, 0, 0, 0))] * 2
    return pl.pallas_call(
        functools.partial(_proj_kernel, tiles_per_seq=tiles, cache_seq=cache_seq),
        out_shape=out_shape,
        grid=(t // TOKEN_TILE,),
        in_specs=in_specs,
        out_specs=out_specs,
        compiler_params=_params(1),
        name="project",
    )(*args)


def _rope_tables(seq_len):
    half = HEAD_DIM // 2
    inv_freq = np.float32(ROPE_BASE) ** (-np.arange(0, half, 2, dtype=np.float32) / half)
    pos = np.arange(seq_len)
    row = (pos // GRID_W).astype(np.float32)
    col = (pos % GRID_W).astype(np.float32)
    within = np.arange(LANES) % HEAD_DIM
    idx = within % half
    freq = inv_freq[idx % (half // 2)].astype(np.float32)
    p = np.where((within // half == 0)[None, :], row[:, None], col[:, None])
    ang = (p * freq[None, :]).astype(np.float32)
    cos, sin = np.cos(ang).astype(np.float32), np.sin(ang).astype(np.float32)
    first = (idx < half // 2)[None, :]
    zero = np.float32(0.0)
    return cos, np.where(first, -sin, zero), np.where(first, zero, sin)


def _attend(q_ref, q_rows, nq, segs, sink_ref):
    lane = lax.broadcasted_iota(jnp.int32, (1, LANES), 1)
    half_of = [lane < HALF, lane >= HALF]
    rows = lax.broadcasted_iota(jnp.int32, (2 * nq, 1), 0)
    zero = jnp.zeros((), BF16)
    out = [None] * (ATT_Q // LANES)
    for g in range(N_KV_HEADS):
        tiles = (2 * g, 2 * g + 1)
        for e in range(2):
            qm = jnp.concatenate(
                [jnp.where(half_of[e], q_ref[q_rows, j * LANES:(j + 1) * LANES], zero)
                 for j in tiles], axis=0)
            sink = jnp.where(rows < nq, sink_ref[2 * tiles[0] + e],
                             sink_ref[2 * tiles[1] + e]) * LOG2_E
            scores = []
            for k, k_sw, _, _, m in segs:
                s = _dot_t(qm, k if e == g else k_sw)
                scores.append(s if m is None else jnp.where(m, s, NEG))
            mx = sink
            for s in scores:
                mx = jnp.maximum(mx, jnp.max(s, axis=-1, keepdims=True))
            probs = [jnp.exp2(s - mx) for s in scores]
            den = jnp.exp2(sink - mx)
            for p in probs:
                den = den + jnp.sum(p, axis=-1, keepdims=True)
            o = None
            for p, (_, _, v, v_sw, _) in zip(probs, segs):
                part = _dot(p.astype(BF16), jnp.where(half_of[e], v if e == g else v_sw, zero))
                o = part if o is None else o + part
            o = o * (1.0 / den)
            for r, j in enumerate(tiles):
                blk = o[r * nq:(r + 1) * nq, :]
                out[j] = blk if out[j] is None else out[j] + blk
    return out


def _context_attn_kernel(sink_ref, a_ref, o_ref, *, seq_len):
    for s in range(a_ref.shape[0] // seq_len):
        rows = slice(s * seq_len, (s + 1) * seq_len)
        segs = [(a_ref[rows, A_K:A_KS], a_ref[rows, A_KS:A_V],
                 a_ref[rows, A_V:A_VS], a_ref[rows, A_VS:ATT_IN_W], None)]
        for j, t in enumerate(_attend(a_ref, rows, seq_len, segs, sink_ref)):
            o_ref[rows, j * LANES:(j + 1) * LANES] = t.astype(o_ref.dtype)


def _latent_attn_kernel(sink_ref, q_ref, kv_ref, kc_ref, vc_ref, o_ref, *, seq_len):
    k_ctx, v_ctx = kc_ref[0], vc_ref[0]
    ctx = (k_ctx.astype(BF16), pltpu.roll(k_ctx, HALF, axis=1).astype(BF16),
           v_ctx.astype(BF16), pltpu.roll(v_ctx, HALF, axis=1).astype(BF16), None)
    span = 3 * BLOCK
    per_step = q_ref.shape[0] // BLOCK
    for s in range(per_step):
        i = pl.program_id(1) * per_step + s
        start = pl.multiple_of(jnp.clip((i - 1) * BLOCK, 0, seq_len - span), BLOCK)
        kpos = start + lax.broadcasted_iota(jnp.int32, (1, span), 1)
        qpos = i * BLOCK + lax.broadcasted_iota(jnp.int32, (BLOCK, 1), 0)
        valid = jnp.abs(kpos - qpos) <= WINDOW
        valid2 = jnp.concatenate([valid, valid], axis=0)
        kv = kv_ref[pl.ds(start, span), :]
        segs = [tuple(kv[:, c * ATT_KV:(c + 1) * ATT_KV] for c in range(4)) + (valid2,), ctx]
        rows = slice(s * BLOCK, (s + 1) * BLOCK)
        for j, t in enumerate(_attend(q_ref, rows, BLOCK, segs, sink_ref)):
            o_ref[rows, j * LANES:(j + 1) * LANES] = t.astype(o_ref.dtype)


def _smem_spec():
    return pl.BlockSpec(memory_space=pltpu.SMEM)


def _context_attention(sink, att_in, batch, seq_len):
    rows = CTX_SEQS_PER_STEP * seq_len

    def seq(width):
        return pl.BlockSpec((rows, width), lambda b: (b, 0))
    return pl.pallas_call(
        functools.partial(_context_attn_kernel, seq_len=seq_len),
        out_shape=jax.ShapeDtypeStruct((batch * seq_len, ATT_Q), BF16),
        grid=(batch // CTX_SEQS_PER_STEP,),
        in_specs=[_smem_spec(), seq(ATT_IN_W)],
        out_specs=seq(ATT_Q),
        compiler_params=_params(1),
        name="context_attention",
    )(sink, att_in)


def _latent_attention(sink, att_in, k_ctx, v_ctx, batch, seq_len):
    steps = seq_len // (LAT_BLOCKS_PER_STEP * BLOCK)
    rows = LAT_BLOCKS_PER_STEP * BLOCK
    past = k_ctx.shape[1]
    assert A_K == ATT_IN_W - A_K, "q and the key / value columns are the two halves of att_in"
    return pl.pallas_call(
        functools.partial(_latent_attn_kernel, seq_len=seq_len),
        out_shape=jax.ShapeDtypeStruct((batch * seq_len, ATT_Q), BF16),
        grid=(batch, steps),
        in_specs=[_smem_spec(),
                  pl.BlockSpec((rows, ATT_Q), lambda b, i: (b * steps + i, 0)),
                  pl.BlockSpec((seq_len, ATT_IN_W - A_K), lambda b, i: (b, 1)),
                  pl.BlockSpec((1, past, ATT_KV), lambda b, i: (b, 0, 0)),
                  pl.BlockSpec((1, past, ATT_KV), lambda b, i: (b, 0, 0))],
        out_specs=pl.BlockSpec((rows, ATT_Q), lambda b, i: (b * steps + i, 0)),
        compiler_params=_params(2),
        name="latent_attention",
    )(sink, att_in, att_in, k_ctx, v_ctx)


def _split2(x):
    hi = x.astype(BF16)
    lo = (x - hi.astype(F32)).astype(BF16)
    return hi, lo


def _gla_kernel(*refs, n_chunks, has_s0, emit_state, tiles_per_seq):
    refs = list(refs)
    x_ref, mod_ref, npre_ref, win_ref, wup_ref, bup_ref, gn_ref = refs[:7]
    pos = 7
    s0_refs = refs[pos:pos + 2] if has_s0 else None
    pos += 2 if has_s0 else 0
    o_ref = refs[pos]
    pos += 1
    sfin_refs = refs[pos:pos + 2] if emit_state else None
    pos += 2 if emit_state else 0
    gf_ref, gv_ref, cum_ref, kv_ref, sent_ref = refs[pos:]

    h = _modulated(x_ref[...], _Mod(mod_ref, tiles_per_seq), npre_ref, 1).astype(BF16)
    _gla_project(h, win_ref, wup_ref, bup_ref, gf_ref, gv_ref)

    C = GLA_CHUNK
    n_pairs = GLA_QK // LANES
    lane = lax.broadcasted_iota(jnp.int32, (1, LANES), 1)
    half_of = [lane < HALF, lane >= HALF]
    r_i = lax.broadcasted_iota(jnp.int32, (C, C), 0)
    c_i = lax.broadcasted_iota(jnp.int32, (C, C), 1)
    lower = c_i <= r_i
    upper = c_i >= r_i
    tri = jnp.concatenate([jnp.where(lower, 1.0, 0.0), jnp.where(upper, 1.0, 0.0)],
                          axis=0).astype(BF16)
    zeros_cc = jnp.zeros((C, C), F32)
    gnorm = gn_ref[...]
    qscale = GLA_DK ** -0.5

    def one_sequence(s):
        def chunk_rows(n):
            return slice((s * n_chunks + n) * C, (s * n_chunks + n + 1) * C)

        for n in range(n_chunks):
            rows = chunk_rows(n)
            hi, lo = _split2(gf_ref[rows, G_LA:GLA_F_W])
            sums = _dot(tri, jnp.concatenate([hi, lo], axis=1))
            cum_f = sums[:C, 0:GLA_QK] + sums[:C, 2 * GLA_QK:3 * GLA_QK]
            cum_b = sums[C:, GLA_QK:2 * GLA_QK] + sums[C:, 3 * GLA_QK:]
            cum_ref[0, rows, :] = cum_f
            cum_ref[1, rows, :] = cum_b
            k = gf_ref[rows, G_K:G_G]
            k_in = (k * jnp.exp(cum_f[C - 1:C, :] - cum_f), k * jnp.exp(cum_b[0:1, :] - cum_b))
            for p in range(n_pairs):
                sl = slice(p * LANES, (p + 1) * LANES)
                kv_t = None
                for e in range(2):
                    h = 2 * p + e
                    v_t = gv_ref[rows, h * GLA_DV:(h + 1) * GLA_DV].T
                    k_e = jnp.concatenate(
                        [jnp.where(half_of[e], k_in[d][:, sl], 0.0) for d in range(2)], axis=1)
                    part = _dot(v_t, k_e.astype(BF16))
                    kv_t = part if kv_t is None else kv_t + part
                kv_ref[s, n, p] = kv_t

        for p in range(n_pairs):
            sl = slice(p * LANES, (p + 1) * LANES)
            st = []
            for d in range(2):
                if has_s0:
                    s0 = s0_refs[d][s, 0, 2 * p:2 * p + 2, :, :].reshape(2 * GLA_DK, GLA_DV)
                    st.append(s0.T)
                else:
                    st.append(jnp.zeros((GLA_DV, 2 * GLA_DK), F32))
            for i in range(n_chunks):
                for d, n in ((0, i), (1, n_chunks - 1 - i)):
                    tot_row = (s * n_chunks + n) * C + (C - 1 if d == 0 else 0)
                    decay = jnp.exp(cum_ref[d, tot_row:tot_row + 1, sl])
                    sent_ref[s, n, p, :, d * LANES:(d + 1) * LANES] = st[d].astype(BF16)
                    st[d] = decay * st[d] + kv_ref[s, n, p, :, d * LANES:(d + 1) * LANES]
            if emit_state:
                for d in range(2):
                    sfin_refs[d][s, 0, 2 * p:2 * p + 2, :, :] = st[d].T.reshape(2, GLA_DK, GLA_DV)

        for n in range(n_chunks):
            rows = chunk_rows(n)
            q = gf_ref[rows, G_Q:G_K] * qscale
            k = gf_ref[rows, G_K:G_G]
            qs, ks, qin = [], [], []
            for d in range(2):
                cum = cum_ref[d, rows, :]
                ref = cum[C // 2:C // 2 + 1, :]
                qs.append(q * jnp.exp(cum - ref))
                ks.append((k * jnp.exp(ref - cum)).astype(BF16))
                qin.append(q * jnp.exp(cum))
            for p in range(n_pairs):
                sl = slice(p * LANES, (p + 1) * LANES)
                lhs = jnp.concatenate(
                    [jnp.where(half_of[e], qs[d][:, sl], 0.0) for e in range(2) for d in range(2)],
                    axis=0).astype(BF16)
                sc = _dot_t(lhs, jnp.concatenate([ks[0][:, sl], ks[1][:, sl]], axis=0))
                prob = []
                for e in range(2):
                    s_f = sc[(2 * e) * C:(2 * e + 1) * C, :C]
                    s_b = sc[(2 * e + 1) * C:(2 * e + 2) * C, C:]
                    prob.append(jnp.where(lower, s_f, 0.0) + jnp.where(upper, s_b, 0.0))
                p_blk = jnp.concatenate(
                    [jnp.concatenate([prob[0], zeros_cc], axis=1),
                     jnp.concatenate([zeros_cc, prob[1]], axis=1)], axis=0).astype(BF16)
                v2 = jnp.concatenate(
                    [gv_ref[rows, (2 * p + e) * GLA_DV:(2 * p + e + 1) * GLA_DV] for e in range(2)],
                    axis=0)
                q_in = jnp.concatenate(
                    [jnp.concatenate([jnp.where(half_of[e], qin[d][:, sl], 0.0) for d in range(2)],
                                     axis=1) for e in range(2)], axis=0).astype(BF16)
                o2 = _dot(p_blk, v2) + _dot_t(q_in, sent_ref[s, n, p])
                for e in range(2):
                    h = 2 * p + e
                    o = o2[e * C:(e + 1) * C, :]
                    o = o * lax.rsqrt(jnp.mean(o * o, axis=-1, keepdims=True) + EPS) * gnorm
                    gate = gf_ref[rows, G_G + h * GLA_DV:G_G + (h + 1) * GLA_DV]
                    o_ref[rows, h * GLA_DV:(h + 1) * GLA_DV] = (o * _silu(gate)).astype(o_ref.dtype)

    for s in range(kv_ref.shape[0]):
        one_sequence(s)


def _gla(x, mod, latent, npre, w_in, w_up, b_up, gnorm, batch, seq_len, seqs_per_step,
         s0=None, emit_state=False):
    n_chunks = seq_len // GLA_CHUNK
    has_s0 = s0 is not None
    rows = seqs_per_step * seq_len
    assert not latent or seqs_per_step == 1

    def seq(width):
        return pl.BlockSpec((rows, width), lambda b: (b, 0))
    state_spec = pl.BlockSpec((seqs_per_step, 1, GLA_HEADS, GLA_DK, GLA_DV),
                              lambda b: (b, 0, 0, 0, 0))
    in_specs = [seq(D_MODEL), _resident(mod.shape), _resident(npre.shape), _resident(w_in.shape),
                _resident(w_up.shape), _resident(b_up.shape), _resident(gnorm.shape)]
    args = [x, mod, npre, w_in, w_up, b_up, gnorm]
    if has_s0:
        in_specs += [state_spec, state_spec]
        args += list(s0)
    out_shape = [jax.ShapeDtypeStruct((batch * seq_len, GLA_V), BF16)]
    out_specs = [seq(GLA_V)]
    if emit_state:
        out_shape += [jax.ShapeDtypeStruct((batch, 1, GLA_HEADS, GLA_DK, GLA_DV), F32)] * 2
        out_specs += [state_spec, state_spec]
    n_pairs = GLA_QK // LANES
    return pl.pallas_call(
        functools.partial(_gla_kernel, n_chunks=n_chunks, has_s0=has_s0, emit_state=emit_state,
                          tiles_per_seq=1 if latent else None),
        out_shape=out_shape,
        grid=(batch // seqs_per_step,),
        in_specs=in_specs,
        out_specs=out_specs,
        scratch_shapes=[pltpu.VMEM((rows, GLA_F_W), F32),
                        pltpu.VMEM((rows, GLA_V), BF16),
                        pltpu.VMEM((2, rows, GLA_QK), F32),
                        pltpu.VMEM((seqs_per_step, n_chunks, n_pairs, GLA_DV, 2 * LANES), F32),
                        pltpu.VMEM((seqs_per_step, n_chunks, n_pairs, GLA_DV, 2 * LANES), BF16)],
        compiler_params=_params(1),
        name="gla",
    )(*args)


def kernel(x_prompt, x_sample, cache_k, cache_v, state_gla_fwd, state_gla_bwd, c, c_ctx,
           w_ada, b_ada, norm_pre, norm_post, ffn_w1, ffn_w3, ffn_w2, w_in,
           gla_w_up, gla_b_up, gla_norm, attn_sink, w_out):
    depth = w_in.shape[0]
    assert depth == 1, "single trunk layer"
    batch, seq = x_prompt.shape[0], x_prompt.shape[1]
    dec_batch, dec_seq = x_sample.shape[0], x_sample.shape[1]
    past = cache_k.shape[2]
    l = 0

    cond = jnp.concatenate(
        [c_ctx[None, :], c, jnp.zeros((MOD_ROWS - 1 - dec_batch, D_MODEL), F32)], axis=0)
    mod = _ada_modulation(cond, w_ada[l], b_ada[l][None, :])

    npre, npost = norm_pre[l], norm_post[l]
    ffn_b = [w[l].astype(BF16) for w in (ffn_w1, ffn_w3, ffn_w2)]
    w_in_b = w_in[l].astype(BF16)
    w_out_b = w_out[l].astype(BF16)
    zeros = jnp.zeros((GLA_LOW_RANK, GLA_QK), F32)
    w_up = jnp.concatenate(
        [jnp.concatenate([gla_w_up[l, 0], zeros], axis=1),
         jnp.concatenate([zeros, gla_w_up[l, 1]], axis=1)], axis=0).astype(BF16)
    b_up = gla_b_up[l].reshape(1, 2 * GLA_QK)
    gnorm = gla_norm[l][None, :]
    sink = attn_sink[l]

    def trunk(x, latent):
        n_batch, n_seq = (dec_batch, dec_seq) if latent else (batch, seq)
        latent_len = n_seq if latent else None
        x1 = _ffn_first(x, mod, latent_len, npre, npost, *ffn_b)
        att_in, *cache_t = _project(
            x1, mod, latent_len, None if latent else n_seq, npre, w_in_b)
        gla_args = (x1, mod, latent, npre, w_in_b, w_up, b_up, gnorm, n_batch, n_seq)
        if latent:
            att = _latent_attention(sink, att_in, cache_k[:, l].reshape(dec_batch, past, ATT_KV),
                                    cache_v[:, l].reshape(dec_batch, past, ATT_KV), n_batch, n_seq)
            (gla,) = _gla(*gla_args, 1,
                          s0=(state_gla_fwd[:, l:l + 1], state_gla_bwd[:, l:l + 1]))
            extras = ()
        else:
            att = _context_attention(sink, att_in, n_batch, n_seq)
            gla, s_f, s_b = _gla(*gla_args, CTX_SEQS_PER_STEP, emit_state=True)
            k_new, v_new = (jnp.transpose(c_t, (0, 1, 4, 2, 3)) for c_t in cache_t)
            extras = (k_new, v_new, s_f, s_b)
        y = _mix_ffn(x1, att, gla, w_out_b, mod, latent_len, npre, npost, *ffn_b)
        return y.reshape(n_batch, n_seq, D_MODEL), extras

    y_prompt, (k_new, v_new, s_f, s_b) = trunk(x_prompt.reshape(batch * seq, D_MODEL), False)
    y_sample, _ = trunk(x_sample.reshape(dec_batch * dec_seq, D_MODEL), True)
    return (y_prompt, y_sample, k_new, v_new, s_f, s_b)
```

```python
import functools

import numpy as np
import jax
import jax.numpy as jnp
from jax import lax
from jax.experimental import pallas as pl
from jax.experimental.pallas import tpu as pltpu

F32 = jnp.float32
BF16 = jnp.bfloat16

D_MODEL = 1024
GRID_W = 64
N_Q_HEADS = 8
N_KV_HEADS = 2
HEAD_DIM = 64
WINDOW = 128
BLOCK = 128
ROPE_BASE = 10000.0
GLA_HEADS = 4
GLA_DK = 64
GLA_DV = 128
GLA_LOW_RANK = 16
GLA_TAU = 16.0
D_FF = 2816
N_MOD = 9
EPS = 1e-6
NEG = -1e30

ATT_Q = N_Q_HEADS * HEAD_DIM
ATT_KV = N_KV_HEADS * HEAD_DIM
GLA_QK = GLA_HEADS * GLA_DK
GLA_V = GLA_HEADS * GLA_DV
OFF_Q = 0
OFF_K = OFF_Q + ATT_Q
OFF_V = OFF_K + ATT_KV
OFF_GQ = OFF_V + ATT_KV
OFF_GK = OFF_GQ + GLA_QK
OFF_GV = OFF_GK + GLA_QK
OFF_GG = OFF_GV + GLA_V
OFF_LR = OFF_GG + GLA_V
IN_WIDTH = OFF_LR + 2 * GLA_LOW_RANK
A_Q = 0
A_K = A_Q + ATT_Q
A_KS = A_K + ATT_KV
A_V = A_KS + ATT_KV
A_VS = A_V + ATT_KV
ATT_IN_W = A_VS + ATT_KV
G_Q = 0
G_K = G_Q + GLA_QK
G_G = G_K + GLA_QK
G_LA = G_G + GLA_V
GLA_F_W = G_LA + 2 * GLA_QK
LOG2_E = 1.4426950408889634
ATT_Q_SCALE = HEAD_DIM ** -0.5 * LOG2_E

LANES = 128
SUBLANES = 8
HALF = LANES // 2
VMEM_MIB = {"ada_modulation": 16, "ffn_first": 40, "mix_ffn": 46, "project": 28,
            "context_attention": 16, "latent_attention": 24, "gla": 30}

TOKEN_TILE = 512
FF_SPLITS = (0, 1536, D_FF)
ADA_K_TILE = 128
CTX_SEQS_PER_STEP = 4
LAT_BLOCKS_PER_STEP = 4
GLA_CHUNK = 128
MOD_ROWS = 8


def _params(n_axes, vmem_mib):
    return pltpu.CompilerParams(
        dimension_semantics=("arbitrary",) * n_axes, vmem_limit_bytes=vmem_mib * 1024 * 1024)


def _resident(shape):
    zeros = (0,) * len(shape)
    return pl.BlockSpec(shape, lambda *_: zeros, pipeline_mode=pl.Buffered(1))


def _sigmoid(x):
    return 1.0 / (1.0 + jnp.exp(-x))


def _silu(x):
    return x * _sigmoid(x)


def _rms(x, g):
    return x * lax.rsqrt(jnp.mean(x * x, axis=-1, keepdims=True) + EPS) * g


def _dot(a, b):
    return jnp.dot(a, b, preferred_element_type=F32)


def _dot_t(a, b):
    return lax.dot_general(a, b, (((1,), (1,)), ((), ())), preferred_element_type=F32)


def _ada_kernel(cond_ref, w_ref, b_ref, o_ref):
    @pl.when(pl.program_id(0) == 0)
    def _():
        for m in range(N_MOD):
            o_ref[m] = jnp.broadcast_to(b_ref[:, m * D_MODEL:(m + 1) * D_MODEL],
                                        (MOD_ROWS, D_MODEL))
    part = _dot(_silu(cond_ref[...]).astype(BF16), w_ref[...].astype(BF16))
    for m in range(N_MOD):
        o_ref[m] += part[:, m * D_MODEL:(m + 1) * D_MODEL]


def _ada_modulation(cond, w_ada, b_ada):
    k_dim, n = w_ada.shape
    return pl.pallas_call(
        _ada_kernel,
        out_shape=jax.ShapeDtypeStruct((N_MOD, MOD_ROWS, D_MODEL), F32),
        grid=(k_dim // ADA_K_TILE,),
        in_specs=[
            pl.BlockSpec((MOD_ROWS, ADA_K_TILE), lambda k: (0, k)),
            pl.BlockSpec((ADA_K_TILE, n), lambda k: (k, 0)),
            pl.BlockSpec((1, n), lambda k: (0, 0)),
        ],
        out_specs=pl.BlockSpec((N_MOD, MOD_ROWS, D_MODEL), lambda k: (0, 0, 0)),
        compiler_params=_params(1, VMEM_MIB["ada_modulation"]),
        name="ada_modulation",
    )(cond, w_ada, b_ada)


class _Mod:
    def __init__(self, mod_ref, tiles_per_seq):
        self.ref = mod_ref
        self.row = 0 if tiles_per_seq is None else 1 + pl.program_id(0) // tiles_per_seq

    def __getitem__(self, m):
        return self.ref[m, pl.ds(self.row, 1), :]


def _modulated(x, mod, npre_ref, i):
    return _rms(x, npre_ref[i:i + 1, :]) * (1.0 + mod[3 * i + 1]) + mod[3 * i]


def _residual(x, out, mod, npost_ref, i, weight):
    return x + (weight * mod[3 * i + 2]) * _rms(out, npost_ref[i:i + 1, :])


def _ffn_sublayer(x, i, mod, npre_ref, npost_ref, w1_ref, w3_ref, w2_ref):
    h = _modulated(x, mod, npre_ref, i).astype(BF16)
    acc = None
    for lo, hi in zip(FF_SPLITS[:-1], FF_SPLITS[1:]):
        a = _dot(h, w1_ref[:, lo:hi])
        g = _dot(h, w3_ref[:, lo:hi])
        part = _dot((_silu(a) * g).astype(BF16), w2_ref[lo:hi, :])
        acc = part if acc is None else acc + part
    return _residual(x, acc, mod, npost_ref, i, 0.5)


def _ffn_first_kernel(x_ref, mod_ref, npre_ref, npost_ref, w1_ref, w3_ref, w2_ref, o_ref,
                      *, tiles_per_seq):
    mod = _Mod(mod_ref, tiles_per_seq)
    o_ref[...] = _ffn_sublayer(x_ref[...], 0, mod, npre_ref, npost_ref, w1_ref, w3_ref, w2_ref)


def _mix_ffn_kernel(x_ref, att_ref, gla_ref, wo_ref, mod_ref, npre_ref, npost_ref,
                    w1_ref, w3_ref, w2_ref, o_ref, *, tiles_per_seq):
    mod = _Mod(mod_ref, tiles_per_seq)
    mix = _dot(att_ref[...], wo_ref[0:ATT_Q, :]) + _dot(gla_ref[...], wo_ref[ATT_Q:, :])
    x = _residual(x_ref[...], mix, mod, npost_ref, 1, 1.0)
    o_ref[...] = _ffn_sublayer(x, 2, mod, npre_ref, npost_ref, w1_ref, w3_ref, w2_ref)


def _rope_tile(x, cos, sin_up, sin_dn):
    up = pltpu.roll(x, LANES - 16, axis=1)
    dn = pltpu.roll(x, 16, axis=1)
    return x * cos + up * sin_up + dn * sin_dn


def _proj_kernel(*refs, tiles_per_seq, cache_seq):
    x_ref, mod_ref, npre_ref, win_ref, wup_ref, bup_ref = refs[:6]
    rope_refs = refs[6:9] if tiles_per_seq is not None else None
    outs = refs[6 + (3 if rope_refs else 0):]
    att_ref, glaf_ref, gv_ref = outs[:3]
    h = _modulated(x_ref[...], _Mod(mod_ref, tiles_per_seq), npre_ref, 1).astype(BF16)
    q = _dot(h, win_ref[:, OFF_Q:OFF_K])
    kv = _dot(h, win_ref[:, OFF_K:OFF_GQ])
    k, v = kv[:, :ATT_KV], kv[:, ATT_KV:]
    lr = _dot(h, win_ref[:, OFF_LR:IN_WIDTH])
    z = _dot(lr.astype(BF16), wup_ref[...]) + bup_ref[...]
    if rope_refs:
        cos, sup, sdn = (r[...] for r in rope_refs)
        q = jnp.concatenate([_rope_tile(q[:, j * LANES:(j + 1) * LANES], cos, sup, sdn)
                             for j in range(ATT_Q // LANES)], axis=1)
        k_att = _rope_tile(k, cos, sup, sdn)
    else:
        k_att = k
    att_ref[:, A_Q:A_K] = (q * ATT_Q_SCALE).astype(BF16)
    att_ref[:, A_K:A_KS] = k_att.astype(BF16)
    att_ref[:, A_KS:A_V] = pltpu.roll(k_att, HALF, axis=1).astype(BF16)
    att_ref[:, A_V:A_VS] = v.astype(BF16)
    att_ref[:, A_VS:ATT_IN_W] = pltpu.roll(v, HALF, axis=1).astype(BF16)
    if cache_seq is not None:
        for t, dst in ((k.T, outs[3]), (v.T, outs[4])):
            for b in range(TOKEN_TILE // cache_seq):
                dst[b, 0] = t[:, b * cache_seq:(b + 1) * cache_seq].reshape(
                    N_KV_HEADS, HEAD_DIM, cache_seq)
    glaf_ref[:, G_Q:G_G] = _dot(h, win_ref[:, OFF_GQ:OFF_GV])
    log_sig = jnp.minimum(z, 0.0) - jnp.log(1.0 + jnp.exp(-jnp.abs(z)))
    glaf_ref[:, G_LA:GLA_F_W] = log_sig * (1.0 / GLA_TAU)
    gv_ref[...] = _dot(h, win_ref[:, OFF_GV:OFF_GG]).astype(BF16)
    glaf_ref[:, G_G:G_LA] = _dot(h, win_ref[:, OFF_GG:OFF_LR])


def _token_spec(width):
    return pl.BlockSpec((TOKEN_TILE, width), lambda i: (i, 0))


def _ffn_weight_spec(shape, j):
    return pl.BlockSpec((None,) + tuple(shape[1:]), lambda i: (j, 0, 0),
                        pipeline_mode=pl.Buffered(1))


def _tiles_per_seq(latent_len):
    return None if latent_len is None else latent_len // TOKEN_TILE


def _ffn_first(x, mod, latent_len, npre, npost, w1, w3, w2):
    t = x.shape[0]
    return pl.pallas_call(
        functools.partial(_ffn_first_kernel, tiles_per_seq=_tiles_per_seq(latent_len)),
        out_shape=jax.ShapeDtypeStruct((t, D_MODEL), F32),
        grid=(t // TOKEN_TILE,),
        in_specs=[_token_spec(D_MODEL), _resident(mod.shape), _resident(npre.shape),
                  _resident(npost.shape), _ffn_weight_spec(w1.shape, 0),
                  _ffn_weight_spec(w3.shape, 0), _ffn_weight_spec(w2.shape, 0)],
        out_specs=_token_spec(D_MODEL),
        compiler_params=_params(1, VMEM_MIB["ffn_first"]),
        name="ffn_first",
    )(x, mod, npre, npost, w1, w3, w2)


def _mix_ffn(x, att, gla, w_out, mod, latent_len, npre, npost, w1, w3, w2):
    t = x.shape[0]
    return pl.pallas_call(
        functools.partial(_mix_ffn_kernel, tiles_per_seq=_tiles_per_seq(latent_len)),
        out_shape=jax.ShapeDtypeStruct((t, D_MODEL), F32),
        grid=(t // TOKEN_TILE,),
        in_specs=[_token_spec(D_MODEL), _token_spec(ATT_Q), _token_spec(GLA_V),
                  _resident(w_out.shape), _resident(mod.shape), _resident(npre.shape),
                  _resident(npost.shape), _ffn_weight_spec(w1.shape, 1),
                  _ffn_weight_spec(w3.shape, 1), _ffn_weight_spec(w2.shape, 1)],
        out_specs=_token_spec(D_MODEL),
        compiler_params=_params(1, VMEM_MIB["mix_ffn"]),
        name="mix_ffn",
    )(x, att, gla, w_out, mod, npre, npost, w1, w3, w2)


def _project(x, mod, latent_len, cache_seq, npre, w_in, w_up, b_up):
    t = x.shape[0]
    tiles = _tiles_per_seq(latent_len)
    in_specs = [_token_spec(D_MODEL), _resident(mod.shape), _resident(npre.shape),
                _resident(w_in.shape), _resident(w_up.shape), _resident(b_up.shape)]
    args = [x, mod, npre, w_in, w_up, b_up]
    if tiles is not None:
        in_specs += [pl.BlockSpec((TOKEN_TILE, LANES), lambda i: (i % tiles, 0))] * 3
        args += [jnp.asarray(tab) for tab in _rope_tables(latent_len)]
    outs = ((ATT_IN_W, BF16), (GLA_F_W, F32), (GLA_V, BF16))
    out_shape = [jax.ShapeDtypeStruct((t, w), dt) for w, dt in outs]
    out_specs = [_token_spec(w) for w, _ in outs]
    if cache_seq is not None:
        seqs = TOKEN_TILE // cache_seq
        cache = (t // cache_seq, 1, N_KV_HEADS, HEAD_DIM, cache_seq)
        out_shape += [jax.ShapeDtypeStruct(cache, F32)] * 2
        out_specs += [pl.BlockSpec((seqs,) + cache[1:], lambda i: (i, 0, 0, 0, 0))] * 2
    return pl.pallas_call(
        functools.partial(_proj_kernel, tiles_per_seq=tiles, cache_seq=cache_seq),
        out_shape=out_shape,
        grid=(t // TOKEN_TILE,),
        in_specs=in_specs,
        out_specs=out_specs,
        compiler_params=_params(1, VMEM_MIB["project"]),
        name="project",
    )(*args)


def _rope_tables(seq_len):
    half = HEAD_DIM // 2
    inv_freq = np.float32(ROPE_BASE) ** (-np.arange(0, half, 2, dtype=np.float32) / half)
    pos = np.arange(seq_len)
    row = (pos // GRID_W).astype(np.float32)
    col = (pos % GRID_W).astype(np.float32)
    within = np.arange(LANES) % HEAD_DIM
    idx = within % half
    freq = inv_freq[idx % (half // 2)].astype(np.float32)
    p = np.where((within // half == 0)[None, :], row[:, None], col[:, None])
    ang = (p * freq[None, :]).astype(np.float32)
    cos, sin = np.cos(ang).astype(np.float32), np.sin(ang).astype(np.float32)
    first = (idx < half // 2)[None, :]
    zero = np.float32(0.0)
    return cos, np.where(first, -sin, zero), np.where(first, zero, sin)


def _attend(q_ref, q_rows, nq, segs, sink_ref):
    lane = lax.broadcasted_iota(jnp.int32, (1, LANES), 1)
    half_of = [lane < HALF, lane >= HALF]
    rows = lax.broadcasted_iota(jnp.int32, (2 * nq, 1), 0)
    zero = jnp.zeros((), BF16)
    out = [None] * (ATT_Q // LANES)
    for g in range(N_KV_HEADS):
        tiles = (2 * g, 2 * g + 1)
        for e in range(2):
            qm = jnp.concatenate(
                [jnp.where(half_of[e], q_ref[q_rows, j * LANES:(j + 1) * LANES], zero)
                 for j in tiles], axis=0)
            sink = jnp.where(rows < nq, sink_ref[2 * tiles[0] + e],
                             sink_ref[2 * tiles[1] + e]) * LOG2_E
            scores = []
            for k, k_sw, _, _, m in segs:
                s = _dot_t(qm, k if e == g else k_sw)
                scores.append(s if m is None else jnp.where(m, s, NEG))
            mx = sink
            for s in scores:
                mx = jnp.maximum(mx, jnp.max(s, axis=-1, keepdims=True))
            probs = [jnp.exp2(s - mx) for s in scores]
            den = jnp.exp2(sink - mx)
            for p in probs:
                den = den + jnp.sum(p, axis=-1, keepdims=True)
            o = None
            for p, (_, _, v, v_sw, _) in zip(probs, segs):
                part = _dot(p.astype(BF16), jnp.where(half_of[e], v if e == g else v_sw, zero))
                o = part if o is None else o + part
            o = o * (1.0 / den)
            for r, j in enumerate(tiles):
                blk = o[r * nq:(r + 1) * nq, :]
                out[j] = blk if out[j] is None else out[j] + blk
    return out


def _context_attn_kernel(sink_ref, a_ref, o_ref, *, seq_len):
    for s in range(a_ref.shape[0] // seq_len):
        rows = slice(s * seq_len, (s + 1) * seq_len)
        segs = [(a_ref[rows, A_K:A_KS], a_ref[rows, A_KS:A_V],
                 a_ref[rows, A_V:A_VS], a_ref[rows, A_VS:ATT_IN_W], None)]
        for j, t in enumerate(_attend(a_ref, rows, seq_len, segs, sink_ref)):
            o_ref[rows, j * LANES:(j + 1) * LANES] = t.astype(o_ref.dtype)


def _latent_attn_kernel(sink_ref, q_ref, kv_ref, kc_ref, vc_ref, o_ref, *, seq_len):
    k_ctx, v_ctx = kc_ref[0], vc_ref[0]
    ctx = (k_ctx.astype(BF16), pltpu.roll(k_ctx, HALF, axis=1).astype(BF16),
           v_ctx.astype(BF16), pltpu.roll(v_ctx, HALF, axis=1).astype(BF16), None)
    span = 3 * BLOCK
    per_step = q_ref.shape[0] // BLOCK
    for s in range(per_step):
        i = pl.program_id(1) * per_step + s
        start = pl.multiple_of(jnp.clip((i - 1) * BLOCK, 0, seq_len - span), BLOCK)
        kpos = start + lax.broadcasted_iota(jnp.int32, (1, span), 1)
        qpos = i * BLOCK + lax.broadcasted_iota(jnp.int32, (BLOCK, 1), 0)
        valid = jnp.abs(kpos - qpos) <= WINDOW
        valid2 = jnp.concatenate([valid, valid], axis=0)
        kv = kv_ref[pl.ds(start, span), :]
        segs = [tuple(kv[:, c * ATT_KV:(c + 1) * ATT_KV] for c in range(4)) + (valid2,), ctx]
        rows = slice(s * BLOCK, (s + 1) * BLOCK)
        for j, t in enumerate(_attend(q_ref, rows, BLOCK, segs, sink_ref)):
            o_ref[rows, j * LANES:(j + 1) * LANES] = t.astype(o_ref.dtype)


def _smem_spec():
    return pl.BlockSpec(memory_space=pltpu.SMEM)


def _context_attention(sink, att_in, batch, seq_len):
    rows = CTX_SEQS_PER_STEP * seq_len

    def seq(width):
        return pl.BlockSpec((rows, width), lambda b: (b, 0))
    return pl.pallas_call(
        functools.partial(_context_attn_kernel, seq_len=seq_len),
        out_shape=jax.ShapeDtypeStruct((batch * seq_len, ATT_Q), BF16),
        grid=(batch // CTX_SEQS_PER_STEP,),
        in_specs=[_smem_spec(), seq(ATT_IN_W)],
        out_specs=seq(ATT_Q),
        compiler_params=_params(1, VMEM_MIB["context_attention"]),
        name="context_attention",
    )(sink, att_in)


def _latent_attention(sink, att_in, k_ctx, v_ctx, batch, seq_len):
    steps = seq_len // (LAT_BLOCKS_PER_STEP * BLOCK)
    rows = LAT_BLOCKS_PER_STEP * BLOCK
    past = k_ctx.shape[1]
    assert A_K == ATT_IN_W - A_K, "q and the key / value columns are the two halves of att_in"
    return pl.pallas_call(
        functools.partial(_latent_attn_kernel, seq_len=seq_len),
        out_shape=jax.ShapeDtypeStruct((batch * seq_len, ATT_Q), BF16),
        grid=(batch, steps),
        in_specs=[_smem_spec(),
                  pl.BlockSpec((rows, ATT_Q), lambda b, i: (b * steps + i, 0)),
                  pl.BlockSpec((seq_len, ATT_IN_W - A_K), lambda b, i: (b, 1)),
                  pl.BlockSpec((1, past, ATT_KV), lambda b, i: (b, 0, 0)),
                  pl.BlockSpec((1, past, ATT_KV), lambda b, i: (b, 0, 0))],
        out_specs=pl.BlockSpec((rows, ATT_Q), lambda b, i: (b * steps + i, 0)),
        compiler_params=_params(2, VMEM_MIB["latent_attention"]),
        name="latent_attention",
    )(sink, att_in, att_in, k_ctx, v_ctx)


def _split2(x):
    hi = x.astype(BF16)
    lo = (x - hi.astype(F32)).astype(BF16)
    return hi, lo


def _gla_kernel(*refs, n_chunks, has_s0, emit_state):
    refs = list(refs)
    gf_ref, gv_ref, gn_ref = refs[:3]
    pos = 3
    s0_refs = refs[pos:pos + 2] if has_s0 else None
    pos += 2 if has_s0 else 0
    o_ref = refs[pos]
    pos += 1
    sfin_refs = refs[pos:pos + 2] if emit_state else None
    pos += 2 if emit_state else 0
    cum_ref, kv_ref, sent_ref = refs[pos:]

    C = GLA_CHUNK
    n_pairs = GLA_QK // LANES
    lane = lax.broadcasted_iota(jnp.int32, (1, LANES), 1)
    half_of = [lane < HALF, lane >= HALF]
    r_i = lax.broadcasted_iota(jnp.int32, (C, C), 0)
    c_i = lax.broadcasted_iota(jnp.int32, (C, C), 1)
    lower = c_i <= r_i
    upper = c_i >= r_i
    tri = jnp.concatenate([jnp.where(lower, 1.0, 0.0), jnp.where(upper, 1.0, 0.0)],
                          axis=0).astype(BF16)
    zeros_cc = jnp.zeros((C, C), F32)
    gnorm = gn_ref[...]
    qscale = GLA_DK ** -0.5

    def one_sequence(s):
        def chunk_rows(n):
            return slice((s * n_chunks + n) * C, (s * n_chunks + n + 1) * C)

        for n in range(n_chunks):
            rows = chunk_rows(n)
            hi, lo = _split2(gf_ref[rows, G_LA:GLA_F_W])
            sums = _dot(tri, jnp.concatenate([hi, lo], axis=1))
            cum_f = sums[:C, 0:GLA_QK] + sums[:C, 2 * GLA_QK:3 * GLA_QK]
            cum_b = sums[C:, GLA_QK:2 * GLA_QK] + sums[C:, 3 * GLA_QK:]
            cum_ref[0, rows, :] = cum_f
            cum_ref[1, rows, :] = cum_b
            k = gf_ref[rows, G_K:G_G]
            k_in = (k * jnp.exp(cum_f[C - 1:C, :] - cum_f), k * jnp.exp(cum_b[0:1, :] - cum_b))
            for p in range(n_pairs):
                sl = slice(p * LANES, (p + 1) * LANES)
                kv_t = None
                for e in range(2):
                    h = 2 * p + e
                    v_t = gv_ref[rows, h * GLA_DV:(h + 1) * GLA_DV].T
                    k_e = jnp.concatenate(
                        [jnp.where(half_of[e], k_in[d][:, sl], 0.0) for d in range(2)], axis=1)
                    part = _dot(v_t, k_e.astype(BF16))
                    kv_t = part if kv_t is None else kv_t + part
                kv_ref[s, n, p] = kv_t

        for p in range(n_pairs):
            sl = slice(p * LANES, (p + 1) * LANES)
            st = []
            for d in range(2):
                if has_s0:
                    s0 = s0_refs[d][s, 0, 2 * p:2 * p + 2, :, :].reshape(2 * GLA_DK, GLA_DV)
                    st.append(s0.T)
                else:
                    st.append(jnp.zeros((GLA_DV, 2 * GLA_DK), F32))
            for i in range(n_chunks):
                for d, n in ((0, i), (1, n_chunks - 1 - i)):
                    tot_row = (s * n_chunks + n) * C + (C - 1 if d == 0 else 0)
                    decay = jnp.exp(cum_ref[d, tot_row:tot_row + 1, sl])
                    sent_ref[s, n, p, :, d * LANES:(d + 1) * LANES] = st[d].astype(BF16)
                    st[d] = decay * st[d] + kv_ref[s, n, p, :, d * LANES:(d + 1) * LANES]
            if emit_state:
                for d in range(2):
                    sfin_refs[d][s, 0, 2 * p:2 * p + 2, :, :] = st[d].T.reshape(2, GLA_DK, GLA_DV)

        for n in range(n_chunks):
            rows = chunk_rows(n)
            q = gf_ref[rows, G_Q:G_K] * qscale
            k = gf_ref[rows, G_K:G_G]
            qs, ks, qin = [], [], []
            for d in range(2):
                cum = cum_ref[d, rows, :]
                ref = cum[C // 2:C // 2 + 1, :]
                qs.append(q * jnp.exp(cum - ref))
                ks.append((k * jnp.exp(ref - cum)).astype(BF16))
                qin.append(q * jnp.exp(cum))
            for p in range(n_pairs):
                sl = slice(p * LANES, (p + 1) * LANES)
                lhs = jnp.concatenate(
                    [jnp.where(half_of[e], qs[d][:, sl], 0.0) for e in range(2) for d in range(2)],
                    axis=0).astype(BF16)
                sc = _dot_t(lhs, jnp.concatenate([ks[0][:, sl], ks[1][:, sl]], axis=0))
                prob = []
                for e in range(2):
                    s_f = sc[(2 * e) * C:(2 * e + 1) * C, :C]
                    s_b = sc[(2 * e + 1) * C:(2 * e + 2) * C, C:]
                    prob.append(jnp.where(lower, s_f, 0.0) + jnp.where(upper, s_b, 0.0))
                p_blk = jnp.concatenate(
                    [jnp.concatenate([prob[0], zeros_cc], axis=1),
                     jnp.concatenate([zeros_cc, prob[1]], axis=1)], axis=0).astype(BF16)
                v2 = jnp.concatenate(
                    [gv_ref[rows, (2 * p + e) * GLA_DV:(2 * p + e + 1) * GLA_DV] for e in range(2)],
                    axis=0)
                q_in = jnp.concatenate(
                    [jnp.concatenate([jnp.where(half_of[e], qin[d][:, sl], 0.0) for d in range(2)],
                                     axis=1) for e in range(2)], axis=0).astype(BF16)
                o2 = _dot(p_blk, v2) + _dot_t(q_in, sent_ref[s, n, p])
                for e in range(2):
                    h = 2 * p + e
                    o = o2[e * C:(e + 1) * C, :]
                    o = o * lax.rsqrt(jnp.mean(o * o, axis=-1, keepdims=True) + EPS) * gnorm
                    gate = gf_ref[rows, G_G + h * GLA_DV:G_G + (h + 1) * GLA_DV]
                    o_ref[rows, h * GLA_DV:(h + 1) * GLA_DV] = (o * _silu(gate)).astype(o_ref.dtype)

    for s in range(kv_ref.shape[0]):
        one_sequence(s)


def _gla(gla_f, gv, gnorm, batch, seq_len, seqs_per_step, s0=None, emit_state=False):
    n_chunks = seq_len // GLA_CHUNK
    has_s0 = s0 is not None
    rows = seqs_per_step * seq_len

    def seq(width):
        return pl.BlockSpec((rows, width), lambda b: (b, 0))
    state_spec = pl.BlockSpec((seqs_per_step, 1, GLA_HEADS, GLA_DK, GLA_DV),
                              lambda b: (b, 0, 0, 0, 0))
    in_specs = [seq(GLA_F_W), seq(GLA_V), pl.BlockSpec((1, GLA_DV), lambda b: (0, 0))]
    args = [gla_f, gv, gnorm]
    if has_s0:
        in_specs += [state_spec, state_spec]
        args += list(s0)
    out_shape = [jax.ShapeDtypeStruct((batch * seq_len, GLA_V), BF16)]
    out_specs = [seq(GLA_V)]
    if emit_state:
        out_shape += [jax.ShapeDtypeStruct((batch, 1, GLA_HEADS, GLA_DK, GLA_DV), F32)] * 2
        out_specs += [state_spec, state_spec]
    n_pairs = GLA_QK // LANES
    return pl.pallas_call(
        functools.partial(_gla_kernel, n_chunks=n_chunks, has_s0=has_s0, emit_state=emit_state),
        out_shape=out_shape,
        grid=(batch // seqs_per_step,),
        in_specs=in_specs,
        out_specs=out_specs,
        scratch_shapes=[pltpu.VMEM((2, rows, GLA_QK), F32),
                        pltpu.VMEM((seqs_per_step, n_chunks, n_pairs, GLA_DV, 2 * LANES), F32),
                        pltpu.VMEM((seqs_per_step, n_chunks, n_pairs, GLA_DV, 2 * LANES), BF16)],
        compiler_params=_params(1, VMEM_MIB["gla"]),
        name="gla",
    )(*args)


def kernel(x_prompt, x_sample, cache_k, cache_v, state_gla_fwd, state_gla_bwd, c, c_ctx,
           w_ada, b_ada, norm_pre, norm_post, ffn_w1, ffn_w3, ffn_w2, w_in,
           gla_w_up, gla_b_up, gla_norm, attn_sink, w_out):
    depth = w_in.shape[0]
    assert depth == 1, "single trunk layer"
    batch, seq = x_prompt.shape[0], x_prompt.shape[1]
    dec_batch, dec_seq = x_sample.shape[0], x_sample.shape[1]
    past = cache_k.shape[2]
    l = 0

    cond = jnp.concatenate(
        [c_ctx[None, :], c, jnp.zeros((MOD_ROWS - 1 - dec_batch, D_MODEL), F32)], axis=0)
    mod = _ada_modulation(cond, w_ada[l], b_ada[l][None, :])

    npre, npost = norm_pre[l], norm_post[l]
    ffn_b = [w[l].astype(BF16) for w in (ffn_w1, ffn_w3, ffn_w2)]
    w_in_b = w_in[l].astype(BF16)
    w_out_b = w_out[l].astype(BF16)
    zeros = jnp.zeros((GLA_LOW_RANK, GLA_QK), F32)
    w_up = jnp.concatenate(
        [jnp.concatenate([gla_w_up[l, 0], zeros], axis=1),
         jnp.concatenate([zeros, gla_w_up[l, 1]], axis=1)], axis=0).astype(BF16)
    b_up = gla_b_up[l].reshape(1, 2 * GLA_QK)
    gnorm = gla_norm[l][None, :]
    sink = attn_sink[l]

    def trunk(x, latent):
        n_batch, n_seq = (dec_batch, dec_seq) if latent else (batch, seq)
        latent_len = n_seq if latent else None
        x1 = _ffn_first(x, mod, latent_len, npre, npost, *ffn_b)
        att_in, gla_f, gv, *cache_t = _project(
            x1, mod, latent_len, None if latent else n_seq, npre, w_in_b, w_up, b_up)
        if latent:
            att = _latent_attention(sink, att_in, cache_k[:, l].reshape(dec_batch, past, ATT_KV),
                                    cache_v[:, l].reshape(dec_batch, past, ATT_KV), n_batch, n_seq)
            (gla,) = _gla(gla_f, gv, gnorm, n_batch, n_seq, 1,
                          s0=(state_gla_fwd[:, l:l + 1], state_gla_bwd[:, l:l + 1]))
            extras = ()
        else:
            att = _context_attention(sink, att_in, n_batch, n_seq)
            gla, s_f, s_b = _gla(gla_f, gv, gnorm, n_batch, n_seq, CTX_SEQS_PER_STEP,
                                 emit_state=True)
            k_new, v_new = (jnp.transpose(c_t, (0, 1, 4, 2, 3)) for c_t in cache_t)
            extras = (k_new, v_new, s_f, s_b)
        y = _mix_ffn(x1, att, gla, w_out_b, mod, latent_len, npre, npost, *ffn_b)
        return y.reshape(n_batch, n_seq, D_MODEL), extras

    y_prompt, (k_new, v_new, s_f, s_b) = trunk(x_prompt.reshape(batch * seq, D_MODEL), False)
    y_sample, _ = trunk(x_sample.reshape(dec_batch * dec_seq, D_MODEL), True)
    return (y_prompt, y_sample, k_new, v_new, s_f, s_b)
```

```python
import functools

import numpy as np
import jax
import jax.numpy as jnp
from jax import lax
from jax.experimental import pallas as pl
from jax.experimental.pallas import tpu as pltpu

F32 = jnp.float32
BF16 = jnp.bfloat16

D_MODEL = 1024
GRID_W = 64
N_Q_HEADS = 8
N_KV_HEADS = 2
HEAD_DIM = 64
WINDOW = 128
BLOCK = 128
ROPE_BASE = 10000.0
GLA_HEADS = 4
GLA_DK = 64
GLA_DV = 128
GLA_LOW_RANK = 16
GLA_TAU = 16.0
D_FF = 2816
N_MOD = 9
EPS = 1e-6
NEG = -1e30

ATT_Q = N_Q_HEADS * HEAD_DIM
ATT_KV = N_KV_HEADS * HEAD_DIM
GLA_QK = GLA_HEADS * GLA_DK
GLA_V = GLA_HEADS * GLA_DV
OFF_Q = 0
OFF_K = OFF_Q + ATT_Q
OFF_V = OFF_K + ATT_KV
OFF_GQ = OFF_V + ATT_KV
OFF_GK = OFF_GQ + GLA_QK
OFF_GV = OFF_GK + GLA_QK
OFF_GG = OFF_GV + GLA_V
OFF_LR = OFF_GG + GLA_V
IN_WIDTH = OFF_LR + 2 * GLA_LOW_RANK
A_Q = 0
A_K = A_Q + ATT_Q
A_KS = A_K + ATT_KV
A_V = A_KS + ATT_KV
A_VS = A_V + ATT_KV
ATT_IN_W = A_VS + ATT_KV
G_Q = 0
G_K = G_Q + GLA_QK
G_G = G_K + GLA_QK
G_LA = G_G + GLA_V
GLA_F_W = G_LA + 2 * GLA_QK
LOG2_E = 1.4426950408889634
ATT_Q_SCALE = HEAD_DIM ** -0.5 * LOG2_E

LANES = 128
SUBLANES = 8
HALF = LANES // 2
VMEM_MIB = {"ada_modulation": 56, "ffn_first": 40, "mix_ffn": 46, "project": 28,
            "context_attention": 56, "latent_attention": 56, "gla": 56}

TOKEN_TILE = 512
FF_SPLITS = (0, 1536, D_FF)
ADA_K_TILE = 128
CTX_SEQS_PER_STEP = 4
LAT_BLOCKS_PER_STEP = 4
GLA_CHUNK = 128
MOD_ROWS = 8


def _params(n_axes, vmem_mib):
    return pltpu.CompilerParams(
        dimension_semantics=("arbitrary",) * n_axes, vmem_limit_bytes=vmem_mib * 1024 * 1024)


def _resident(shape):
    zeros = (0,) * len(shape)
    return pl.BlockSpec(shape, lambda *_: zeros, pipeline_mode=pl.Buffered(1))


def _sigmoid(x):
    return 1.0 / (1.0 + jnp.exp(-x))


def _silu(x):
    return x * _sigmoid(x)


def _rms(x, g):
    return x * lax.rsqrt(jnp.mean(x * x, axis=-1, keepdims=True) + EPS) * g


def _dot(a, b):
    return jnp.dot(a, b, preferred_element_type=F32)


def _dot_t(a, b):
    return lax.dot_general(a, b, (((1,), (1,)), ((), ())), preferred_element_type=F32)


def _ada_kernel(cond_ref, w_ref, b_ref, o_ref):
    @pl.when(pl.program_id(0) == 0)
    def _():
        for m in range(N_MOD):
            o_ref[m] = jnp.broadcast_to(b_ref[:, m * D_MODEL:(m + 1) * D_MODEL],
                                        (MOD_ROWS, D_MODEL))
    part = _dot(_silu(cond_ref[...]).astype(BF16), w_ref[...].astype(BF16))
    for m in range(N_MOD):
        o_ref[m] += part[:, m * D_MODEL:(m + 1) * D_MODEL]


def _ada_modulation(cond, w_ada, b_ada):
    k_dim, n = w_ada.shape
    return pl.pallas_call(
        _ada_kernel,
        out_shape=jax.ShapeDtypeStruct((N_MOD, MOD_ROWS, D_MODEL), F32),
        grid=(k_dim // ADA_K_TILE,),
        in_specs=[
            pl.BlockSpec((MOD_ROWS, ADA_K_TILE), lambda k: (0, k)),
            pl.BlockSpec((ADA_K_TILE, n), lambda k: (k, 0)),
            pl.BlockSpec((1, n), lambda k: (0, 0)),
        ],
        out_specs=pl.BlockSpec((N_MOD, MOD_ROWS, D_MODEL), lambda k: (0, 0, 0)),
        compiler_params=_params(1, VMEM_MIB["ada_modulation"]),
        name="ada_modulation",
    )(cond, w_ada, b_ada)


class _Mod:
    def __init__(self, mod_ref, tiles_per_seq):
        self.ref = mod_ref
        self.row = 0 if tiles_per_seq is None else 1 + pl.program_id(0) // tiles_per_seq

    def __getitem__(self, m):
        return self.ref[m, pl.ds(self.row, 1), :]


def _modulated(x, mod, npre_ref, i):
    return _rms(x, npre_ref[i:i + 1, :]) * (1.0 + mod[3 * i + 1]) + mod[3 * i]


def _residual(x, out, mod, npost_ref, i, weight):
    return x + (weight * mod[3 * i + 2]) * _rms(out, npost_ref[i:i + 1, :])


def _ffn_sublayer(x, i, mod, npre_ref, npost_ref, w1_ref, w3_ref, w2_ref):
    h = _modulated(x, mod, npre_ref, i).astype(BF16)
    acc = None
    for lo, hi in zip(FF_SPLITS[:-1], FF_SPLITS[1:]):
        a = _dot(h, w1_ref[:, lo:hi])
        g = _dot(h, w3_ref[:, lo:hi])
        part = _dot((_silu(a) * g).astype(BF16), w2_ref[lo:hi, :])
        acc = part if acc is None else acc + part
    return _residual(x, acc, mod, npost_ref, i, 0.5)


def _ffn_first_kernel(x_ref, mod_ref, npre_ref, npost_ref, w1_ref, w3_ref, w2_ref, o_ref,
                      *, tiles_per_seq):
    mod = _Mod(mod_ref, tiles_per_seq)
    o_ref[...] = _ffn_sublayer(x_ref[...], 0, mod, npre_ref, npost_ref, w1_ref, w3_ref, w2_ref)


def _mix_ffn_kernel(x_ref, att_ref, gla_ref, wo_ref, mod_ref, npre_ref, npost_ref,
                    w1_ref, w3_ref, w2_ref, o_ref, *, tiles_per_seq):
    mod = _Mod(mod_ref, tiles_per_seq)
    mix = _dot(att_ref[...], wo_ref[0:ATT_Q, :]) + _dot(gla_ref[...], wo_ref[ATT_Q:, :])
    x = _residual(x_ref[...], mix, mod, npost_ref, 1, 1.0)
    o_ref[...] = _ffn_sublayer(x, 2, mod, npre_ref, npost_ref, w1_ref, w3_ref, w2_ref)


def _rope_tile(x, cos, sin_up, sin_dn):
    up = pltpu.roll(x, LANES - 16, axis=1)
    dn = pltpu.roll(x, 16, axis=1)
    return x * cos + up * sin_up + dn * sin_dn


def _proj_kernel(*refs, tiles_per_seq, cache_seq):
    x_ref, mod_ref, npre_ref, win_ref, wup_ref, bup_ref = refs[:6]
    rope_refs = refs[6:9] if tiles_per_seq is not None else None
    outs = refs[6 + (3 if rope_refs else 0):]
    att_ref, glaf_ref, gv_ref = outs[:3]
    h = _modulated(x_ref[...], _Mod(mod_ref, tiles_per_seq), npre_ref, 1).astype(BF16)
    q = _dot(h, win_ref[:, OFF_Q:OFF_K])
    kv = _dot(h, win_ref[:, OFF_K:OFF_GQ])
    k, v = kv[:, :ATT_KV], kv[:, ATT_KV:]
    lr = _dot(h, win_ref[:, OFF_LR:IN_WIDTH])
    z = _dot(lr.astype(BF16), wup_ref[...]) + bup_ref[...]
    if rope_refs:
        cos, sup, sdn = (r[...] for r in rope_refs)
        q = jnp.concatenate([_rope_tile(q[:, j * LANES:(j + 1) * LANES], cos, sup, sdn)
                             for j in range(ATT_Q // LANES)], axis=1)
        k_att = _rope_tile(k, cos, sup, sdn)
    else:
        k_att = k
    att_ref[:, A_Q:A_K] = (q * ATT_Q_SCALE).astype(BF16)
    att_ref[:, A_K:A_KS] = k_att.astype(BF16)
    att_ref[:, A_KS:A_V] = pltpu.roll(k_att, HALF, axis=1).astype(BF16)
    att_ref[:, A_V:A_VS] = v.astype(BF16)
    att_ref[:, A_VS:ATT_IN_W] = pltpu.roll(v, HALF, axis=1).astype(BF16)
    if cache_seq is not None:
        for t, dst in ((k.T, outs[3]), (v.T, outs[4])):
            for b in range(TOKEN_TILE // cache_seq):
                dst[b, 0] = t[:, b * cache_seq:(b + 1) * cache_seq].reshape(
                    N_KV_HEADS, HEAD_DIM, cache_seq)
    glaf_ref[:, G_Q:G_G] = _dot(h, win_ref[:, OFF_GQ:OFF_GV])
    log_sig = jnp.minimum(z, 0.0) - jnp.log(1.0 + jnp.exp(-jnp.abs(z)))
    glaf_ref[:, G_LA:GLA_F_W] = log_sig * (1.0 / GLA_TAU)
    gv_ref[...] = _dot(h, win_ref[:, OFF_GV:OFF_GG]).astype(BF16)
    glaf_ref[:, G_G:G_LA] = _dot(h, win_ref[:, OFF_GG:OFF_LR])


def _token_spec(width):
    return pl.BlockSpec((TOKEN_TILE, width), lambda i: (i, 0))


def _ffn_weight_spec(shape, j):
    return pl.BlockSpec((None,) + tuple(shape[1:]), lambda i: (j, 0, 0),
                        pipeline_mode=pl.Buffered(1))


def _tiles_per_seq(latent_len):
    return None if latent_len is None else latent_len // TOKEN_TILE


def _ffn_first(x, mod, latent_len, npre, npost, w1, w3, w2):
    t = x.shape[0]
    return pl.pallas_call(
        functools.partial(_ffn_first_kernel, tiles_per_seq=_tiles_per_seq(latent_len)),
        out_shape=jax.ShapeDtypeStruct((t, D_MODEL), F32),
        grid=(t // TOKEN_TILE,),
        in_specs=[_token_spec(D_MODEL), _resident(mod.shape), _resident(npre.shape),
                  _resident(npost.shape), _ffn_weight_spec(w1.shape, 0),
                  _ffn_weight_spec(w3.shape, 0), _ffn_weight_spec(w2.shape, 0)],
        out_specs=_token_spec(D_MODEL),
        compiler_params=_params(1, VMEM_MIB["ffn_first"]),
        name="ffn_first",
    )(x, mod, npre, npost, w1, w3, w2)


def _mix_ffn(x, att, gla, w_out, mod, latent_len, npre, npost, w1, w3, w2):
    t = x.shape[0]
    return pl.pallas_call(
        functools.partial(_mix_ffn_kernel, tiles_per_seq=_tiles_per_seq(latent_len)),
        out_shape=jax.ShapeDtypeStruct((t, D_MODEL), F32),
        grid=(t // TOKEN_TILE,),
        in_specs=[_token_spec(D_MODEL), _token_spec(ATT_Q), _token_spec(GLA_V),
                  _resident(w_out.shape), _resident(mod.shape), _resident(npre.shape),
                  _resident(npost.shape), _ffn_weight_spec(w1.shape, 1),
                  _ffn_weight_spec(w3.shape, 1), _ffn_weight_spec(w2.shape, 1)],
        out_specs=_token_spec(D_MODEL),
        compiler_params=_params(1, VMEM_MIB["mix_ffn"]),
        name="mix_ffn",
    )(x, att, gla, w_out, mod, npre, npost, w1, w3, w2)


def _project(x, mod, latent_len, cache_seq, npre, w_in, w_up, b_up):
    t = x.shape[0]
    tiles = _tiles_per_seq(latent_len)
    in_specs = [_token_spec(D_MODEL), _resident(mod.shape), _resident(npre.shape),
                _resident(w_in.shape), _resident(w_up.shape), _resident(b_up.shape)]
    args = [x, mod, npre, w_in, w_up, b_up]
    if tiles is not None:
        in_specs += [pl.BlockSpec((TOKEN_TILE, LANES), lambda i: (i % tiles, 0))] * 3
        args += [jnp.asarray(tab) for tab in _rope_tables(latent_len)]
    outs = ((ATT_IN_W, BF16), (GLA_F_W, F32), (GLA_V, BF16))
    out_shape = [jax.ShapeDtypeStruct((t, w), dt) for w, dt in outs]
    out_specs = [_token_spec(w) for w, _ in outs]
    if cache_seq is not None:
        seqs = TOKEN_TILE // cache_seq
        cache = (t // cache_seq, 1, N_KV_HEADS, HEAD_DIM, cache_seq)
        out_shape += [jax.ShapeDtypeStruct(cache, F32)] * 2
        out_specs += [pl.BlockSpec((seqs,) + cache[1:], lambda i: (i, 0, 0, 0, 0))] * 2
    return pl.pallas_call(
        functools.partial(_proj_kernel, tiles_per_seq=tiles, cache_seq=cache_seq),
        out_shape=out_shape,
        grid=(t // TOKEN_TILE,),
        in_specs=in_specs,
        out_specs=out_specs,
        compiler_params=_params(1, VMEM_MIB["project"]),
        name="project",
    )(*args)


def _rope_tables(seq_len):
    half = HEAD_DIM // 2
    inv_freq = np.float32(ROPE_BASE) ** (-np.arange(0, half, 2, dtype=np.float32) / half)
    pos = np.arange(seq_len)
    row = (pos // GRID_W).astype(np.float32)
    col = (pos % GRID_W).astype(np.float32)
    within = np.arange(LANES) % HEAD_DIM
    idx = within % half
    freq = inv_freq[idx % (half // 2)].astype(np.float32)
    p = np.where((within // half == 0)[None, :], row[:, None], col[:, None])
    ang = (p * freq[None, :]).astype(np.float32)
    cos, sin = np.cos(ang).astype(np.float32), np.sin(ang).astype(np.float32)
    first = (idx < half // 2)[None, :]
    zero = np.float32(0.0)
    return cos, np.where(first, -sin, zero), np.where(first, zero, sin)


def _attend(q_ref, q_rows, nq, segs, sink_ref):
    lane = lax.broadcasted_iota(jnp.int32, (1, LANES), 1)
    half_of = [lane < HALF, lane >= HALF]
    rows = lax.broadcasted_iota(jnp.int32, (2 * nq, 1), 0)
    zero = jnp.zeros((), BF16)
    out = [None] * (ATT_Q // LANES)
    for g in range(N_KV_HEADS):
        tiles = (2 * g, 2 * g + 1)
        for e in range(2):
            qm = jnp.concatenate(
                [jnp.where(half_of[e], q_ref[q_rows, j * LANES:(j + 1) * LANES], zero)
                 for j in tiles], axis=0)
            sink = jnp.where(rows < nq, sink_ref[2 * tiles[0] + e],
                             sink_ref[2 * tiles[1] + e]) * LOG2_E
            scores = []
            for k, k_sw, _, _, m in segs:
                s = _dot_t(qm, k if e == g else k_sw)
                scores.append(s if m is None else jnp.where(m, s, NEG))
            mx = sink
            for s in scores:
                mx = jnp.maximum(mx, jnp.max(s, axis=-1, keepdims=True))
            probs = [jnp.exp2(s - mx) for s in scores]
            den = jnp.exp2(sink - mx)
            for p in probs:
                den = den + jnp.sum(p, axis=-1, keepdims=True)
            o = None
            for p, (_, _, v, v_sw, _) in zip(probs, segs):
                part = _dot(p.astype(BF16), jnp.where(half_of[e], v if e == g else v_sw, zero))
                o = part if o is None else o + part
            o = o * (1.0 / den)
            for r, j in enumerate(tiles):
                blk = o[r * nq:(r + 1) * nq, :]
                out[j] = blk if out[j] is None else out[j] + blk
    return out


def _context_attn_kernel(sink_ref, a_ref, o_ref, *, seq_len):
    for s in range(a_ref.shape[0] // seq_len):
        rows = slice(s * seq_len, (s + 1) * seq_len)
        segs = [(a_ref[rows, A_K:A_KS], a_ref[rows, A_KS:A_V],
                 a_ref[rows, A_V:A_VS], a_ref[rows, A_VS:ATT_IN_W], None)]
        for j, t in enumerate(_attend(a_ref, rows, seq_len, segs, sink_ref)):
            o_ref[rows, j * LANES:(j + 1) * LANES] = t.astype(o_ref.dtype)


def _latent_attn_kernel(sink_ref, q_ref, kv_ref, kc_ref, vc_ref, o_ref, *, seq_len):
    k_ctx, v_ctx = kc_ref[0], vc_ref[0]
    ctx = (k_ctx.astype(BF16), pltpu.roll(k_ctx, HALF, axis=1).astype(BF16),
           v_ctx.astype(BF16), pltpu.roll(v_ctx, HALF, axis=1).astype(BF16), None)
    span = 3 * BLOCK
    per_step = q_ref.shape[0] // BLOCK
    for s in range(per_step):
        i = pl.program_id(1) * per_step + s
        start = pl.multiple_of(jnp.clip((i - 1) * BLOCK, 0, seq_len - span), BLOCK)
        kpos = start + lax.broadcasted_iota(jnp.int32, (1, span), 1)
        qpos = i * BLOCK + lax.broadcasted_iota(jnp.int32, (BLOCK, 1), 0)
        valid = jnp.abs(kpos - qpos) <= WINDOW
        valid2 = jnp.concatenate([valid, valid], axis=0)
        kv = kv_ref[pl.ds(start, span), :]
        segs = [tuple(kv[:, c * ATT_KV:(c + 1) * ATT_KV] for c in range(4)) + (valid2,), ctx]
        rows = slice(s * BLOCK, (s + 1) * BLOCK)
        for j, t in enumerate(_attend(q_ref, rows, BLOCK, segs, sink_ref)):
            o_ref[rows, j * LANES:(j + 1) * LANES] = t.astype(o_ref.dtype)


def _smem_spec():
    return pl.BlockSpec(memory_space=pltpu.SMEM)


def _context_attention(sink, att_in, batch, seq_len):
    rows = CTX_SEQS_PER_STEP * seq_len

    def seq(width):
        return pl.BlockSpec((rows, width), lambda b: (b, 0))
    return pl.pallas_call(
        functools.partial(_context_attn_kernel, seq_len=seq_len),
        out_shape=jax.ShapeDtypeStruct((batch * seq_len, ATT_Q), BF16),
        grid=(batch // CTX_SEQS_PER_STEP,),
        in_specs=[_smem_spec(), seq(ATT_IN_W)],
        out_specs=seq(ATT_Q),
        compiler_params=_params(1, VMEM_MIB["context_attention"]),
        name="context_attention",
    )(sink, att_in)


def _latent_attention(sink, att_in, k_ctx, v_ctx, batch, seq_len):
    steps = seq_len // (LAT_BLOCKS_PER_STEP * BLOCK)
    rows = LAT_BLOCKS_PER_STEP * BLOCK
    past = k_ctx.shape[1]
    assert A_K == ATT_IN_W - A_K, "q and the key / value columns are the two halves of att_in"
    return pl.pallas_call(
        functools.partial(_latent_attn_kernel, seq_len=seq_len),
        out_shape=jax.ShapeDtypeStruct((batch * seq_len, ATT_Q), BF16),
        grid=(batch, steps),
        in_specs=[_smem_spec(),
                  pl.BlockSpec((rows, ATT_Q), lambda b, i: (b * steps + i, 0)),
                  pl.BlockSpec((seq_len, ATT_IN_W - A_K), lambda b, i: (b, 1)),
                  pl.BlockSpec((1, past, ATT_KV), lambda b, i: (b, 0, 0)),
                  pl.BlockSpec((1, past, ATT_KV), lambda b, i: (b, 0, 0))],
        out_specs=pl.BlockSpec((rows, ATT_Q), lambda b, i: (b * steps + i, 0)),
        compiler_params=_params(2, VMEM_MIB["latent_attention"]),
        name="latent_attention",
    )(sink, att_in, att_in, k_ctx, v_ctx)


def _split2(x):
    hi = x.astype(BF16)
    lo = (x - hi.astype(F32)).astype(BF16)
    return hi, lo


def _gla_kernel(*refs, n_chunks, has_s0, emit_state):
    refs = list(refs)
    gf_ref, gv_ref, gn_ref = refs[:3]
    pos = 3
    s0_refs = refs[pos:pos + 2] if has_s0 else None
    pos += 2 if has_s0 else 0
    o_ref = refs[pos]
    pos += 1
    sfin_refs = refs[pos:pos + 2] if emit_state else None
    pos += 2 if emit_state else 0
    cum_ref, kv_ref, sent_ref = refs[pos:]

    C = GLA_CHUNK
    n_pairs = GLA_QK // LANES
    lane = lax.broadcasted_iota(jnp.int32, (1, LANES), 1)
    half_of = [lane < HALF, lane >= HALF]
    r_i = lax.broadcasted_iota(jnp.int32, (C, C), 0)
    c_i = lax.broadcasted_iota(jnp.int32, (C, C), 1)
    lower = c_i <= r_i
    upper = c_i >= r_i
    tri = jnp.concatenate([jnp.where(lower, 1.0, 0.0), jnp.where(upper, 1.0, 0.0)],
                          axis=0).astype(BF16)
    zeros_cc = jnp.zeros((C, C), F32)
    gnorm = gn_ref[...]
    qscale = GLA_DK ** -0.5

    def one_sequence(s):
        def chunk_rows(n):
            return slice((s * n_chunks + n) * C, (s * n_chunks + n + 1) * C)

        for n in range(n_chunks):
            rows = chunk_rows(n)
            hi, lo = _split2(gf_ref[rows, G_LA:GLA_F_W])
            sums = _dot(tri, jnp.concatenate([hi, lo], axis=1))
            cum_f = sums[:C, 0:GLA_QK] + sums[:C, 2 * GLA_QK:3 * GLA_QK]
            cum_b = sums[C:, GLA_QK:2 * GLA_QK] + sums[C:, 3 * GLA_QK:]
            cum_ref[0, rows, :] = cum_f
            cum_ref[1, rows, :] = cum_b
            k = gf_ref[rows, G_K:G_G]
            k_in = (k * jnp.exp(cum_f[C - 1:C, :] - cum_f), k * jnp.exp(cum_b[0:1, :] - cum_b))
            for p in range(n_pairs):
                sl = slice(p * LANES, (p + 1) * LANES)
                kv_t = None
                for e in range(2):
                    h = 2 * p + e
                    v_t = gv_ref[rows, h * GLA_DV:(h + 1) * GLA_DV].T
                    k_e = jnp.concatenate(
                        [jnp.where(half_of[e], k_in[d][:, sl], 0.0) for d in range(2)], axis=1)
                    part = _dot(v_t, k_e.astype(BF16))
                    kv_t = part if kv_t is None else kv_t + part
                kv_ref[s, n, p] = kv_t

        for p in range(n_pairs):
            sl = slice(p * LANES, (p + 1) * LANES)
            st = []
            for d in range(2):
                if has_s0:
                    s0 = s0_refs[d][s, 0, 2 * p:2 * p + 2, :, :].reshape(2 * GLA_DK, GLA_DV)
                    st.append(s0.T)
                else:
                    st.append(jnp.zeros((GLA_DV, 2 * GLA_DK), F32))
            for i in range(n_chunks):
                for d, n in ((0, i), (1, n_chunks - 1 - i)):
                    tot_row = (s * n_chunks + n) * C + (C - 1 if d == 0 else 0)
                    decay = jnp.exp(cum_ref[d, tot_row:tot_row + 1, sl])
                    sent_ref[s, n, p, :, d * LANES:(d + 1) * LANES] = st[d].astype(BF16)
                    st[d] = decay * st[d] + kv_ref[s, n, p, :, d * LANES:(d + 1) * LANES]
            if emit_state:
                for d in range(2):
                    sfin_refs[d][s, 0, 2 * p:2 * p + 2, :, :] = st[d].T.reshape(2, GLA_DK, GLA_DV)

        for n in range(n_chunks):
            rows = chunk_rows(n)
            q = gf_ref[rows, G_Q:G_K] * qscale
            k = gf_ref[rows, G_K:G_G]
            qs, ks, qin = [], [], []
            for d in range(2):
                cum = cum_ref[d, rows, :]
                ref = cum[C // 2:C // 2 + 1, :]
                qs.append(q * jnp.exp(cum - ref))
                ks.append((k * jnp.exp(ref - cum)).astype(BF16))
                qin.append(q * jnp.exp(cum))
            for p in range(n_pairs):
                sl = slice(p * LANES, (p + 1) * LANES)
                lhs = jnp.concatenate(
                    [jnp.where(half_of[e], qs[d][:, sl], 0.0) for e in range(2) for d in range(2)],
                    axis=0).astype(BF16)
                sc = _dot_t(lhs, jnp.concatenate([ks[0][:, sl], ks[1][:, sl]], axis=0))
                prob = []
                for e in range(2):
                    s_f = sc[(2 * e) * C:(2 * e + 1) * C, :C]
                    s_b = sc[(2 * e + 1) * C:(2 * e + 2) * C, C:]
                    prob.append(jnp.where(lower, s_f, 0.0) + jnp.where(upper, s_b, 0.0))
                p_blk = jnp.concatenate(
                    [jnp.concatenate([prob[0], zeros_cc], axis=1),
                     jnp.concatenate([zeros_cc, prob[1]], axis=1)], axis=0).astype(BF16)
                v2 = jnp.concatenate(
                    [gv_ref[rows, (2 * p + e) * GLA_DV:(2 * p + e + 1) * GLA_DV] for e in range(2)],
                    axis=0)
                q_in = jnp.concatenate(
                    [jnp.concatenate([jnp.where(half_of[e], qin[d][:, sl], 0.0) for d in range(2)],
                                     axis=1) for e in range(2)], axis=0).astype(BF16)
                o2 = _dot(p_blk, v2) + _dot_t(q_in, sent_ref[s, n, p])
                for e in range(2):
                    h = 2 * p + e
                    o = o2[e * C:(e + 1) * C, :]
                    o = o * lax.rsqrt(jnp.mean(o * o, axis=-1, keepdims=True) + EPS) * gnorm
                    gate = gf_ref[rows, G_G + h * GLA_DV:G_G + (h + 1) * GLA_DV]
                    o_ref[rows, h * GLA_DV:(h + 1) * GLA_DV] = (o * _silu(gate)).astype(o_ref.dtype)

    for s in range(kv_ref.shape[0]):
        one_sequence(s)


def _gla(gla_f, gv, gnorm, batch, seq_len, seqs_per_step, s0=None, emit_state=False):
    n_chunks = seq_len // GLA_CHUNK
    has_s0 = s0 is not None
    rows = seqs_per_step * seq_len

    def seq(width):
        return pl.BlockSpec((rows, width), lambda b: (b, 0))
    state_spec = pl.BlockSpec((seqs_per_step, 1, GLA_HEADS, GLA_DK, GLA_DV),
                              lambda b: (b, 0, 0, 0, 0))
    in_specs = [seq(GLA_F_W), seq(GLA_V), pl.BlockSpec((1, GLA_DV), lambda b: (0, 0))]
    args = [gla_f, gv, gnorm]
    if has_s0:
        in_specs += [state_spec, state_spec]
        args += list(s0)
    out_shape = [jax.ShapeDtypeStruct((batch * seq_len, GLA_V), BF16)]
    out_specs = [seq(GLA_V)]
    if emit_state:
        out_shape += [jax.ShapeDtypeStruct((batch, 1, GLA_HEADS, GLA_DK, GLA_DV), F32)] * 2
        out_specs += [state_spec, state_spec]
    n_pairs = GLA_QK // LANES
    return pl.pallas_call(
        functools.partial(_gla_kernel, n_chunks=n_chunks, has_s0=has_s0, emit_state=emit_state),
        out_shape=out_shape,
        grid=(batch // seqs_per_step,),
        in_specs=in_specs,
        out_specs=out_specs,
        scratch_shapes=[pltpu.VMEM((2, rows, GLA_QK), F32),
                        pltpu.VMEM((seqs_per_step, n_chunks, n_pairs, GLA_DV, 2 * LANES), F32),
                        pltpu.VMEM((seqs_per_step, n_chunks, n_pairs, GLA_DV, 2 * LANES), BF16)],
        compiler_params=_params(1, VMEM_MIB["gla"]),
        name="gla",
    )(*args)


def kernel(x_prompt, x_sample, cache_k, cache_v, state_gla_fwd, state_gla_bwd, c, c_ctx,
           w_ada, b_ada, norm_pre, norm_post, ffn_w1, ffn_w3, ffn_w2, w_in,
           gla_w_up, gla_b_up, gla_norm, attn_sink, w_out):
    depth = w_in.shape[0]
    assert depth == 1, "single trunk layer"
    batch, seq = x_prompt.shape[0], x_prompt.shape[1]
    dec_batch, dec_seq = x_sample.shape[0], x_sample.shape[1]
    past = cache_k.shape[2]
    l = 0

    cond = jnp.concatenate(
        [c_ctx[None, :], c, jnp.zeros((MOD_ROWS - 1 - dec_batch, D_MODEL), F32)], axis=0)
    mod = _ada_modulation(cond, w_ada[l], b_ada[l][None, :])

    npre, npost = norm_pre[l], norm_post[l]
    ffn_b = [w[l].astype(BF16) for w in (ffn_w1, ffn_w3, ffn_w2)]
    w_in_b = w_in[l].astype(BF16)
    w_out_b = w_out[l].astype(BF16)
    zeros = jnp.zeros((GLA_LOW_RANK, GLA_QK), F32)
    w_up = jnp.concatenate(
        [jnp.concatenate([gla_w_up[l, 0], zeros], axis=1),
         jnp.concatenate([zeros, gla_w_up[l, 1]], axis=1)], axis=0).astype(BF16)
    b_up = gla_b_up[l].reshape(1, 2 * GLA_QK)
    gnorm = gla_norm[l][None, :]
    sink = attn_sink[l]

    def trunk(x, latent):
        n_batch, n_seq = (dec_batch, dec_seq) if latent else (batch, seq)
        latent_len = n_seq if latent else None
        x1 = _ffn_first(x, mod, latent_len, npre, npost, *ffn_b)
        att_in, gla_f, gv, *cache_t = _project(
            x1, mod, latent_len, None if latent else n_seq, npre, w_in_b, w_up, b_up)
        if latent:
            att = _latent_attention(sink, att_in, cache_k[:, l].reshape(dec_batch, past, ATT_KV),
                                    cache_v[:, l].reshape(dec_batch, past, ATT_KV), n_batch, n_seq)
            (gla,) = _gla(gla_f, gv, gnorm, n_batch, n_seq, 1,
                          s0=(state_gla_fwd[:, l:l + 1], state_gla_bwd[:, l:l + 1]))
            extras = ()
        else:
            att = _context_attention(sink, att_in, n_batch, n_seq)
            gla, s_f, s_b = _gla(gla_f, gv, gnorm, n_batch, n_seq, CTX_SEQS_PER_STEP,
                                 emit_state=True)
            k_new, v_new = (jnp.transpose(c_t, (0, 1, 4, 2, 3)) for c_t in cache_t)
            extras = (k_new, v_new, s_f, s_b)
        y = _mix_ffn(x1, att, gla, w_out_b, mod, latent_len, npre, npost, *ffn_b)
        return y.reshape(n_batch, n_seq, D_MODEL), extras

    y_prompt, (k_new, v_new, s_f, s_b) = trunk(x_prompt.reshape(batch * seq, D_MODEL), False)
    y_sample, _ = trunk(x_sample.reshape(dec_batch * dec_seq, D_MODEL), True)
    return (y_prompt, y_sample, k_new, v_new, s_f, s_b)
```

```python
import functools

import numpy as np
import jax
import jax.numpy as jnp
from jax import lax
from jax.experimental import pallas as pl
from jax.experimental.pallas import tpu as pltpu

F32 = jnp.float32
BF16 = jnp.bfloat16

D_MODEL = 1024
GRID_W = 64
N_Q_HEADS = 8
N_KV_HEADS = 2
HEAD_DIM = 64
WINDOW = 128
BLOCK = 128
ROPE_BASE = 10000.0
GLA_HEADS = 4
GLA_DK = 64
GLA_DV = 128
GLA_LOW_RANK = 16
GLA_TAU = 16.0
D_FF = 2816
N_MOD = 9
EPS = 1e-6
NEG = -1e30

ATT_Q = N_Q_HEADS * HEAD_DIM
ATT_KV = N_KV_HEADS * HEAD_DIM
GLA_QK = GLA_HEADS * GLA_DK
GLA_V = GLA_HEADS * GLA_DV
OFF_Q = 0
OFF_K = OFF_Q + ATT_Q
OFF_V = OFF_K + ATT_KV
OFF_GQ = OFF_V + ATT_KV
OFF_GK = OFF_GQ + GLA_QK
OFF_GV = OFF_GK + GLA_QK
OFF_GG = OFF_GV + GLA_V
OFF_LR = OFF_GG + GLA_V
IN_WIDTH = OFF_LR + 2 * GLA_LOW_RANK
A_Q = 0
A_K = A_Q + ATT_Q
A_KS = A_K + ATT_KV
A_V = A_KS + ATT_KV
A_VS = A_V + ATT_KV
ATT_IN_W = A_VS + ATT_KV
G_Q = 0
G_K = G_Q + GLA_QK
G_G = G_K + GLA_QK
G_LA = G_G + GLA_V
GLA_F_W = G_LA + 2 * GLA_QK
LOG2_E = 1.4426950408889634
ATT_Q_SCALE = HEAD_DIM ** -0.5 * LOG2_E

LANES = 128
SUBLANES = 8
HALF = LANES // 2
VMEM_LIMIT = 56 * 1024 * 1024

TOKEN_TILE = 512
FF_SPLITS = (0, 1536, D_FF)
CAST_SLABS = 8
ADA_K_TILE = 128
CTX_SEQS_PER_STEP = 4
LAT_BLOCKS_PER_STEP = 4
GLA_CHUNK = 128
MOD_ROWS = 8


def _params(n_axes):
    return pltpu.CompilerParams(
        dimension_semantics=("arbitrary",) * n_axes, vmem_limit_bytes=VMEM_LIMIT)


def _resident(shape):
    zeros = (0,) * len(shape)
    return pl.BlockSpec(shape, lambda *_: zeros, pipeline_mode=pl.Buffered(1))


def _sigmoid(x):
    return 1.0 / (1.0 + jnp.exp(-x))


def _silu(x):
    return x * _sigmoid(x)


def _rms(x, g):
    return x * lax.rsqrt(jnp.mean(x * x, axis=-1, keepdims=True) + EPS) * g


def _dot(a, b):
    return jnp.dot(a, b, preferred_element_type=F32)


def _dot_t(a, b):
    return lax.dot_general(a, b, (((1,), (1,)), ((), ())), preferred_element_type=F32)


def _ada_kernel(cond_ref, w_ref, b_ref, o_ref):
    @pl.when(pl.program_id(0) == 0)
    def _():
        for m in range(N_MOD):
            o_ref[m] = jnp.broadcast_to(b_ref[:, m * D_MODEL:(m + 1) * D_MODEL],
                                        (MOD_ROWS, D_MODEL))
    part = _dot(_silu(cond_ref[...]).astype(BF16), w_ref[...].astype(BF16))
    for m in range(N_MOD):
        o_ref[m] += part[:, m * D_MODEL:(m + 1) * D_MODEL]


def _ada_modulation(cond, w_ada, b_ada):
    k_dim, n = w_ada.shape
    return pl.pallas_call(
        _ada_kernel,
        out_shape=jax.ShapeDtypeStruct((N_MOD, MOD_ROWS, D_MODEL), F32),
        grid=(k_dim // ADA_K_TILE,),
        in_specs=[
            pl.BlockSpec((MOD_ROWS, ADA_K_TILE), lambda k: (0, k)),
            pl.BlockSpec((ADA_K_TILE, n), lambda k: (k, 0)),
            pl.BlockSpec((1, n), lambda k: (0, 0)),
        ],
        out_specs=pl.BlockSpec((N_MOD, MOD_ROWS, D_MODEL), lambda k: (0, 0, 0)),
        compiler_params=_params(1),
        name="ada_modulation",
    )(cond, w_ada, b_ada)


class _Mod:
    def __init__(self, mod_ref, tiles_per_seq, first_latent_tile=0):
        self.ref = mod_ref
        if tiles_per_seq is None:
            self.row = 0
        else:
            tile = pl.program_id(0) - first_latent_tile
            self.row = jnp.where(tile >= 0, 1 + tile // tiles_per_seq, 0)

    def __getitem__(self, m):
        return self.ref[m, pl.ds(self.row, 1), :]


def _modulated(x, mod, npre_ref, i):
    return _rms(x, npre_ref[i:i + 1, :]) * (1.0 + mod[3 * i + 1]) + mod[3 * i]


def _residual(x, out, mod, npost_ref, i, weight):
    return x + (weight * mod[3 * i + 2]) * _rms(out, npost_ref[i:i + 1, :])


def _ffn_sublayer(x, i, mod, npre_ref, npost_ref, w1_ref, w3_ref, w2_ref):
    h = _modulated(x, mod, npre_ref, i).astype(BF16)
    acc = None
    for lo, hi in zip(FF_SPLITS[:-1], FF_SPLITS[1:]):
        a = _dot(h, w1_ref[:, lo:hi])
        g = _dot(h, w3_ref[:, lo:hi])
        part = _dot((_silu(a) * g).astype(BF16), w2_ref[lo:hi, :])
        acc = part if acc is None else acc + part
    return _residual(x, acc, mod, npost_ref, i, 0.5)


def _ffn_first_kernel(*refs, ctx_tiles, tiles_per_seq, n_cast):
    xc_ref, xl_ref, mod_ref, npre_ref, npost_ref, w1_ref, w3_ref, w2_ref = refs[:8]
    cast_src = refs[8:8 + n_cast]
    o_ref = refs[8 + n_cast]
    cast_dst = refs[9 + n_cast:]
    mod = _Mod(mod_ref, tiles_per_seq, ctx_tiles)
    x = jnp.where(pl.program_id(0) >= ctx_tiles, xl_ref[...], xc_ref[...])
    o_ref[...] = _ffn_sublayer(x, 0, mod, npre_ref, npost_ref, w1_ref, w3_ref, w2_ref)
    for src, dst in zip(cast_src, cast_dst):
        dst[...] = src[...].astype(BF16)


def _mix_ffn_kernel(x_ref, att_ref, gla_ref, wo_ref, mod_ref, npre_ref, npost_ref,
                    w1_ref, w3_ref, w2_ref, o_ref, *, tiles_per_seq):
    mod = _Mod(mod_ref, tiles_per_seq)
    mix = _dot(att_ref[...], wo_ref[0:ATT_Q, :]) + _dot(gla_ref[...], wo_ref[ATT_Q:, :])
    x = _residual(x_ref[...], mix, mod, npost_ref, 1, 1.0)
    o_ref[...] = _ffn_sublayer(x, 2, mod, npre_ref, npost_ref, w1_ref, w3_ref, w2_ref)


def _rope_tile(x, cos, sin_up, sin_dn):
    up = pltpu.roll(x, LANES - 16, axis=1)
    dn = pltpu.roll(x, 16, axis=1)
    return x * cos + up * sin_up + dn * sin_dn


def _proj_kernel(*refs, tiles_per_seq, cache_seq):
    x_ref, mod_ref, npre_ref, win_ref, wup_ref, bup_ref = refs[:6]
    rope_refs = refs[6:9] if tiles_per_seq is not None else None
    outs = refs[6 + (3 if rope_refs else 0):]
    att_ref, glaf_ref, gv_ref = outs[:3]
    h = _modulated(x_ref[...], _Mod(mod_ref, tiles_per_seq), npre_ref, 1).astype(BF16)
    q = _dot(h, win_ref[:, OFF_Q:OFF_K])
    kv = _dot(h, win_ref[:, OFF_K:OFF_GQ])
    k, v = kv[:, :ATT_KV], kv[:, ATT_KV:]
    lr = _dot(h, win_ref[:, OFF_LR:IN_WIDTH])
    z = _dot(lr.astype(BF16), wup_ref[...]) + bup_ref[...]
    if rope_refs:
        cos, sup, sdn = (r[...] for r in rope_refs)
        q = jnp.concatenate([_rope_tile(q[:, j * LANES:(j + 1) * LANES], cos, sup, sdn)
                             for j in range(ATT_Q // LANES)], axis=1)
        k_att = _rope_tile(k, cos, sup, sdn)
    else:
        k_att = k
    att_ref[:, A_Q:A_K] = (q * ATT_Q_SCALE).astype(BF16)
    att_ref[:, A_K:A_KS] = k_att.astype(BF16)
    att_ref[:, A_KS:A_V] = pltpu.roll(k_att, HALF, axis=1).astype(BF16)
    att_ref[:, A_V:A_VS] = v.astype(BF16)
    att_ref[:, A_VS:ATT_IN_W] = pltpu.roll(v, HALF, axis=1).astype(BF16)
    if cache_seq is not None:
        for t, dst in ((k.T, outs[3]), (v.T, outs[4])):
            for b in range(TOKEN_TILE // cache_seq):
                dst[b, 0] = t[:, b * cache_seq:(b + 1) * cache_seq].reshape(
                    N_KV_HEADS, HEAD_DIM, cache_seq)
    glaf_ref[:, G_Q:G_G] = _dot(h, win_ref[:, OFF_GQ:OFF_GV])
    log_sig = jnp.minimum(z, 0.0) - jnp.log(1.0 + jnp.exp(-jnp.abs(z)))
    glaf_ref[:, G_LA:GLA_F_W] = log_sig * (1.0 / GLA_TAU)
    gv_ref[...] = _dot(h, win_ref[:, OFF_GV:OFF_GG]).astype(BF16)
    glaf_ref[:, G_G:G_LA] = _dot(h, win_ref[:, OFF_GG:OFF_LR])


def _token_spec(width, first_tile=0):
    return pl.BlockSpec((TOKEN_TILE, width), lambda i: (i + first_tile, 0))


def _stack_spec(shape, j):
    return pl.BlockSpec((None,) + tuple(shape[1:]), lambda i: (j, 0, 0),
                        pipeline_mode=pl.Buffered(1))


def _tiles_per_seq(latent_len):
    return None if latent_len is None else latent_len // TOKEN_TILE


def _ffn_first(x_ctx, x_lat, mod, latent_len, npre, npost, w1, w3, w2, cast):
    ctx_tiles = x_ctx.shape[0] // TOKEN_TILE
    steps = ctx_tiles + x_lat.shape[0] // TOKEN_TILE
    hold = steps // CAST_SLABS
    assert hold * CAST_SLABS == steps
    in_specs = [pl.BlockSpec((TOKEN_TILE, D_MODEL), lambda i: (jnp.minimum(i, ctx_tiles - 1), 0)),
                pl.BlockSpec((TOKEN_TILE, D_MODEL), lambda i: (jnp.maximum(i - ctx_tiles, 0), 0)),
                _resident(mod.shape), _resident(npre.shape),
                _resident(npost.shape), _stack_spec(w1.shape, 0),
                _stack_spec(w3.shape, 0), _stack_spec(w2.shape, 0)]
    out_shape = [jax.ShapeDtypeStruct((steps * TOKEN_TILE, D_MODEL), F32)]
    out_specs = [_token_spec(D_MODEL)]
    for stack, j in cast:
        _, rows, cols = stack.shape
        slab = rows // CAST_SLABS
        assert slab * CAST_SLABS == rows and slab % 16 == 0, "row slabs must be whole bf16 tiles"
        in_specs.append(pl.BlockSpec((None, slab, cols), functools.partial(
            lambda i, j: (j, i // hold, 0), j=j)))
        out_shape.append(jax.ShapeDtypeStruct((1, rows, cols), BF16))
        out_specs.append(pl.BlockSpec((None, slab, cols), lambda i: (0, i // hold, 0)))
    return pl.pallas_call(
        functools.partial(_ffn_first_kernel, ctx_tiles=ctx_tiles,
                          tiles_per_seq=_tiles_per_seq(latent_len), n_cast=len(cast)),
        out_shape=out_shape,
        grid=(steps,),
        in_specs=in_specs,
        out_specs=out_specs,
        compiler_params=_params(1),
        name="ffn_first",
    )(x_ctx, x_lat, mod, npre, npost, w1, w3, w2, *[stack for stack, _ in cast])


def _mix_ffn(x, first_tile, att, gla, w_out, mod, latent_len, npre, npost, w1, w3, w2):
    t = att.shape[0]
    return pl.pallas_call(
        functools.partial(_mix_ffn_kernel, tiles_per_seq=_tiles_per_seq(latent_len)),
        out_shape=jax.ShapeDtypeStruct((t, D_MODEL), F32),
        grid=(t // TOKEN_TILE,),
        in_specs=[_token_spec(D_MODEL, first_tile), _token_spec(ATT_Q), _token_spec(GLA_V),
                  _stack_spec(w_out.shape, 0), _resident(mod.shape), _resident(npre.shape),
                  _resident(npost.shape), _stack_spec(w1.shape, 0),
                  _stack_spec(w3.shape, 0), _stack_spec(w2.shape, 0)],
        out_specs=_token_spec(D_MODEL),
        compiler_params=_params(1),
        name="mix_ffn",
    )(x, att, gla, w_out, mod, npre, npost, w1, w3, w2)


def _project(x, first_tile, t, mod, latent_len, cache_seq, npre, w_in, w_up, b_up):
    tiles = _tiles_per_seq(latent_len)
    in_specs = [_token_spec(D_MODEL, first_tile), _resident(mod.shape), _resident(npre.shape),
                _stack_spec(w_in.shape, 0), _resident(w_up.shape), _resident(b_up.shape)]
    args = [x, mod, npre, w_in, w_up, b_up]
    if tiles is not None:
        in_specs += [pl.BlockSpec((TOKEN_TILE, LANES), lambda i: (i % tiles, 0))] * 3
        args += [jnp.asarray(tab) for tab in _rope_tables(latent_len)]
    outs = ((ATT_IN_W, BF16), (GLA_F_W, F32), (GLA_V, BF16))
    out_shape = [jax.ShapeDtypeStruct((t, w), dt) for w, dt in outs]
    out_specs = [_token_spec(w) for w, _ in outs]
    if cache_seq is not None:
        seqs = TOKEN_TILE // cache_seq
        cache = (t // cache_seq, 1, N_KV_HEADS, HEAD_DIM, cache_seq)
        out_shape += [jax.ShapeDtypeStruct(cache, F32)] * 2
        out_specs += [pl.BlockSpec((seqs,) + cache[1:], lambda i: (i, 0, 0, 0, 0))] * 2
    return pl.pallas_call(
        functools.partial(_proj_kernel, tiles_per_seq=tiles, cache_seq=cache_seq),
        out_shape=out_shape,
        grid=(t // TOKEN_TILE,),
        in_specs=in_specs,
        out_specs=out_specs,
        compiler_params=_params(1),
        name="project",
    )(*args)


def _rope_tables(seq_len):
    half = HEAD_DIM // 2
    inv_freq = np.float32(ROPE_BASE) ** (-np.arange(0, half, 2, dtype=np.float32) / half)
    pos = np.arange(seq_len)
    row = (pos // GRID_W).astype(np.float32)
    col = (pos % GRID_W).astype(np.float32)
    within = np.arange(LANES) % HEAD_DIM
    idx = within % half
    freq = inv_freq[idx % (half // 2)].astype(np.float32)
    p = np.where((within // half == 0)[None, :], row[:, None], col[:, None])
    ang = (p * freq[None, :]).astype(np.float32)
    cos, sin = np.cos(ang).astype(np.float32), np.sin(ang).astype(np.float32)
    first = (idx < half // 2)[None, :]
    zero = np.float32(0.0)
    return cos, np.where(first, -sin, zero), np.where(first, zero, sin)


def _attend(q_ref, q_rows, nq, segs, sink_ref):
    lane = lax.broadcasted_iota(jnp.int32, (1, LANES), 1)
    half_of = [lane < HALF, lane >= HALF]
    rows = lax.broadcasted_iota(jnp.int32, (2 * nq, 1), 0)
    zero = jnp.zeros((), BF16)
    out = [None] * (ATT_Q // LANES)
    for g in range(N_KV_HEADS):
        tiles = (2 * g, 2 * g + 1)
        for e in range(2):
            qm = jnp.concatenate(
                [jnp.where(half_of[e], q_ref[q_rows, j * LANES:(j + 1) * LANES], zero)
                 for j in tiles], axis=0)
            sink = jnp.where(rows < nq, sink_ref[2 * tiles[0] + e],
                             sink_ref[2 * tiles[1] + e]) * LOG2_E
            scores = []
            for k, k_sw, _, _, m in segs:
                s = _dot_t(qm, k if e == g else k_sw)
                scores.append(s if m is None else jnp.where(m, s, NEG))
            mx = sink
            for s in scores:
                mx = jnp.maximum(mx, jnp.max(s, axis=-1, keepdims=True))
            probs = [jnp.exp2(s - mx) for s in scores]
            den = jnp.exp2(sink - mx)
            for p in probs:
                den = den + jnp.sum(p, axis=-1, keepdims=True)
            o = None
            for p, (_, _, v, v_sw, _) in zip(probs, segs):
                part = _dot(p.astype(BF16), jnp.where(half_of[e], v if e == g else v_sw, zero))
                o = part if o is None else o + part
            o = o * (1.0 / den)
            for r, j in enumerate(tiles):
                blk = o[r * nq:(r + 1) * nq, :]
                out[j] = blk if out[j] is None else out[j] + blk
    return out


def _context_attn_kernel(sink_ref, a_ref, o_ref, *, seq_len):
    for s in range(a_ref.shape[0] // seq_len):
        rows = slice(s * seq_len, (s + 1) * seq_len)
        segs = [(a_ref[rows, A_K:A_KS], a_ref[rows, A_KS:A_V],
                 a_ref[rows, A_V:A_VS], a_ref[rows, A_VS:ATT_IN_W], None)]
        for j, t in enumerate(_attend(a_ref, rows, seq_len, segs, sink_ref)):
            o_ref[rows, j * LANES:(j + 1) * LANES] = t.astype(o_ref.dtype)


def _latent_attn_kernel(sink_ref, q_ref, kv_ref, kc_ref, vc_ref, o_ref, *, seq_len):
    k_ctx, v_ctx = kc_ref[0], vc_ref[0]
    ctx = (k_ctx.astype(BF16), pltpu.roll(k_ctx, HALF, axis=1).astype(BF16),
           v_ctx.astype(BF16), pltpu.roll(v_ctx, HALF, axis=1).astype(BF16), None)
    span = 3 * BLOCK
    per_step = q_ref.shape[0] // BLOCK
    for s in range(per_step):
        i = pl.program_id(1) * per_step + s
        start = pl.multiple_of(jnp.clip((i - 1) * BLOCK, 0, seq_len - span), BLOCK)
        kpos = start + lax.broadcasted_iota(jnp.int32, (1, span), 1)
        qpos = i * BLOCK + lax.broadcasted_iota(jnp.int32, (BLOCK, 1), 0)
        valid = jnp.abs(kpos - qpos) <= WINDOW
        valid2 = jnp.concatenate([valid, valid], axis=0)
        kv = kv_ref[pl.ds(start, span), :]
        segs = [tuple(kv[:, c * ATT_KV:(c + 1) * ATT_KV] for c in range(4)) + (valid2,), ctx]
        rows = slice(s * BLOCK, (s + 1) * BLOCK)
        for j, t in enumerate(_attend(q_ref, rows, BLOCK, segs, sink_ref)):
            o_ref[rows, j * LANES:(j + 1) * LANES] = t.astype(o_ref.dtype)


def _smem_spec():
    return pl.BlockSpec(memory_space=pltpu.SMEM)


def _context_attention(sink, att_in, batch, seq_len):
    rows = CTX_SEQS_PER_STEP * seq_len

    def seq(width):
        return pl.BlockSpec((rows, width), lambda b: (b, 0))
    return pl.pallas_call(
        functools.partial(_context_attn_kernel, seq_len=seq_len),
        out_shape=jax.ShapeDtypeStruct((batch * seq_len, ATT_Q), BF16),
        grid=(batch // CTX_SEQS_PER_STEP,),
        in_specs=[_smem_spec(), seq(ATT_IN_W)],
        out_specs=seq(ATT_Q),
        compiler_params=_params(1),
        name="context_attention",
    )(sink, att_in)


def _latent_attention(sink, att_in, k_ctx, v_ctx, batch, seq_len):
    steps = seq_len // (LAT_BLOCKS_PER_STEP * BLOCK)
    rows = LAT_BLOCKS_PER_STEP * BLOCK
    past = k_ctx.shape[1]
    assert A_K == ATT_IN_W - A_K, "q and the key / value columns are the two halves of att_in"
    return pl.pallas_call(
        functools.partial(_latent_attn_kernel, seq_len=seq_len),
        out_shape=jax.ShapeDtypeStruct((batch * seq_len, ATT_Q), BF16),
        grid=(batch, steps),
        in_specs=[_smem_spec(),
                  pl.BlockSpec((rows, ATT_Q), lambda b, i: (b * steps + i, 0)),
                  pl.BlockSpec((seq_len, ATT_IN_W - A_K), lambda b, i: (b, 1)),
                  pl.BlockSpec((1, past, ATT_KV), lambda b, i: (b, 0, 0)),
                  pl.BlockSpec((1, past, ATT_KV), lambda b, i: (b, 0, 0))],
        out_specs=pl.BlockSpec((rows, ATT_Q), lambda b, i: (b * steps + i, 0)),
        compiler_params=_params(2),
        name="latent_attention",
    )(sink, att_in, att_in, k_ctx, v_ctx)


def _split2(x):
    hi = x.astype(BF16)
    lo = (x - hi.astype(F32)).astype(BF16)
    return hi, lo


def _gla_kernel(*refs, n_chunks, has_s0, emit_state):
    refs = list(refs)
    gf_ref, gv_ref, gn_ref = refs[:3]
    pos = 3
    s0_refs = refs[pos:pos + 2] if has_s0 else None
    pos += 2 if has_s0 else 0
    o_ref = refs[pos]
    pos += 1
    sfin_refs = refs[pos:pos + 2] if emit_state else None
    pos += 2 if emit_state else 0
    cum_ref, kv_ref, sent_ref = refs[pos:]

    C = GLA_CHUNK
    n_pairs = GLA_QK // LANES
    lane = lax.broadcasted_iota(jnp.int32, (1, LANES), 1)
    half_of = [lane < HALF, lane >= HALF]
    r_i = lax.broadcasted_iota(jnp.int32, (C, C), 0)
    c_i = lax.broadcasted_iota(jnp.int32, (C, C), 1)
    lower = c_i <= r_i
    upper = c_i >= r_i
    tri = jnp.concatenate([jnp.where(lower, 1.0, 0.0), jnp.where(upper, 1.0, 0.0)],
                          axis=0).astype(BF16)
    zeros_cc = jnp.zeros((C, C), F32)
    gnorm = gn_ref[...]
    qscale = GLA_DK ** -0.5

    def one_sequence(s):
        def chunk_rows(n):
            return slice((s * n_chunks + n) * C, (s * n_chunks + n + 1) * C)

        for n in range(n_chunks):
            rows = chunk_rows(n)
            hi, lo = _split2(gf_ref[rows, G_LA:GLA_F_W])
            sums = _dot(tri, jnp.concatenate([hi, lo], axis=1))
            cum_f = sums[:C, 0:GLA_QK] + sums[:C, 2 * GLA_QK:3 * GLA_QK]
            cum_b = sums[C:, GLA_QK:2 * GLA_QK] + sums[C:, 3 * GLA_QK:]
            cum_ref[0, rows, :] = cum_f
            cum_ref[1, rows, :] = cum_b
            k = gf_ref[rows, G_K:G_G]
            k_in = (k * jnp.exp(cum_f[C - 1:C, :] - cum_f), k * jnp.exp(cum_b[0:1, :] - cum_b))
            for p in range(n_pairs):
                sl = slice(p * LANES, (p + 1) * LANES)
                kv_t = None
                for e in range(2):
                    h = 2 * p + e
                    v_t = gv_ref[rows, h * GLA_DV:(h + 1) * GLA_DV].T
                    k_e = jnp.concatenate(
                        [jnp.where(half_of[e], k_in[d][:, sl], 0.0) for d in range(2)], axis=1)
                    part = _dot(v_t, k_e.astype(BF16))
                    kv_t = part if kv_t is None else kv_t + part
                kv_ref[s, n, p] = kv_t

        for p in range(n_pairs):
            sl = slice(p * LANES, (p + 1) * LANES)
            st = []
            for d in range(2):
                if has_s0:
                    s0 = s0_refs[d][s, 0, 2 * p:2 * p + 2, :, :].reshape(2 * GLA_DK, GLA_DV)
                    st.append(s0.T)
                else:
                    st.append(jnp.zeros((GLA_DV, 2 * GLA_DK), F32))
            for i in range(n_chunks):
                for d, n in ((0, i), (1, n_chunks - 1 - i)):
                    tot_row = (s * n_chunks + n) * C + (C - 1 if d == 0 else 0)
                    decay = jnp.exp(cum_ref[d, tot_row:tot_row + 1, sl])
                    sent_ref[s, n, p, :, d * LANES:(d + 1) * LANES] = st[d].astype(BF16)
                    st[d] = decay * st[d] + kv_ref[s, n, p, :, d * LANES:(d + 1) * LANES]
            if emit_state:
                for d in range(2):
                    sfin_refs[d][s, 0, 2 * p:2 * p + 2, :, :] = st[d].T.reshape(2, GLA_DK, GLA_DV)

        for n in range(n_chunks):
            rows = chunk_rows(n)
            q = gf_ref[rows, G_Q:G_K] * qscale
            k = gf_ref[rows, G_K:G_G]
            qs, ks, qin = [], [], []
            for d in range(2):
                cum = cum_ref[d, rows, :]
                ref = cum[C // 2:C // 2 + 1, :]
                qs.append(q * jnp.exp(cum - ref))
                ks.append((k * jnp.exp(ref - cum)).astype(BF16))
                qin.append(q * jnp.exp(cum))
            for p in range(n_pairs):
                sl = slice(p * LANES, (p + 1) * LANES)
                lhs = jnp.concatenate(
                    [jnp.where(half_of[e], qs[d][:, sl], 0.0) for e in range(2) for d in range(2)],
                    axis=0).astype(BF16)
                sc = _dot_t(lhs, jnp.concatenate([ks[0][:, sl], ks[1][:, sl]], axis=0))
                prob = []
                for e in range(2):
                    s_f = sc[(2 * e) * C:(2 * e + 1) * C, :C]
                    s_b = sc[(2 * e + 1) * C:(2 * e + 2) * C, C:]
                    prob.append(jnp.where(lower, s_f, 0.0) + jnp.where(upper, s_b, 0.0))
                p_blk = jnp.concatenate(
                    [jnp.concatenate([prob[0], zeros_cc], axis=1),
                     jnp.concatenate([zeros_cc, prob[1]], axis=1)], axis=0).astype(BF16)
                v2 = jnp.concatenate(
                    [gv_ref[rows, (2 * p + e) * GLA_DV:(2 * p + e + 1) * GLA_DV] for e in range(2)],
                    axis=0)
                q_in = jnp.concatenate(
                    [jnp.concatenate([jnp.where(half_of[e], qin[d][:, sl], 0.0) for d in range(2)],
                                     axis=1) for e in range(2)], axis=0).astype(BF16)
                o2 = _dot(p_blk, v2) + _dot_t(q_in, sent_ref[s, n, p])
                for e in range(2):
                    h = 2 * p + e
                    o = o2[e * C:(e + 1) * C, :]
                    o = o * lax.rsqrt(jnp.mean(o * o, axis=-1, keepdims=True) + EPS) * gnorm
                    gate = gf_ref[rows, G_G + h * GLA_DV:G_G + (h + 1) * GLA_DV]
                    o_ref[rows, h * GLA_DV:(h + 1) * GLA_DV] = (o * _silu(gate)).astype(o_ref.dtype)

    for s in range(kv_ref.shape[0]):
        one_sequence(s)


def _gla(gla_f, gv, gnorm, batch, seq_len, seqs_per_step, s0=None, emit_state=False):
    n_chunks = seq_len // GLA_CHUNK
    has_s0 = s0 is not None
    rows = seqs_per_step * seq_len

    def seq(width):
        return pl.BlockSpec((rows, width), lambda b: (b, 0))
    state_spec = pl.BlockSpec((seqs_per_step, 1, GLA_HEADS, GLA_DK, GLA_DV),
                              lambda b: (b, 0, 0, 0, 0))
    in_specs = [seq(GLA_F_W), seq(GLA_V), pl.BlockSpec((1, GLA_DV), lambda b: (0, 0))]
    args = [gla_f, gv, gnorm]
    if has_s0:
        in_specs += [state_spec, state_spec]
        args += list(s0)
    out_shape = [jax.ShapeDtypeStruct((batch * seq_len, GLA_V), BF16)]
    out_specs = [seq(GLA_V)]
    if emit_state:
        out_shape += [jax.ShapeDtypeStruct((batch, 1, GLA_HEADS, GLA_DK, GLA_DV), F32)] * 2
        out_specs += [state_spec, state_spec]
    n_pairs = GLA_QK // LANES
    return pl.pallas_call(
        functools.partial(_gla_kernel, n_chunks=n_chunks, has_s0=has_s0, emit_state=emit_state),
        out_shape=out_shape,
        grid=(batch // seqs_per_step,),
        in_specs=in_specs,
        out_specs=out_specs,
        scratch_shapes=[pltpu.VMEM((2, rows, GLA_QK), F32),
                        pltpu.VMEM((seqs_per_step, n_chunks, n_pairs, GLA_DV, 2 * LANES), F32),
                        pltpu.VMEM((seqs_per_step, n_chunks, n_pairs, GLA_DV, 2 * LANES), BF16)],
        compiler_params=_params(1),
        name="gla",
    )(*args)


def kernel(x_prompt, x_sample, cache_k, cache_v, state_gla_fwd, state_gla_bwd, c, c_ctx,
           w_ada, b_ada, norm_pre, norm_post, ffn_w1, ffn_w3, ffn_w2, w_in,
           gla_w_up, gla_b_up, gla_norm, attn_sink, w_out):
    depth = w_in.shape[0]
    assert depth == 1, "single trunk layer"
    batch, seq = x_prompt.shape[0], x_prompt.shape[1]
    dec_batch, dec_seq = x_sample.shape[0], x_sample.shape[1]
    past = cache_k.shape[2]
    l = 0

    cond = jnp.concatenate(
        [c_ctx[None, :], c, jnp.zeros((MOD_ROWS - 1 - dec_batch, D_MODEL), F32)], axis=0)
    mod = _ada_modulation(cond, w_ada[l], b_ada[l][None, :])

    npre, npost = norm_pre[l], norm_post[l]
    ffn_first_b = [w[l, :1].astype(BF16) for w in (ffn_w1, ffn_w3, ffn_w2)]
    ffn_second_f32 = [(w[l], 1) for w in (ffn_w1, ffn_w3, ffn_w2)]
    w_in_b = w_in[l:l + 1].astype(BF16)
    w_out_b = w_out[l:l + 1].astype(BF16)
    zeros = jnp.zeros((GLA_LOW_RANK, GLA_QK), F32)
    w_up = jnp.concatenate(
        [jnp.concatenate([gla_w_up[l, 0], zeros], axis=1),
         jnp.concatenate([zeros, gla_w_up[l, 1]], axis=1)], axis=0).astype(BF16)
    b_up = gla_b_up[l].reshape(1, 2 * GLA_QK)
    gnorm = gla_norm[l][None, :]
    sink = attn_sink[l]

    x1, *ffn_second_b = _ffn_first(
        x_prompt.reshape(batch * seq, D_MODEL), x_sample.reshape(dec_batch * dec_seq, D_MODEL),
        mod, dec_seq, npre, npost, *ffn_first_b, cast=ffn_second_f32)
    ctx_tiles = batch * seq // TOKEN_TILE

    def trunk(latent):
        n_batch, n_seq = (dec_batch, dec_seq) if latent else (batch, seq)
        latent_len = n_seq if latent else None
        first_tile = ctx_tiles if latent else 0
        att_in, gla_f, gv, *cache_t = _project(
            x1, first_tile, n_batch * n_seq, mod, latent_len, None if latent else n_seq, npre,
            w_in_b, w_up, b_up)
        if latent:
            att = _latent_attention(sink, att_in, cache_k[:, l].reshape(dec_batch, past, ATT_KV),
                                    cache_v[:, l].reshape(dec_batch, past, ATT_KV), n_batch, n_seq)
            (gla,) = _gla(gla_f, gv, gnorm, n_batch, n_seq, 1,
                          s0=(state_gla_fwd[:, l:l + 1], state_gla_bwd[:, l:l + 1]))
            extras = ()
        else:
            att = _context_attention(sink, att_in, n_batch, n_seq)
            gla, s_f, s_b = _gla(gla_f, gv, gnorm, n_batch, n_seq, CTX_SEQS_PER_STEP,
                                 emit_state=True)
            k_new, v_new = (jnp.transpose(c_t, (0, 1, 4, 2, 3)) for c_t in cache_t)
            extras = (k_new, v_new, s_f, s_b)
        y = _mix_ffn(x1, first_tile, att, gla, w_out_b, mod, latent_len, npre, npost,
                     *ffn_second_b)
        return y.reshape(n_batch, n_seq, D_MODEL), extras

    y_prompt, (k_new, v_new, s_f, s_b) = trunk(False)
    y_sample, _ = trunk(True)
    return (y_prompt, y_sample, k_new, v_new, s_f, s_b)
```

```python
import functools

import numpy as np
import jax
import jax.numpy as jnp
from jax import lax
from jax.experimental import pallas as pl
from jax.experimental.pallas import tpu as pltpu

F32 = jnp.float32
BF16 = jnp.bfloat16

D_MODEL = 1024
GRID_W = 64
N_Q_HEADS = 8
N_KV_HEADS = 2
HEAD_DIM = 64
WINDOW = 128
BLOCK = 128
ROPE_BASE = 10000.0
GLA_HEADS = 4
GLA_DK = 64
GLA_DV = 128
GLA_LOW_RANK = 16
GLA_TAU = 16.0
D_FF = 2816
N_MOD = 9
EPS = 1e-6
NEG = -1e30

ATT_Q = N_Q_HEADS * HEAD_DIM
ATT_KV = N_KV_HEADS * HEAD_DIM
GLA_QK = GLA_HEADS * GLA_DK
GLA_V = GLA_HEADS * GLA_DV
OFF_Q = 0
OFF_K = OFF_Q + ATT_Q
OFF_V = OFF_K + ATT_KV
OFF_GQ = OFF_V + ATT_KV
OFF_GK = OFF_GQ + GLA_QK
OFF_GV = OFF_GK + GLA_QK
OFF_GG = OFF_GV + GLA_V
OFF_LR = OFF_GG + GLA_V
IN_WIDTH = OFF_LR + 2 * GLA_LOW_RANK
A_Q = 0
A_K = A_Q + ATT_Q
A_KS = A_K + ATT_KV
A_V = A_KS + ATT_KV
A_VS = A_V + ATT_KV
ATT_IN_W = A_VS + ATT_KV
G_Q = 0
G_K = G_Q + GLA_QK
G_G = G_K + GLA_QK
G_LA = G_G + GLA_V
GLA_F_W = G_LA + 2 * GLA_QK
LOG2_E = 1.4426950408889634
ATT_Q_SCALE = HEAD_DIM ** -0.5 * LOG2_E

LANES = 128
SUBLANES = 8
HALF = LANES // 2
VMEM_LIMIT = 56 * 1024 * 1024

TOKEN_TILE = 512
FF_SPLITS = (0, 1536, D_FF)
CAST_SLABS = 16
ADA_K_TILE = 128
CTX_SEQS_PER_STEP = 4
LAT_BLOCKS_PER_STEP = 4
GLA_CHUNK = 128
MOD_ROWS = 8


def _params(n_axes):
    return pltpu.CompilerParams(
        dimension_semantics=("arbitrary",) * n_axes, vmem_limit_bytes=VMEM_LIMIT)


def _resident(shape):
    zeros = (0,) * len(shape)
    return pl.BlockSpec(shape, lambda *_: zeros, pipeline_mode=pl.Buffered(1))


def _sigmoid(x):
    return 1.0 / (1.0 + jnp.exp(-x))


def _silu(x):
    return x * _sigmoid(x)


def _rms(x, g):
    return x * lax.rsqrt(jnp.mean(x * x, axis=-1, keepdims=True) + EPS) * g


def _dot(a, b):
    return jnp.dot(a, b, preferred_element_type=F32)


def _dot_t(a, b):
    return lax.dot_general(a, b, (((1,), (1,)), ((), ())), preferred_element_type=F32)


def _ada_kernel(cond_ref, w_ref, b_ref, o_ref):
    @pl.when(pl.program_id(0) == 0)
    def _():
        for m in range(N_MOD):
            o_ref[m] = jnp.broadcast_to(b_ref[:, m * D_MODEL:(m + 1) * D_MODEL],
                                        (MOD_ROWS, D_MODEL))
    part = _dot(_silu(cond_ref[...]).astype(BF16), w_ref[...].astype(BF16))
    for m in range(N_MOD):
        o_ref[m] += part[:, m * D_MODEL:(m + 1) * D_MODEL]


def _ada_modulation(cond, w_ada, b_ada):
    k_dim, n = w_ada.shape
    return pl.pallas_call(
        _ada_kernel,
        out_shape=jax.ShapeDtypeStruct((N_MOD, MOD_ROWS, D_MODEL), F32),
        grid=(k_dim // ADA_K_TILE,),
        in_specs=[
            pl.BlockSpec((MOD_ROWS, ADA_K_TILE), lambda k: (0, k)),
            pl.BlockSpec((ADA_K_TILE, n), lambda k: (k, 0)),
            pl.BlockSpec((1, n), lambda k: (0, 0)),
        ],
        out_specs=pl.BlockSpec((N_MOD, MOD_ROWS, D_MODEL), lambda k: (0, 0, 0)),
        compiler_params=_params(1),
        name="ada_modulation",
    )(cond, w_ada, b_ada)


class _Mod:
    def __init__(self, mod_ref, tiles_per_seq, first_latent_tile=0):
        self.ref = mod_ref
        if tiles_per_seq is None:
            self.row = 0
        else:
            tile = pl.program_id(0) - first_latent_tile
            self.row = jnp.where(tile >= 0, 1 + tile // tiles_per_seq, 0)

    def __getitem__(self, m):
        return self.ref[m, pl.ds(self.row, 1), :]


def _modulated(x, mod, npre_ref, i):
    return _rms(x, npre_ref[i:i + 1, :]) * (1.0 + mod[3 * i + 1]) + mod[3 * i]


def _residual(x, out, mod, npost_ref, i, weight):
    return x + (weight * mod[3 * i + 2]) * _rms(out, npost_ref[i:i + 1, :])


def _ffn_sublayer(x, i, mod, npre_ref, npost_ref, w1_ref, w3_ref, w2_ref):
    h = _modulated(x, mod, npre_ref, i).astype(BF16)
    acc = None
    for lo, hi in zip(FF_SPLITS[:-1], FF_SPLITS[1:]):
        a = _dot(h, w1_ref[:, lo:hi])
        g = _dot(h, w3_ref[:, lo:hi])
        part = _dot((_silu(a) * g).astype(BF16), w2_ref[lo:hi, :])
        acc = part if acc is None else acc + part
    return _residual(x, acc, mod, npost_ref, i, 0.5)


def _ffn_first_kernel(*refs, ctx_tiles, tiles_per_seq, n_cast):
    xc_ref, xl_ref, mod_ref, npre_ref, npost_ref, w1_ref, w3_ref, w2_ref = refs[:8]
    cast_src = refs[8:8 + n_cast]
    o_ref = refs[8 + n_cast]
    cast_dst = refs[9 + n_cast:]
    mod = _Mod(mod_ref, tiles_per_seq, ctx_tiles)
    x = jnp.where(pl.program_id(0) >= ctx_tiles, xl_ref[...], xc_ref[...])
    o_ref[...] = _ffn_sublayer(x, 0, mod, npre_ref, npost_ref, w1_ref, w3_ref, w2_ref)
    for src, dst in zip(cast_src, cast_dst):
        dst[...] = src[...].astype(BF16)


def _mix_ffn_kernel(x_ref, att_ref, gla_ref, wo_ref, mod_ref, npre_ref, npost_ref,
                    w1_ref, w3_ref, w2_ref, o_ref, *, tiles_per_seq):
    mod = _Mod(mod_ref, tiles_per_seq)
    mix = _dot(att_ref[...], wo_ref[0:ATT_Q, :]) + _dot(gla_ref[...], wo_ref[ATT_Q:, :])
    x = _residual(x_ref[...], mix, mod, npost_ref, 1, 1.0)
    o_ref[...] = _ffn_sublayer(x, 2, mod, npre_ref, npost_ref, w1_ref, w3_ref, w2_ref)


def _rope_tile(x, cos, sin_up, sin_dn):
    up = pltpu.roll(x, LANES - 16, axis=1)
    dn = pltpu.roll(x, 16, axis=1)
    return x * cos + up * sin_up + dn * sin_dn


def _proj_kernel(*refs, tiles_per_seq, cache_seq):
    x_ref, mod_ref, npre_ref, win_ref, wup_ref, bup_ref = refs[:6]
    rope_refs = refs[6:9] if tiles_per_seq is not None else None
    outs = refs[6 + (3 if rope_refs else 0):]
    att_ref, glaf_ref, gv_ref = outs[:3]
    h = _modulated(x_ref[...], _Mod(mod_ref, tiles_per_seq), npre_ref, 1).astype(BF16)
    q = _dot(h, win_ref[:, OFF_Q:OFF_K])
    kv = _dot(h, win_ref[:, OFF_K:OFF_GQ])
    k, v = kv[:, :ATT_KV], kv[:, ATT_KV:]
    lr = _dot(h, win_ref[:, OFF_LR:IN_WIDTH])
    z = _dot(lr.astype(BF16), wup_ref[...]) + bup_ref[...]
    if rope_refs:
        cos, sup, sdn = (r[...] for r in rope_refs)
        q = jnp.concatenate([_rope_tile(q[:, j * LANES:(j + 1) * LANES], cos, sup, sdn)
                             for j in range(ATT_Q // LANES)], axis=1)
        k_att = _rope_tile(k, cos, sup, sdn)
    else:
        k_att = k
    att_ref[:, A_Q:A_K] = (q * ATT_Q_SCALE).astype(BF16)
    att_ref[:, A_K:A_KS] = k_att.astype(BF16)
    att_ref[:, A_KS:A_V] = pltpu.roll(k_att, HALF, axis=1).astype(BF16)
    att_ref[:, A_V:A_VS] = v.astype(BF16)
    att_ref[:, A_VS:ATT_IN_W] = pltpu.roll(v, HALF, axis=1).astype(BF16)
    if cache_seq is not None:
        for t, dst in ((k.T, outs[3]), (v.T, outs[4])):
            for b in range(TOKEN_TILE // cache_seq):
                dst[b, 0] = t[:, b * cache_seq:(b + 1) * cache_seq].reshape(
                    N_KV_HEADS, HEAD_DIM, cache_seq)
    glaf_ref[:, G_Q:G_G] = _dot(h, win_ref[:, OFF_GQ:OFF_GV])
    log_sig = jnp.minimum(z, 0.0) - jnp.log(1.0 + jnp.exp(-jnp.abs(z)))
    glaf_ref[:, G_LA:GLA_F_W] = log_sig * (1.0 / GLA_TAU)
    gv_ref[...] = _dot(h, win_ref[:, OFF_GV:OFF_GG]).astype(BF16)
    glaf_ref[:, G_G:G_LA] = _dot(h, win_ref[:, OFF_GG:OFF_LR])


def _token_spec(width, first_tile=0):
    return pl.BlockSpec((TOKEN_TILE, width), lambda i: (i + first_tile, 0))


def _stack_spec(shape, j):
    return pl.BlockSpec((None,) + tuple(shape[1:]), lambda i: (j, 0, 0),
                        pipeline_mode=pl.Buffered(1))


def _tiles_per_seq(latent_len):
    return None if latent_len is None else latent_len // TOKEN_TILE


def _ffn_first(x_ctx, x_lat, mod, latent_len, npre, npost, w1, w3, w2, cast):
    ctx_tiles = x_ctx.shape[0] // TOKEN_TILE
    steps = ctx_tiles + x_lat.shape[0] // TOKEN_TILE
    assert steps >= CAST_SLABS, "every slab needs its own grid step"
    in_specs = [pl.BlockSpec((TOKEN_TILE, D_MODEL), lambda i: (jnp.minimum(i, ctx_tiles - 1), 0)),
                pl.BlockSpec((TOKEN_TILE, D_MODEL), lambda i: (jnp.maximum(i - ctx_tiles, 0), 0)),
                _resident(mod.shape), _resident(npre.shape),
                _resident(npost.shape), _stack_spec(w1.shape, 0),
                _stack_spec(w3.shape, 0), _stack_spec(w2.shape, 0)]
    out_shape = [jax.ShapeDtypeStruct((steps * TOKEN_TILE, D_MODEL), F32)]
    out_specs = [_token_spec(D_MODEL)]
    for stack, j in cast:
        _, rows, cols = stack.shape
        slab = rows // CAST_SLABS
        assert slab * CAST_SLABS == rows and slab % 16 == 0, "row slabs must be whole bf16 tiles"
        in_specs.append(pl.BlockSpec((None, slab, cols), functools.partial(
            lambda i, j: (j, i * CAST_SLABS // steps, 0), j=j)))
        out_shape.append(jax.ShapeDtypeStruct((1, rows, cols), BF16))
        out_specs.append(pl.BlockSpec((None, slab, cols), lambda i: (0, i * CAST_SLABS // steps, 0)))
    return pl.pallas_call(
        functools.partial(_ffn_first_kernel, ctx_tiles=ctx_tiles,
                          tiles_per_seq=_tiles_per_seq(latent_len), n_cast=len(cast)),
        out_shape=out_shape,
        grid=(steps,),
        in_specs=in_specs,
        out_specs=out_specs,
        compiler_params=_params(1),
        name="ffn_first",
    )(x_ctx, x_lat, mod, npre, npost, w1, w3, w2, *[stack for stack, _ in cast])


def _mix_ffn(x, first_tile, att, gla, w_out, mod, latent_len, npre, npost, w1, w3, w2):
    t = att.shape[0]
    return pl.pallas_call(
        functools.partial(_mix_ffn_kernel, tiles_per_seq=_tiles_per_seq(latent_len)),
        out_shape=jax.ShapeDtypeStruct((t, D_MODEL), F32),
        grid=(t // TOKEN_TILE,),
        in_specs=[_token_spec(D_MODEL, first_tile), _token_spec(ATT_Q), _token_spec(GLA_V),
                  _stack_spec(w_out.shape, 0), _resident(mod.shape), _resident(npre.shape),
                  _resident(npost.shape), _stack_spec(w1.shape, 0),
                  _stack_spec(w3.shape, 0), _stack_spec(w2.shape, 0)],
        out_specs=_token_spec(D_MODEL),
        compiler_params=_params(1),
        name="mix_ffn",
    )(x, att, gla, w_out, mod, npre, npost, w1, w3, w2)


def _project(x, first_tile, t, mod, latent_len, cache_seq, npre, w_in, w_up, b_up):
    tiles = _tiles_per_seq(latent_len)
    in_specs = [_token_spec(D_MODEL, first_tile), _resident(mod.shape), _resident(npre.shape),
                _stack_spec(w_in.shape, 0), _resident(w_up.shape), _resident(b_up.shape)]
    args = [x, mod, npre, w_in, w_up, b_up]
    if tiles is not None:
        in_specs += [pl.BlockSpec((TOKEN_TILE, LANES), lambda i: (i % tiles, 0))] * 3
        args += [jnp.asarray(tab) for tab in _rope_tables(latent_len)]
    outs = ((ATT_IN_W, BF16), (GLA_F_W, F32), (GLA_V, BF16))
    out_shape = [jax.ShapeDtypeStruct((t, w), dt) for w, dt in outs]
    out_specs = [_token_spec(w) for w, _ in outs]
    if cache_seq is not None:
        seqs = TOKEN_TILE // cache_seq
        cache = (t // cache_seq, 1, N_KV_HEADS, HEAD_DIM, cache_seq)
        out_shape += [jax.ShapeDtypeStruct(cache, F32)] * 2
        out_specs += [pl.BlockSpec((seqs,) + cache[1:], lambda i: (i, 0, 0, 0, 0))] * 2
    return pl.pallas_call(
        functools.partial(_proj_kernel, tiles_per_seq=tiles, cache_seq=cache_seq),
        out_shape=out_shape,
        grid=(t // TOKEN_TILE,),
        in_specs=in_specs,
        out_specs=out_specs,
        compiler_params=_params(1),
        name="project",
    )(*args)


def _rope_tables(seq_len):
    half = HEAD_DIM // 2
    inv_freq = np.float32(ROPE_BASE) ** (-np.arange(0, half, 2, dtype=np.float32) / half)
    pos = np.arange(seq_len)
    row = (pos // GRID_W).astype(np.float32)
    col = (pos % GRID_W).astype(np.float32)
    within = np.arange(LANES) % HEAD_DIM
    idx = within % half
    freq = inv_freq[idx % (half // 2)].astype(np.float32)
    p = np.where((within // half == 0)[None, :], row[:, None], col[:, None])
    ang = (p * freq[None, :]).astype(np.float32)
    cos, sin = np.cos(ang).astype(np.float32), np.sin(ang).astype(np.float32)
    first = (idx < half // 2)[None, :]
    zero = np.float32(0.0)
    return cos, np.where(first, -sin, zero), np.where(first, zero, sin)


def _attend(q_ref, q_rows, nq, segs, sink_ref):
    lane = lax.broadcasted_iota(jnp.int32, (1, LANES), 1)
    half_of = [lane < HALF, lane >= HALF]
    rows = lax.broadcasted_iota(jnp.int32, (2 * nq, 1), 0)
    zero = jnp.zeros((), BF16)
    out = [None] * (ATT_Q // LANES)
    for g in range(N_KV_HEADS):
        tiles = (2 * g, 2 * g + 1)
        for e in range(2):
            qm = jnp.concatenate(
                [jnp.where(half_of[e], q_ref[q_rows, j * LANES:(j + 1) * LANES], zero)
                 for j in tiles], axis=0)
            sink = jnp.where(rows < nq, sink_ref[2 * tiles[0] + e],
                             sink_ref[2 * tiles[1] + e]) * LOG2_E
            scores = []
            for k, k_sw, _, _, m in segs:
                s = _dot_t(qm, k if e == g else k_sw)
                scores.append(s if m is None else jnp.where(m, s, NEG))
            mx = sink
            for s in scores:
                mx = jnp.maximum(mx, jnp.max(s, axis=-1, keepdims=True))
            probs = [jnp.exp2(s - mx) for s in scores]
            den = jnp.exp2(sink - mx)
            for p in probs:
                den = den + jnp.sum(p, axis=-1, keepdims=True)
            o = None
            for p, (_, _, v, v_sw, _) in zip(probs, segs):
                part = _dot(p.astype(BF16), jnp.where(half_of[e], v if e == g else v_sw, zero))
                o = part if o is None else o + part
            o = o * (1.0 / den)
            for r, j in enumerate(tiles):
                blk = o[r * nq:(r + 1) * nq, :]
                out[j] = blk if out[j] is None else out[j] + blk
    return out


def _context_attn_kernel(sink_ref, a_ref, o_ref, *, seq_len):
    for s in range(a_ref.shape[0] // seq_len):
        rows = slice(s * seq_len, (s + 1) * seq_len)
        segs = [(a_ref[rows, A_K:A_KS], a_ref[rows, A_KS:A_V],
                 a_ref[rows, A_V:A_VS], a_ref[rows, A_VS:ATT_IN_W], None)]
        for j, t in enumerate(_attend(a_ref, rows, seq_len, segs, sink_ref)):
            o_ref[rows, j * LANES:(j + 1) * LANES] = t.astype(o_ref.dtype)


def _latent_attn_kernel(sink_ref, q_ref, kv_ref, kc_ref, vc_ref, o_ref, *, seq_len):
    k_ctx, v_ctx = kc_ref[0], vc_ref[0]
    ctx = (k_ctx.astype(BF16), pltpu.roll(k_ctx, HALF, axis=1).astype(BF16),
           v_ctx.astype(BF16), pltpu.roll(v_ctx, HALF, axis=1).astype(BF16), None)
    span = 3 * BLOCK
    per_step = q_ref.shape[0] // BLOCK
    for s in range(per_step):
        i = pl.program_id(1) * per_step + s
        start = pl.multiple_of(jnp.clip((i - 1) * BLOCK, 0, seq_len - span), BLOCK)
        kpos = start + lax.broadcasted_iota(jnp.int32, (1, span), 1)
        qpos = i * BLOCK + lax.broadcasted_iota(jnp.int32, (BLOCK, 1), 0)
        valid = jnp.abs(kpos - qpos) <= WINDOW
        valid2 = jnp.concatenate([valid, valid], axis=0)
        kv = kv_ref[pl.ds(start, span), :]
        segs = [tuple(kv[:, c * ATT_KV:(c + 1) * ATT_KV] for c in range(4)) + (valid2,), ctx]
        rows = slice(s * BLOCK, (s + 1) * BLOCK)
        for j, t in enumerate(_attend(q_ref, rows, BLOCK, segs, sink_ref)):
            o_ref[rows, j * LANES:(j + 1) * LANES] = t.astype(o_ref.dtype)


def _smem_spec():
    return pl.BlockSpec(memory_space=pltpu.SMEM)


def _context_attention(sink, att_in, batch, seq_len):
    rows = CTX_SEQS_PER_STEP * seq_len

    def seq(width):
        return pl.BlockSpec((rows, width), lambda b: (b, 0))
    return pl.pallas_call(
        functools.partial(_context_attn_kernel, seq_len=seq_len),
        out_shape=jax.ShapeDtypeStruct((batch * seq_len, ATT_Q), BF16),
        grid=(batch // CTX_SEQS_PER_STEP,),
        in_specs=[_smem_spec(), seq(ATT_IN_W)],
        out_specs=seq(ATT_Q),
        compiler_params=_params(1),
        name="context_attention",
    )(sink, att_in)


def _latent_attention(sink, att_in, k_ctx, v_ctx, batch, seq_len):
    steps = seq_len // (LAT_BLOCKS_PER_STEP * BLOCK)
    rows = LAT_BLOCKS_PER_STEP * BLOCK
    past = k_ctx.shape[1]
    assert A_K == ATT_IN_W - A_K, "q and the key / value columns are the two halves of att_in"
    return pl.pallas_call(
        functools.partial(_latent_attn_kernel, seq_len=seq_len),
        out_shape=jax.ShapeDtypeStruct((batch * seq_len, ATT_Q), BF16),
        grid=(batch, steps),
        in_specs=[_smem_spec(),
                  pl.BlockSpec((rows, ATT_Q), lambda b, i: (b * steps + i, 0)),
                  pl.BlockSpec((seq_len, ATT_IN_W - A_K), lambda b, i: (b, 1)),
                  pl.BlockSpec((1, past, ATT_KV), lambda b, i: (b, 0, 0)),
                  pl.BlockSpec((1, past, ATT_KV), lambda b, i: (b, 0, 0))],
        out_specs=pl.BlockSpec((rows, ATT_Q), lambda b, i: (b * steps + i, 0)),
        compiler_params=_params(2),
        name="latent_attention",
    )(sink, att_in, att_in, k_ctx, v_ctx)


def _split2(x):
    hi = x.astype(BF16)
    lo = (x - hi.astype(F32)).astype(BF16)
    return hi, lo


def _gla_kernel(*refs, n_chunks, has_s0, emit_state):
    refs = list(refs)
    gf_ref, gv_ref, gn_ref = refs[:3]
    pos = 3
    s0_refs = refs[pos:pos + 2] if has_s0 else None
    pos += 2 if has_s0 else 0
    o_ref = refs[pos]
    pos += 1
    sfin_refs = refs[pos:pos + 2] if emit_state else None
    pos += 2 if emit_state else 0
    cum_ref, kv_ref, sent_ref = refs[pos:]

    C = GLA_CHUNK
    n_pairs = GLA_QK // LANES
    lane = lax.broadcasted_iota(jnp.int32, (1, LANES), 1)
    half_of = [lane < HALF, lane >= HALF]
    r_i = lax.broadcasted_iota(jnp.int32, (C, C), 0)
    c_i = lax.broadcasted_iota(jnp.int32, (C, C), 1)
    lower = c_i <= r_i
    upper = c_i >= r_i
    tri = jnp.concatenate([jnp.where(lower, 1.0, 0.0), jnp.where(upper, 1.0, 0.0)],
                          axis=0).astype(BF16)
    zeros_cc = jnp.zeros((C, C), F32)
    gnorm = gn_ref[...]
    qscale = GLA_DK ** -0.5

    def one_sequence(s):
        def chunk_rows(n):
            return slice((s * n_chunks + n) * C, (s * n_chunks + n + 1) * C)

        for n in range(n_chunks):
            rows = chunk_rows(n)
            hi, lo = _split2(gf_ref[rows, G_LA:GLA_F_W])
            sums = _dot(tri, jnp.concatenate([hi, lo], axis=1))
            cum_f = sums[:C, 0:GLA_QK] + sums[:C, 2 * GLA_QK:3 * GLA_QK]
            cum_b = sums[C:, GLA_QK:2 * GLA_QK] + sums[C:, 3 * GLA_QK:]
            cum_ref[0, rows, :] = cum_f
            cum_ref[1, rows, :] = cum_b
            k = gf_ref[rows, G_K:G_G]
            k_in = (k * jnp.exp(cum_f[C - 1:C, :] - cum_f), k * jnp.exp(cum_b[0:1, :] - cum_b))
            for p in range(n_pairs):
                sl = slice(p * LANES, (p + 1) * LANES)
                kv_t = None
                for e in range(2):
                    h = 2 * p + e
                    v_t = gv_ref[rows, h * GLA_DV:(h + 1) * GLA_DV].T
                    k_e = jnp.concatenate(
                        [jnp.where(half_of[e], k_in[d][:, sl], 0.0) for d in range(2)], axis=1)
                    part = _dot(v_t, k_e.astype(BF16))
                    kv_t = part if kv_t is None else kv_t + part
                kv_ref[s, n, p] = kv_t

        for p in range(n_pairs):
            sl = slice(p * LANES, (p + 1) * LANES)
            st = []
            for d in range(2):
                if has_s0:
                    s0 = s0_refs[d][s, 0, 2 * p:2 * p + 2, :, :].reshape(2 * GLA_DK, GLA_DV)
                    st.append(s0.T)
                else:
                    st.append(jnp.zeros((GLA_DV, 2 * GLA_DK), F32))
            for i in range(n_chunks):
                for d, n in ((0, i), (1, n_chunks - 1 - i)):
                    tot_row = (s * n_chunks + n) * C + (C - 1 if d == 0 else 0)
                    decay = jnp.exp(cum_ref[d, tot_row:tot_row + 1, sl])
                    sent_ref[s, n, p, :, d * LANES:(d + 1) * LANES] = st[d].astype(BF16)
                    st[d] = decay * st[d] + kv_ref[s, n, p, :, d * LANES:(d + 1) * LANES]
            if emit_state:
                for d in range(2):
                    sfin_refs[d][s, 0, 2 * p:2 * p + 2, :, :] = st[d].T.reshape(2, GLA_DK, GLA_DV)

        for n in range(n_chunks):
            rows = chunk_rows(n)
            q = gf_ref[rows, G_Q:G_K] * qscale
            k = gf_ref[rows, G_K:G_G]
            qs, ks, qin = [], [], []
            for d in range(2):
                cum = cum_ref[d, rows, :]
                ref = cum[C // 2:C // 2 + 1, :]
                qs.append(q * jnp.exp(cum - ref))
                ks.append((k * jnp.exp(ref - cum)).astype(BF16))
                qin.append(q * jnp.exp(cum))
            for p in range(n_pairs):
                sl = slice(p * LANES, (p + 1) * LANES)
                lhs = jnp.concatenate(
                    [jnp.where(half_of[e], qs[d][:, sl], 0.0) for e in range(2) for d in range(2)],
                    axis=0).astype(BF16)
                sc = _dot_t(lhs, jnp.concatenate([ks[0][:, sl], ks[1][:, sl]], axis=0))
                prob = []
                for e in range(2):
                    s_f = sc[(2 * e) * C:(2 * e + 1) * C, :C]
                    s_b = sc[(2 * e + 1) * C:(2 * e + 2) * C, C:]
                    prob.append(jnp.where(lower, s_f, 0.0) + jnp.where(upper, s_b, 0.0))
                p_blk = jnp.concatenate(
                    [jnp.concatenate([prob[0], zeros_cc], axis=1),
                     jnp.concatenate([zeros_cc, prob[1]], axis=1)], axis=0).astype(BF16)
                v2 = jnp.concatenate(
                    [gv_ref[rows, (2 * p + e) * GLA_DV:(2 * p + e + 1) * GLA_DV] for e in range(2)],
                    axis=0)
                q_in = jnp.concatenate(
                    [jnp.concatenate([jnp.where(half_of[e], qin[d][:, sl], 0.0) for d in range(2)],
                                     axis=1) for e in range(2)], axis=0).astype(BF16)
                o2 = _dot(p_blk, v2) + _dot_t(q_in, sent_ref[s, n, p])
                for e in range(2):
                    h = 2 * p + e
                    o = o2[e * C:(e + 1) * C, :]
                    o = o * lax.rsqrt(jnp.mean(o * o, axis=-1, keepdims=True) + EPS) * gnorm
                    gate = gf_ref[rows, G_G + h * GLA_DV:G_G + (h + 1) * GLA_DV]
                    o_ref[rows, h * GLA_DV:(h + 1) * GLA_DV] = (o * _silu(gate)).astype(o_ref.dtype)

    for s in range(kv_ref.shape[0]):
        one_sequence(s)


def _gla(gla_f, gv, gnorm, batch, seq_len, seqs_per_step, s0=None, emit_state=False):
    n_chunks = seq_len // GLA_CHUNK
    has_s0 = s0 is not None
    rows = seqs_per_step * seq_len

    def seq(width):
        return pl.BlockSpec((rows, width), lambda b: (b, 0))
    state_spec = pl.BlockSpec((seqs_per_step, 1, GLA_HEADS, GLA_DK, GLA_DV),
                              lambda b: (b, 0, 0, 0, 0))
    in_specs = [seq(GLA_F_W), seq(GLA_V), pl.BlockSpec((1, GLA_DV), lambda b: (0, 0))]
    args = [gla_f, gv, gnorm]
    if has_s0:
        in_specs += [state_spec, state_spec]
        args += list(s0)
    out_shape = [jax.ShapeDtypeStruct((batch * seq_len, GLA_V), BF16)]
    out_specs = [seq(GLA_V)]
    if emit_state:
        out_shape += [jax.ShapeDtypeStruct((batch, 1, GLA_HEADS, GLA_DK, GLA_DV), F32)] * 2
        out_specs += [state_spec, state_spec]
    n_pairs = GLA_QK // LANES
    return pl.pallas_call(
        functools.partial(_gla_kernel, n_chunks=n_chunks, has_s0=has_s0, emit_state=emit_state),
        out_shape=out_shape,
        grid=(batch // seqs_per_step,),
        in_specs=in_specs,
        out_specs=out_specs,
        scratch_shapes=[pltpu.VMEM((2, rows, GLA_QK), F32),
                        pltpu.VMEM((seqs_per_step, n_chunks, n_pairs, GLA_DV, 2 * LANES), F32),
                        pltpu.VMEM((seqs_per_step, n_chunks, n_pairs, GLA_DV, 2 * LANES), BF16)],
        compiler_params=_params(1),
        name="gla",
    )(*args)


def kernel(x_prompt, x_sample, cache_k, cache_v, state_gla_fwd, state_gla_bwd, c, c_ctx,
           w_ada, b_ada, norm_pre, norm_post, ffn_w1, ffn_w3, ffn_w2, w_in,
           gla_w_up, gla_b_up, gla_norm, attn_sink, w_out):
    depth = w_in.shape[0]
    assert depth == 1, "single trunk layer"
    batch, seq = x_prompt.shape[0], x_prompt.shape[1]
    dec_batch, dec_seq = x_sample.shape[0], x_sample.shape[1]
    past = cache_k.shape[2]
    l = 0

    cond = jnp.concatenate(
        [c_ctx[None, :], c, jnp.zeros((MOD_ROWS - 1 - dec_batch, D_MODEL), F32)], axis=0)
    mod = _ada_modulation(cond, w_ada[l], b_ada[l][None, :])

    npre, npost = norm_pre[l], norm_post[l]
    ffn_first_b = [w[l, :1].astype(BF16) for w in (ffn_w1, ffn_w3, ffn_w2)]
    ffn_second_f32 = [(w[l], 1) for w in (ffn_w1, ffn_w3, ffn_w2)]
    w_in_b = w_in[l:l + 1].astype(BF16)
    w_out_b = w_out[l:l + 1].astype(BF16)
    zeros = jnp.zeros((GLA_LOW_RANK, GLA_QK), F32)
    w_up = jnp.concatenate(
        [jnp.concatenate([gla_w_up[l, 0], zeros], axis=1),
         jnp.concatenate([zeros, gla_w_up[l, 1]], axis=1)], axis=0).astype(BF16)
    b_up = gla_b_up[l].reshape(1, 2 * GLA_QK)
    gnorm = gla_norm[l][None, :]
    sink = attn_sink[l]

    x1, *ffn_second_b = _ffn_first(
        x_prompt.reshape(batch * seq, D_MODEL), x_sample.reshape(dec_batch * dec_seq, D_MODEL),
        mod, dec_seq, npre, npost, *ffn_first_b, cast=ffn_second_f32)
    ctx_tiles = batch * seq // TOKEN_TILE

    def trunk(latent):
        n_batch, n_seq = (dec_batch, dec_seq) if latent else (batch, seq)
        latent_len = n_seq if latent else None
        first_tile = ctx_tiles if latent else 0
        att_in, gla_f, gv, *cache_t = _project(
            x1, first_tile, n_batch * n_seq, mod, latent_len, None if latent else n_seq, npre,
            w_in_b, w_up, b_up)
        if latent:
            att = _latent_attention(sink, att_in, cache_k[:, l].reshape(dec_batch, past, ATT_KV),
                                    cache_v[:, l].reshape(dec_batch, past, ATT_KV), n_batch, n_seq)
            (gla,) = _gla(gla_f, gv, gnorm, n_batch, n_seq, 1,
                          s0=(state_gla_fwd[:, l:l + 1], state_gla_bwd[:, l:l + 1]))
            extras = ()
        else:
            att = _context_attention(sink, att_in, n_batch, n_seq)
            gla, s_f, s_b = _gla(gla_f, gv, gnorm, n_batch, n_seq, CTX_SEQS_PER_STEP,
                                 emit_state=True)
            k_new, v_new = (jnp.transpose(c_t, (0, 1, 4, 2, 3)) for c_t in cache_t)
            extras = (k_new, v_new, s_f, s_b)
        y = _mix_ffn(x1, first_tile, att, gla, w_out_b, mod, latent_len, npre, npost,
                     *ffn_second_b)
        return y.reshape(n_batch, n_seq, D_MODEL), extras

    y_prompt, (k_new, v_new, s_f, s_b) = trunk(False)
    y_sample, _ = trunk(True)
    return (y_prompt, y_sample, k_new, v_new, s_f, s_b)
```

```python
import functools

import numpy as np
import jax
import jax.numpy as jnp
from jax import lax
from jax.experimental import pallas as pl
from jax.experimental.pallas import tpu as pltpu

F32 = jnp.float32
BF16 = jnp.bfloat16

D_MODEL = 1024
GRID_W = 64
N_Q_HEADS = 8
N_KV_HEADS = 2
HEAD_DIM = 64
WINDOW = 128
BLOCK = 128
ROPE_BASE = 10000.0
GLA_HEADS = 4
GLA_DK = 64
GLA_DV = 128
GLA_LOW_RANK = 16
GLA_TAU = 16.0
D_FF = 2816
N_MOD = 9
EPS = 1e-6
NEG = -1e30

ATT_Q = N_Q_HEADS * HEAD_DIM
ATT_KV = N_KV_HEADS * HEAD_DIM
GLA_QK = GLA_HEADS * GLA_DK
GLA_V = GLA_HEADS * GLA_DV
OFF_Q = 0
OFF_K = OFF_Q + ATT_Q
OFF_V = OFF_K + ATT_KV
OFF_GQ = OFF_V + ATT_KV
OFF_GK = OFF_GQ + GLA_QK
OFF_GV = OFF_GK + GLA_QK
OFF_GG = OFF_GV + GLA_V
OFF_LR = OFF_GG + GLA_V
IN_WIDTH = OFF_LR + 2 * GLA_LOW_RANK
A_Q = 0
A_K = A_Q + ATT_Q
A_KS = A_K + ATT_KV
A_V = A_KS + ATT_KV
A_VS = A_V + ATT_KV
ATT_IN_W = A_VS + ATT_KV
G_Q = 0
G_K = G_Q + GLA_QK
G_G = G_K + GLA_QK
G_LA = G_G + GLA_V
GLA_F_W = G_LA + 2 * GLA_QK
ROPE_PAIR = HEAD_DIM // 4
LOG2_E = 1.4426950408889634
ATT_Q_SCALE = HEAD_DIM ** -0.5 * LOG2_E

LANES = 128
SUBLANES = 8
HALF = LANES // 2
VMEM_LIMIT = 56 * 1024 * 1024

TOKEN_TILE = 512
FF_SPLITS = (0, 1536, D_FF)
CAST_SLABS = 16
ADA_K_TILE = 128
CTX_SEQS_PER_STEP = 4
LAT_BLOCKS_PER_STEP = 4
GLA_CHUNK = 128
MOD_ROWS = 8


def _params(n_axes):
    return pltpu.CompilerParams(
        dimension_semantics=("arbitrary",) * n_axes, vmem_limit_bytes=VMEM_LIMIT)


def _resident(shape):
    zeros = (0,) * len(shape)
    return pl.BlockSpec(shape, lambda *_: zeros, pipeline_mode=pl.Buffered(1))


def _sigmoid(x):
    return 1.0 / (1.0 + jnp.exp(-x))


def _silu(x):
    return x * _sigmoid(x)


def _rms(x, g):
    return x * lax.rsqrt(jnp.mean(x * x, axis=-1, keepdims=True) + EPS) * g


def _dot(a, b):
    return jnp.dot(a, b, preferred_element_type=F32)


def _dot_t(a, b):
    return lax.dot_general(a, b, (((1,), (1,)), ((), ())), preferred_element_type=F32)


def _ada_kernel(cond_ref, w_ref, b_ref, o_ref):
    @pl.when(pl.program_id(0) == 0)
    def _():
        for m in range(N_MOD):
            o_ref[m] = jnp.broadcast_to(b_ref[:, m * D_MODEL:(m + 1) * D_MODEL],
                                        (MOD_ROWS, D_MODEL))
    part = _dot(_silu(cond_ref[...]).astype(BF16), w_ref[...].astype(BF16))
    for m in range(N_MOD):
        o_ref[m] += part[:, m * D_MODEL:(m + 1) * D_MODEL]


def _ada_modulation(cond, w_ada, b_ada):
    k_dim, n = w_ada.shape
    return pl.pallas_call(
        _ada_kernel,
        out_shape=jax.ShapeDtypeStruct((N_MOD, MOD_ROWS, D_MODEL), F32),
        grid=(k_dim // ADA_K_TILE,),
        in_specs=[
            pl.BlockSpec((MOD_ROWS, ADA_K_TILE), lambda k: (0, k)),
            pl.BlockSpec((ADA_K_TILE, n), lambda k: (k, 0)),
            pl.BlockSpec((1, n), lambda k: (0, 0)),
        ],
        out_specs=pl.BlockSpec((N_MOD, MOD_ROWS, D_MODEL), lambda k: (0, 0, 0)),
        compiler_params=_params(1),
        name="ada_modulation",
    )(cond, w_ada, b_ada)


class _Mod:
    def __init__(self, mod_ref, tiles_per_seq, first_latent_tile=0):
        self.ref = mod_ref
        if tiles_per_seq is None:
            self.row = 0
        else:
            tile = pl.program_id(0) - first_latent_tile
            self.row = jnp.where(tile >= 0, 1 + tile // tiles_per_seq, 0)

    def __getitem__(self, m):
        return self.ref[m, pl.ds(self.row, 1), :]


def _modulated(x, mod, npre_ref, i):
    return _rms(x, npre_ref[i:i + 1, :]) * (1.0 + mod[3 * i + 1]) + mod[3 * i]


def _residual(x, out, mod, npost_ref, i, weight):
    return x + (weight * mod[3 * i + 2]) * _rms(out, npost_ref[i:i + 1, :])


def _ffn_sublayer(x, i, mod, npre_ref, npost_ref, w1_ref, w3_ref, w2_ref):
    h = _modulated(x, mod, npre_ref, i).astype(BF16)
    acc = None
    for lo, hi in zip(FF_SPLITS[:-1], FF_SPLITS[1:]):
        a = _dot(h, w1_ref[:, lo:hi])
        g = _dot(h, w3_ref[:, lo:hi])
        part = _dot((_silu(a) * g).astype(BF16), w2_ref[lo:hi, :])
        acc = part if acc is None else acc + part
    return _residual(x, acc, mod, npost_ref, i, 0.5)


def _ffn_first_kernel(*refs, ctx_tiles, tiles_per_seq, n_cast):
    xc_ref, xl_ref, mod_ref, npre_ref, npost_ref, w1_ref, w3_ref, w2_ref = refs[:8]
    cast_src = refs[8:8 + n_cast]
    o_ref = refs[8 + n_cast]
    cast_dst = refs[9 + n_cast:]
    mod = _Mod(mod_ref, tiles_per_seq, ctx_tiles)
    x = jnp.where(pl.program_id(0) >= ctx_tiles, xl_ref[...], xc_ref[...])
    o_ref[...] = _ffn_sublayer(x, 0, mod, npre_ref, npost_ref, w1_ref, w3_ref, w2_ref)
    for src, dst in zip(cast_src, cast_dst):
        dst[...] = src[...].astype(BF16)


def _mix_ffn_kernel(x_ref, att_ref, gla_ref, wo_ref, mod_ref, npre_ref, npost_ref,
                    w1_ref, w3_ref, w2_ref, o_ref, *, tiles_per_seq):
    mod = _Mod(mod_ref, tiles_per_seq)
    mix = _dot(att_ref[...], wo_ref[0:ATT_Q, :]) + _dot(gla_ref[...], wo_ref[ATT_Q:, :])
    x = _residual(x_ref[...], mix, mod, npost_ref, 1, 1.0)
    o_ref[...] = _ffn_sublayer(x, 2, mod, npre_ref, npost_ref, w1_ref, w3_ref, w2_ref)


def _rope_tile(x, cos, sin_up, sin_dn):
    up = pltpu.roll(x, LANES - ROPE_PAIR, axis=1)
    dn = pltpu.roll(x, ROPE_PAIR, axis=1)
    return x * cos + up * sin_up + dn * sin_dn


def _proj_kernel(*refs, tiles_per_seq, cache_seq):
    x_ref, mod_ref, npre_ref, win_ref, wup_ref, bup_ref = refs[:6]
    rope_refs = refs[6:9] if tiles_per_seq is not None else None
    outs = refs[6 + (3 if rope_refs else 0):]
    att_ref, glaf_ref, gv_ref = outs[:3]
    h = _modulated(x_ref[...], _Mod(mod_ref, tiles_per_seq), npre_ref, 1).astype(BF16)
    q = _dot(h, win_ref[:, OFF_Q:OFF_K])
    kv = _dot(h, win_ref[:, OFF_K:OFF_GQ])
    k, v = kv[:, :ATT_KV], kv[:, ATT_KV:]
    lr = _dot(h, win_ref[:, OFF_LR:IN_WIDTH])
    z = _dot(lr.astype(BF16), wup_ref[...]) + bup_ref[...]
    if rope_refs:
        cos, sup, sdn = (r[...] for r in rope_refs)
        q = jnp.concatenate([_rope_tile(q[:, j * LANES:(j + 1) * LANES], cos, sup, sdn)
                             for j in range(ATT_Q // LANES)], axis=1)
        k_att = _rope_tile(k, cos, sup, sdn)
    else:
        k_att = k
    att_ref[:, A_Q:A_K] = (q * ATT_Q_SCALE).astype(BF16)
    att_ref[:, A_K:A_KS] = k_att.astype(BF16)
    att_ref[:, A_KS:A_V] = pltpu.roll(k_att, HALF, axis=1).astype(BF16)
    att_ref[:, A_V:A_VS] = v.astype(BF16)
    att_ref[:, A_VS:ATT_IN_W] = pltpu.roll(v, HALF, axis=1).astype(BF16)
    if cache_seq is not None:
        for t, dst in ((k.T, outs[3]), (v.T, outs[4])):
            for b in range(TOKEN_TILE // cache_seq):
                dst[b, 0] = t[:, b * cache_seq:(b + 1) * cache_seq].reshape(
                    N_KV_HEADS, HEAD_DIM, cache_seq)
    glaf_ref[:, G_Q:G_G] = _dot(h, win_ref[:, OFF_GQ:OFF_GV])
    log_sig = jnp.minimum(z, 0.0) - jnp.log(1.0 + jnp.exp(-jnp.abs(z)))
    glaf_ref[:, G_LA:GLA_F_W] = log_sig * (1.0 / GLA_TAU)
    gv_ref[...] = _dot(h, win_ref[:, OFF_GV:OFF_GG]).astype(BF16)
    glaf_ref[:, G_G:G_LA] = _dot(h, win_ref[:, OFF_GG:OFF_LR])


def _token_spec(width, first_tile=0):
    return pl.BlockSpec((TOKEN_TILE, width), lambda i: (i + first_tile, 0))


def _stack_spec(shape, j):
    return pl.BlockSpec((None,) + tuple(shape[1:]), lambda i: (j, 0, 0),
                        pipeline_mode=pl.Buffered(1))


def _tiles_per_seq(latent_len):
    return None if latent_len is None else latent_len // TOKEN_TILE


def _ffn_first(x_ctx, x_lat, mod, latent_len, npre, npost, w1, w3, w2, cast):
    ctx_tiles = x_ctx.shape[0] // TOKEN_TILE
    steps = ctx_tiles + x_lat.shape[0] // TOKEN_TILE
    assert steps >= CAST_SLABS, "every slab needs its own grid step"
    in_specs = [pl.BlockSpec((TOKEN_TILE, D_MODEL), lambda i: (jnp.minimum(i, ctx_tiles - 1), 0)),
                pl.BlockSpec((TOKEN_TILE, D_MODEL), lambda i: (jnp.maximum(i - ctx_tiles, 0), 0)),
                _resident(mod.shape), _resident(npre.shape),
                _resident(npost.shape), _stack_spec(w1.shape, 0),
                _stack_spec(w3.shape, 0), _stack_spec(w2.shape, 0)]
    out_shape = [jax.ShapeDtypeStruct((steps * TOKEN_TILE, D_MODEL), F32)]
    out_specs = [_token_spec(D_MODEL)]
    for stack, j in cast:
        _, rows, cols = stack.shape
        slab = rows // CAST_SLABS
        assert slab * CAST_SLABS == rows and slab % (2 * SUBLANES) == 0, "whole bf16 row tiles"
        in_specs.append(pl.BlockSpec((None, slab, cols), functools.partial(
            lambda i, j: (j, i * CAST_SLABS // steps, 0), j=j)))
        out_shape.append(jax.ShapeDtypeStruct((1, rows, cols), BF16))
        out_specs.append(pl.BlockSpec((None, slab, cols), lambda i: (0, i * CAST_SLABS // steps, 0)))
    return pl.pallas_call(
        functools.partial(_ffn_first_kernel, ctx_tiles=ctx_tiles,
                          tiles_per_seq=_tiles_per_seq(latent_len), n_cast=len(cast)),
        out_shape=out_shape,
        grid=(steps,),
        in_specs=in_specs,
        out_specs=out_specs,
        compiler_params=_params(1),
        name="ffn_first",
    )(x_ctx, x_lat, mod, npre, npost, w1, w3, w2, *[stack for stack, _ in cast])


def _mix_ffn(x, first_tile, att, gla, w_out, mod, latent_len, npre, npost, w1, w3, w2):
    t = att.shape[0]
    return pl.pallas_call(
        functools.partial(_mix_ffn_kernel, tiles_per_seq=_tiles_per_seq(latent_len)),
        out_shape=jax.ShapeDtypeStruct((t, D_MODEL), F32),
        grid=(t // TOKEN_TILE,),
        in_specs=[_token_spec(D_MODEL, first_tile), _token_spec(ATT_Q), _token_spec(GLA_V),
                  _stack_spec(w_out.shape, 0), _resident(mod.shape), _resident(npre.shape),
                  _resident(npost.shape), _stack_spec(w1.shape, 0),
                  _stack_spec(w3.shape, 0), _stack_spec(w2.shape, 0)],
        out_specs=_token_spec(D_MODEL),
        compiler_params=_params(1),
        name="mix_ffn",
    )(x, att, gla, w_out, mod, npre, npost, w1, w3, w2)


def _project(x, first_tile, t, mod, latent_len, cache_seq, npre, w_in, w_up, b_up):
    tiles = _tiles_per_seq(latent_len)
    in_specs = [_token_spec(D_MODEL, first_tile), _resident(mod.shape), _resident(npre.shape),
                _stack_spec(w_in.shape, 0), _resident(w_up.shape), _resident(b_up.shape)]
    args = [x, mod, npre, w_in, w_up, b_up]
    if tiles is not None:
        in_specs += [pl.BlockSpec((TOKEN_TILE, LANES), lambda i: (i % tiles, 0))] * 3
        args += [jnp.asarray(tab) for tab in _rope_tables(latent_len)]
    outs = ((ATT_IN_W, BF16), (GLA_F_W, F32), (GLA_V, BF16))
    out_shape = [jax.ShapeDtypeStruct((t, w), dt) for w, dt in outs]
    out_specs = [_token_spec(w) for w, _ in outs]
    if cache_seq is not None:
        seqs = TOKEN_TILE // cache_seq
        cache = (t // cache_seq, 1, N_KV_HEADS, HEAD_DIM, cache_seq)
        out_shape += [jax.ShapeDtypeStruct(cache, F32)] * 2
        out_specs += [pl.BlockSpec((seqs,) + cache[1:], lambda i: (i, 0, 0, 0, 0))] * 2
    return pl.pallas_call(
        functools.partial(_proj_kernel, tiles_per_seq=tiles, cache_seq=cache_seq),
        out_shape=out_shape,
        grid=(t // TOKEN_TILE,),
        in_specs=in_specs,
        out_specs=out_specs,
        compiler_params=_params(1),
        name="project",
    )(*args)


def _rope_tables(seq_len):
    half = HEAD_DIM // 2
    inv_freq = np.float32(ROPE_BASE) ** (-np.arange(0, half, 2, dtype=np.float32) / half)
    pos = np.arange(seq_len)
    row = (pos // GRID_W).astype(np.float32)
    col = (pos % GRID_W).astype(np.float32)
    within = np.arange(LANES) % HEAD_DIM
    idx = within % half
    freq = inv_freq[idx % ROPE_PAIR].astype(np.float32)
    p = np.where((within // half == 0)[None, :], row[:, None], col[:, None])
    ang = (p * freq[None, :]).astype(np.float32)
    cos, sin = np.cos(ang).astype(np.float32), np.sin(ang).astype(np.float32)
    first = (idx < ROPE_PAIR)[None, :]
    zero = np.float32(0.0)
    return cos, np.where(first, -sin, zero), np.where(first, zero, sin)


def _attend(q_ref, q_rows, nq, segs, sink_ref):
    lane = lax.broadcasted_iota(jnp.int32, (1, LANES), 1)
    half_of = [lane < HALF, lane >= HALF]
    rows = lax.broadcasted_iota(jnp.int32, (2 * nq, 1), 0)
    zero = jnp.zeros((), BF16)
    out = [None] * (ATT_Q // LANES)
    for g in range(N_KV_HEADS):
        tiles = (2 * g, 2 * g + 1)
        for e in range(2):
            qm = jnp.concatenate(
                [jnp.where(half_of[e], q_ref[q_rows, j * LANES:(j + 1) * LANES], zero)
                 for j in tiles], axis=0)
            sink = jnp.where(rows < nq, sink_ref[2 * tiles[0] + e],
                             sink_ref[2 * tiles[1] + e]) * LOG2_E
            scores = []
            for k, k_sw, _, _, m in segs:
                s = _dot_t(qm, k if e == g else k_sw)
                scores.append(s if m is None else jnp.where(m, s, NEG))
            mx = sink
            for s in scores:
                mx = jnp.maximum(mx, jnp.max(s, axis=-1, keepdims=True))
            probs = [jnp.exp2(s - mx) for s in scores]
            den = jnp.exp2(sink - mx)
            for p in probs:
                den = den + jnp.sum(p, axis=-1, keepdims=True)
            o = None
            for p, (_, _, v, v_sw, _) in zip(probs, segs):
                part = _dot(p.astype(BF16), jnp.where(half_of[e], v if e == g else v_sw, zero))
                o = part if o is None else o + part
            o = o * (1.0 / den)
            for r, j in enumerate(tiles):
                blk = o[r * nq:(r + 1) * nq, :]
                out[j] = blk if out[j] is None else out[j] + blk
    return out


def _context_attn_kernel(sink_ref, a_ref, o_ref, *, seq_len):
    for s in range(a_ref.shape[0] // seq_len):
        rows = slice(s * seq_len, (s + 1) * seq_len)
        segs = [(a_ref[rows, A_K:A_KS], a_ref[rows, A_KS:A_V],
                 a_ref[rows, A_V:A_VS], a_ref[rows, A_VS:ATT_IN_W], None)]
        for j, t in enumerate(_attend(a_ref, rows, seq_len, segs, sink_ref)):
            o_ref[rows, j * LANES:(j + 1) * LANES] = t.astype(o_ref.dtype)


def _latent_attn_kernel(sink_ref, q_ref, kv_ref, kc_ref, vc_ref, o_ref, *, seq_len):
    k_ctx, v_ctx = kc_ref[0], vc_ref[0]
    ctx = (k_ctx.astype(BF16), pltpu.roll(k_ctx, HALF, axis=1).astype(BF16),
           v_ctx.astype(BF16), pltpu.roll(v_ctx, HALF, axis=1).astype(BF16), None)
    span = 3 * BLOCK
    per_step = q_ref.shape[0] // BLOCK
    for s in range(per_step):
        i = pl.program_id(1) * per_step + s
        start = pl.multiple_of(jnp.clip((i - 1) * BLOCK, 0, seq_len - span), BLOCK)
        kpos = start + lax.broadcasted_iota(jnp.int32, (1, span), 1)
        qpos = i * BLOCK + lax.broadcasted_iota(jnp.int32, (BLOCK, 1), 0)
        valid = jnp.abs(kpos - qpos) <= WINDOW
        valid2 = jnp.concatenate([valid, valid], axis=0)
        kv = kv_ref[pl.ds(start, span), :]
        segs = [tuple(kv[:, c * ATT_KV:(c + 1) * ATT_KV] for c in range(4)) + (valid2,), ctx]
        rows = slice(s * BLOCK, (s + 1) * BLOCK)
        for j, t in enumerate(_attend(q_ref, rows, BLOCK, segs, sink_ref)):
            o_ref[rows, j * LANES:(j + 1) * LANES] = t.astype(o_ref.dtype)


def _smem_spec():
    return pl.BlockSpec(memory_space=pltpu.SMEM)


def _context_attention(sink, att_in, batch, seq_len):
    rows = CTX_SEQS_PER_STEP * seq_len

    def seq(width):
        return pl.BlockSpec((rows, width), lambda b: (b, 0))
    return pl.pallas_call(
        functools.partial(_context_attn_kernel, seq_len=seq_len),
        out_shape=jax.ShapeDtypeStruct((batch * seq_len, ATT_Q), BF16),
        grid=(batch // CTX_SEQS_PER_STEP,),
        in_specs=[_smem_spec(), seq(ATT_IN_W)],
        out_specs=seq(ATT_Q),
        compiler_params=_params(1),
        name="context_attention",
    )(sink, att_in)


def _latent_attention(sink, att_in, k_ctx, v_ctx, batch, seq_len):
    steps = seq_len // (LAT_BLOCKS_PER_STEP * BLOCK)
    rows = LAT_BLOCKS_PER_STEP * BLOCK
    past = k_ctx.shape[1]
    assert A_K == ATT_IN_W - A_K, "q and the key / value columns are the two halves of att_in"
    return pl.pallas_call(
        functools.partial(_latent_attn_kernel, seq_len=seq_len),
        out_shape=jax.ShapeDtypeStruct((batch * seq_len, ATT_Q), BF16),
        grid=(batch, steps),
        in_specs=[_smem_spec(),
                  pl.BlockSpec((rows, ATT_Q), lambda b, i: (b * steps + i, 0)),
                  pl.BlockSpec((seq_len, ATT_IN_W - A_K), lambda b, i: (b, 1)),
                  pl.BlockSpec((1, past, ATT_KV), lambda b, i: (b, 0, 0)),
                  pl.BlockSpec((1, past, ATT_KV), lambda b, i: (b, 0, 0))],
        out_specs=pl.BlockSpec((rows, ATT_Q), lambda b, i: (b * steps + i, 0)),
        compiler_params=_params(2),
        name="latent_attention",
    )(sink, att_in, att_in, k_ctx, v_ctx)


def _split2(x):
    hi = x.astype(BF16)
    lo = (x - hi.astype(F32)).astype(BF16)
    return hi, lo


def _gla_kernel(*refs, n_chunks, has_s0, emit_state):
    refs = list(refs)
    gf_ref, gv_ref, gn_ref = refs[:3]
    pos = 3
    s0_refs = refs[pos:pos + 2] if has_s0 else None
    pos += 2 if has_s0 else 0
    o_ref = refs[pos]
    pos += 1
    sfin_refs = refs[pos:pos + 2] if emit_state else None
    pos += 2 if emit_state else 0
    cum_ref, kv_ref, sent_ref = refs[pos:]

    C = GLA_CHUNK
    n_pairs = GLA_QK // LANES
    lane = lax.broadcasted_iota(jnp.int32, (1, LANES), 1)
    half_of = [lane < HALF, lane >= HALF]
    r_i = lax.broadcasted_iota(jnp.int32, (C, C), 0)
    c_i = lax.broadcasted_iota(jnp.int32, (C, C), 1)
    lower = c_i <= r_i
    upper = c_i >= r_i
    tri = jnp.concatenate([jnp.where(lower, 1.0, 0.0), jnp.where(upper, 1.0, 0.0)],
                          axis=0).astype(BF16)
    zeros_cc = jnp.zeros((C, C), F32)
    gnorm = gn_ref[...]
    qscale = GLA_DK ** -0.5

    def one_sequence(s):
        def chunk_rows(n):
            return slice((s * n_chunks + n) * C, (s * n_chunks + n + 1) * C)

        for n in range(n_chunks):
            rows = chunk_rows(n)
            hi, lo = _split2(gf_ref[rows, G_LA:GLA_F_W])
            sums = _dot(tri, jnp.concatenate([hi, lo], axis=1))
            cum_f = sums[:C, 0:GLA_QK] + sums[:C, 2 * GLA_QK:3 * GLA_QK]
            cum_b = sums[C:, GLA_QK:2 * GLA_QK] + sums[C:, 3 * GLA_QK:]
            cum_ref[0, rows, :] = cum_f
            cum_ref[1, rows, :] = cum_b
            k = gf_ref[rows, G_K:G_G]
            k_in = (k * jnp.exp(cum_f[C - 1:C, :] - cum_f), k * jnp.exp(cum_b[0:1, :] - cum_b))
            for p in range(n_pairs):
                sl = slice(p * LANES, (p + 1) * LANES)
                kv_t = None
                for e in range(2):
                    h = 2 * p + e
                    v_t = gv_ref[rows, h * GLA_DV:(h + 1) * GLA_DV].T
                    k_e = jnp.concatenate(
                        [jnp.where(half_of[e], k_in[d][:, sl], 0.0) for d in range(2)], axis=1)
                    part = _dot(v_t, k_e.astype(BF16))
                    kv_t = part if kv_t is None else kv_t + part
                kv_ref[s, n, p] = kv_t

        for p in range(n_pairs):
            sl = slice(p * LANES, (p + 1) * LANES)
            st = []
            for d in range(2):
                if has_s0:
                    s0 = s0_refs[d][s, 0, 2 * p:2 * p + 2, :, :].reshape(2 * GLA_DK, GLA_DV)
                    st.append(s0.T)
                else:
                    st.append(jnp.zeros((GLA_DV, 2 * GLA_DK), F32))
            for i in range(n_chunks):
                for d, n in ((0, i), (1, n_chunks - 1 - i)):
                    tot_row = (s * n_chunks + n) * C + (C - 1 if d == 0 else 0)
                    decay = jnp.exp(cum_ref[d, tot_row:tot_row + 1, sl])
                    sent_ref[s, n, p, :, d * LANES:(d + 1) * LANES] = st[d].astype(BF16)
                    st[d] = decay * st[d] + kv_ref[s, n, p, :, d * LANES:(d + 1) * LANES]
            if emit_state:
                for d in range(2):
                    sfin_refs[d][s, 0, 2 * p:2 * p + 2, :, :] = st[d].T.reshape(2, GLA_DK, GLA_DV)

        for n in range(n_chunks):
            rows = chunk_rows(n)
            q = gf_ref[rows, G_Q:G_K] * qscale
            k = gf_ref[rows, G_K:G_G]
            qs, ks, qin = [], [], []
            for d in range(2):
                cum = cum_ref[d, rows, :]
                ref = cum[C // 2:C // 2 + 1, :]
                qs.append(q * jnp.exp(cum - ref))
                ks.append((k * jnp.exp(ref - cum)).astype(BF16))
                qin.append(q * jnp.exp(cum))
            for p in range(n_pairs):
                sl = slice(p * LANES, (p + 1) * LANES)
                lhs = jnp.concatenate(
                    [jnp.where(half_of[e], qs[d][:, sl], 0.0) for e in range(2) for d in range(2)],
                    axis=0).astype(BF16)
                sc = _dot_t(lhs, jnp.concatenate([ks[0][:, sl], ks[1][:, sl]], axis=0))
                prob = []
                for e in range(2):
                    s_f = sc[(2 * e) * C:(2 * e + 1) * C, :C]
                    s_b = sc[(2 * e + 1) * C:(2 * e + 2) * C, C:]
                    prob.append(jnp.where(lower, s_f, 0.0) + jnp.where(upper, s_b, 0.0))
                p_blk = jnp.concatenate(
                    [jnp.concatenate([prob[0], zeros_cc], axis=1),
                     jnp.concatenate([zeros_cc, prob[1]], axis=1)], axis=0).astype(BF16)
                v2 = jnp.concatenate(
                    [gv_ref[rows, (2 * p + e) * GLA_DV:(2 * p + e + 1) * GLA_DV] for e in range(2)],
                    axis=0)
                q_in = jnp.concatenate(
                    [jnp.concatenate([jnp.where(half_of[e], qin[d][:, sl], 0.0) for d in range(2)],
                                     axis=1) for e in range(2)], axis=0).astype(BF16)
                o2 = _dot(p_blk, v2) + _dot_t(q_in, sent_ref[s, n, p])
                for e in range(2):
                    h = 2 * p + e
                    o = o2[e * C:(e + 1) * C, :]
                    o = o * lax.rsqrt(jnp.mean(o * o, axis=-1, keepdims=True) + EPS) * gnorm
                    gate = gf_ref[rows, G_G + h * GLA_DV:G_G + (h + 1) * GLA_DV]
                    o_ref[rows, h * GLA_DV:(h + 1) * GLA_DV] = (o * _silu(gate)).astype(o_ref.dtype)

    for s in range(kv_ref.shape[0]):
        one_sequence(s)


def _gla(gla_f, gv, gnorm, batch, seq_len, seqs_per_step, s0=None, emit_state=False):
    n_chunks = seq_len // GLA_CHUNK
    has_s0 = s0 is not None
    rows = seqs_per_step * seq_len

    def seq(width):
        return pl.BlockSpec((rows, width), lambda b: (b, 0))
    state_spec = pl.BlockSpec((seqs_per_step, 1, GLA_HEADS, GLA_DK, GLA_DV),
                              lambda b: (b, 0, 0, 0, 0))
    in_specs = [seq(GLA_F_W), seq(GLA_V), pl.BlockSpec((1, GLA_DV), lambda b: (0, 0))]
    args = [gla_f, gv, gnorm]
    if has_s0:
        in_specs += [state_spec, state_spec]
        args += list(s0)
    out_shape = [jax.ShapeDtypeStruct((batch * seq_len, GLA_V), BF16)]
    out_specs = [seq(GLA_V)]
    if emit_state:
        out_shape += [jax.ShapeDtypeStruct((batch, 1, GLA_HEADS, GLA_DK, GLA_DV), F32)] * 2
        out_specs += [state_spec, state_spec]
    n_pairs = GLA_QK // LANES
    return pl.pallas_call(
        functools.partial(_gla_kernel, n_chunks=n_chunks, has_s0=has_s0, emit_state=emit_state),
        out_shape=out_shape,
        grid=(batch // seqs_per_step,),
        in_specs=in_specs,
        out_specs=out_specs,
        scratch_shapes=[pltpu.VMEM((2, rows, GLA_QK), F32),
                        pltpu.VMEM((seqs_per_step, n_chunks, n_pairs, GLA_DV, 2 * LANES), F32),
                        pltpu.VMEM((seqs_per_step, n_chunks, n_pairs, GLA_DV, 2 * LANES), BF16)],
        compiler_params=_params(1),
        name="gla",
    )(*args)


def kernel(x_prompt, x_sample, cache_k, cache_v, state_gla_fwd, state_gla_bwd, c, c_ctx,
           w_ada, b_ada, norm_pre, norm_post, ffn_w1, ffn_w3, ffn_w2, w_in,
           gla_w_up, gla_b_up, gla_norm, attn_sink, w_out):
    depth = w_in.shape[0]
    assert depth == 1, "single trunk layer"
    batch, seq = x_prompt.shape[0], x_prompt.shape[1]
    dec_batch, dec_seq = x_sample.shape[0], x_sample.shape[1]
    past = cache_k.shape[2]
    l = 0

    cond = jnp.concatenate(
        [c_ctx[None, :], c, jnp.zeros((MOD_ROWS - 1 - dec_batch, D_MODEL), F32)], axis=0)
    mod = _ada_modulation(cond, w_ada[l], b_ada[l][None, :])

    npre, npost = norm_pre[l], norm_post[l]
    ffn_first_b = [w[l, :1].astype(BF16) for w in (ffn_w1, ffn_w3, ffn_w2)]
    ffn_second_f32 = [(w[l], 1) for w in (ffn_w1, ffn_w3, ffn_w2)]
    w_in_b = w_in[l:l + 1].astype(BF16)
    w_out_b = w_out[l:l + 1].astype(BF16)
    zeros = jnp.zeros((GLA_LOW_RANK, GLA_QK), F32)
    w_up = jnp.concatenate(
        [jnp.concatenate([gla_w_up[l, 0], zeros], axis=1),
         jnp.concatenate([zeros, gla_w_up[l, 1]], axis=1)], axis=0).astype(BF16)
    b_up = gla_b_up[l].reshape(1, 2 * GLA_QK)
    gnorm = gla_norm[l][None, :]
    sink = attn_sink[l]

    x1, *ffn_second_b = _ffn_first(
        x_prompt.reshape(batch * seq, D_MODEL), x_sample.reshape(dec_batch * dec_seq, D_MODEL),
        mod, dec_seq, npre, npost, *ffn_first_b, cast=ffn_second_f32)
    ctx_tiles = batch * seq // TOKEN_TILE

    def trunk(latent):
        n_batch, n_seq = (dec_batch, dec_seq) if latent else (batch, seq)
        latent_len = n_seq if latent else None
        first_tile = ctx_tiles if latent else 0
        att_in, gla_f, gv, *cache_t = _project(
            x1, first_tile, n_batch * n_seq, mod, latent_len, None if latent else n_seq, npre,
            w_in_b, w_up, b_up)
        if latent:
            att = _latent_attention(sink, att_in, cache_k[:, l].reshape(dec_batch, past, ATT_KV),
                                    cache_v[:, l].reshape(dec_batch, past, ATT_KV), n_batch, n_seq)
            (gla,) = _gla(gla_f, gv, gnorm, n_batch, n_seq, 1,
                          s0=(state_gla_fwd[:, l:l + 1], state_gla_bwd[:, l:l + 1]))
            extras = ()
        else:
            att = _context_attention(sink, att_in, n_batch, n_seq)
            gla, s_f, s_b = _gla(gla_f, gv, gnorm, n_batch, n_seq, CTX_SEQS_PER_STEP,
                                 emit_state=True)
            k_new, v_new = (jnp.transpose(c_t, (0, 1, 4, 2, 3)) for c_t in cache_t)
            extras = (k_new, v_new, s_f, s_b)
        y = _mix_ffn(x1, first_tile, att, gla, w_out_b, mod, latent_len, npre, npost,
                     *ffn_second_b)
        return y.reshape(n_batch, n_seq, D_MODEL), extras

    y_prompt, (k_new, v_new, s_f, s_b) = trunk(False)
    y_sample, _ = trunk(True)
    return (y_prompt, y_sample, k_new, v_new, s_f, s_b)
```

```python
import functools

import numpy as np
import jax
import jax.numpy as jnp
from jax import lax
from jax.experimental import pallas as pl
from jax.experimental.pallas import tpu as pltpu

F32 = jnp.float32
BF16 = jnp.bfloat16

D_MODEL = 1024
GRID_W = 64
N_Q_HEADS = 8
N_KV_HEADS = 2
HEAD_DIM = 64
WINDOW = 128
BLOCK = 128
ROPE_BASE = 10000.0
GLA_HEADS = 4
GLA_DK = 64
GLA_DV = 128
GLA_LOW_RANK = 16
GLA_TAU = 16.0
D_FF = 2816
N_MOD = 9
EPS = 1e-6
NEG = -1e30

ATT_Q = N_Q_HEADS * HEAD_DIM
ATT_KV = N_KV_HEADS * HEAD_DIM
GLA_QK = GLA_HEADS * GLA_DK
GLA_V = GLA_HEADS * GLA_DV
OFF_Q = 0
OFF_K = OFF_Q + ATT_Q
OFF_V = OFF_K + ATT_KV
OFF_GQ = OFF_V + ATT_KV
OFF_GK = OFF_GQ + GLA_QK
OFF_GV = OFF_GK + GLA_QK
OFF_GG = OFF_GV + GLA_V
OFF_LR = OFF_GG + GLA_V
IN_WIDTH = OFF_LR + 2 * GLA_LOW_RANK
A_Q = 0
A_K = A_Q + ATT_Q
A_KS = A_K + ATT_KV
A_V = A_KS + ATT_KV
A_VS = A_V + ATT_KV
ATT_IN_W = A_VS + ATT_KV
G_Q = 0
G_K = G_Q + GLA_QK
G_G = G_K + GLA_QK
G_LA = G_G + GLA_V
GLA_F_W = G_LA + 2 * GLA_QK
ROPE_PAIR = HEAD_DIM // 4
LOG2_E = 1.4426950408889634
ATT_Q_SCALE = HEAD_DIM ** -0.5 * LOG2_E

LANES = 128
SUBLANES = 8
HALF = LANES // 2
VMEM_LIMIT = 56 * 1024 * 1024

TOKEN_TILE = 512
FF_SPLITS = (0, 1536, D_FF)
EARLY_MODS = 3
CAST_SLABS = 16
ADA_K_TILE = 128
CTX_SEQS_PER_STEP = 4
LAT_BLOCKS_PER_STEP = 4
GLA_CHUNK = 128
MOD_ROWS = 8


def _params(n_axes):
    return pltpu.CompilerParams(
        dimension_semantics=("arbitrary",) * n_axes, vmem_limit_bytes=VMEM_LIMIT)


def _resident(shape):
    zeros = (0,) * len(shape)
    return pl.BlockSpec(shape, lambda *_: zeros, pipeline_mode=pl.Buffered(1))


def _sigmoid(x):
    return 1.0 / (1.0 + jnp.exp(-x))


def _silu(x):
    return x * _sigmoid(x)


def _rms(x, g):
    return x * lax.rsqrt(jnp.mean(x * x, axis=-1, keepdims=True) + EPS) * g


def _dot(a, b):
    return jnp.dot(a, b, preferred_element_type=F32)


def _dot_t(a, b):
    return lax.dot_general(a, b, (((1,), (1,)), ((), ())), preferred_element_type=F32)


def _ada_kernel(cond_ref, w_ref, b_ref, o_ref):
    n_mod = o_ref.shape[0]

    @pl.when(pl.program_id(0) == 0)
    def _():
        for m in range(n_mod):
            o_ref[m] = jnp.broadcast_to(b_ref[:, m * D_MODEL:(m + 1) * D_MODEL],
                                        (MOD_ROWS, D_MODEL))
    part = _dot(_silu(cond_ref[...]).astype(BF16), w_ref[...].astype(BF16))
    for m in range(n_mod):
        o_ref[m] += part[:, m * D_MODEL:(m + 1) * D_MODEL]


def _ada_modulation(cond, w_ada, b_ada, n_mod):
    k_dim = w_ada.shape[0]
    n = n_mod * D_MODEL
    return pl.pallas_call(
        _ada_kernel,
        out_shape=jax.ShapeDtypeStruct((n_mod, MOD_ROWS, D_MODEL), F32),
        grid=(k_dim // ADA_K_TILE,),
        in_specs=[
            pl.BlockSpec((MOD_ROWS, ADA_K_TILE), lambda k: (0, k)),
            pl.BlockSpec((ADA_K_TILE, n), lambda k: (k, 0)),
            pl.BlockSpec((1, n), lambda k: (0, 0)),
        ],
        out_specs=pl.BlockSpec((n_mod, MOD_ROWS, D_MODEL), lambda k: (0, 0, 0)),
        compiler_params=_params(1),
        name="ada_modulation",
    )(cond, w_ada, b_ada)


class _Mod:
    def __init__(self, mod_ref, tiles_per_seq, first_latent_tile=0, first_mod=0):
        self.ref = mod_ref
        self.first_mod = first_mod
        if tiles_per_seq is None:
            self.row = 0
        else:
            tile = pl.program_id(0) - first_latent_tile
            self.row = jnp.where(tile >= 0, 1 + tile // tiles_per_seq, 0)

    def __getitem__(self, m):
        return self.ref[m - self.first_mod, pl.ds(self.row, 1), :]


def _modulated(x, mod, npre_ref, i):
    return _rms(x, npre_ref[i:i + 1, :]) * (1.0 + mod[3 * i + 1]) + mod[3 * i]


def _residual(x, out, mod, npost_ref, i, weight):
    return x + (weight * mod[3 * i + 2]) * _rms(out, npost_ref[i:i + 1, :])


def _ffn_sublayer(x, i, mod, npre_ref, npost_ref, w1_ref, w3_ref, w2_ref):
    h = _modulated(x, mod, npre_ref, i).astype(BF16)
    acc = None
    for lo, hi in zip(FF_SPLITS[:-1], FF_SPLITS[1:]):
        a = _dot(h, w1_ref[:, lo:hi])
        g = _dot(h, w3_ref[:, lo:hi])
        part = _dot((_silu(a) * g).astype(BF16), w2_ref[lo:hi, :])
        acc = part if acc is None else acc + part
    return _residual(x, acc, mod, npost_ref, i, 0.5)


def _ffn_first_kernel(*refs, ctx_tiles, tiles_per_seq, n_cast):
    xc_ref, xl_ref, mod_ref, npre_ref, npost_ref, w1_ref, w3_ref, w2_ref = refs[:8]
    cond_ref, wada_ref, bada_ref = refs[8:11]
    cast_src = refs[11:11 + n_cast]
    o_ref, late_mod_ref = refs[11 + n_cast:13 + n_cast]
    cast_dst = refs[13 + n_cast:]
    mod = _Mod(mod_ref, tiles_per_seq, ctx_tiles)
    x = jnp.where(pl.program_id(0) >= ctx_tiles, xl_ref[...], xc_ref[...])
    o_ref[...] = _ffn_sublayer(x, 0, mod, npre_ref, npost_ref, w1_ref, w3_ref, w2_ref)
    for src, dst in zip(cast_src, cast_dst):
        dst[...] = src[...].astype(BF16)
    late_mod_ref[...] = _dot(_silu(cond_ref[...]).astype(BF16),
                             wada_ref[...].astype(BF16)) + bada_ref[...]


def _mix_ffn_kernel(x_ref, att_ref, gla_ref, wo_ref, mod_ref, npre_ref, npost_ref,
                    w1_ref, w3_ref, w2_ref, o_ref, *, tiles_per_seq):
    mod = _Mod(mod_ref, tiles_per_seq, first_mod=EARLY_MODS)
    mix = _dot(att_ref[...], wo_ref[0:ATT_Q, :]) + _dot(gla_ref[...], wo_ref[ATT_Q:, :])
    x = _residual(x_ref[...], mix, mod, npost_ref, 1, 1.0)
    o_ref[...] = _ffn_sublayer(x, 2, mod, npre_ref, npost_ref, w1_ref, w3_ref, w2_ref)


def _rope_tile(x, cos, sin_up, sin_dn):
    up = pltpu.roll(x, LANES - ROPE_PAIR, axis=1)
    dn = pltpu.roll(x, ROPE_PAIR, axis=1)
    return x * cos + up * sin_up + dn * sin_dn


def _proj_kernel(*refs, tiles_per_seq, cache_seq):
    x_ref, mod_ref, npre_ref, win_ref, wup_ref, bup_ref = refs[:6]
    rope_refs = refs[6:9] if tiles_per_seq is not None else None
    outs = refs[6 + (3 if rope_refs else 0):]
    att_ref, glaf_ref, gv_ref = outs[:3]
    mod = _Mod(mod_ref, tiles_per_seq, first_mod=EARLY_MODS)
    h = _modulated(x_ref[...], mod, npre_ref, 1).astype(BF16)
    q = _dot(h, win_ref[:, OFF_Q:OFF_K])
    kv = _dot(h, win_ref[:, OFF_K:OFF_GQ])
    k, v = kv[:, :ATT_KV], kv[:, ATT_KV:]
    lr = _dot(h, win_ref[:, OFF_LR:IN_WIDTH])
    z = _dot(lr.astype(BF16), wup_ref[...]) + bup_ref[...]
    if rope_refs:
        cos, sup, sdn = (r[...] for r in rope_refs)
        q = jnp.concatenate([_rope_tile(q[:, j * LANES:(j + 1) * LANES], cos, sup, sdn)
                             for j in range(ATT_Q // LANES)], axis=1)
        k_att = _rope_tile(k, cos, sup, sdn)
    else:
        k_att = k
    att_ref[:, A_Q:A_K] = (q * ATT_Q_SCALE).astype(BF16)
    att_ref[:, A_K:A_KS] = k_att.astype(BF16)
    att_ref[:, A_KS:A_V] = pltpu.roll(k_att, HALF, axis=1).astype(BF16)
    att_ref[:, A_V:A_VS] = v.astype(BF16)
    att_ref[:, A_VS:ATT_IN_W] = pltpu.roll(v, HALF, axis=1).astype(BF16)
    if cache_seq is not None:
        for t, dst in ((k.T, outs[3]), (v.T, outs[4])):
            for b in range(TOKEN_TILE // cache_seq):
                dst[b, 0] = t[:, b * cache_seq:(b + 1) * cache_seq].reshape(
                    N_KV_HEADS, HEAD_DIM, cache_seq)
    glaf_ref[:, G_Q:G_G] = _dot(h, win_ref[:, OFF_GQ:OFF_GV])
    log_sig = jnp.minimum(z, 0.0) - jnp.log(1.0 + jnp.exp(-jnp.abs(z)))
    glaf_ref[:, G_LA:GLA_F_W] = log_sig * (1.0 / GLA_TAU)
    gv_ref[...] = _dot(h, win_ref[:, OFF_GV:OFF_GG]).astype(BF16)
    glaf_ref[:, G_G:G_LA] = _dot(h, win_ref[:, OFF_GG:OFF_LR])


def _token_spec(width, first_tile=0):
    return pl.BlockSpec((TOKEN_TILE, width), lambda i: (i + first_tile, 0))


def _stack_spec(shape, j):
    return pl.BlockSpec((None,) + tuple(shape[1:]), lambda i: (j, 0, 0),
                        pipeline_mode=pl.Buffered(1))


def _tiles_per_seq(latent_len):
    return None if latent_len is None else latent_len // TOKEN_TILE


def _ffn_first(x_ctx, x_lat, mod, latent_len, npre, npost, w1, w3, w2, ada, cast):
    ctx_tiles = x_ctx.shape[0] // TOKEN_TILE
    steps = ctx_tiles + x_lat.shape[0] // TOKEN_TILE
    assert steps >= CAST_SLABS, "every slab needs its own grid step"
    cond, w_ada, b_ada = ada
    late_mods = N_MOD - EARLY_MODS
    ada_cols = late_mods * D_MODEL // steps
    per_mod = D_MODEL // ada_cols
    assert ada_cols * steps == late_mods * D_MODEL and ada_cols % LANES == 0
    first_slab = EARLY_MODS * per_mod
    in_specs = [pl.BlockSpec((TOKEN_TILE, D_MODEL), lambda i: (jnp.minimum(i, ctx_tiles - 1), 0)),
                pl.BlockSpec((TOKEN_TILE, D_MODEL), lambda i: (jnp.maximum(i - ctx_tiles, 0), 0)),
                _resident(mod.shape), _resident(npre.shape),
                _resident(npost.shape), _stack_spec(w1.shape, 0),
                _stack_spec(w3.shape, 0), _stack_spec(w2.shape, 0),
                _resident(cond.shape),
                pl.BlockSpec((D_MODEL, ada_cols), lambda i: (0, first_slab + i)),
                pl.BlockSpec((1, ada_cols), lambda i: (0, first_slab + i))]
    out_shape = [jax.ShapeDtypeStruct((steps * TOKEN_TILE, D_MODEL), F32),
                 jax.ShapeDtypeStruct((late_mods, MOD_ROWS, D_MODEL), F32)]
    out_specs = [_token_spec(D_MODEL),
                 pl.BlockSpec((None, MOD_ROWS, ada_cols), lambda i: (i // per_mod, 0, i % per_mod))]
    for stack, j in cast:
        _, rows, cols = stack.shape
        slab = rows // CAST_SLABS
        assert slab * CAST_SLABS == rows and slab % (2 * SUBLANES) == 0, "whole bf16 row tiles"
        in_specs.append(pl.BlockSpec((None, slab, cols), functools.partial(
            lambda i, j: (j, i * CAST_SLABS // steps, 0), j=j)))
        out_shape.append(jax.ShapeDtypeStruct((1, rows, cols), BF16))
        out_specs.append(pl.BlockSpec((None, slab, cols), lambda i: (0, i * CAST_SLABS // steps, 0)))
    return pl.pallas_call(
        functools.partial(_ffn_first_kernel, ctx_tiles=ctx_tiles,
                          tiles_per_seq=_tiles_per_seq(latent_len), n_cast=len(cast)),
        out_shape=out_shape,
        grid=(steps,),
        in_specs=in_specs,
        out_specs=out_specs,
        compiler_params=_params(1),
        name="ffn_first",
    )(x_ctx, x_lat, mod, npre, npost, w1, w3, w2, cond, w_ada, b_ada,
      *[stack for stack, _ in cast])


def _mix_ffn(x, first_tile, att, gla, w_out, mod, latent_len, npre, npost, w1, w3, w2):
    t = att.shape[0]
    return pl.pallas_call(
        functools.partial(_mix_ffn_kernel, tiles_per_seq=_tiles_per_seq(latent_len)),
        out_shape=jax.ShapeDtypeStruct((t, D_MODEL), F32),
        grid=(t // TOKEN_TILE,),
        in_specs=[_token_spec(D_MODEL, first_tile), _token_spec(ATT_Q), _token_spec(GLA_V),
                  _stack_spec(w_out.shape, 0), _resident(mod.shape), _resident(npre.shape),
                  _resident(npost.shape), _stack_spec(w1.shape, 0),
                  _stack_spec(w3.shape, 0), _stack_spec(w2.shape, 0)],
        out_specs=_token_spec(D_MODEL),
        compiler_params=_params(1),
        name="mix_ffn",
    )(x, att, gla, w_out, mod, npre, npost, w1, w3, w2)


def _project(x, first_tile, t, mod, latent_len, cache_seq, npre, w_in, w_up, b_up):
    tiles = _tiles_per_seq(latent_len)
    in_specs = [_token_spec(D_MODEL, first_tile), _resident(mod.shape), _resident(npre.shape),
                _stack_spec(w_in.shape, 0), _resident(w_up.shape), _resident(b_up.shape)]
    args = [x, mod, npre, w_in, w_up, b_up]
    if tiles is not None:
        in_specs += [pl.BlockSpec((TOKEN_TILE, LANES), lambda i: (i % tiles, 0))] * 3
        args += [jnp.asarray(tab) for tab in _rope_tables(latent_len)]
    outs = ((ATT_IN_W, BF16), (GLA_F_W, F32), (GLA_V, BF16))
    out_shape = [jax.ShapeDtypeStruct((t, w), dt) for w, dt in outs]
    out_specs = [_token_spec(w) for w, _ in outs]
    if cache_seq is not None:
        seqs = TOKEN_TILE // cache_seq
        cache = (t // cache_seq, 1, N_KV_HEADS, HEAD_DIM, cache_seq)
        out_shape += [jax.ShapeDtypeStruct(cache, F32)] * 2
        out_specs += [pl.BlockSpec((seqs,) + cache[1:], lambda i: (i, 0, 0, 0, 0))] * 2
    return pl.pallas_call(
        functools.partial(_proj_kernel, tiles_per_seq=tiles, cache_seq=cache_seq),
        out_shape=out_shape,
        grid=(t // TOKEN_TILE,),
        in_specs=in_specs,
        out_specs=out_specs,
        compiler_params=_params(1),
        name="project",
    )(*args)


def _rope_tables(seq_len):
    half = HEAD_DIM // 2
    inv_freq = np.float32(ROPE_BASE) ** (-np.arange(0, half, 2, dtype=np.float32) / half)
    pos = np.arange(seq_len)
    row = (pos // GRID_W).astype(np.float32)
    col = (pos % GRID_W).astype(np.float32)
    within = np.arange(LANES) % HEAD_DIM
    idx = within % half
    freq = inv_freq[idx % ROPE_PAIR].astype(np.float32)
    p = np.where((within // half == 0)[None, :], row[:, None], col[:, None])
    ang = (p * freq[None, :]).astype(np.float32)
    cos, sin = np.cos(ang).astype(np.float32), np.sin(ang).astype(np.float32)
    first = (idx < ROPE_PAIR)[None, :]
    zero = np.float32(0.0)
    return cos, np.where(first, -sin, zero), np.where(first, zero, sin)


def _attend(q_ref, q_rows, nq, segs, sink_ref):
    lane = lax.broadcasted_iota(jnp.int32, (1, LANES), 1)
    half_of = [lane < HALF, lane >= HALF]
    rows = lax.broadcasted_iota(jnp.int32, (2 * nq, 1), 0)
    zero = jnp.zeros((), BF16)
    out = [None] * (ATT_Q // LANES)
    for g in range(N_KV_HEADS):
        tiles = (2 * g, 2 * g + 1)
        for e in range(2):
            qm = jnp.concatenate(
                [jnp.where(half_of[e], q_ref[q_rows, j * LANES:(j + 1) * LANES], zero)
                 for j in tiles], axis=0)
            sink = jnp.where(rows < nq, sink_ref[2 * tiles[0] + e],
                             sink_ref[2 * tiles[1] + e]) * LOG2_E
            scores = []
            for k, k_sw, _, _, m in segs:
                s = _dot_t(qm, k if e == g else k_sw)
                scores.append(s if m is None else jnp.where(m, s, NEG))
            mx = sink
            for s in scores:
                mx = jnp.maximum(mx, jnp.max(s, axis=-1, keepdims=True))
            probs = [jnp.exp2(s - mx) for s in scores]
            den = jnp.exp2(sink - mx)
            for p in probs:
                den = den + jnp.sum(p, axis=-1, keepdims=True)
            o = None
            for p, (_, _, v, v_sw, _) in zip(probs, segs):
                part = _dot(p.astype(BF16), jnp.where(half_of[e], v if e == g else v_sw, zero))
                o = part if o is None else o + part
            o = o * (1.0 / den)
            for r, j in enumerate(tiles):
                blk = o[r * nq:(r + 1) * nq, :]
                out[j] = blk if out[j] is None else out[j] + blk
    return out


def _context_attn_kernel(sink_ref, a_ref, o_ref, *, seq_len):
    for s in range(a_ref.shape[0] // seq_len):
        rows = slice(s * seq_len, (s + 1) * seq_len)
        segs = [(a_ref[rows, A_K:A_KS], a_ref[rows, A_KS:A_V],
                 a_ref[rows, A_V:A_VS], a_ref[rows, A_VS:ATT_IN_W], None)]
        for j, t in enumerate(_attend(a_ref, rows, seq_len, segs, sink_ref)):
            o_ref[rows, j * LANES:(j + 1) * LANES] = t.astype(o_ref.dtype)


def _latent_attn_kernel(sink_ref, q_ref, kv_ref, kc_ref, vc_ref, o_ref, *, seq_len):
    k_ctx, v_ctx = kc_ref[0], vc_ref[0]
    ctx = (k_ctx.astype(BF16), pltpu.roll(k_ctx, HALF, axis=1).astype(BF16),
           v_ctx.astype(BF16), pltpu.roll(v_ctx, HALF, axis=1).astype(BF16), None)
    span = 3 * BLOCK
    per_step = q_ref.shape[0] // BLOCK
    for s in range(per_step):
        i = pl.program_id(1) * per_step + s
        start = pl.multiple_of(jnp.clip((i - 1) * BLOCK, 0, seq_len - span), BLOCK)
        kpos = start + lax.broadcasted_iota(jnp.int32, (1, span), 1)
        qpos = i * BLOCK + lax.broadcasted_iota(jnp.int32, (BLOCK, 1), 0)
        valid = jnp.abs(kpos - qpos) <= WINDOW
        valid2 = jnp.concatenate([valid, valid], axis=0)
        kv = kv_ref[pl.ds(start, span), :]
        segs = [tuple(kv[:, c * ATT_KV:(c + 1) * ATT_KV] for c in range(4)) + (valid2,), ctx]
        rows = slice(s * BLOCK, (s + 1) * BLOCK)
        for j, t in enumerate(_attend(q_ref, rows, BLOCK, segs, sink_ref)):
            o_ref[rows, j * LANES:(j + 1) * LANES] = t.astype(o_ref.dtype)


def _smem_spec():
    return pl.BlockSpec(memory_space=pltpu.SMEM)


def _context_attention(sink, att_in, batch, seq_len):
    rows = CTX_SEQS_PER_STEP * seq_len

    def seq(width):
        return pl.BlockSpec((rows, width), lambda b: (b, 0))
    return pl.pallas_call(
        functools.partial(_context_attn_kernel, seq_len=seq_len),
        out_shape=jax.ShapeDtypeStruct((batch * seq_len, ATT_Q), BF16),
        grid=(batch // CTX_SEQS_PER_STEP,),
        in_specs=[_smem_spec(), seq(ATT_IN_W)],
        out_specs=seq(ATT_Q),
        compiler_params=_params(1),
        name="context_attention",
    )(sink, att_in)


def _latent_attention(sink, att_in, k_ctx, v_ctx, batch, seq_len):
    steps = seq_len // (LAT_BLOCKS_PER_STEP * BLOCK)
    rows = LAT_BLOCKS_PER_STEP * BLOCK
    past = k_ctx.shape[1]
    assert A_K == ATT_IN_W - A_K, "q and the key / value columns are the two halves of att_in"
    return pl.pallas_call(
        functools.partial(_latent_attn_kernel, seq_len=seq_len),
        out_shape=jax.ShapeDtypeStruct((batch * seq_len, ATT_Q), BF16),
        grid=(batch, steps),
        in_specs=[_smem_spec(),
                  pl.BlockSpec((rows, ATT_Q), lambda b, i: (b * steps + i, 0)),
                  pl.BlockSpec((seq_len, ATT_IN_W - A_K), lambda b, i: (b, 1)),
                  pl.BlockSpec((1, past, ATT_KV), lambda b, i: (b, 0, 0)),
                  pl.BlockSpec((1, past, ATT_KV), lambda b, i: (b, 0, 0))],
        out_specs=pl.BlockSpec((rows, ATT_Q), lambda b, i: (b * steps + i, 0)),
        compiler_params=_params(2),
        name="latent_attention",
    )(sink, att_in, att_in, k_ctx, v_ctx)


def _split2(x):
    hi = x.astype(BF16)
    lo = (x - hi.astype(F32)).astype(BF16)
    return hi, lo


def _gla_kernel(*refs, n_chunks, has_s0, emit_state):
    refs = list(refs)
    gf_ref, gv_ref, gn_ref = refs[:3]
    pos = 3
    s0_refs = refs[pos:pos + 2] if has_s0 else None
    pos += 2 if has_s0 else 0
    o_ref = refs[pos]
    pos += 1
    sfin_refs = refs[pos:pos + 2] if emit_state else None
    pos += 2 if emit_state else 0
    cum_ref, kv_ref, sent_ref = refs[pos:]

    C = GLA_CHUNK
    n_pairs = GLA_QK // LANES
    lane = lax.broadcasted_iota(jnp.int32, (1, LANES), 1)
    half_of = [lane < HALF, lane >= HALF]
    r_i = lax.broadcasted_iota(jnp.int32, (C, C), 0)
    c_i = lax.broadcasted_iota(jnp.int32, (C, C), 1)
    lower = c_i <= r_i
    upper = c_i >= r_i
    tri = jnp.concatenate([jnp.where(lower, 1.0, 0.0), jnp.where(upper, 1.0, 0.0)],
                          axis=0).astype(BF16)
    zeros_cc = jnp.zeros((C, C), F32)
    gnorm = gn_ref[...]
    qscale = GLA_DK ** -0.5

    def one_sequence(s):
        def chunk_rows(n):
            return slice((s * n_chunks + n) * C, (s * n_chunks + n + 1) * C)

        for n in range(n_chunks):
            rows = chunk_rows(n)
            hi, lo = _split2(gf_ref[rows, G_LA:GLA_F_W])
            sums = _dot(tri, jnp.concatenate([hi, lo], axis=1))
            cum_f = sums[:C, 0:GLA_QK] + sums[:C, 2 * GLA_QK:3 * GLA_QK]
            cum_b = sums[C:, GLA_QK:2 * GLA_QK] + sums[C:, 3 * GLA_QK:]
            cum_ref[0, rows, :] = cum_f
            cum_ref[1, rows, :] = cum_b
            k = gf_ref[rows, G_K:G_G]
            k_in = (k * jnp.exp(cum_f[C - 1:C, :] - cum_f), k * jnp.exp(cum_b[0:1, :] - cum_b))
            for p in range(n_pairs):
                sl = slice(p * LANES, (p + 1) * LANES)
                kv_t = None
                for e in range(2):
                    h = 2 * p + e
                    v_t = gv_ref[rows, h * GLA_DV:(h + 1) * GLA_DV].T
                    k_e = jnp.concatenate(
                        [jnp.where(half_of[e], k_in[d][:, sl], 0.0) for d in range(2)], axis=1)
                    part = _dot(v_t, k_e.astype(BF16))
                    kv_t = part if kv_t is None else kv_t + part
                kv_ref[s, n, p] = kv_t

        for p in range(n_pairs):
            sl = slice(p * LANES, (p + 1) * LANES)
            st = []
            for d in range(2):
                if has_s0:
                    s0 = s0_refs[d][s, 0, 2 * p:2 * p + 2, :, :].reshape(2 * GLA_DK, GLA_DV)
                    st.append(s0.T)
                else:
                    st.append(jnp.zeros((GLA_DV, 2 * GLA_DK), F32))
            for i in range(n_chunks):
                for d, n in ((0, i), (1, n_chunks - 1 - i)):
                    tot_row = (s * n_chunks + n) * C + (C - 1 if d == 0 else 0)
                    decay = jnp.exp(cum_ref[d, tot_row:tot_row + 1, sl])
                    sent_ref[s, n, p, :, d * LANES:(d + 1) * LANES] = st[d].astype(BF16)
                    st[d] = decay * st[d] + kv_ref[s, n, p, :, d * LANES:(d + 1) * LANES]
            if emit_state:
                for d in range(2):
                    sfin_refs[d][s, 0, 2 * p:2 * p + 2, :, :] = st[d].T.reshape(2, GLA_DK, GLA_DV)

        for n in range(n_chunks):
            rows = chunk_rows(n)
            q = gf_ref[rows, G_Q:G_K] * qscale
            k = gf_ref[rows, G_K:G_G]
            qs, ks, qin = [], [], []
            for d in range(2):
                cum = cum_ref[d, rows, :]
                ref = cum[C // 2:C // 2 + 1, :]
                qs.append(q * jnp.exp(cum - ref))
                ks.append((k * jnp.exp(ref - cum)).astype(BF16))
                qin.append(q * jnp.exp(cum))
            for p in range(n_pairs):
                sl = slice(p * LANES, (p + 1) * LANES)
                lhs = jnp.concatenate(
                    [jnp.where(half_of[e], qs[d][:, sl], 0.0) for e in range(2) for d in range(2)],
                    axis=0).astype(BF16)
                sc = _dot_t(lhs, jnp.concatenate([ks[0][:, sl], ks[1][:, sl]], axis=0))
                prob = []
                for e in range(2):
                    s_f = sc[(2 * e) * C:(2 * e + 1) * C, :C]
                    s_b = sc[(2 * e + 1) * C:(2 * e + 2) * C, C:]
                    prob.append(jnp.where(lower, s_f, 0.0) + jnp.where(upper, s_b, 0.0))
                p_blk = jnp.concatenate(
                    [jnp.concatenate([prob[0], zeros_cc], axis=1),
                     jnp.concatenate([zeros_cc, prob[1]], axis=1)], axis=0).astype(BF16)
                v2 = jnp.concatenate(
                    [gv_ref[rows, (2 * p + e) * GLA_DV:(2 * p + e + 1) * GLA_DV] for e in range(2)],
                    axis=0)
                q_in = jnp.concatenate(
                    [jnp.concatenate([jnp.where(half_of[e], qin[d][:, sl], 0.0) for d in range(2)],
                                     axis=1) for e in range(2)], axis=0).astype(BF16)
                o2 = _dot(p_blk, v2) + _dot_t(q_in, sent_ref[s, n, p])
                for e in range(2):
                    h = 2 * p + e
                    o = o2[e * C:(e + 1) * C, :]
                    o = o * lax.rsqrt(jnp.mean(o * o, axis=-1, keepdims=True) + EPS) * gnorm
                    gate = gf_ref[rows, G_G + h * GLA_DV:G_G + (h + 1) * GLA_DV]
                    o_ref[rows, h * GLA_DV:(h + 1) * GLA_DV] = (o * _silu(gate)).astype(o_ref.dtype)

    for s in range(kv_ref.shape[0]):
        one_sequence(s)


def _gla(gla_f, gv, gnorm, batch, seq_len, seqs_per_step, s0=None, emit_state=False):
    n_chunks = seq_len // GLA_CHUNK
    has_s0 = s0 is not None
    rows = seqs_per_step * seq_len

    def seq(width):
        return pl.BlockSpec((rows, width), lambda b: (b, 0))
    state_spec = pl.BlockSpec((seqs_per_step, 1, GLA_HEADS, GLA_DK, GLA_DV),
                              lambda b: (b, 0, 0, 0, 0))
    in_specs = [seq(GLA_F_W), seq(GLA_V), pl.BlockSpec((1, GLA_DV), lambda b: (0, 0))]
    args = [gla_f, gv, gnorm]
    if has_s0:
        in_specs += [state_spec, state_spec]
        args += list(s0)
    out_shape = [jax.ShapeDtypeStruct((batch * seq_len, GLA_V), BF16)]
    out_specs = [seq(GLA_V)]
    if emit_state:
        out_shape += [jax.ShapeDtypeStruct((batch, 1, GLA_HEADS, GLA_DK, GLA_DV), F32)] * 2
        out_specs += [state_spec, state_spec]
    n_pairs = GLA_QK // LANES
    return pl.pallas_call(
        functools.partial(_gla_kernel, n_chunks=n_chunks, has_s0=has_s0, emit_state=emit_state),
        out_shape=out_shape,
        grid=(batch // seqs_per_step,),
        in_specs=in_specs,
        out_specs=out_specs,
        scratch_shapes=[pltpu.VMEM((2, rows, GLA_QK), F32),
                        pltpu.VMEM((seqs_per_step, n_chunks, n_pairs, GLA_DV, 2 * LANES), F32),
                        pltpu.VMEM((seqs_per_step, n_chunks, n_pairs, GLA_DV, 2 * LANES), BF16)],
        compiler_params=_params(1),
        name="gla",
    )(*args)


def kernel(x_prompt, x_sample, cache_k, cache_v, state_gla_fwd, state_gla_bwd, c, c_ctx,
           w_ada, b_ada, norm_pre, norm_post, ffn_w1, ffn_w3, ffn_w2, w_in,
           gla_w_up, gla_b_up, gla_norm, attn_sink, w_out):
    depth = w_in.shape[0]
    assert depth == 1, "single trunk layer"
    batch, seq = x_prompt.shape[0], x_prompt.shape[1]
    dec_batch, dec_seq = x_sample.shape[0], x_sample.shape[1]
    past = cache_k.shape[2]
    l = 0

    cond = jnp.concatenate(
        [c_ctx[None, :], c, jnp.zeros((MOD_ROWS - 1 - dec_batch, D_MODEL), F32)], axis=0)
    ada = (cond, w_ada[l], b_ada[l][None, :])
    mod_early = _ada_modulation(*ada, EARLY_MODS)

    npre, npost = norm_pre[l], norm_post[l]
    ffn_first_b = [w[l, :1].astype(BF16) for w in (ffn_w1, ffn_w3, ffn_w2)]
    ffn_second_f32 = [(w[l], 1) for w in (ffn_w1, ffn_w3, ffn_w2)]
    w_in_b = w_in[l:l + 1].astype(BF16)
    w_out_b = w_out[l:l + 1].astype(BF16)
    zeros = jnp.zeros((GLA_LOW_RANK, GLA_QK), F32)
    w_up = jnp.concatenate(
        [jnp.concatenate([gla_w_up[l, 0], zeros], axis=1),
         jnp.concatenate([zeros, gla_w_up[l, 1]], axis=1)], axis=0).astype(BF16)
    b_up = gla_b_up[l].reshape(1, 2 * GLA_QK)
    gnorm = gla_norm[l][None, :]
    sink = attn_sink[l]

    x1, mod, *ffn_second_b = _ffn_first(
        x_prompt.reshape(batch * seq, D_MODEL), x_sample.reshape(dec_batch * dec_seq, D_MODEL),
        mod_early, dec_seq, npre, npost, *ffn_first_b, ada=ada, cast=ffn_second_f32)
    ctx_tiles = batch * seq // TOKEN_TILE

    def trunk(latent):
        n_batch, n_seq = (dec_batch, dec_seq) if latent else (batch, seq)
        latent_len = n_seq if latent else None
        first_tile = ctx_tiles if latent else 0
        att_in, gla_f, gv, *cache_t = _project(
            x1, first_tile, n_batch * n_seq, mod, latent_len, None if latent else n_seq, npre,
            w_in_b, w_up, b_up)
        if latent:
            att = _latent_attention(sink, att_in, cache_k[:, l].reshape(dec_batch, past, ATT_KV),
                                    cache_v[:, l].reshape(dec_batch, past, ATT_KV), n_batch, n_seq)
            (gla,) = _gla(gla_f, gv, gnorm, n_batch, n_seq, 1,
                          s0=(state_gla_fwd[:, l:l + 1], state_gla_bwd[:, l:l + 1]))
            extras = ()
        else:
            att = _context_attention(sink, att_in, n_batch, n_seq)
            gla, s_f, s_b = _gla(gla_f, gv, gnorm, n_batch, n_seq, CTX_SEQS_PER_STEP,
                                 emit_state=True)
            k_new, v_new = (jnp.transpose(c_t, (0, 1, 4, 2, 3)) for c_t in cache_t)
            extras = (k_new, v_new, s_f, s_b)
        y = _mix_ffn(x1, first_tile, att, gla, w_out_b, mod, latent_len, npre, npost,
                     *ffn_second_b)
        return y.reshape(n_batch, n_seq, D_MODEL), extras

    y_prompt, (k_new, v_new, s_f, s_b) = trunk(False)
    y_sample, _ = trunk(True)
    return (y_prompt, y_sample, k_new, v_new, s_f, s_b)
```

```python
import functools

import numpy as np
import jax
import jax.numpy as jnp
from jax import lax
from jax.experimental import pallas as pl
from jax.experimental.pallas import tpu as pltpu

F32 = jnp.float32
BF16 = jnp.bfloat16

D_MODEL = 1024
GRID_W = 64
N_Q_HEADS = 8
N_KV_HEADS = 2
HEAD_DIM = 64
WINDOW = 128
BLOCK = 128
ROPE_BASE = 10000.0
GLA_HEADS = 4
GLA_DK = 64
GLA_DV = 128
GLA_LOW_RANK = 16
GLA_TAU = 16.0
D_FF = 2816
N_MOD = 9
EPS = 1e-6
NEG = -1e30

ATT_Q = N_Q_HEADS * HEAD_DIM
ATT_KV = N_KV_HEADS * HEAD_DIM
GLA_QK = GLA_HEADS * GLA_DK
GLA_V = GLA_HEADS * GLA_DV
OFF_Q = 0
OFF_K = OFF_Q + ATT_Q
OFF_V = OFF_K + ATT_KV
OFF_GQ = OFF_V + ATT_KV
OFF_GK = OFF_GQ + GLA_QK
OFF_GV = OFF_GK + GLA_QK
OFF_GG = OFF_GV + GLA_V
OFF_LR = OFF_GG + GLA_V
IN_WIDTH = OFF_LR + 2 * GLA_LOW_RANK
A_Q = 0
A_K = A_Q + ATT_Q
A_KS = A_K + ATT_KV
A_V = A_KS + ATT_KV
A_VS = A_V + ATT_KV
ATT_IN_W = A_VS + ATT_KV
G_Q = 0
G_K = G_Q + GLA_QK
G_G = G_K + GLA_QK
G_LA = G_G + GLA_V
GLA_F_W = G_LA + 2 * GLA_QK
ROPE_PAIR = HEAD_DIM // 4
LOG2_E = 1.4426950408889634
ATT_Q_SCALE = HEAD_DIM ** -0.5 * LOG2_E

LANES = 128
SUBLANES = 8
HALF = LANES // 2
VMEM_LIMIT = 56 * 1024 * 1024

TOKEN_TILE = 512
FF_SPLITS = (0, 1536, D_FF)
EARLY_MODS = 3
CAST_SLABS = 16
ADA_K_TILE = 512
CTX_SEQS_PER_STEP = 4
LAT_BLOCKS_PER_STEP = 4
GLA_CHUNK = 128
MOD_ROWS = 8


def _params(n_axes):
    return pltpu.CompilerParams(
        dimension_semantics=("arbitrary",) * n_axes, vmem_limit_bytes=VMEM_LIMIT)


def _resident(shape):
    zeros = (0,) * len(shape)
    return pl.BlockSpec(shape, lambda *_: zeros, pipeline_mode=pl.Buffered(1))


def _sigmoid(x):
    return 1.0 / (1.0 + jnp.exp(-x))


def _silu(x):
    return x * _sigmoid(x)


def _rms(x, g):
    return x * lax.rsqrt(jnp.mean(x * x, axis=-1, keepdims=True) + EPS) * g


def _dot(a, b):
    return jnp.dot(a, b, preferred_element_type=F32)


def _dot_t(a, b):
    return lax.dot_general(a, b, (((1,), (1,)), ((), ())), preferred_element_type=F32)


def _ada_kernel(cond_ref, w_ref, b_ref, o_ref):
    n_mod = o_ref.shape[0]

    @pl.when(pl.program_id(0) == 0)
    def _():
        for m in range(n_mod):
            o_ref[m] = jnp.broadcast_to(b_ref[:, m * D_MODEL:(m + 1) * D_MODEL],
                                        (MOD_ROWS, D_MODEL))
    part = _dot(_silu(cond_ref[...]).astype(BF16), w_ref[...].astype(BF16))
    for m in range(n_mod):
        o_ref[m] += part[:, m * D_MODEL:(m + 1) * D_MODEL]


def _ada_modulation(cond, w_ada, b_ada, n_mod):
    k_dim = w_ada.shape[0]
    n = n_mod * D_MODEL
    return pl.pallas_call(
        _ada_kernel,
        out_shape=jax.ShapeDtypeStruct((n_mod, MOD_ROWS, D_MODEL), F32),
        grid=(k_dim // ADA_K_TILE,),
        in_specs=[
            pl.BlockSpec((MOD_ROWS, ADA_K_TILE), lambda k: (0, k)),
            pl.BlockSpec((ADA_K_TILE, n), lambda k: (k, 0)),
            pl.BlockSpec((1, n), lambda k: (0, 0)),
        ],
        out_specs=pl.BlockSpec((n_mod, MOD_ROWS, D_MODEL), lambda k: (0, 0, 0)),
        compiler_params=_params(1),
        name="ada_modulation",
    )(cond, w_ada, b_ada)


class _Mod:
    def __init__(self, mod_ref, tiles_per_seq, first_latent_tile=0, first_mod=0):
        self.ref = mod_ref
        self.first_mod = first_mod
        if tiles_per_seq is None:
            self.row = 0
        else:
            tile = pl.program_id(0) - first_latent_tile
            self.row = jnp.where(tile >= 0, 1 + tile // tiles_per_seq, 0)

    def __getitem__(self, m):
        return self.ref[m - self.first_mod, pl.ds(self.row, 1), :]


def _modulated(x, mod, npre_ref, i):
    return _rms(x, npre_ref[i:i + 1, :]) * (1.0 + mod[3 * i + 1]) + mod[3 * i]


def _residual(x, out, mod, npost_ref, i, weight):
    return x + (weight * mod[3 * i + 2]) * _rms(out, npost_ref[i:i + 1, :])


def _ffn_sublayer(x, i, mod, npre_ref, npost_ref, w1_ref, w3_ref, w2_ref):
    h = _modulated(x, mod, npre_ref, i).astype(BF16)
    acc = None
    for lo, hi in zip(FF_SPLITS[:-1], FF_SPLITS[1:]):
        a = _dot(h, w1_ref[:, lo:hi])
        g = _dot(h, w3_ref[:, lo:hi])
        part = _dot((_silu(a) * g).astype(BF16), w2_ref[lo:hi, :])
        acc = part if acc is None else acc + part
    return _residual(x, acc, mod, npost_ref, i, 0.5)


def _ffn_first_kernel(*refs, ctx_tiles, tiles_per_seq, n_cast):
    xc_ref, xl_ref, mod_ref, npre_ref, npost_ref, w1_ref, w3_ref, w2_ref = refs[:8]
    cond_ref, wada_ref, bada_ref = refs[8:11]
    cast_src = refs[11:11 + n_cast]
    o_ref, late_mod_ref = refs[11 + n_cast:13 + n_cast]
    cast_dst = refs[13 + n_cast:]
    mod = _Mod(mod_ref, tiles_per_seq, ctx_tiles)
    x = jnp.where(pl.program_id(0) >= ctx_tiles, xl_ref[...], xc_ref[...])
    o_ref[...] = _ffn_sublayer(x, 0, mod, npre_ref, npost_ref, w1_ref, w3_ref, w2_ref)
    for src, dst in zip(cast_src, cast_dst):
        dst[...] = src[...].astype(BF16)
    late_mod_ref[...] = _dot(_silu(cond_ref[...]).astype(BF16),
                             wada_ref[...].astype(BF16)) + bada_ref[...]


def _mix_ffn_kernel(x_ref, att_ref, gla_ref, wo_ref, mod_ref, npre_ref, npost_ref,
                    w1_ref, w3_ref, w2_ref, o_ref, *, tiles_per_seq):
    mod = _Mod(mod_ref, tiles_per_seq, first_mod=EARLY_MODS)
    mix = _dot(att_ref[...], wo_ref[0:ATT_Q, :]) + _dot(gla_ref[...], wo_ref[ATT_Q:, :])
    x = _residual(x_ref[...], mix, mod, npost_ref, 1, 1.0)
    o_ref[...] = _ffn_sublayer(x, 2, mod, npre_ref, npost_ref, w1_ref, w3_ref, w2_ref)


def _rope_tile(x, cos, sin_up, sin_dn):
    up = pltpu.roll(x, LANES - ROPE_PAIR, axis=1)
    dn = pltpu.roll(x, ROPE_PAIR, axis=1)
    return x * cos + up * sin_up + dn * sin_dn


def _proj_kernel(*refs, tiles_per_seq, cache_seq):
    x_ref, mod_ref, npre_ref, win_ref, wup_ref, bup_ref = refs[:6]
    rope_refs = refs[6:9] if tiles_per_seq is not None else None
    outs = refs[6 + (3 if rope_refs else 0):]
    att_ref, glaf_ref, gv_ref = outs[:3]
    mod = _Mod(mod_ref, tiles_per_seq, first_mod=EARLY_MODS)
    h = _modulated(x_ref[...], mod, npre_ref, 1).astype(BF16)
    q = _dot(h, win_ref[:, OFF_Q:OFF_K])
    kv = _dot(h, win_ref[:, OFF_K:OFF_GQ])
    k, v = kv[:, :ATT_KV], kv[:, ATT_KV:]
    lr = _dot(h, win_ref[:, OFF_LR:IN_WIDTH])
    z = _dot(lr.astype(BF16), wup_ref[...]) + bup_ref[...]
    if rope_refs:
        cos, sup, sdn = (r[...] for r in rope_refs)
        q = jnp.concatenate([_rope_tile(q[:, j * LANES:(j + 1) * LANES], cos, sup, sdn)
                             for j in range(ATT_Q // LANES)], axis=1)
        k_att = _rope_tile(k, cos, sup, sdn)
    else:
        k_att = k
    att_ref[:, A_Q:A_K] = (q * ATT_Q_SCALE).astype(BF16)
    att_ref[:, A_K:A_KS] = k_att.astype(BF16)
    att_ref[:, A_KS:A_V] = pltpu.roll(k_att, HALF, axis=1).astype(BF16)
    att_ref[:, A_V:A_VS] = v.astype(BF16)
    att_ref[:, A_VS:ATT_IN_W] = pltpu.roll(v, HALF, axis=1).astype(BF16)
    if cache_seq is not None:
        for t, dst in ((k.T, outs[3]), (v.T, outs[4])):
            for b in range(TOKEN_TILE // cache_seq):
                dst[b, 0] = t[:, b * cache_seq:(b + 1) * cache_seq].reshape(
                    N_KV_HEADS, HEAD_DIM, cache_seq)
    glaf_ref[:, G_Q:G_G] = _dot(h, win_ref[:, OFF_GQ:OFF_GV])
    log_sig = jnp.minimum(z, 0.0) - jnp.log(1.0 + jnp.exp(-jnp.abs(z)))
    glaf_ref[:, G_LA:GLA_F_W] = log_sig * (1.0 / GLA_TAU)
    gv_ref[...] = _dot(h, win_ref[:, OFF_GV:OFF_GG]).astype(BF16)
    glaf_ref[:, G_G:G_LA] = _dot(h, win_ref[:, OFF_GG:OFF_LR])


def _token_spec(width, first_tile=0):
    return pl.BlockSpec((TOKEN_TILE, width), lambda i: (i + first_tile, 0))


def _stack_spec(shape, j):
    return pl.BlockSpec((None,) + tuple(shape[1:]), lambda i: (j, 0, 0),
                        pipeline_mode=pl.Buffered(1))


def _tiles_per_seq(latent_len):
    return None if latent_len is None else latent_len // TOKEN_TILE


def _ffn_first(x_ctx, x_lat, mod, latent_len, npre, npost, w1, w3, w2, ada, cast):
    ctx_tiles = x_ctx.shape[0] // TOKEN_TILE
    steps = ctx_tiles + x_lat.shape[0] // TOKEN_TILE
    assert steps >= CAST_SLABS, "every slab needs its own grid step"
    cond, w_ada, b_ada = ada
    late_mods = N_MOD - EARLY_MODS
    ada_cols = late_mods * D_MODEL // steps
    per_mod = D_MODEL // ada_cols
    assert ada_cols * steps == late_mods * D_MODEL and ada_cols % LANES == 0
    first_slab = EARLY_MODS * per_mod
    in_specs = [pl.BlockSpec((TOKEN_TILE, D_MODEL), lambda i: (jnp.minimum(i, ctx_tiles - 1), 0)),
                pl.BlockSpec((TOKEN_TILE, D_MODEL), lambda i: (jnp.maximum(i - ctx_tiles, 0), 0)),
                _resident(mod.shape), _resident(npre.shape),
                _resident(npost.shape), _stack_spec(w1.shape, 0),
                _stack_spec(w3.shape, 0), _stack_spec(w2.shape, 0),
                _resident(cond.shape),
                pl.BlockSpec((D_MODEL, ada_cols), lambda i: (0, first_slab + i)),
                pl.BlockSpec((1, ada_cols), lambda i: (0, first_slab + i))]
    out_shape = [jax.ShapeDtypeStruct((steps * TOKEN_TILE, D_MODEL), F32),
                 jax.ShapeDtypeStruct((late_mods, MOD_ROWS, D_MODEL), F32)]
    out_specs = [_token_spec(D_MODEL),
                 pl.BlockSpec((None, MOD_ROWS, ada_cols), lambda i: (i // per_mod, 0, i % per_mod))]
    for stack, j in cast:
        _, rows, cols = stack.shape
        slab = rows // CAST_SLABS
        assert slab * CAST_SLABS == rows and slab % (2 * SUBLANES) == 0, "whole bf16 row tiles"
        in_specs.append(pl.BlockSpec((None, slab, cols), functools.partial(
            lambda i, j: (j, i * CAST_SLABS // steps, 0), j=j)))
        out_shape.append(jax.ShapeDtypeStruct((1, rows, cols), BF16))
        out_specs.append(pl.BlockSpec((None, slab, cols), lambda i: (0, i * CAST_SLABS // steps, 0)))
    return pl.pallas_call(
        functools.partial(_ffn_first_kernel, ctx_tiles=ctx_tiles,
                          tiles_per_seq=_tiles_per_seq(latent_len), n_cast=len(cast)),
        out_shape=out_shape,
        grid=(steps,),
        in_specs=in_specs,
        out_specs=out_specs,
        compiler_params=_params(1),
        name="ffn_first",
    )(x_ctx, x_lat, mod, npre, npost, w1, w3, w2, cond, w_ada, b_ada,
      *[stack for stack, _ in cast])


def _mix_ffn(x, first_tile, att, gla, w_out, mod, latent_len, npre, npost, w1, w3, w2):
    t = att.shape[0]
    return pl.pallas_call(
        functools.partial(_mix_ffn_kernel, tiles_per_seq=_tiles_per_seq(latent_len)),
        out_shape=jax.ShapeDtypeStruct((t, D_MODEL), F32),
        grid=(t // TOKEN_TILE,),
        in_specs=[_token_spec(D_MODEL, first_tile), _token_spec(ATT_Q), _token_spec(GLA_V),
                  _stack_spec(w_out.shape, 0), _resident(mod.shape), _resident(npre.shape),
                  _resident(npost.shape), _stack_spec(w1.shape, 0),
                  _stack_spec(w3.shape, 0), _stack_spec(w2.shape, 0)],
        out_specs=_token_spec(D_MODEL),
        compiler_params=_params(1),
        name="mix_ffn",
    )(x, att, gla, w_out, mod, npre, npost, w1, w3, w2)


def _project(x, first_tile, t, mod, latent_len, cache_seq, npre, w_in, w_up, b_up):
    tiles = _tiles_per_seq(latent_len)
    in_specs = [_token_spec(D_MODEL, first_tile), _resident(mod.shape), _resident(npre.shape),
                _stack_spec(w_in.shape, 0), _resident(w_up.shape), _resident(b_up.shape)]
    args = [x, mod, npre, w_in, w_up, b_up]
    if tiles is not None:
        in_specs += [pl.BlockSpec((TOKEN_TILE, LANES), lambda i: (i % tiles, 0))] * 3
        args += [jnp.asarray(tab) for tab in _rope_tables(latent_len)]
    outs = ((ATT_IN_W, BF16), (GLA_F_W, F32), (GLA_V, BF16))
    out_shape = [jax.ShapeDtypeStruct((t, w), dt) for w, dt in outs]
    out_specs = [_token_spec(w) for w, _ in outs]
    if cache_seq is not None:
        seqs = TOKEN_TILE // cache_seq
        cache = (t // cache_seq, 1, N_KV_HEADS, HEAD_DIM, cache_seq)
        out_shape += [jax.ShapeDtypeStruct(cache, F32)] * 2
        out_specs += [pl.BlockSpec((seqs,) + cache[1:], lambda i: (i, 0, 0, 0, 0))] * 2
    return pl.pallas_call(
        functools.partial(_proj_kernel, tiles_per_seq=tiles, cache_seq=cache_seq),
        out_shape=out_shape,
        grid=(t // TOKEN_TILE,),
        in_specs=in_specs,
        out_specs=out_specs,
        compiler_params=_params(1),
        name="project",
    )(*args)


def _rope_tables(seq_len):
    half = HEAD_DIM // 2
    inv_freq = np.float32(ROPE_BASE) ** (-np.arange(0, half, 2, dtype=np.float32) / half)
    pos = np.arange(seq_len)
    row = (pos // GRID_W).astype(np.float32)
    col = (pos % GRID_W).astype(np.float32)
    within = np.arange(LANES) % HEAD_DIM
    idx = within % half
    freq = inv_freq[idx % ROPE_PAIR].astype(np.float32)
    p = np.where((within // half == 0)[None, :], row[:, None], col[:, None])
    ang = (p * freq[None, :]).astype(np.float32)
    cos, sin = np.cos(ang).astype(np.float32), np.sin(ang).astype(np.float32)
    first = (idx < ROPE_PAIR)[None, :]
    zero = np.float32(0.0)
    return cos, np.where(first, -sin, zero), np.where(first, zero, sin)


def _attend(q_ref, q_rows, nq, segs, sink_ref):
    lane = lax.broadcasted_iota(jnp.int32, (1, LANES), 1)
    half_of = [lane < HALF, lane >= HALF]
    rows = lax.broadcasted_iota(jnp.int32, (2 * nq, 1), 0)
    zero = jnp.zeros((), BF16)
    out = [None] * (ATT_Q // LANES)
    for g in range(N_KV_HEADS):
        tiles = (2 * g, 2 * g + 1)
        for e in range(2):
            qm = jnp.concatenate(
                [jnp.where(half_of[e], q_ref[q_rows, j * LANES:(j + 1) * LANES], zero)
                 for j in tiles], axis=0)
            sink = jnp.where(rows < nq, sink_ref[2 * tiles[0] + e],
                             sink_ref[2 * tiles[1] + e]) * LOG2_E
            scores = []
            for k, k_sw, _, _, m in segs:
                s = _dot_t(qm, k if e == g else k_sw)
                scores.append(s if m is None else jnp.where(m, s, NEG))
            mx = sink
            for s in scores:
                mx = jnp.maximum(mx, jnp.max(s, axis=-1, keepdims=True))
            probs = [jnp.exp2(s - mx) for s in scores]
            den = jnp.exp2(sink - mx)
            for p in probs:
                den = den + jnp.sum(p, axis=-1, keepdims=True)
            o = None
            for p, (_, _, v, v_sw, _) in zip(probs, segs):
                part = _dot(p.astype(BF16), jnp.where(half_of[e], v if e == g else v_sw, zero))
                o = part if o is None else o + part
            o = o * (1.0 / den)
            for r, j in enumerate(tiles):
                blk = o[r * nq:(r + 1) * nq, :]
                out[j] = blk if out[j] is None else out[j] + blk
    return out


def _context_attn_kernel(sink_ref, a_ref, o_ref, *, seq_len):
    for s in range(a_ref.shape[0] // seq_len):
        rows = slice(s * seq_len, (s + 1) * seq_len)
        segs = [(a_ref[rows, A_K:A_KS], a_ref[rows, A_KS:A_V],
                 a_ref[rows, A_V:A_VS], a_ref[rows, A_VS:ATT_IN_W], None)]
        for j, t in enumerate(_attend(a_ref, rows, seq_len, segs, sink_ref)):
            o_ref[rows, j * LANES:(j + 1) * LANES] = t.astype(o_ref.dtype)


def _latent_attn_kernel(sink_ref, q_ref, kv_ref, kc_ref, vc_ref, o_ref, *, seq_len):
    k_ctx, v_ctx = kc_ref[0], vc_ref[0]
    ctx = (k_ctx.astype(BF16), pltpu.roll(k_ctx, HALF, axis=1).astype(BF16),
           v_ctx.astype(BF16), pltpu.roll(v_ctx, HALF, axis=1).astype(BF16), None)
    span = 3 * BLOCK
    per_step = q_ref.shape[0] // BLOCK
    for s in range(per_step):
        i = pl.program_id(1) * per_step + s
        start = pl.multiple_of(jnp.clip((i - 1) * BLOCK, 0, seq_len - span), BLOCK)
        kpos = start + lax.broadcasted_iota(jnp.int32, (1, span), 1)
        qpos = i * BLOCK + lax.broadcasted_iota(jnp.int32, (BLOCK, 1), 0)
        valid = jnp.abs(kpos - qpos) <= WINDOW
        valid2 = jnp.concatenate([valid, valid], axis=0)
        kv = kv_ref[pl.ds(start, span), :]
        segs = [tuple(kv[:, c * ATT_KV:(c + 1) * ATT_KV] for c in range(4)) + (valid2,), ctx]
        rows = slice(s * BLOCK, (s + 1) * BLOCK)
        for j, t in enumerate(_attend(q_ref, rows, BLOCK, segs, sink_ref)):
            o_ref[rows, j * LANES:(j + 1) * LANES] = t.astype(o_ref.dtype)


def _smem_spec():
    return pl.BlockSpec(memory_space=pltpu.SMEM)


def _context_attention(sink, att_in, batch, seq_len):
    rows = CTX_SEQS_PER_STEP * seq_len

    def seq(width):
        return pl.BlockSpec((rows, width), lambda b: (b, 0))
    return pl.pallas_call(
        functools.partial(_context_attn_kernel, seq_len=seq_len),
        out_shape=jax.ShapeDtypeStruct((batch * seq_len, ATT_Q), BF16),
        grid=(batch // CTX_SEQS_PER_STEP,),
        in_specs=[_smem_spec(), seq(ATT_IN_W)],
        out_specs=seq(ATT_Q),
        compiler_params=_params(1),
        name="context_attention",
    )(sink, att_in)


def _latent_attention(sink, att_in, k_ctx, v_ctx, batch, seq_len):
    steps = seq_len // (LAT_BLOCKS_PER_STEP * BLOCK)
    rows = LAT_BLOCKS_PER_STEP * BLOCK
    past = k_ctx.shape[1]
    assert A_K == ATT_IN_W - A_K, "q and the key / value columns are the two halves of att_in"
    return pl.pallas_call(
        functools.partial(_latent_attn_kernel, seq_len=seq_len),
        out_shape=jax.ShapeDtypeStruct((batch * seq_len, ATT_Q), BF16),
        grid=(batch, steps),
        in_specs=[_smem_spec(),
                  pl.BlockSpec((rows, ATT_Q), lambda b, i: (b * steps + i, 0)),
                  pl.BlockSpec((seq_len, ATT_IN_W - A_K), lambda b, i: (b, 1)),
                  pl.BlockSpec((1, past, ATT_KV), lambda b, i: (b, 0, 0)),
                  pl.BlockSpec((1, past, ATT_KV), lambda b, i: (b, 0, 0))],
        out_specs=pl.BlockSpec((rows, ATT_Q), lambda b, i: (b * steps + i, 0)),
        compiler_params=_params(2),
        name="latent_attention",
    )(sink, att_in, att_in, k_ctx, v_ctx)


def _split2(x):
    hi = x.astype(BF16)
    lo = (x - hi.astype(F32)).astype(BF16)
    return hi, lo


def _gla_kernel(*refs, n_chunks, has_s0, emit_state):
    refs = list(refs)
    gf_ref, gv_ref, gn_ref = refs[:3]
    pos = 3
    s0_refs = refs[pos:pos + 2] if has_s0 else None
    pos += 2 if has_s0 else 0
    o_ref = refs[pos]
    pos += 1
    sfin_refs = refs[pos:pos + 2] if emit_state else None
    pos += 2 if emit_state else 0
    cum_ref, kv_ref, sent_ref = refs[pos:]

    C = GLA_CHUNK
    n_pairs = GLA_QK // LANES
    lane = lax.broadcasted_iota(jnp.int32, (1, LANES), 1)
    half_of = [lane < HALF, lane >= HALF]
    r_i = lax.broadcasted_iota(jnp.int32, (C, C), 0)
    c_i = lax.broadcasted_iota(jnp.int32, (C, C), 1)
    lower = c_i <= r_i
    upper = c_i >= r_i
    tri = jnp.concatenate([jnp.where(lower, 1.0, 0.0), jnp.where(upper, 1.0, 0.0)],
                          axis=0).astype(BF16)
    zeros_cc = jnp.zeros((C, C), F32)
    gnorm = gn_ref[...]
    qscale = GLA_DK ** -0.5

    def one_sequence(s):
        def chunk_rows(n):
            return slice((s * n_chunks + n) * C, (s * n_chunks + n + 1) * C)

        for n in range(n_chunks):
            rows = chunk_rows(n)
            hi, lo = _split2(gf_ref[rows, G_LA:GLA_F_W])
            sums = _dot(tri, jnp.concatenate([hi, lo], axis=1))
            cum_f = sums[:C, 0:GLA_QK] + sums[:C, 2 * GLA_QK:3 * GLA_QK]
            cum_b = sums[C:, GLA_QK:2 * GLA_QK] + sums[C:, 3 * GLA_QK:]
            cum_ref[0, rows, :] = cum_f
            cum_ref[1, rows, :] = cum_b
            k = gf_ref[rows, G_K:G_G]
            k_in = (k * jnp.exp(cum_f[C - 1:C, :] - cum_f), k * jnp.exp(cum_b[0:1, :] - cum_b))
            for p in range(n_pairs):
                sl = slice(p * LANES, (p + 1) * LANES)
                kv_t = None
                for e in range(2):
                    h = 2 * p + e
                    v_t = gv_ref[rows, h * GLA_DV:(h + 1) * GLA_DV].T
                    k_e = jnp.concatenate(
                        [jnp.where(half_of[e], k_in[d][:, sl], 0.0) for d in range(2)], axis=1)
                    part = _dot(v_t, k_e.astype(BF16))
                    kv_t = part if kv_t is None else kv_t + part
                kv_ref[s, n, p] = kv_t

        for p in range(n_pairs):
            sl = slice(p * LANES, (p + 1) * LANES)
            st = []
            for d in range(2):
                if has_s0:
                    s0 = s0_refs[d][s, 0, 2 * p:2 * p + 2, :, :].reshape(2 * GLA_DK, GLA_DV)
                    st.append(s0.T)
                else:
                    st.append(jnp.zeros((GLA_DV, 2 * GLA_DK), F32))
            for i in range(n_chunks):
                for d, n in ((0, i), (1, n_chunks - 1 - i)):
                    tot_row = (s * n_chunks + n) * C + (C - 1 if d == 0 else 0)
                    decay = jnp.exp(cum_ref[d, tot_row:tot_row + 1, sl])
                    sent_ref[s, n, p, :, d * LANES:(d + 1) * LANES] = st[d].astype(BF16)
                    st[d] = decay * st[d] + kv_ref[s, n, p, :, d * LANES:(d + 1) * LANES]
            if emit_state:
                for d in range(2):
                    sfin_refs[d][s, 0, 2 * p:2 * p + 2, :, :] = st[d].T.reshape(2, GLA_DK, GLA_DV)

        for n in range(n_chunks):
            rows = chunk_rows(n)
            q = gf_ref[rows, G_Q:G_K] * qscale
            k = gf_ref[rows, G_K:G_G]
            qs, ks, qin = [], [], []
            for d in range(2):
                cum = cum_ref[d, rows, :]
                ref = cum[C // 2:C // 2 + 1, :]
                qs.append(q * jnp.exp(cum - ref))
                ks.append((k * jnp.exp(ref - cum)).astype(BF16))
                qin.append(q * jnp.exp(cum))
            for p in range(n_pairs):
                sl = slice(p * LANES, (p + 1) * LANES)
                lhs = jnp.concatenate(
                    [jnp.where(half_of[e], qs[d][:, sl], 0.0) for e in range(2) for d in range(2)],
                    axis=0).astype(BF16)
                sc = _dot_t(lhs, jnp.concatenate([ks[0][:, sl], ks[1][:, sl]], axis=0))
                prob = []
                for e in range(2):
                    s_f = sc[(2 * e) * C:(2 * e + 1) * C, :C]
                    s_b = sc[(2 * e + 1) * C:(2 * e + 2) * C, C:]
                    prob.append(jnp.where(lower, s_f, 0.0) + jnp.where(upper, s_b, 0.0))
                p_blk = jnp.concatenate(
                    [jnp.concatenate([prob[0], zeros_cc], axis=1),
                     jnp.concatenate([zeros_cc, prob[1]], axis=1)], axis=0).astype(BF16)
                v2 = jnp.concatenate(
                    [gv_ref[rows, (2 * p + e) * GLA_DV:(2 * p + e + 1) * GLA_DV] for e in range(2)],
                    axis=0)
                q_in = jnp.concatenate(
                    [jnp.concatenate([jnp.where(half_of[e], qin[d][:, sl], 0.0) for d in range(2)],
                                     axis=1) for e in range(2)], axis=0).astype(BF16)
                o2 = _dot(p_blk, v2) + _dot_t(q_in, sent_ref[s, n, p])
                for e in range(2):
                    h = 2 * p + e
                    o = o2[e * C:(e + 1) * C, :]
                    o = o * lax.rsqrt(jnp.mean(o * o, axis=-1, keepdims=True) + EPS) * gnorm
                    gate = gf_ref[rows, G_G + h * GLA_DV:G_G + (h + 1) * GLA_DV]
                    o_ref[rows, h * GLA_DV:(h + 1) * GLA_DV] = (o * _silu(gate)).astype(o_ref.dtype)

    for s in range(kv_ref.shape[0]):
        one_sequence(s)


def _gla(gla_f, gv, gnorm, batch, seq_len, seqs_per_step, s0=None, emit_state=False):
    n_chunks = seq_len // GLA_CHUNK
    has_s0 = s0 is not None
    rows = seqs_per_step * seq_len

    def seq(width):
        return pl.BlockSpec((rows, width), lambda b: (b, 0))
    state_spec = pl.BlockSpec((seqs_per_step, 1, GLA_HEADS, GLA_DK, GLA_DV),
                              lambda b: (b, 0, 0, 0, 0))
    in_specs = [seq(GLA_F_W), seq(GLA_V), pl.BlockSpec((1, GLA_DV), lambda b: (0, 0))]
    args = [gla_f, gv, gnorm]
    if has_s0:
        in_specs += [state_spec, state_spec]
        args += list(s0)
    out_shape = [jax.ShapeDtypeStruct((batch * seq_len, GLA_V), BF16)]
    out_specs = [seq(GLA_V)]
    if emit_state:
        out_shape += [jax.ShapeDtypeStruct((batch, 1, GLA_HEADS, GLA_DK, GLA_DV), F32)] * 2
        out_specs += [state_spec, state_spec]
    n_pairs = GLA_QK // LANES
    return pl.pallas_call(
        functools.partial(_gla_kernel, n_chunks=n_chunks, has_s0=has_s0, emit_state=emit_state),
        out_shape=out_shape,
        grid=(batch // seqs_per_step,),
        in_specs=in_specs,
        out_specs=out_specs,
        scratch_shapes=[pltpu.VMEM((2, rows, GLA_QK), F32),
                        pltpu.VMEM((seqs_per_step, n_chunks, n_pairs, GLA_DV, 2 * LANES), F32),
                        pltpu.VMEM((seqs_per_step, n_chunks, n_pairs, GLA_DV, 2 * LANES), BF16)],
        compiler_params=_params(1),
        name="gla",
    )(*args)


def kernel(x_prompt, x_sample, cache_k, cache_v, state_gla_fwd, state_gla_bwd, c, c_ctx,
           w_ada, b_ada, norm_pre, norm_post, ffn_w1, ffn_w3, ffn_w2, w_in,
           gla_w_up, gla_b_up, gla_norm, attn_sink, w_out):
    depth = w_in.shape[0]
    assert depth == 1, "single trunk layer"
    batch, seq = x_prompt.shape[0], x_prompt.shape[1]
    dec_batch, dec_seq = x_sample.shape[0], x_sample.shape[1]
    past = cache_k.shape[2]
    l = 0

    cond = jnp.concatenate(
        [c_ctx[None, :], c, jnp.zeros((MOD_ROWS - 1 - dec_batch, D_MODEL), F32)], axis=0)
    ada = (cond, w_ada[l], b_ada[l][None, :])
    mod_early = _ada_modulation(*ada, EARLY_MODS)

    npre, npost = norm_pre[l], norm_post[l]
    ffn_first_b = [w[l, :1].astype(BF16) for w in (ffn_w1, ffn_w3, ffn_w2)]
    ffn_second_f32 = [(w[l], 1) for w in (ffn_w1, ffn_w3, ffn_w2)]
    w_in_b = w_in[l:l + 1].astype(BF16)
    w_out_b = w_out[l:l + 1].astype(BF16)
    zeros = jnp.zeros((GLA_LOW_RANK, GLA_QK), F32)
    w_up = jnp.concatenate(
        [jnp.concatenate([gla_w_up[l, 0], zeros], axis=1),
         jnp.concatenate([zeros, gla_w_up[l, 1]], axis=1)], axis=0).astype(BF16)
    b_up = gla_b_up[l].reshape(1, 2 * GLA_QK)
    gnorm = gla_norm[l][None, :]
    sink = attn_sink[l]

    x1, mod, *ffn_second_b = _ffn_first(
        x_prompt.reshape(batch * seq, D_MODEL), x_sample.reshape(dec_batch * dec_seq, D_MODEL),
        mod_early, dec_seq, npre, npost, *ffn_first_b, ada=ada, cast=ffn_second_f32)
    ctx_tiles = batch * seq // TOKEN_TILE

    def trunk(latent):
        n_batch, n_seq = (dec_batch, dec_seq) if latent else (batch, seq)
        latent_len = n_seq if latent else None
        first_tile = ctx_tiles if latent else 0
        att_in, gla_f, gv, *cache_t = _project(
            x1, first_tile, n_batch * n_seq, mod, latent_len, None if latent else n_seq, npre,
            w_in_b, w_up, b_up)
        if latent:
            att = _latent_attention(sink, att_in, cache_k[:, l].reshape(dec_batch, past, ATT_KV),
                                    cache_v[:, l].reshape(dec_batch, past, ATT_KV), n_batch, n_seq)
            (gla,) = _gla(gla_f, gv, gnorm, n_batch, n_seq, 1,
                          s0=(state_gla_fwd[:, l:l + 1], state_gla_bwd[:, l:l + 1]))
            extras = ()
        else:
            att = _context_attention(sink, att_in, n_batch, n_seq)
            gla, s_f, s_b = _gla(gla_f, gv, gnorm, n_batch, n_seq, CTX_SEQS_PER_STEP,
                                 emit_state=True)
            k_new, v_new = (jnp.transpose(c_t, (0, 1, 4, 2, 3)) for c_t in cache_t)
            extras = (k_new, v_new, s_f, s_b)
        y = _mix_ffn(x1, first_tile, att, gla, w_out_b, mod, latent_len, npre, npost,
                     *ffn_second_b)
        return y.reshape(n_batch, n_seq, D_MODEL), extras

    y_prompt, (k_new, v_new, s_f, s_b) = trunk(False)
    y_sample, _ = trunk(True)
    return (y_prompt, y_sample, k_new, v_new, s_f, s_b)
```

```python
import functools

import numpy as np
import jax
import jax.numpy as jnp
from jax import lax
from jax.experimental import pallas as pl
from jax.experimental.pallas import tpu as pltpu

F32 = jnp.float32
BF16 = jnp.bfloat16

D_MODEL = 1024
GRID_W = 64
N_Q_HEADS = 8
N_KV_HEADS = 2
HEAD_DIM = 64
WINDOW = 128
BLOCK = 128
ROPE_BASE = 10000.0
GLA_HEADS = 4
GLA_DK = 64
GLA_DV = 128
GLA_LOW_RANK = 16
GLA_TAU = 16.0
D_FF = 2816
N_MOD = 9
EPS = 1e-6
NEG = -1e30

ATT_Q = N_Q_HEADS * HEAD_DIM
ATT_KV = N_KV_HEADS * HEAD_DIM
GLA_QK = GLA_HEADS * GLA_DK
GLA_V = GLA_HEADS * GLA_DV
OFF_Q = 0
OFF_K = OFF_Q + ATT_Q
OFF_V = OFF_K + ATT_KV
OFF_GQ = OFF_V + ATT_KV
OFF_GK = OFF_GQ + GLA_QK
OFF_GV = OFF_GK + GLA_QK
OFF_GG = OFF_GV + GLA_V
OFF_LR = OFF_GG + GLA_V
IN_WIDTH = OFF_LR + 2 * GLA_LOW_RANK
A_Q = 0
A_K = A_Q + ATT_Q
A_KS = A_K + ATT_KV
A_V = A_KS + ATT_KV
A_VS = A_V + ATT_KV
ATT_IN_W = A_VS + ATT_KV
G_Q = 0
G_K = G_Q + GLA_QK
G_G = G_K + GLA_QK
G_LA = G_G + GLA_V
GLA_F_W = G_LA + 2 * GLA_QK
ROPE_PAIR = HEAD_DIM // 4
LOG2_E = 1.4426950408889634
ATT_Q_SCALE = HEAD_DIM ** -0.5 * LOG2_E

LANES = 128
SUBLANES = 8
HALF = LANES // 2
VMEM_LIMIT = 56 * 1024 * 1024

TOKEN_TILE = 512
FF_SPLITS = (0, 1536, D_FF)
EARLY_MODS = 3
CAST_SLABS = 16
ADA_K_TILE = 1024
CTX_SEQS_PER_STEP = 4
LAT_BLOCKS_PER_STEP = 4
GLA_CHUNK = 128
MOD_ROWS = 8


def _params(n_axes):
    return pltpu.CompilerParams(
        dimension_semantics=("arbitrary",) * n_axes, vmem_limit_bytes=VMEM_LIMIT)


def _resident(shape):
    zeros = (0,) * len(shape)
    return pl.BlockSpec(shape, lambda *_: zeros, pipeline_mode=pl.Buffered(1))


def _sigmoid(x):
    return 1.0 / (1.0 + jnp.exp(-x))


def _silu(x):
    return x * _sigmoid(x)


def _rms(x, g):
    return x * lax.rsqrt(jnp.mean(x * x, axis=-1, keepdims=True) + EPS) * g


def _dot(a, b):
    return jnp.dot(a, b, preferred_element_type=F32)


def _dot_t(a, b):
    return lax.dot_general(a, b, (((1,), (1,)), ((), ())), preferred_element_type=F32)


def _ada_kernel(cond_ref, w_ref, b_ref, o_ref):
    n_mod = o_ref.shape[0]

    @pl.when(pl.program_id(0) == 0)
    def _():
        for m in range(n_mod):
            o_ref[m] = jnp.broadcast_to(b_ref[:, m * D_MODEL:(m + 1) * D_MODEL],
                                        (MOD_ROWS, D_MODEL))
    part = _dot(_silu(cond_ref[...]).astype(BF16), w_ref[...].astype(BF16))
    for m in range(n_mod):
        o_ref[m] += part[:, m * D_MODEL:(m + 1) * D_MODEL]


def _ada_modulation(cond, w_ada, b_ada, n_mod):
    k_dim = w_ada.shape[0]
    n = n_mod * D_MODEL
    return pl.pallas_call(
        _ada_kernel,
        out_shape=jax.ShapeDtypeStruct((n_mod, MOD_ROWS, D_MODEL), F32),
        grid=(k_dim // ADA_K_TILE,),
        in_specs=[
            pl.BlockSpec((MOD_ROWS, ADA_K_TILE), lambda k: (0, k)),
            pl.BlockSpec((ADA_K_TILE, n), lambda k: (k, 0)),
            pl.BlockSpec((1, n), lambda k: (0, 0)),
        ],
        out_specs=pl.BlockSpec((n_mod, MOD_ROWS, D_MODEL), lambda k: (0, 0, 0)),
        compiler_params=_params(1),
        name="ada_modulation",
    )(cond, w_ada, b_ada)


class _Mod:
    def __init__(self, mod_ref, tiles_per_seq, first_latent_tile=0, first_mod=0):
        self.ref = mod_ref
        self.first_mod = first_mod
        if tiles_per_seq is None:
            self.row = 0
        else:
            tile = pl.program_id(0) - first_latent_tile
            self.row = jnp.where(tile >= 0, 1 + tile // tiles_per_seq, 0)

    def __getitem__(self, m):
        return self.ref[m - self.first_mod, pl.ds(self.row, 1), :]


def _modulated(x, mod, npre_ref, i):
    return _rms(x, npre_ref[i:i + 1, :]) * (1.0 + mod[3 * i + 1]) + mod[3 * i]


def _residual(x, out, mod, npost_ref, i, weight):
    return x + (weight * mod[3 * i + 2]) * _rms(out, npost_ref[i:i + 1, :])


def _ffn_sublayer(x, i, mod, npre_ref, npost_ref, w1_ref, w3_ref, w2_ref):
    h = _modulated(x, mod, npre_ref, i).astype(BF16)
    acc = None
    for lo, hi in zip(FF_SPLITS[:-1], FF_SPLITS[1:]):
        a = _dot(h, w1_ref[:, lo:hi])
        g = _dot(h, w3_ref[:, lo:hi])
        part = _dot((_silu(a) * g).astype(BF16), w2_ref[lo:hi, :])
        acc = part if acc is None else acc + part
    return _residual(x, acc, mod, npost_ref, i, 0.5)


def _ffn_first_kernel(*refs, ctx_tiles, tiles_per_seq, n_cast):
    xc_ref, xl_ref, mod_ref, npre_ref, npost_ref, w1_ref, w3_ref, w2_ref = refs[:8]
    cond_ref, wada_ref, bada_ref = refs[8:11]
    cast_src = refs[11:11 + n_cast]
    o_ref, late_mod_ref = refs[11 + n_cast:13 + n_cast]
    cast_dst = refs[13 + n_cast:]
    mod = _Mod(mod_ref, tiles_per_seq, ctx_tiles)
    x = jnp.where(pl.program_id(0) >= ctx_tiles, xl_ref[...], xc_ref[...])
    o_ref[...] = _ffn_sublayer(x, 0, mod, npre_ref, npost_ref, w1_ref, w3_ref, w2_ref)
    for src, dst in zip(cast_src, cast_dst):
        dst[...] = src[...].astype(BF16)
    late_mod_ref[...] = _dot(_silu(cond_ref[...]).astype(BF16),
                             wada_ref[...].astype(BF16)) + bada_ref[...]


def _mix_ffn_kernel(x_ref, att_ref, gla_ref, wo_ref, mod_ref, npre_ref, npost_ref,
                    w1_ref, w3_ref, w2_ref, o_ref, *, tiles_per_seq):
    mod = _Mod(mod_ref, tiles_per_seq, first_mod=EARLY_MODS)
    mix = _dot(att_ref[...], wo_ref[0:ATT_Q, :]) + _dot(gla_ref[...], wo_ref[ATT_Q:, :])
    x = _residual(x_ref[...], mix, mod, npost_ref, 1, 1.0)
    o_ref[...] = _ffn_sublayer(x, 2, mod, npre_ref, npost_ref, w1_ref, w3_ref, w2_ref)


def _rope_tile(x, cos, sin_up, sin_dn):
    up = pltpu.roll(x, LANES - ROPE_PAIR, axis=1)
    dn = pltpu.roll(x, ROPE_PAIR, axis=1)
    return x * cos + up * sin_up + dn * sin_dn


def _proj_kernel(*refs, tiles_per_seq, cache_seq):
    x_ref, mod_ref, npre_ref, win_ref, wup_ref, bup_ref = refs[:6]
    rope_refs = refs[6:9] if tiles_per_seq is not None else None
    outs = refs[6 + (3 if rope_refs else 0):]
    att_ref, glaf_ref, gv_ref = outs[:3]
    mod = _Mod(mod_ref, tiles_per_seq, first_mod=EARLY_MODS)
    h = _modulated(x_ref[...], mod, npre_ref, 1).astype(BF16)
    q = _dot(h, win_ref[:, OFF_Q:OFF_K])
    kv = _dot(h, win_ref[:, OFF_K:OFF_GQ])
    k, v = kv[:, :ATT_KV], kv[:, ATT_KV:]
    lr = _dot(h, win_ref[:, OFF_LR:IN_WIDTH])
    z = _dot(lr.astype(BF16), wup_ref[...]) + bup_ref[...]
    if rope_refs:
        cos, sup, sdn = (r[...] for r in rope_refs)
        q = jnp.concatenate([_rope_tile(q[:, j * LANES:(j + 1) * LANES], cos, sup, sdn)
                             for j in range(ATT_Q // LANES)], axis=1)
        k_att = _rope_tile(k, cos, sup, sdn)
    else:
        k_att = k
    att_ref[:, A_Q:A_K] = (q * ATT_Q_SCALE).astype(BF16)
    att_ref[:, A_K:A_KS] = k_att.astype(BF16)
    att_ref[:, A_KS:A_V] = pltpu.roll(k_att, HALF, axis=1).astype(BF16)
    att_ref[:, A_V:A_VS] = v.astype(BF16)
    att_ref[:, A_VS:ATT_IN_W] = pltpu.roll(v, HALF, axis=1).astype(BF16)
    if cache_seq is not None:
        for t, dst in ((k.T, outs[3]), (v.T, outs[4])):
            for b in range(TOKEN_TILE // cache_seq):
                dst[b, 0] = t[:, b * cache_seq:(b + 1) * cache_seq].reshape(
                    N_KV_HEADS, HEAD_DIM, cache_seq)
    glaf_ref[:, G_Q:G_G] = _dot(h, win_ref[:, OFF_GQ:OFF_GV])
    log_sig = jnp.minimum(z, 0.0) - jnp.log(1.0 + jnp.exp(-jnp.abs(z)))
    glaf_ref[:, G_LA:GLA_F_W] = log_sig * (1.0 / GLA_TAU)
    gv_ref[...] = _dot(h, win_ref[:, OFF_GV:OFF_GG]).astype(BF16)
    glaf_ref[:, G_G:G_LA] = _dot(h, win_ref[:, OFF_GG:OFF_LR])


def _token_spec(width, first_tile=0):
    return pl.BlockSpec((TOKEN_TILE, width), lambda i: (i + first_tile, 0))


def _stack_spec(shape, j):
    return pl.BlockSpec((None,) + tuple(shape[1:]), lambda i: (j, 0, 0),
                        pipeline_mode=pl.Buffered(1))


def _tiles_per_seq(latent_len):
    return None if latent_len is None else latent_len // TOKEN_TILE


def _ffn_first(x_ctx, x_lat, mod, latent_len, npre, npost, w1, w3, w2, ada, cast):
    ctx_tiles = x_ctx.shape[0] // TOKEN_TILE
    steps = ctx_tiles + x_lat.shape[0] // TOKEN_TILE
    assert steps >= CAST_SLABS, "every slab needs its own grid step"
    cond, w_ada, b_ada = ada
    late_mods = N_MOD - EARLY_MODS
    ada_cols = late_mods * D_MODEL // steps
    per_mod = D_MODEL // ada_cols
    assert ada_cols * steps == late_mods * D_MODEL and ada_cols % LANES == 0
    first_slab = EARLY_MODS * per_mod
    in_specs = [pl.BlockSpec((TOKEN_TILE, D_MODEL), lambda i: (jnp.minimum(i, ctx_tiles - 1), 0)),
                pl.BlockSpec((TOKEN_TILE, D_MODEL), lambda i: (jnp.maximum(i - ctx_tiles, 0), 0)),
                _resident(mod.shape), _resident(npre.shape),
                _resident(npost.shape), _stack_spec(w1.shape, 0),
                _stack_spec(w3.shape, 0), _stack_spec(w2.shape, 0),
                _resident(cond.shape),
                pl.BlockSpec((D_MODEL, ada_cols), lambda i: (0, first_slab + i)),
                pl.BlockSpec((1, ada_cols), lambda i: (0, first_slab + i))]
    out_shape = [jax.ShapeDtypeStruct((steps * TOKEN_TILE, D_MODEL), F32),
                 jax.ShapeDtypeStruct((late_mods, MOD_ROWS, D_MODEL), F32)]
    out_specs = [_token_spec(D_MODEL),
                 pl.BlockSpec((None, MOD_ROWS, ada_cols), lambda i: (i // per_mod, 0, i % per_mod))]
    for stack, j in cast:
        _, rows, cols = stack.shape
        slab = rows // CAST_SLABS
        assert slab * CAST_SLABS == rows and slab % (2 * SUBLANES) == 0, "whole bf16 row tiles"
        in_specs.append(pl.BlockSpec((None, slab, cols), functools.partial(
            lambda i, j: (j, i * CAST_SLABS // steps, 0), j=j)))
        out_shape.append(jax.ShapeDtypeStruct((1, rows, cols), BF16))
        out_specs.append(pl.BlockSpec((None, slab, cols), lambda i: (0, i * CAST_SLABS // steps, 0)))
    return pl.pallas_call(
        functools.partial(_ffn_first_kernel, ctx_tiles=ctx_tiles,
                          tiles_per_seq=_tiles_per_seq(latent_len), n_cast=len(cast)),
        out_shape=out_shape,
        grid=(steps,),
        in_specs=in_specs,
        out_specs=out_specs,
        compiler_params=_params(1),
        name="ffn_first",
    )(x_ctx, x_lat, mod, npre, npost, w1, w3, w2, cond, w_ada, b_ada,
      *[stack for stack, _ in cast])


def _mix_ffn(x, first_tile, att, gla, w_out, mod, latent_len, npre, npost, w1, w3, w2):
    t = att.shape[0]
    return pl.pallas_call(
        functools.partial(_mix_ffn_kernel, tiles_per_seq=_tiles_per_seq(latent_len)),
        out_shape=jax.ShapeDtypeStruct((t, D_MODEL), F32),
        grid=(t // TOKEN_TILE,),
        in_specs=[_token_spec(D_MODEL, first_tile), _token_spec(ATT_Q), _token_spec(GLA_V),
                  _stack_spec(w_out.shape, 0), _resident(mod.shape), _resident(npre.shape),
                  _resident(npost.shape), _stack_spec(w1.shape, 0),
                  _stack_spec(w3.shape, 0), _stack_spec(w2.shape, 0)],
        out_specs=_token_spec(D_MODEL),
        compiler_params=_params(1),
        name="mix_ffn",
    )(x, att, gla, w_out, mod, npre, npost, w1, w3, w2)


def _project(x, first_tile, t, mod, latent_len, cache_seq, npre, w_in, w_up, b_up):
    tiles = _tiles_per_seq(latent_len)
    in_specs = [_token_spec(D_MODEL, first_tile), _resident(mod.shape), _resident(npre.shape),
                _stack_spec(w_in.shape, 0), _resident(w_up.shape), _resident(b_up.shape)]
    args = [x, mod, npre, w_in, w_up, b_up]
    if tiles is not None:
        in_specs += [pl.BlockSpec((TOKEN_TILE, LANES), lambda i: (i % tiles, 0))] * 3
        args += [jnp.asarray(tab) for tab in _rope_tables(latent_len)]
    outs = ((ATT_IN_W, BF16), (GLA_F_W, F32), (GLA_V, BF16))
    out_shape = [jax.ShapeDtypeStruct((t, w), dt) for w, dt in outs]
    out_specs = [_token_spec(w) for w, _ in outs]
    if cache_seq is not None:
        seqs = TOKEN_TILE // cache_seq
        cache = (t // cache_seq, 1, N_KV_HEADS, HEAD_DIM, cache_seq)
        out_shape += [jax.ShapeDtypeStruct(cache, F32)] * 2
        out_specs += [pl.BlockSpec((seqs,) + cache[1:], lambda i: (i, 0, 0, 0, 0))] * 2
    return pl.pallas_call(
        functools.partial(_proj_kernel, tiles_per_seq=tiles, cache_seq=cache_seq),
        out_shape=out_shape,
        grid=(t // TOKEN_TILE,),
        in_specs=in_specs,
        out_specs=out_specs,
        compiler_params=_params(1),
        name="project",
    )(*args)


def _rope_tables(seq_len):
    half = HEAD_DIM // 2
    inv_freq = np.float32(ROPE_BASE) ** (-np.arange(0, half, 2, dtype=np.float32) / half)
    pos = np.arange(seq_len)
    row = (pos // GRID_W).astype(np.float32)
    col = (pos % GRID_W).astype(np.float32)
    within = np.arange(LANES) % HEAD_DIM
    idx = within % half
    freq = inv_freq[idx % ROPE_PAIR].astype(np.float32)
    p = np.where((within // half == 0)[None, :], row[:, None], col[:, None])
    ang = (p * freq[None, :]).astype(np.float32)
    cos, sin = np.cos(ang).astype(np.float32), np.sin(ang).astype(np.float32)
    first = (idx < ROPE_PAIR)[None, :]
    zero = np.float32(0.0)
    return cos, np.where(first, -sin, zero), np.where(first, zero, sin)


def _attend(q_ref, q_rows, nq, segs, sink_ref):
    lane = lax.broadcasted_iota(jnp.int32, (1, LANES), 1)
    half_of = [lane < HALF, lane >= HALF]
    rows = lax.broadcasted_iota(jnp.int32, (2 * nq, 1), 0)
    zero = jnp.zeros((), BF16)
    out = [None] * (ATT_Q // LANES)
    for g in range(N_KV_HEADS):
        tiles = (2 * g, 2 * g + 1)
        for e in range(2):
            qm = jnp.concatenate(
                [jnp.where(half_of[e], q_ref[q_rows, j * LANES:(j + 1) * LANES], zero)
                 for j in tiles], axis=0)
            sink = jnp.where(rows < nq, sink_ref[2 * tiles[0] + e],
                             sink_ref[2 * tiles[1] + e]) * LOG2_E
            scores = []
            for k, k_sw, _, _, m in segs:
                s = _dot_t(qm, k if e == g else k_sw)
                scores.append(s if m is None else jnp.where(m, s, NEG))
            mx = sink
            for s in scores:
                mx = jnp.maximum(mx, jnp.max(s, axis=-1, keepdims=True))
            probs = [jnp.exp2(s - mx) for s in scores]
            den = jnp.exp2(sink - mx)
            for p in probs:
                den = den + jnp.sum(p, axis=-1, keepdims=True)
            o = None
            for p, (_, _, v, v_sw, _) in zip(probs, segs):
                part = _dot(p.astype(BF16), jnp.where(half_of[e], v if e == g else v_sw, zero))
                o = part if o is None else o + part
            o = o * (1.0 / den)
            for r, j in enumerate(tiles):
                blk = o[r * nq:(r + 1) * nq, :]
                out[j] = blk if out[j] is None else out[j] + blk
    return out


def _context_attn_kernel(sink_ref, a_ref, o_ref, *, seq_len):
    for s in range(a_ref.shape[0] // seq_len):
        rows = slice(s * seq_len, (s + 1) * seq_len)
        segs = [(a_ref[rows, A_K:A_KS], a_ref[rows, A_KS:A_V],
                 a_ref[rows, A_V:A_VS], a_ref[rows, A_VS:ATT_IN_W], None)]
        for j, t in enumerate(_attend(a_ref, rows, seq_len, segs, sink_ref)):
            o_ref[rows, j * LANES:(j + 1) * LANES] = t.astype(o_ref.dtype)


def _latent_attn_kernel(sink_ref, q_ref, kv_ref, kc_ref, vc_ref, o_ref, *, seq_len):
    k_ctx, v_ctx = kc_ref[0], vc_ref[0]
    ctx = (k_ctx.astype(BF16), pltpu.roll(k_ctx, HALF, axis=1).astype(BF16),
           v_ctx.astype(BF16), pltpu.roll(v_ctx, HALF, axis=1).astype(BF16), None)
    span = 3 * BLOCK
    per_step = q_ref.shape[0] // BLOCK
    for s in range(per_step):
        i = pl.program_id(1) * per_step + s
        start = pl.multiple_of(jnp.clip((i - 1) * BLOCK, 0, seq_len - span), BLOCK)
        kpos = start + lax.broadcasted_iota(jnp.int32, (1, span), 1)
        qpos = i * BLOCK + lax.broadcasted_iota(jnp.int32, (BLOCK, 1), 0)
        valid = jnp.abs(kpos - qpos) <= WINDOW
        valid2 = jnp.concatenate([valid, valid], axis=0)
        kv = kv_ref[pl.ds(start, span), :]
        segs = [tuple(kv[:, c * ATT_KV:(c + 1) * ATT_KV] for c in range(4)) + (valid2,), ctx]
        rows = slice(s * BLOCK, (s + 1) * BLOCK)
        for j, t in enumerate(_attend(q_ref, rows, BLOCK, segs, sink_ref)):
            o_ref[rows, j * LANES:(j + 1) * LANES] = t.astype(o_ref.dtype)


def _smem_spec():
    return pl.BlockSpec(memory_space=pltpu.SMEM)


def _context_attention(sink, att_in, batch, seq_len):
    rows = CTX_SEQS_PER_STEP * seq_len

    def seq(width):
        return pl.BlockSpec((rows, width), lambda b: (b, 0))
    return pl.pallas_call(
        functools.partial(_context_attn_kernel, seq_len=seq_len),
        out_shape=jax.ShapeDtypeStruct((batch * seq_len, ATT_Q), BF16),
        grid=(batch // CTX_SEQS_PER_STEP,),
        in_specs=[_smem_spec(), seq(ATT_IN_W)],
        out_specs=seq(ATT_Q),
        compiler_params=_params(1),
        name="context_attention",
    )(sink, att_in)


def _latent_attention(sink, att_in, k_ctx, v_ctx, batch, seq_len):
    steps = seq_len // (LAT_BLOCKS_PER_STEP * BLOCK)
    rows = LAT_BLOCKS_PER_STEP * BLOCK
    past = k_ctx.shape[1]
    assert A_K == ATT_IN_W - A_K, "q and the key / value columns are the two halves of att_in"
    return pl.pallas_call(
        functools.partial(_latent_attn_kernel, seq_len=seq_len),
        out_shape=jax.ShapeDtypeStruct((batch * seq_len, ATT_Q), BF16),
        grid=(batch, steps),
        in_specs=[_smem_spec(),
                  pl.BlockSpec((rows, ATT_Q), lambda b, i: (b * steps + i, 0)),
                  pl.BlockSpec((seq_len, ATT_IN_W - A_K), lambda b, i: (b, 1)),
                  pl.BlockSpec((1, past, ATT_KV), lambda b, i: (b, 0, 0)),
                  pl.BlockSpec((1, past, ATT_KV), lambda b, i: (b, 0, 0))],
        out_specs=pl.BlockSpec((rows, ATT_Q), lambda b, i: (b * steps + i, 0)),
        compiler_params=_params(2),
        name="latent_attention",
    )(sink, att_in, att_in, k_ctx, v_ctx)


def _split2(x):
    hi = x.astype(BF16)
    lo = (x - hi.astype(F32)).astype(BF16)
    return hi, lo


def _gla_kernel(*refs, n_chunks, has_s0, emit_state):
    refs = list(refs)
    gf_ref, gv_ref, gn_ref = refs[:3]
    pos = 3
    s0_refs = refs[pos:pos + 2] if has_s0 else None
    pos += 2 if has_s0 else 0
    o_ref = refs[pos]
    pos += 1
    sfin_refs = refs[pos:pos + 2] if emit_state else None
    pos += 2 if emit_state else 0
    cum_ref, kv_ref, sent_ref = refs[pos:]

    C = GLA_CHUNK
    n_pairs = GLA_QK // LANES
    lane = lax.broadcasted_iota(jnp.int32, (1, LANES), 1)
    half_of = [lane < HALF, lane >= HALF]
    r_i = lax.broadcasted_iota(jnp.int32, (C, C), 0)
    c_i = lax.broadcasted_iota(jnp.int32, (C, C), 1)
    lower = c_i <= r_i
    upper = c_i >= r_i
    tri = jnp.concatenate([jnp.where(lower, 1.0, 0.0), jnp.where(upper, 1.0, 0.0)],
                          axis=0).astype(BF16)
    zeros_cc = jnp.zeros((C, C), F32)
    gnorm = gn_ref[...]
    qscale = GLA_DK ** -0.5

    def one_sequence(s):
        def chunk_rows(n):
            return slice((s * n_chunks + n) * C, (s * n_chunks + n + 1) * C)

        for n in range(n_chunks):
            rows = chunk_rows(n)
            hi, lo = _split2(gf_ref[rows, G_LA:GLA_F_W])
            sums = _dot(tri, jnp.concatenate([hi, lo], axis=1))
            cum_f = sums[:C, 0:GLA_QK] + sums[:C, 2 * GLA_QK:3 * GLA_QK]
            cum_b = sums[C:, GLA_QK:2 * GLA_QK] + sums[C:, 3 * GLA_QK:]
            cum_ref[0, rows, :] = cum_f
            cum_ref[1, rows, :] = cum_b
            k = gf_ref[rows, G_K:G_G]
            k_in = (k * jnp.exp(cum_f[C - 1:C, :] - cum_f), k * jnp.exp(cum_b[0:1, :] - cum_b))
            for p in range(n_pairs):
                sl = slice(p * LANES, (p + 1) * LANES)
                kv_t = None
                for e in range(2):
                    h = 2 * p + e
                    v_t = gv_ref[rows, h * GLA_DV:(h + 1) * GLA_DV].T
                    k_e = jnp.concatenate(
                        [jnp.where(half_of[e], k_in[d][:, sl], 0.0) for d in range(2)], axis=1)
                    part = _dot(v_t, k_e.astype(BF16))
                    kv_t = part if kv_t is None else kv_t + part
                kv_ref[s, n, p] = kv_t

        for p in range(n_pairs):
            sl = slice(p * LANES, (p + 1) * LANES)
            st = []
            for d in range(2):
                if has_s0:
                    s0 = s0_refs[d][s, 0, 2 * p:2 * p + 2, :, :].reshape(2 * GLA_DK, GLA_DV)
                    st.append(s0.T)
                else:
                    st.append(jnp.zeros((GLA_DV, 2 * GLA_DK), F32))
            for i in range(n_chunks):
                for d, n in ((0, i), (1, n_chunks - 1 - i)):
                    tot_row = (s * n_chunks + n) * C + (C - 1 if d == 0 else 0)
                    decay = jnp.exp(cum_ref[d, tot_row:tot_row + 1, sl])
                    sent_ref[s, n, p, :, d * LANES:(d + 1) * LANES] = st[d].astype(BF16)
                    st[d] = decay * st[d] + kv_ref[s, n, p, :, d * LANES:(d + 1) * LANES]
            if emit_state:
                for d in range(2):
                    sfin_refs[d][s, 0, 2 * p:2 * p + 2, :, :] = st[d].T.reshape(2, GLA_DK, GLA_DV)

        for n in range(n_chunks):
            rows = chunk_rows(n)
            q = gf_ref[rows, G_Q:G_K] * qscale
            k = gf_ref[rows, G_K:G_G]
            qs, ks, qin = [], [], []
            for d in range(2):
                cum = cum_ref[d, rows, :]
                ref = cum[C // 2:C // 2 + 1, :]
                qs.append(q * jnp.exp(cum - ref))
                ks.append((k * jnp.exp(ref - cum)).astype(BF16))
                qin.append(q * jnp.exp(cum))
            for p in range(n_pairs):
                sl = slice(p * LANES, (p + 1) * LANES)
                lhs = jnp.concatenate(
                    [jnp.where(half_of[e], qs[d][:, sl], 0.0) for e in range(2) for d in range(2)],
                    axis=0).astype(BF16)
                sc = _dot_t(lhs, jnp.concatenate([ks[0][:, sl], ks[1][:, sl]], axis=0))
                prob = []
                for e in range(2):
                    s_f = sc[(2 * e) * C:(2 * e + 1) * C, :C]
                    s_b = sc[(2 * e + 1) * C:(2 * e + 2) * C, C:]
                    prob.append(jnp.where(lower, s_f, 0.0) + jnp.where(upper, s_b, 0.0))
                p_blk = jnp.concatenate(
                    [jnp.concatenate([prob[0], zeros_cc], axis=1),
                     jnp.concatenate([zeros_cc, prob[1]], axis=1)], axis=0).astype(BF16)
                v2 = jnp.concatenate(
                    [gv_ref[rows, (2 * p + e) * GLA_DV:(2 * p + e + 1) * GLA_DV] for e in range(2)],
                    axis=0)
                q_in = jnp.concatenate(
                    [jnp.concatenate([jnp.where(half_of[e], qin[d][:, sl], 0.0) for d in range(2)],
                                     axis=1) for e in range(2)], axis=0).astype(BF16)
                o2 = _dot(p_blk, v2) + _dot_t(q_in, sent_ref[s, n, p])
                for e in range(2):
                    h = 2 * p + e
                    o = o2[e * C:(e + 1) * C, :]
                    o = o * lax.rsqrt(jnp.mean(o * o, axis=-1, keepdims=True) + EPS) * gnorm
                    gate = gf_ref[rows, G_G + h * GLA_DV:G_G + (h + 1) * GLA_DV]
                    o_ref[rows, h * GLA_DV:(h + 1) * GLA_DV] = (o * _silu(gate)).astype(o_ref.dtype)

    for s in range(kv_ref.shape[0]):
        one_sequence(s)


def _gla(gla_f, gv, gnorm, batch, seq_len, seqs_per_step, s0=None, emit_state=False):
    n_chunks = seq_len // GLA_CHUNK
    has_s0 = s0 is not None
    rows = seqs_per_step * seq_len

    def seq(width):
        return pl.BlockSpec((rows, width), lambda b: (b, 0))
    state_spec = pl.BlockSpec((seqs_per_step, 1, GLA_HEADS, GLA_DK, GLA_DV),
                              lambda b: (b, 0, 0, 0, 0))
    in_specs = [seq(GLA_F_W), seq(GLA_V), pl.BlockSpec((1, GLA_DV), lambda b: (0, 0))]
    args = [gla_f, gv, gnorm]
    if has_s0:
        in_specs += [state_spec, state_spec]
        args += list(s0)
    out_shape = [jax.ShapeDtypeStruct((batch * seq_len, GLA_V), BF16)]
    out_specs = [seq(GLA_V)]
    if emit_state:
        out_shape += [jax.ShapeDtypeStruct((batch, 1, GLA_HEADS, GLA_DK, GLA_DV), F32)] * 2
        out_specs += [state_spec, state_spec]
    n_pairs = GLA_QK // LANES
    return pl.pallas_call(
        functools.partial(_gla_kernel, n_chunks=n_chunks, has_s0=has_s0, emit_state=emit_state),
        out_shape=out_shape,
        grid=(batch // seqs_per_step,),
        in_specs=in_specs,
        out_specs=out_specs,
        scratch_shapes=[pltpu.VMEM((2, rows, GLA_QK), F32),
                        pltpu.VMEM((seqs_per_step, n_chunks, n_pairs, GLA_DV, 2 * LANES), F32),
                        pltpu.VMEM((seqs_per_step, n_chunks, n_pairs, GLA_DV, 2 * LANES), BF16)],
        compiler_params=_params(1),
        name="gla",
    )(*args)


def kernel(x_prompt, x_sample, cache_k, cache_v, state_gla_fwd, state_gla_bwd, c, c_ctx,
           w_ada, b_ada, norm_pre, norm_post, ffn_w1, ffn_w3, ffn_w2, w_in,
           gla_w_up, gla_b_up, gla_norm, attn_sink, w_out):
    depth = w_in.shape[0]
    assert depth == 1, "single trunk layer"
    batch, seq = x_prompt.shape[0], x_prompt.shape[1]
    dec_batch, dec_seq = x_sample.shape[0], x_sample.shape[1]
    past = cache_k.shape[2]
    l = 0

    cond = jnp.concatenate(
        [c_ctx[None, :], c, jnp.zeros((MOD_ROWS - 1 - dec_batch, D_MODEL), F32)], axis=0)
    ada = (cond, w_ada[l], b_ada[l][None, :])
    mod_early = _ada_modulation(*ada, EARLY_MODS)

    npre, npost = norm_pre[l], norm_post[l]
    ffn_first_b = [w[l, :1].astype(BF16) for w in (ffn_w1, ffn_w3, ffn_w2)]
    ffn_second_f32 = [(w[l], 1) for w in (ffn_w1, ffn_w3, ffn_w2)]
    w_in_b = w_in[l:l + 1].astype(BF16)
    w_out_b = w_out[l:l + 1].astype(BF16)
    zeros = jnp.zeros((GLA_LOW_RANK, GLA_QK), F32)
    w_up = jnp.concatenate(
        [jnp.concatenate([gla_w_up[l, 0], zeros], axis=1),
         jnp.concatenate([zeros, gla_w_up[l, 1]], axis=1)], axis=0).astype(BF16)
    b_up = gla_b_up[l].reshape(1, 2 * GLA_QK)
    gnorm = gla_norm[l][None, :]
    sink = attn_sink[l]

    x1, mod, *ffn_second_b = _ffn_first(
        x_prompt.reshape(batch * seq, D_MODEL), x_sample.reshape(dec_batch * dec_seq, D_MODEL),
        mod_early, dec_seq, npre, npost, *ffn_first_b, ada=ada, cast=ffn_second_f32)
    ctx_tiles = batch * seq // TOKEN_TILE

    def trunk(latent):
        n_batch, n_seq = (dec_batch, dec_seq) if latent else (batch, seq)
        latent_len = n_seq if latent else None
        first_tile = ctx_tiles if latent else 0
        att_in, gla_f, gv, *cache_t = _project(
            x1, first_tile, n_batch * n_seq, mod, latent_len, None if latent else n_seq, npre,
            w_in_b, w_up, b_up)
        if latent:
            att = _latent_attention(sink, att_in, cache_k[:, l].reshape(dec_batch, past, ATT_KV),
                                    cache_v[:, l].reshape(dec_batch, past, ATT_KV), n_batch, n_seq)
            (gla,) = _gla(gla_f, gv, gnorm, n_batch, n_seq, 1,
                          s0=(state_gla_fwd[:, l:l + 1], state_gla_bwd[:, l:l + 1]))
            extras = ()
        else:
            att = _context_attention(sink, att_in, n_batch, n_seq)
            gla, s_f, s_b = _gla(gla_f, gv, gnorm, n_batch, n_seq, CTX_SEQS_PER_STEP,
                                 emit_state=True)
            k_new, v_new = (jnp.transpose(c_t, (0, 1, 4, 2, 3)) for c_t in cache_t)
            extras = (k_new, v_new, s_f, s_b)
        y = _mix_ffn(x1, first_tile, att, gla, w_out_b, mod, latent_len, npre, npost,
                     *ffn_second_b)
        return y.reshape(n_batch, n_seq, D_MODEL), extras

    y_prompt, (k_new, v_new, s_f, s_b) = trunk(False)
    y_sample, _ = trunk(True)
    return (y_prompt, y_sample, k_new, v_new, s_f, s_b)
```

```python
import functools

import numpy as np
import jax
import jax.numpy as jnp
from jax import lax
from jax.experimental import pallas as pl
from jax.experimental.pallas import tpu as pltpu

F32 = jnp.float32
BF16 = jnp.bfloat16

D_MODEL = 1024
GRID_W = 64
N_Q_HEADS = 8
N_KV_HEADS = 2
HEAD_DIM = 64
WINDOW = 128
BLOCK = 128
ROPE_BASE = 10000.0
GLA_HEADS = 4
GLA_DK = 64
GLA_DV = 128
GLA_LOW_RANK = 16
GLA_TAU = 16.0
D_FF = 2816
N_MOD = 9
EPS = 1e-6
NEG = -1e30

ATT_Q = N_Q_HEADS * HEAD_DIM
ATT_KV = N_KV_HEADS * HEAD_DIM
GLA_QK = GLA_HEADS * GLA_DK
GLA_V = GLA_HEADS * GLA_DV
OFF_Q = 0
OFF_K = OFF_Q + ATT_Q
OFF_V = OFF_K + ATT_KV
OFF_GQ = OFF_V + ATT_KV
OFF_GK = OFF_GQ + GLA_QK
OFF_GV = OFF_GK + GLA_QK
OFF_GG = OFF_GV + GLA_V
OFF_LR = OFF_GG + GLA_V
IN_WIDTH = OFF_LR + 2 * GLA_LOW_RANK
A_Q = 0
A_K = A_Q + ATT_Q
A_KS = A_K + ATT_KV
A_V = A_KS + ATT_KV
A_VS = A_V + ATT_KV
ATT_IN_W = A_VS + ATT_KV
G_Q = 0
G_K = G_Q + GLA_QK
G_G = G_K + GLA_QK
G_LA = G_G + GLA_V
GLA_F_W = G_LA + 2 * GLA_QK
ROPE_PAIR = HEAD_DIM // 4
LOG2_E = 1.4426950408889634
ATT_Q_SCALE = HEAD_DIM ** -0.5 * LOG2_E

LANES = 128
SUBLANES = 8
HALF = LANES // 2
VMEM_LIMIT = 56 * 1024 * 1024

TOKEN_TILE = 512
FF_SPLITS = (0, 1536, D_FF)
EARLY_MODS = 3
CAST_SLABS = 16
ADA_K_TILE = 512
CTX_SEQS_PER_STEP = 4
LAT_BLOCKS_PER_STEP = 4
GLA_CHUNK = 128
MOD_ROWS = 8


def _params(n_axes):
    return pltpu.CompilerParams(
        dimension_semantics=("arbitrary",) * n_axes, vmem_limit_bytes=VMEM_LIMIT)


def _resident(shape):
    zeros = (0,) * len(shape)
    return pl.BlockSpec(shape, lambda *_: zeros, pipeline_mode=pl.Buffered(1))


def _sigmoid(x):
    return 1.0 / (1.0 + jnp.exp(-x))


def _silu(x):
    return x * _sigmoid(x)


def _rms(x, g):
    return x * lax.rsqrt(jnp.mean(x * x, axis=-1, keepdims=True) + EPS) * g


def _dot(a, b):
    return jnp.dot(a, b, preferred_element_type=F32)


def _dot_t(a, b):
    return lax.dot_general(a, b, (((1,), (1,)), ((), ())), preferred_element_type=F32)


def _ada_kernel(cond_ref, w_ref, b_ref, o_ref):
    n_mod = o_ref.shape[0]

    @pl.when(pl.program_id(0) == 0)
    def _():
        for m in range(n_mod):
            o_ref[m] = jnp.broadcast_to(b_ref[:, m * D_MODEL:(m + 1) * D_MODEL],
                                        (MOD_ROWS, D_MODEL))
    part = _dot(_silu(cond_ref[...]).astype(BF16), w_ref[...].astype(BF16))
    for m in range(n_mod):
        o_ref[m] += part[:, m * D_MODEL:(m + 1) * D_MODEL]


def _ada_modulation(cond, w_ada, b_ada, n_mod):
    k_dim = w_ada.shape[0]
    n = n_mod * D_MODEL
    return pl.pallas_call(
        _ada_kernel,
        out_shape=jax.ShapeDtypeStruct((n_mod, MOD_ROWS, D_MODEL), F32),
        grid=(k_dim // ADA_K_TILE,),
        in_specs=[
            pl.BlockSpec((MOD_ROWS, ADA_K_TILE), lambda k: (0, k)),
            pl.BlockSpec((ADA_K_TILE, n), lambda k: (k, 0)),
            pl.BlockSpec((1, n), lambda k: (0, 0)),
        ],
        out_specs=pl.BlockSpec((n_mod, MOD_ROWS, D_MODEL), lambda k: (0, 0, 0)),
        compiler_params=_params(1),
        name="ada_modulation",
    )(cond, w_ada, b_ada)


class _Mod:
    def __init__(self, mod_ref, tiles_per_seq, first_latent_tile=0, first_mod=0):
        self.ref = mod_ref
        self.first_mod = first_mod
        if tiles_per_seq is None:
            self.row = 0
        else:
            tile = pl.program_id(0) - first_latent_tile
            self.row = jnp.where(tile >= 0, 1 + tile // tiles_per_seq, 0)

    def __getitem__(self, m):
        return self.ref[m - self.first_mod, pl.ds(self.row, 1), :]


def _modulated(x, mod, npre_ref, i):
    return _rms(x, npre_ref[i:i + 1, :]) * (1.0 + mod[3 * i + 1]) + mod[3 * i]


def _residual(x, out, mod, npost_ref, i, weight):
    return x + (weight * mod[3 * i + 2]) * _rms(out, npost_ref[i:i + 1, :])


def _ffn_sublayer(x, i, mod, npre_ref, npost_ref, w1_ref, w3_ref, w2_ref):
    h = _modulated(x, mod, npre_ref, i).astype(BF16)
    acc = None
    for lo, hi in zip(FF_SPLITS[:-1], FF_SPLITS[1:]):
        a = _dot(h, w1_ref[:, lo:hi])
        g = _dot(h, w3_ref[:, lo:hi])
        part = _dot((_silu(a) * g).astype(BF16), w2_ref[lo:hi, :])
        acc = part if acc is None else acc + part
    return _residual(x, acc, mod, npost_ref, i, 0.5)


def _ffn_first_kernel(*refs, ctx_tiles, tiles_per_seq, n_cast):
    xc_ref, xl_ref, mod_ref, npre_ref, npost_ref, w1_ref, w3_ref, w2_ref = refs[:8]
    cond_ref, wada_ref, bada_ref = refs[8:11]
    cast_src = refs[11:11 + n_cast]
    o_ref, late_mod_ref = refs[11 + n_cast:13 + n_cast]
    cast_dst = refs[13 + n_cast:]
    mod = _Mod(mod_ref, tiles_per_seq, ctx_tiles)
    x = jnp.where(pl.program_id(0) >= ctx_tiles, xl_ref[...], xc_ref[...])
    o_ref[...] = _ffn_sublayer(x, 0, mod, npre_ref, npost_ref, w1_ref, w3_ref, w2_ref)
    for src, dst in zip(cast_src, cast_dst):
        dst[...] = src[...].astype(BF16)
    late_mod_ref[...] = _dot(_silu(cond_ref[...]).astype(BF16),
                             wada_ref[...].astype(BF16)) + bada_ref[...]


def _mix_ffn_kernel(x_ref, att_ref, gla_ref, wo_ref, mod_ref, npre_ref, npost_ref,
                    w1_ref, w3_ref, w2_ref, o_ref, *, tiles_per_seq):
    mod = _Mod(mod_ref, tiles_per_seq, first_mod=EARLY_MODS)
    mix = _dot(att_ref[...], wo_ref[0:ATT_Q, :]) + _dot(gla_ref[...], wo_ref[ATT_Q:, :])
    x = _residual(x_ref[...], mix, mod, npost_ref, 1, 1.0)
    o_ref[...] = _ffn_sublayer(x, 2, mod, npre_ref, npost_ref, w1_ref, w3_ref, w2_ref)


def _rope_tile(x, cos, sin_up, sin_dn):
    up = pltpu.roll(x, LANES - ROPE_PAIR, axis=1)
    dn = pltpu.roll(x, ROPE_PAIR, axis=1)
    return x * cos + up * sin_up + dn * sin_dn


def _proj_kernel(*refs, tiles_per_seq, cache_seq):
    x_ref, mod_ref, npre_ref, win_ref, wup_ref, bup_ref = refs[:6]
    rope_refs = refs[6:9] if tiles_per_seq is not None else None
    outs = refs[6 + (3 if rope_refs else 0):]
    att_ref, glaf_ref, gv_ref = outs[:3]
    mod = _Mod(mod_ref, tiles_per_seq, first_mod=EARLY_MODS)
    h = _modulated(x_ref[...], mod, npre_ref, 1).astype(BF16)
    q = _dot(h, win_ref[:, OFF_Q:OFF_K])
    kv = _dot(h, win_ref[:, OFF_K:OFF_GQ])
    k, v = kv[:, :ATT_KV], kv[:, ATT_KV:]
    lr = _dot(h, win_ref[:, OFF_LR:IN_WIDTH])
    z = _dot(lr.astype(BF16), wup_ref[...]) + bup_ref[...]
    if rope_refs:
        cos, sup, sdn = (r[...] for r in rope_refs)
        q = jnp.concatenate([_rope_tile(q[:, j * LANES:(j + 1) * LANES], cos, sup, sdn)
                             for j in range(ATT_Q // LANES)], axis=1)
        k_att = _rope_tile(k, cos, sup, sdn)
    else:
        k_att = k
    att_ref[:, A_Q:A_K] = (q * ATT_Q_SCALE).astype(BF16)
    att_ref[:, A_K:A_KS] = k_att.astype(BF16)
    att_ref[:, A_KS:A_V] = pltpu.roll(k_att, HALF, axis=1).astype(BF16)
    att_ref[:, A_V:A_VS] = v.astype(BF16)
    att_ref[:, A_VS:ATT_IN_W] = pltpu.roll(v, HALF, axis=1).astype(BF16)
    if cache_seq is not None:
        for t, dst in ((k.T, outs[3]), (v.T, outs[4])):
            for b in range(TOKEN_TILE // cache_seq):
                dst[b, 0] = t[:, b * cache_seq:(b + 1) * cache_seq].reshape(
                    N_KV_HEADS, HEAD_DIM, cache_seq)
    glaf_ref[:, G_Q:G_G] = _dot(h, win_ref[:, OFF_GQ:OFF_GV])
    log_sig = jnp.minimum(z, 0.0) - jnp.log(1.0 + jnp.exp(-jnp.abs(z)))
    glaf_ref[:, G_LA:GLA_F_W] = log_sig * (1.0 / GLA_TAU)
    gv_ref[...] = _dot(h, win_ref[:, OFF_GV:OFF_GG]).astype(BF16)
    glaf_ref[:, G_G:G_LA] = _dot(h, win_ref[:, OFF_GG:OFF_LR])


def _token_spec(width, first_tile=0):
    return pl.BlockSpec((TOKEN_TILE, width), lambda i: (i + first_tile, 0))


def _stack_spec(shape, j):
    return pl.BlockSpec((None,) + tuple(shape[1:]), lambda i: (j, 0, 0),
                        pipeline_mode=pl.Buffered(1))


def _tiles_per_seq(latent_len):
    return None if latent_len is None else latent_len // TOKEN_TILE


def _ffn_first(x_ctx, x_lat, mod, latent_len, npre, npost, w1, w3, w2, ada, cast):
    ctx_tiles = x_ctx.shape[0] // TOKEN_TILE
    steps = ctx_tiles + x_lat.shape[0] // TOKEN_TILE
    assert steps >= CAST_SLABS, "every slab needs its own grid step"
    cond, w_ada, b_ada = ada
    late_mods = N_MOD - EARLY_MODS
    ada_cols = late_mods * D_MODEL // steps
    per_mod = D_MODEL // ada_cols
    assert ada_cols * steps == late_mods * D_MODEL and ada_cols % LANES == 0
    first_slab = EARLY_MODS * per_mod
    in_specs = [pl.BlockSpec((TOKEN_TILE, D_MODEL), lambda i: (jnp.minimum(i, ctx_tiles - 1), 0)),
                pl.BlockSpec((TOKEN_TILE, D_MODEL), lambda i: (jnp.maximum(i - ctx_tiles, 0), 0)),
                _resident(mod.shape), _resident(npre.shape),
                _resident(npost.shape), _stack_spec(w1.shape, 0),
                _stack_spec(w3.shape, 0), _stack_spec(w2.shape, 0),
                _resident(cond.shape),
                pl.BlockSpec((D_MODEL, ada_cols), lambda i: (0, first_slab + i)),
                pl.BlockSpec((1, ada_cols), lambda i: (0, first_slab + i))]
    out_shape = [jax.ShapeDtypeStruct((steps * TOKEN_TILE, D_MODEL), F32),
                 jax.ShapeDtypeStruct((late_mods, MOD_ROWS, D_MODEL), F32)]
    out_specs = [_token_spec(D_MODEL),
                 pl.BlockSpec((None, MOD_ROWS, ada_cols), lambda i: (i // per_mod, 0, i % per_mod))]
    for stack, j in cast:
        _, rows, cols = stack.shape
        slab = rows // CAST_SLABS
        assert slab * CAST_SLABS == rows and slab % (2 * SUBLANES) == 0, "whole bf16 row tiles"
        in_specs.append(pl.BlockSpec((None, slab, cols), functools.partial(
            lambda i, j: (j, i * CAST_SLABS // steps, 0), j=j)))
        out_shape.append(jax.ShapeDtypeStruct((1, rows, cols), BF16))
        out_specs.append(pl.BlockSpec((None, slab, cols), lambda i: (0, i * CAST_SLABS // steps, 0)))
    return pl.pallas_call(
        functools.partial(_ffn_first_kernel, ctx_tiles=ctx_tiles,
                          tiles_per_seq=_tiles_per_seq(latent_len), n_cast=len(cast)),
        out_shape=out_shape,
        grid=(steps,),
        in_specs=in_specs,
        out_specs=out_specs,
        compiler_params=_params(1),
        name="ffn_first",
    )(x_ctx, x_lat, mod, npre, npost, w1, w3, w2, cond, w_ada, b_ada,
      *[stack for stack, _ in cast])


def _mix_ffn(x, first_tile, att, gla, w_out, mod, latent_len, npre, npost, w1, w3, w2):
    t = att.shape[0]
    return pl.pallas_call(
        functools.partial(_mix_ffn_kernel, tiles_per_seq=_tiles_per_seq(latent_len)),
        out_shape=jax.ShapeDtypeStruct((t, D_MODEL), F32),
        grid=(t // TOKEN_TILE,),
        in_specs=[_token_spec(D_MODEL, first_tile), _token_spec(ATT_Q), _token_spec(GLA_V),
                  _stack_spec(w_out.shape, 0), _resident(mod.shape), _resident(npre.shape),
                  _resident(npost.shape), _stack_spec(w1.shape, 0),
                  _stack_spec(w3.shape, 0), _stack_spec(w2.shape, 0)],
        out_specs=_token_spec(D_MODEL),
        compiler_params=_params(1),
        name="mix_ffn",
    )(x, att, gla, w_out, mod, npre, npost, w1, w3, w2)


def _project(x, first_tile, t, mod, latent_len, cache_seq, npre, w_in, w_up, b_up):
    tiles = _tiles_per_seq(latent_len)
    in_specs = [_token_spec(D_MODEL, first_tile), _resident(mod.shape), _resident(npre.shape),
                _stack_spec(w_in.shape, 0), _resident(w_up.shape), _resident(b_up.shape)]
    args = [x, mod, npre, w_in, w_up, b_up]
    if tiles is not None:
        in_specs += [pl.BlockSpec((TOKEN_TILE, LANES), lambda i: (i % tiles, 0))] * 3
        args += [jnp.asarray(tab) for tab in _rope_tables(latent_len)]
    outs = ((ATT_IN_W, BF16), (GLA_F_W, F32), (GLA_V, BF16))
    out_shape = [jax.ShapeDtypeStruct((t, w), dt) for w, dt in outs]
    out_specs = [_token_spec(w) for w, _ in outs]
    if cache_seq is not None:
        seqs = TOKEN_TILE // cache_seq
        cache = (t // cache_seq, 1, N_KV_HEADS, HEAD_DIM, cache_seq)
        out_shape += [jax.ShapeDtypeStruct(cache, F32)] * 2
        out_specs += [pl.BlockSpec((seqs,) + cache[1:], lambda i: (i, 0, 0, 0, 0))] * 2
    return pl.pallas_call(
        functools.partial(_proj_kernel, tiles_per_seq=tiles, cache_seq=cache_seq),
        out_shape=out_shape,
        grid=(t // TOKEN_TILE,),
        in_specs=in_specs,
        out_specs=out_specs,
        compiler_params=_params(1),
        name="project",
    )(*args)


def _rope_tables(seq_len):
    half = HEAD_DIM // 2
    inv_freq = np.float32(ROPE_BASE) ** (-np.arange(0, half, 2, dtype=np.float32) / half)
    pos = np.arange(seq_len)
    row = (pos // GRID_W).astype(np.float32)
    col = (pos % GRID_W).astype(np.float32)
    within = np.arange(LANES) % HEAD_DIM
    idx = within % half
    freq = inv_freq[idx % ROPE_PAIR].astype(np.float32)
    p = np.where((within // half == 0)[None, :], row[:, None], col[:, None])
    ang = (p * freq[None, :]).astype(np.float32)
    cos, sin = np.cos(ang).astype(np.float32), np.sin(ang).astype(np.float32)
    first = (idx < ROPE_PAIR)[None, :]
    zero = np.float32(0.0)
    return cos, np.where(first, -sin, zero), np.where(first, zero, sin)


def _attend(q_ref, q_rows, nq, segs, sink_ref):
    lane = lax.broadcasted_iota(jnp.int32, (1, LANES), 1)
    half_of = [lane < HALF, lane >= HALF]
    rows = lax.broadcasted_iota(jnp.int32, (2 * nq, 1), 0)
    zero = jnp.zeros((), BF16)
    out = [None] * (ATT_Q // LANES)
    for g in range(N_KV_HEADS):
        tiles = (2 * g, 2 * g + 1)
        for e in range(2):
            qm = jnp.concatenate(
                [jnp.where(half_of[e], q_ref[q_rows, j * LANES:(j + 1) * LANES], zero)
                 for j in tiles], axis=0)
            sink = jnp.where(rows < nq, sink_ref[2 * tiles[0] + e],
                             sink_ref[2 * tiles[1] + e]) * LOG2_E
            scores = []
            for k, k_sw, _, _, m in segs:
                s = _dot_t(qm, k if e == g else k_sw)
                scores.append(s if m is None else jnp.where(m, s, NEG))
            mx = sink
            for s in scores:
                mx = jnp.maximum(mx, jnp.max(s, axis=-1, keepdims=True))
            probs = [jnp.exp2(s - mx) for s in scores]
            den = jnp.exp2(sink - mx)
            for p in probs:
                den = den + jnp.sum(p, axis=-1, keepdims=True)
            o = None
            for p, (_, _, v, v_sw, _) in zip(probs, segs):
                part = _dot(p.astype(BF16), jnp.where(half_of[e], v if e == g else v_sw, zero))
                o = part if o is None else o + part
            o = o * (1.0 / den)
            for r, j in enumerate(tiles):
                blk = o[r * nq:(r + 1) * nq, :]
                out[j] = blk if out[j] is None else out[j] + blk
    return out


def _context_attn_kernel(sink_ref, a_ref, o_ref, *, seq_len):
    for s in range(a_ref.shape[0] // seq_len):
        rows = slice(s * seq_len, (s + 1) * seq_len)
        segs = [(a_ref[rows, A_K:A_KS], a_ref[rows, A_KS:A_V],
                 a_ref[rows, A_V:A_VS], a_ref[rows, A_VS:ATT_IN_W], None)]
        for j, t in enumerate(_attend(a_ref, rows, seq_len, segs, sink_ref)):
            o_ref[rows, j * LANES:(j + 1) * LANES] = t.astype(o_ref.dtype)


def _latent_attn_kernel(sink_ref, q_ref, kv_ref, kc_ref, vc_ref, o_ref, *, seq_len):
    k_ctx, v_ctx = kc_ref[0], vc_ref[0]
    ctx = (k_ctx.astype(BF16), pltpu.roll(k_ctx, HALF, axis=1).astype(BF16),
           v_ctx.astype(BF16), pltpu.roll(v_ctx, HALF, axis=1).astype(BF16), None)
    span = 3 * BLOCK
    per_step = q_ref.shape[0] // BLOCK
    for s in range(per_step):
        i = pl.program_id(1) * per_step + s
        start = pl.multiple_of(jnp.clip((i - 1) * BLOCK, 0, seq_len - span), BLOCK)
        kpos = start + lax.broadcasted_iota(jnp.int32, (1, span), 1)
        qpos = i * BLOCK + lax.broadcasted_iota(jnp.int32, (BLOCK, 1), 0)
        valid = jnp.abs(kpos - qpos) <= WINDOW
        valid2 = jnp.concatenate([valid, valid], axis=0)
        kv = kv_ref[pl.ds(start, span), :]
        segs = [tuple(kv[:, c * ATT_KV:(c + 1) * ATT_KV] for c in range(4)) + (valid2,), ctx]
        rows = slice(s * BLOCK, (s + 1) * BLOCK)
        for j, t in enumerate(_attend(q_ref, rows, BLOCK, segs, sink_ref)):
            o_ref[rows, j * LANES:(j + 1) * LANES] = t.astype(o_ref.dtype)


def _smem_spec():
    return pl.BlockSpec(memory_space=pltpu.SMEM)


def _context_attention(sink, att_in, batch, seq_len):
    rows = CTX_SEQS_PER_STEP * seq_len

    def seq(width):
        return pl.BlockSpec((rows, width), lambda b: (b, 0))
    return pl.pallas_call(
        functools.partial(_context_attn_kernel, seq_len=seq_len),
        out_shape=jax.ShapeDtypeStruct((batch * seq_len, ATT_Q), BF16),
        grid=(batch // CTX_SEQS_PER_STEP,),
        in_specs=[_smem_spec(), seq(ATT_IN_W)],
        out_specs=seq(ATT_Q),
        compiler_params=_params(1),
        name="context_attention",
    )(sink, att_in)


def _latent_attention(sink, att_in, k_ctx, v_ctx, batch, seq_len):
    steps = seq_len // (LAT_BLOCKS_PER_STEP * BLOCK)
    rows = LAT_BLOCKS_PER_STEP * BLOCK
    past = k_ctx.shape[1]
    assert A_K == ATT_IN_W - A_K, "q and the key / value columns are the two halves of att_in"
    return pl.pallas_call(
        functools.partial(_latent_attn_kernel, seq_len=seq_len),
        out_shape=jax.ShapeDtypeStruct((batch * seq_len, ATT_Q), BF16),
        grid=(batch, steps),
        in_specs=[_smem_spec(),
                  pl.BlockSpec((rows, ATT_Q), lambda b, i: (b * steps + i, 0)),
                  pl.BlockSpec((seq_len, ATT_IN_W - A_K), lambda b, i: (b, 1)),
                  pl.BlockSpec((1, past, ATT_KV), lambda b, i: (b, 0, 0)),
                  pl.BlockSpec((1, past, ATT_KV), lambda b, i: (b, 0, 0))],
        out_specs=pl.BlockSpec((rows, ATT_Q), lambda b, i: (b * steps + i, 0)),
        compiler_params=_params(2),
        name="latent_attention",
    )(sink, att_in, att_in, k_ctx, v_ctx)


def _split2(x):
    hi = x.astype(BF16)
    lo = (x - hi.astype(F32)).astype(BF16)
    return hi, lo


def _gla_kernel(*refs, n_chunks, has_s0, emit_state):
    refs = list(refs)
    gf_ref, gv_ref, gn_ref = refs[:3]
    pos = 3
    s0_refs = refs[pos:pos + 2] if has_s0 else None
    pos += 2 if has_s0 else 0
    o_ref = refs[pos]
    pos += 1
    sfin_refs = refs[pos:pos + 2] if emit_state else None
    pos += 2 if emit_state else 0
    cum_ref, kv_ref, sent_ref = refs[pos:]

    C = GLA_CHUNK
    n_pairs = GLA_QK // LANES
    lane = lax.broadcasted_iota(jnp.int32, (1, LANES), 1)
    half_of = [lane < HALF, lane >= HALF]
    r_i = lax.broadcasted_iota(jnp.int32, (C, C), 0)
    c_i = lax.broadcasted_iota(jnp.int32, (C, C), 1)
    lower = c_i <= r_i
    upper = c_i >= r_i
    tri = jnp.concatenate([jnp.where(lower, 1.0, 0.0), jnp.where(upper, 1.0, 0.0)],
                          axis=0).astype(BF16)
    zeros_cc = jnp.zeros((C, C), F32)
    gnorm = gn_ref[...]
    qscale = GLA_DK ** -0.5

    def one_sequence(s):
        def chunk_rows(n):
            return slice((s * n_chunks + n) * C, (s * n_chunks + n + 1) * C)

        for n in range(n_chunks):
            rows = chunk_rows(n)
            hi, lo = _split2(gf_ref[rows, G_LA:GLA_F_W])
            sums = _dot(tri, jnp.concatenate([hi, lo], axis=1))
            cum_f = sums[:C, 0:GLA_QK] + sums[:C, 2 * GLA_QK:3 * GLA_QK]
            cum_b = sums[C:, GLA_QK:2 * GLA_QK] + sums[C:, 3 * GLA_QK:]
            cum_ref[0, rows, :] = cum_f
            cum_ref[1, rows, :] = cum_b
            k = gf_ref[rows, G_K:G_G]
            k_in = (k * jnp.exp(cum_f[C - 1:C, :] - cum_f), k * jnp.exp(cum_b[0:1, :] - cum_b))
            for p in range(n_pairs):
                sl = slice(p * LANES, (p + 1) * LANES)
                kv_t = None
                for e in range(2):
                    h = 2 * p + e
                    v_t = gv_ref[rows, h * GLA_DV:(h + 1) * GLA_DV].T
                    k_e = jnp.concatenate(
                        [jnp.where(half_of[e], k_in[d][:, sl], 0.0) for d in range(2)], axis=1)
                    part = _dot(v_t, k_e.astype(BF16))
                    kv_t = part if kv_t is None else kv_t + part
                kv_ref[s, n, p] = kv_t

        for p in range(n_pairs):
            sl = slice(p * LANES, (p + 1) * LANES)
            st = []
            for d in range(2):
                if has_s0:
                    s0 = s0_refs[d][s, 0, 2 * p:2 * p + 2, :, :].reshape(2 * GLA_DK, GLA_DV)
                    st.append(s0.T)
                else:
                    st.append(jnp.zeros((GLA_DV, 2 * GLA_DK), F32))
            for i in range(n_chunks):
                for d, n in ((0, i), (1, n_chunks - 1 - i)):
                    tot_row = (s * n_chunks + n) * C + (C - 1 if d == 0 else 0)
                    decay = jnp.exp(cum_ref[d, tot_row:tot_row + 1, sl])
                    sent_ref[s, n, p, :, d * LANES:(d + 1) * LANES] = st[d].astype(BF16)
                    st[d] = decay * st[d] + kv_ref[s, n, p, :, d * LANES:(d + 1) * LANES]
            if emit_state:
                for d in range(2):
                    sfin_refs[d][s, 0, 2 * p:2 * p + 2, :, :] = st[d].T.reshape(2, GLA_DK, GLA_DV)

        for n in range(n_chunks):
            rows = chunk_rows(n)
            q = gf_ref[rows, G_Q:G_K] * qscale
            k = gf_ref[rows, G_K:G_G]
            qs, ks, qin = [], [], []
            for d in range(2):
                cum = cum_ref[d, rows, :]
                ref = cum[C // 2:C // 2 + 1, :]
                qs.append(q * jnp.exp(cum - ref))
                ks.append((k * jnp.exp(ref - cum)).astype(BF16))
                qin.append(q * jnp.exp(cum))
            for p in range(n_pairs):
                sl = slice(p * LANES, (p + 1) * LANES)
                lhs = jnp.concatenate(
                    [jnp.where(half_of[e], qs[d][:, sl], 0.0) for e in range(2) for d in range(2)],
                    axis=0).astype(BF16)
                sc = _dot_t(lhs, jnp.concatenate([ks[0][:, sl], ks[1][:, sl]], axis=0))
                prob = []
                for e in range(2):
                    s_f = sc[(2 * e) * C:(2 * e + 1) * C, :C]
                    s_b = sc[(2 * e + 1) * C:(2 * e + 2) * C, C:]
                    prob.append(jnp.where(lower, s_f, 0.0) + jnp.where(upper, s_b, 0.0))
                p_blk = jnp.concatenate(
                    [jnp.concatenate([prob[0], zeros_cc], axis=1),
                     jnp.concatenate([zeros_cc, prob[1]], axis=1)], axis=0).astype(BF16)
                v2 = jnp.concatenate(
                    [gv_ref[rows, (2 * p + e) * GLA_DV:(2 * p + e + 1) * GLA_DV] for e in range(2)],
                    axis=0)
                q_in = jnp.concatenate(
                    [jnp.concatenate([jnp.where(half_of[e], qin[d][:, sl], 0.0) for d in range(2)],
                                     axis=1) for e in range(2)], axis=0).astype(BF16)
                o2 = _dot(p_blk, v2) + _dot_t(q_in, sent_ref[s, n, p])
                for e in range(2):
                    h = 2 * p + e
                    o = o2[e * C:(e + 1) * C, :]
                    o = o * lax.rsqrt(jnp.mean(o * o, axis=-1, keepdims=True) + EPS) * gnorm
                    gate = gf_ref[rows, G_G + h * GLA_DV:G_G + (h + 1) * GLA_DV]
                    o_ref[rows, h * GLA_DV:(h + 1) * GLA_DV] = (o * _silu(gate)).astype(o_ref.dtype)

    for s in range(kv_ref.shape[0]):
        one_sequence(s)


def _gla(gla_f, gv, gnorm, batch, seq_len, seqs_per_step, s0=None, emit_state=False):
    n_chunks = seq_len // GLA_CHUNK
    has_s0 = s0 is not None
    rows = seqs_per_step * seq_len

    def seq(width):
        return pl.BlockSpec((rows, width), lambda b: (b, 0))
    state_spec = pl.BlockSpec((seqs_per_step, 1, GLA_HEADS, GLA_DK, GLA_DV),
                              lambda b: (b, 0, 0, 0, 0))
    in_specs = [seq(GLA_F_W), seq(GLA_V), pl.BlockSpec((1, GLA_DV), lambda b: (0, 0))]
    args = [gla_f, gv, gnorm]
    if has_s0:
        in_specs += [state_spec, state_spec]
        args += list(s0)
    out_shape = [jax.ShapeDtypeStruct((batch * seq_len, GLA_V), BF16)]
    out_specs = [seq(GLA_V)]
    if emit_state:
        out_shape += [jax.ShapeDtypeStruct((batch, 1, GLA_HEADS, GLA_DK, GLA_DV), F32)] * 2
        out_specs += [state_spec, state_spec]
    n_pairs = GLA_QK // LANES
    return pl.pallas_call(
        functools.partial(_gla_kernel, n_chunks=n_chunks, has_s0=has_s0, emit_state=emit_state),
        out_shape=out_shape,
        grid=(batch // seqs_per_step,),
        in_specs=in_specs,
        out_specs=out_specs,
        scratch_shapes=[pltpu.VMEM((2, rows, GLA_QK), F32),
                        pltpu.VMEM((seqs_per_step, n_chunks, n_pairs, GLA_DV, 2 * LANES), F32),
                        pltpu.VMEM((seqs_per_step, n_chunks, n_pairs, GLA_DV, 2 * LANES), BF16)],
        compiler_params=_params(1),
        name="gla",
    )(*args)


def kernel(x_prompt, x_sample, cache_k, cache_v, state_gla_fwd, state_gla_bwd, c, c_ctx,
           w_ada, b_ada, norm_pre, norm_post, ffn_w1, ffn_w3, ffn_w2, w_in,
           gla_w_up, gla_b_up, gla_norm, attn_sink, w_out):
    depth = w_in.shape[0]
    assert depth == 1, "single trunk layer"
    batch, seq = x_prompt.shape[0], x_prompt.shape[1]
    dec_batch, dec_seq = x_sample.shape[0], x_sample.shape[1]
    past = cache_k.shape[2]
    l = 0

    cond = jnp.concatenate(
        [c_ctx[None, :], c, jnp.zeros((MOD_ROWS - 1 - dec_batch, D_MODEL), F32)], axis=0)
    ada = (cond, w_ada[l], b_ada[l][None, :])
    mod_early = _ada_modulation(*ada, EARLY_MODS)

    npre, npost = norm_pre[l], norm_post[l]
    ffn_first_b = [w[l, :1].astype(BF16) for w in (ffn_w1, ffn_w3, ffn_w2)]
    late_f32 = [(w[l], 1) for w in (ffn_w1, ffn_w3, ffn_w2)] + [(w_out, l)]
    w_in_b = w_in[l:l + 1].astype(BF16)
    zeros = jnp.zeros((GLA_LOW_RANK, GLA_QK), F32)
    w_up = jnp.concatenate(
        [jnp.concatenate([gla_w_up[l, 0], zeros], axis=1),
         jnp.concatenate([zeros, gla_w_up[l, 1]], axis=1)], axis=0).astype(BF16)
    b_up = gla_b_up[l].reshape(1, 2 * GLA_QK)
    gnorm = gla_norm[l][None, :]
    sink = attn_sink[l]

    x1, mod, *late_b = _ffn_first(
        x_prompt.reshape(batch * seq, D_MODEL), x_sample.reshape(dec_batch * dec_seq, D_MODEL),
        mod_early, dec_seq, npre, npost, *ffn_first_b, ada=ada, cast=late_f32)
    *ffn_second_b, w_out_b = late_b
    ctx_tiles = batch * seq // TOKEN_TILE

    def trunk(latent):
        n_batch, n_seq = (dec_batch, dec_seq) if latent else (batch, seq)
        latent_len = n_seq if latent else None
        first_tile = ctx_tiles if latent else 0
        att_in, gla_f, gv, *cache_t = _project(
            x1, first_tile, n_batch * n_seq, mod, latent_len, None if latent else n_seq, npre,
            w_in_b, w_up, b_up)
        if latent:
            att = _latent_attention(sink, att_in, cache_k[:, l].reshape(dec_batch, past, ATT_KV),
                                    cache_v[:, l].reshape(dec_batch, past, ATT_KV), n_batch, n_seq)
            (gla,) = _gla(gla_f, gv, gnorm, n_batch, n_seq, 1,
                          s0=(state_gla_fwd[:, l:l + 1], state_gla_bwd[:, l:l + 1]))
            extras = ()
        else:
            att = _context_attention(sink, att_in, n_batch, n_seq)
            gla, s_f, s_b = _gla(gla_f, gv, gnorm, n_batch, n_seq, CTX_SEQS_PER_STEP,
                                 emit_state=True)
            k_new, v_new = (jnp.transpose(c_t, (0, 1, 4, 2, 3)) for c_t in cache_t)
            extras = (k_new, v_new, s_f, s_b)
        y = _mix_ffn(x1, first_tile, att, gla, w_out_b, mod, latent_len, npre, npost,
                     *ffn_second_b)
        return y.reshape(n_batch, n_seq, D_MODEL), extras

    y_prompt, (k_new, v_new, s_f, s_b) = trunk(False)
    y_sample, _ = trunk(True)
    return (y_prompt, y_sample, k_new, v_new, s_f, s_b)
```

```python
import functools

import numpy as np
import jax
import jax.numpy as jnp
from jax import lax
from jax.experimental import pallas as pl
from jax.experimental.pallas import tpu as pltpu

F32 = jnp.float32
BF16 = jnp.bfloat16

D_MODEL = 1024
GRID_W = 64
N_Q_HEADS = 8
N_KV_HEADS = 2
HEAD_DIM = 64
WINDOW = 128
BLOCK = 128
ROPE_BASE = 10000.0
GLA_HEADS = 4
GLA_DK = 64
GLA_DV = 128
GLA_LOW_RANK = 16
GLA_TAU = 16.0
D_FF = 2816
N_MOD = 9
EPS = 1e-6
NEG = -1e30

ATT_Q = N_Q_HEADS * HEAD_DIM
ATT_KV = N_KV_HEADS * HEAD_DIM
GLA_QK = GLA_HEADS * GLA_DK
GLA_V = GLA_HEADS * GLA_DV
OFF_Q = 0
OFF_K = OFF_Q + ATT_Q
OFF_V = OFF_K + ATT_KV
OFF_GQ = OFF_V + ATT_KV
OFF_GK = OFF_GQ + GLA_QK
OFF_GV = OFF_GK + GLA_QK
OFF_GG = OFF_GV + GLA_V
OFF_LR = OFF_GG + GLA_V
IN_WIDTH = OFF_LR + 2 * GLA_LOW_RANK
A_Q = 0
A_K = A_Q + ATT_Q
A_KS = A_K + ATT_KV
A_V = A_KS + ATT_KV
A_VS = A_V + ATT_KV
ATT_IN_W = A_VS + ATT_KV
G_Q = 0
G_K = G_Q + GLA_QK
G_G = G_K + GLA_QK
G_LA = G_G + GLA_V
GLA_F_W = G_LA + 2 * GLA_QK
ROPE_PAIR = HEAD_DIM // 4
LOG2_E = 1.4426950408889634
ATT_Q_SCALE = HEAD_DIM ** -0.5 * LOG2_E

LANES = 128
SUBLANES = 8
HALF = LANES // 2
VMEM_LIMIT = 56 * 1024 * 1024

TOKEN_TILE = 512
FF_SPLITS = (0, 1536, D_FF)
EARLY_MODS = 3
CAST_SLABS = 16
ADA_K_TILE = 512
CTX_SEQS_PER_STEP = 4
LAT_BLOCKS_PER_STEP = 4
GLA_CHUNK = 128
MOD_ROWS = 8


def _params(n_axes):
    return pltpu.CompilerParams(
        dimension_semantics=("arbitrary",) * n_axes, vmem_limit_bytes=VMEM_LIMIT)


def _resident(shape):
    zeros = (0,) * len(shape)
    return pl.BlockSpec(shape, lambda *_: zeros, pipeline_mode=pl.Buffered(1))


def _sigmoid(x):
    return 1.0 / (1.0 + jnp.exp(-x))


def _silu(x):
    return x * _sigmoid(x)


def _rms(x, g):
    return x * lax.rsqrt(jnp.mean(x * x, axis=-1, keepdims=True) + EPS) * g


def _dot(a, b):
    return jnp.dot(a, b, preferred_element_type=F32)


def _dot_t(a, b):
    return lax.dot_general(a, b, (((1,), (1,)), ((), ())), preferred_element_type=F32)


def _ada_kernel(cond_ref, w_ref, b_ref, o_ref):
    n_mod = o_ref.shape[0]

    @pl.when(pl.program_id(0) == 0)
    def _():
        for m in range(n_mod):
            o_ref[m] = jnp.broadcast_to(b_ref[:, m * D_MODEL:(m + 1) * D_MODEL],
                                        (MOD_ROWS, D_MODEL))
    part = _dot(_silu(cond_ref[...]).astype(BF16), w_ref[...].astype(BF16))
    for m in range(n_mod):
        o_ref[m] += part[:, m * D_MODEL:(m + 1) * D_MODEL]


def _ada_modulation(cond, w_ada, b_ada, n_mod):
    k_dim = w_ada.shape[0]
    n = n_mod * D_MODEL
    return pl.pallas_call(
        _ada_kernel,
        out_shape=jax.ShapeDtypeStruct((n_mod, MOD_ROWS, D_MODEL), F32),
        grid=(k_dim // ADA_K_TILE,),
        in_specs=[
            pl.BlockSpec((MOD_ROWS, ADA_K_TILE), lambda k: (0, k)),
            pl.BlockSpec((ADA_K_TILE, n), lambda k: (k, 0)),
            pl.BlockSpec((1, n), lambda k: (0, 0)),
        ],
        out_specs=pl.BlockSpec((n_mod, MOD_ROWS, D_MODEL), lambda k: (0, 0, 0)),
        compiler_params=_params(1),
        name="ada_modulation",
    )(cond, w_ada, b_ada)


class _Mod:
    def __init__(self, mod_ref, tiles_per_seq, first_latent_tile=0, first_mod=0):
        self.ref = mod_ref
        self.first_mod = first_mod
        if tiles_per_seq is None:
            self.row = 0
        else:
            tile = pl.program_id(0) - first_latent_tile
            self.row = jnp.where(tile >= 0, 1 + tile // tiles_per_seq, 0)

    def __getitem__(self, m):
        return self.ref[m - self.first_mod, pl.ds(self.row, 1), :]


def _modulated(x, mod, npre_ref, i):
    return _rms(x, npre_ref[i:i + 1, :]) * (1.0 + mod[3 * i + 1]) + mod[3 * i]


def _residual(x, out, mod, npost_ref, i, weight):
    return x + (weight * mod[3 * i + 2]) * _rms(out, npost_ref[i:i + 1, :])


def _ffn_sublayer(x, i, mod, npre_ref, npost_ref, w1_ref, w3_ref, w2_ref):
    h = _modulated(x, mod, npre_ref, i).astype(BF16)
    acc = None
    for lo, hi in zip(FF_SPLITS[:-1], FF_SPLITS[1:]):
        a = _dot(h, w1_ref[:, lo:hi])
        g = _dot(h, w3_ref[:, lo:hi])
        part = _dot((_silu(a) * g).astype(BF16), w2_ref[lo:hi, :])
        acc = part if acc is None else acc + part
    return _residual(x, acc, mod, npost_ref, i, 0.5)


def _ffn_first_kernel(*refs, ctx_tiles, tiles_per_seq, n_cast):
    xc_ref, xl_ref, mod_ref, npre_ref, npost_ref, w1_ref, w3_ref, w2_ref = refs[:8]
    cond_ref, wada_ref, bada_ref = refs[8:11]
    cast_src = refs[11:11 + n_cast]
    o_ref, late_mod_ref = refs[11 + n_cast:13 + n_cast]
    cast_dst = refs[13 + n_cast:]
    mod = _Mod(mod_ref, tiles_per_seq, ctx_tiles)
    x = jnp.where(pl.program_id(0) >= ctx_tiles, xl_ref[...], xc_ref[...])
    o_ref[...] = _ffn_sublayer(x, 0, mod, npre_ref, npost_ref, w1_ref, w3_ref, w2_ref)
    for src, dst in zip(cast_src, cast_dst):
        dst[...] = src[...].astype(BF16)
    late_mod_ref[...] = _dot(_silu(cond_ref[...]).astype(BF16),
                             wada_ref[...].astype(BF16)) + bada_ref[...]


def _mix_ffn_kernel(x_ref, att_ref, gla_ref, wo_ref, mod_ref, npre_ref, npost_ref,
                    w1_ref, w3_ref, w2_ref, o_ref, *, tiles_per_seq):
    mod = _Mod(mod_ref, tiles_per_seq, first_mod=EARLY_MODS)
    mix = _dot(att_ref[...], wo_ref[0:ATT_Q, :]) + _dot(gla_ref[...], wo_ref[ATT_Q:, :])
    x = _residual(x_ref[...], mix, mod, npost_ref, 1, 1.0)
    o_ref[...] = _ffn_sublayer(x, 2, mod, npre_ref, npost_ref, w1_ref, w3_ref, w2_ref)


def _rope_tile(x, cos, sin_up, sin_dn):
    up = pltpu.roll(x, LANES - ROPE_PAIR, axis=1)
    dn = pltpu.roll(x, ROPE_PAIR, axis=1)
    return x * cos + up * sin_up + dn * sin_dn


def _proj_kernel(*refs, tiles_per_seq, cache_seq):
    x_ref, mod_ref, npre_ref, win_ref, wlr_ref, wup_ref, bup_ref = refs[:7]
    rope_refs = refs[7:10] if tiles_per_seq is not None else None
    outs = refs[7 + (3 if rope_refs else 0):]
    att_ref, glaf_ref, gv_ref = outs[:3]
    mod = _Mod(mod_ref, tiles_per_seq, first_mod=EARLY_MODS)
    h = _modulated(x_ref[...], mod, npre_ref, 1).astype(BF16)
    q = _dot(h, win_ref[:, OFF_Q:OFF_K])
    kv = _dot(h, win_ref[:, OFF_K:OFF_GQ])
    k, v = kv[:, :ATT_KV], kv[:, ATT_KV:]
    lr = _dot(h, wlr_ref[...])
    z = _dot(lr.astype(BF16), wup_ref[...]) + bup_ref[...]
    if rope_refs:
        cos, sup, sdn = (r[...] for r in rope_refs)
        q = jnp.concatenate([_rope_tile(q[:, j * LANES:(j + 1) * LANES], cos, sup, sdn)
                             for j in range(ATT_Q // LANES)], axis=1)
        k_att = _rope_tile(k, cos, sup, sdn)
    else:
        k_att = k
    att_ref[:, A_Q:A_K] = (q * ATT_Q_SCALE).astype(BF16)
    att_ref[:, A_K:A_KS] = k_att.astype(BF16)
    att_ref[:, A_KS:A_V] = pltpu.roll(k_att, HALF, axis=1).astype(BF16)
    att_ref[:, A_V:A_VS] = v.astype(BF16)
    att_ref[:, A_VS:ATT_IN_W] = pltpu.roll(v, HALF, axis=1).astype(BF16)
    if cache_seq is not None:
        for t, dst in ((k.T, outs[3]), (v.T, outs[4])):
            for b in range(TOKEN_TILE // cache_seq):
                dst[b, 0] = t[:, b * cache_seq:(b + 1) * cache_seq].reshape(
                    N_KV_HEADS, HEAD_DIM, cache_seq)
    glaf_ref[:, G_Q:G_G] = _dot(h, win_ref[:, OFF_GQ:OFF_GV])
    log_sig = jnp.minimum(z, 0.0) - jnp.log(1.0 + jnp.exp(-jnp.abs(z)))
    glaf_ref[:, G_LA:GLA_F_W] = log_sig * (1.0 / GLA_TAU)
    gv_ref[...] = _dot(h, win_ref[:, OFF_GV:OFF_GG]).astype(BF16)
    glaf_ref[:, G_G:G_LA] = _dot(h, win_ref[:, OFF_GG:OFF_LR])


def _token_spec(width, first_tile=0):
    return pl.BlockSpec((TOKEN_TILE, width), lambda i: (i + first_tile, 0))


def _stack_spec(shape, j):
    return pl.BlockSpec((None,) + tuple(shape[1:]), lambda i: (j, 0, 0),
                        pipeline_mode=pl.Buffered(1))


def _tiles_per_seq(latent_len):
    return None if latent_len is None else latent_len // TOKEN_TILE


def _ffn_first(x_ctx, x_lat, mod, latent_len, npre, npost, w1, w3, w2, ada, cast):
    ctx_tiles = x_ctx.shape[0] // TOKEN_TILE
    steps = ctx_tiles + x_lat.shape[0] // TOKEN_TILE
    assert steps >= CAST_SLABS, "every slab needs its own grid step"
    cond, w_ada, b_ada = ada
    late_mods = N_MOD - EARLY_MODS
    ada_cols = late_mods * D_MODEL // steps
    per_mod = D_MODEL // ada_cols
    assert ada_cols * steps == late_mods * D_MODEL and ada_cols % LANES == 0
    first_slab = EARLY_MODS * per_mod
    in_specs = [pl.BlockSpec((TOKEN_TILE, D_MODEL), lambda i: (jnp.minimum(i, ctx_tiles - 1), 0)),
                pl.BlockSpec((TOKEN_TILE, D_MODEL), lambda i: (jnp.maximum(i - ctx_tiles, 0), 0)),
                _resident(mod.shape), _resident(npre.shape),
                _resident(npost.shape), _stack_spec(w1.shape, 0),
                _stack_spec(w3.shape, 0), _stack_spec(w2.shape, 0),
                _resident(cond.shape),
                pl.BlockSpec((D_MODEL, ada_cols), lambda i: (0, first_slab + i)),
                pl.BlockSpec((1, ada_cols), lambda i: (0, first_slab + i))]
    out_shape = [jax.ShapeDtypeStruct((steps * TOKEN_TILE, D_MODEL), F32),
                 jax.ShapeDtypeStruct((late_mods, MOD_ROWS, D_MODEL), F32)]
    out_specs = [_token_spec(D_MODEL),
                 pl.BlockSpec((None, MOD_ROWS, ada_cols), lambda i: (i // per_mod, 0, i % per_mod))]
    for stack, j in cast:
        _, rows, cols = stack.shape
        slab = rows // CAST_SLABS
        assert slab * CAST_SLABS == rows and slab % (2 * SUBLANES) == 0, "whole bf16 row tiles"
        in_specs.append(pl.BlockSpec((None, slab, cols), functools.partial(
            lambda i, j: (j, i * CAST_SLABS // steps, 0), j=j)))
        out_shape.append(jax.ShapeDtypeStruct((1, rows, cols), BF16))
        out_specs.append(pl.BlockSpec((None, slab, cols), lambda i: (0, i * CAST_SLABS // steps, 0)))
    return pl.pallas_call(
        functools.partial(_ffn_first_kernel, ctx_tiles=ctx_tiles,
                          tiles_per_seq=_tiles_per_seq(latent_len), n_cast=len(cast)),
        out_shape=out_shape,
        grid=(steps,),
        in_specs=in_specs,
        out_specs=out_specs,
        compiler_params=_params(1),
        name="ffn_first",
    )(x_ctx, x_lat, mod, npre, npost, w1, w3, w2, cond, w_ada, b_ada,
      *[stack for stack, _ in cast])


def _mix_ffn(x, first_tile, att, gla, w_out, mod, latent_len, npre, npost, w1, w3, w2):
    t = att.shape[0]
    return pl.pallas_call(
        functools.partial(_mix_ffn_kernel, tiles_per_seq=_tiles_per_seq(latent_len)),
        out_shape=jax.ShapeDtypeStruct((t, D_MODEL), F32),
        grid=(t // TOKEN_TILE,),
        in_specs=[_token_spec(D_MODEL, first_tile), _token_spec(ATT_Q), _token_spec(GLA_V),
                  _stack_spec(w_out.shape, 0), _resident(mod.shape), _resident(npre.shape),
                  _resident(npost.shape), _stack_spec(w1.shape, 0),
                  _stack_spec(w3.shape, 0), _stack_spec(w2.shape, 0)],
        out_specs=_token_spec(D_MODEL),
        compiler_params=_params(1),
        name="mix_ffn",
    )(x, att, gla, w_out, mod, npre, npost, w1, w3, w2)


def _project(x, first_tile, t, mod, latent_len, cache_seq, npre, w_in, w_up, b_up):
    tiles = _tiles_per_seq(latent_len)
    w_main, w_lr = w_in
    in_specs = [_token_spec(D_MODEL, first_tile), _resident(mod.shape), _resident(npre.shape),
                _stack_spec(w_main.shape, 0), _stack_spec(w_lr.shape, 0),
                _resident(w_up.shape), _resident(b_up.shape)]
    args = [x, mod, npre, w_main, w_lr, w_up, b_up]
    if tiles is not None:
        in_specs += [pl.BlockSpec((TOKEN_TILE, LANES), lambda i: (i % tiles, 0))] * 3
        args += [jnp.asarray(tab) for tab in _rope_tables(latent_len)]
    outs = ((ATT_IN_W, BF16), (GLA_F_W, F32), (GLA_V, BF16))
    out_shape = [jax.ShapeDtypeStruct((t, w), dt) for w, dt in outs]
    out_specs = [_token_spec(w) for w, _ in outs]
    if cache_seq is not None:
        seqs = TOKEN_TILE // cache_seq
        cache = (t // cache_seq, 1, N_KV_HEADS, HEAD_DIM, cache_seq)
        out_shape += [jax.ShapeDtypeStruct(cache, F32)] * 2
        out_specs += [pl.BlockSpec((seqs,) + cache[1:], lambda i: (i, 0, 0, 0, 0))] * 2
    return pl.pallas_call(
        functools.partial(_proj_kernel, tiles_per_seq=tiles, cache_seq=cache_seq),
        out_shape=out_shape,
        grid=(t // TOKEN_TILE,),
        in_specs=in_specs,
        out_specs=out_specs,
        compiler_params=_params(1),
        name="project",
    )(*args)


def _rope_tables(seq_len):
    half = HEAD_DIM // 2
    inv_freq = np.float32(ROPE_BASE) ** (-np.arange(0, half, 2, dtype=np.float32) / half)
    pos = np.arange(seq_len)
    row = (pos // GRID_W).astype(np.float32)
    col = (pos % GRID_W).astype(np.float32)
    within = np.arange(LANES) % HEAD_DIM
    idx = within % half
    freq = inv_freq[idx % ROPE_PAIR].astype(np.float32)
    p = np.where((within // half == 0)[None, :], row[:, None], col[:, None])
    ang = (p * freq[None, :]).astype(np.float32)
    cos, sin = np.cos(ang).astype(np.float32), np.sin(ang).astype(np.float32)
    first = (idx < ROPE_PAIR)[None, :]
    zero = np.float32(0.0)
    return cos, np.where(first, -sin, zero), np.where(first, zero, sin)


def _attend(q_ref, q_rows, nq, segs, sink_ref):
    lane = lax.broadcasted_iota(jnp.int32, (1, LANES), 1)
    half_of = [lane < HALF, lane >= HALF]
    rows = lax.broadcasted_iota(jnp.int32, (2 * nq, 1), 0)
    zero = jnp.zeros((), BF16)
    out = [None] * (ATT_Q // LANES)
    for g in range(N_KV_HEADS):
        tiles = (2 * g, 2 * g + 1)
        for e in range(2):
            qm = jnp.concatenate(
                [jnp.where(half_of[e], q_ref[q_rows, j * LANES:(j + 1) * LANES], zero)
                 for j in tiles], axis=0)
            sink = jnp.where(rows < nq, sink_ref[2 * tiles[0] + e],
                             sink_ref[2 * tiles[1] + e]) * LOG2_E
            scores = []
            for k, k_sw, _, _, m in segs:
                s = _dot_t(qm, k if e == g else k_sw)
                scores.append(s if m is None else jnp.where(m, s, NEG))
            mx = sink
            for s in scores:
                mx = jnp.maximum(mx, jnp.max(s, axis=-1, keepdims=True))
            probs = [jnp.exp2(s - mx) for s in scores]
            den = jnp.exp2(sink - mx)
            for p in probs:
                den = den + jnp.sum(p, axis=-1, keepdims=True)
            o = None
            for p, (_, _, v, v_sw, _) in zip(probs, segs):
                part = _dot(p.astype(BF16), jnp.where(half_of[e], v if e == g else v_sw, zero))
                o = part if o is None else o + part
            o = o * (1.0 / den)
            for r, j in enumerate(tiles):
                blk = o[r * nq:(r + 1) * nq, :]
                out[j] = blk if out[j] is None else out[j] + blk
    return out


def _context_attn_kernel(sink_ref, a_ref, o_ref, *, seq_len):
    for s in range(a_ref.shape[0] // seq_len):
        rows = slice(s * seq_len, (s + 1) * seq_len)
        segs = [(a_ref[rows, A_K:A_KS], a_ref[rows, A_KS:A_V],
                 a_ref[rows, A_V:A_VS], a_ref[rows, A_VS:ATT_IN_W], None)]
        for j, t in enumerate(_attend(a_ref, rows, seq_len, segs, sink_ref)):
            o_ref[rows, j * LANES:(j + 1) * LANES] = t.astype(o_ref.dtype)


def _latent_attn_kernel(sink_ref, q_ref, kv_ref, kc_ref, vc_ref, o_ref, *, seq_len):
    k_ctx, v_ctx = kc_ref[0], vc_ref[0]
    ctx = (k_ctx.astype(BF16), pltpu.roll(k_ctx, HALF, axis=1).astype(BF16),
           v_ctx.astype(BF16), pltpu.roll(v_ctx, HALF, axis=1).astype(BF16), None)
    span = 3 * BLOCK
    per_step = q_ref.shape[0] // BLOCK
    for s in range(per_step):
        i = pl.program_id(1) * per_step + s
        start = pl.multiple_of(jnp.clip((i - 1) * BLOCK, 0, seq_len - span), BLOCK)
        kpos = start + lax.broadcasted_iota(jnp.int32, (1, span), 1)
        qpos = i * BLOCK + lax.broadcasted_iota(jnp.int32, (BLOCK, 1), 0)
        valid = jnp.abs(kpos - qpos) <= WINDOW
        valid2 = jnp.concatenate([valid, valid], axis=0)
        kv = kv_ref[pl.ds(start, span), :]
        segs = [tuple(kv[:, c * ATT_KV:(c + 1) * ATT_KV] for c in range(4)) + (valid2,), ctx]
        rows = slice(s * BLOCK, (s + 1) * BLOCK)
        for j, t in enumerate(_attend(q_ref, rows, BLOCK, segs, sink_ref)):
            o_ref[rows, j * LANES:(j + 1) * LANES] = t.astype(o_ref.dtype)


def _smem_spec():
    return pl.BlockSpec(memory_space=pltpu.SMEM)


def _context_attention(sink, att_in, batch, seq_len):
    rows = CTX_SEQS_PER_STEP * seq_len

    def seq(width):
        return pl.BlockSpec((rows, width), lambda b: (b, 0))
    return pl.pallas_call(
        functools.partial(_context_attn_kernel, seq_len=seq_len),
        out_shape=jax.ShapeDtypeStruct((batch * seq_len, ATT_Q), BF16),
        grid=(batch // CTX_SEQS_PER_STEP,),
        in_specs=[_smem_spec(), seq(ATT_IN_W)],
        out_specs=seq(ATT_Q),
        compiler_params=_params(1),
        name="context_attention",
    )(sink, att_in)


def _latent_attention(sink, att_in, k_ctx, v_ctx, batch, seq_len):
    steps = seq_len // (LAT_BLOCKS_PER_STEP * BLOCK)
    rows = LAT_BLOCKS_PER_STEP * BLOCK
    past = k_ctx.shape[1]
    assert A_K == ATT_IN_W - A_K, "q and the key / value columns are the two halves of att_in"
    return pl.pallas_call(
        functools.partial(_latent_attn_kernel, seq_len=seq_len),
        out_shape=jax.ShapeDtypeStruct((batch * seq_len, ATT_Q), BF16),
        grid=(batch, steps),
        in_specs=[_smem_spec(),
                  pl.BlockSpec((rows, ATT_Q), lambda b, i: (b * steps + i, 0)),
                  pl.BlockSpec((seq_len, ATT_IN_W - A_K), lambda b, i: (b, 1)),
                  pl.BlockSpec((1, past, ATT_KV), lambda b, i: (b, 0, 0)),
                  pl.BlockSpec((1, past, ATT_KV), lambda b, i: (b, 0, 0))],
        out_specs=pl.BlockSpec((rows, ATT_Q), lambda b, i: (b * steps + i, 0)),
        compiler_params=_params(2),
        name="latent_attention",
    )(sink, att_in, att_in, k_ctx, v_ctx)


def _split2(x):
    hi = x.astype(BF16)
    lo = (x - hi.astype(F32)).astype(BF16)
    return hi, lo


def _gla_kernel(*refs, n_chunks, has_s0, emit_state):
    refs = list(refs)
    gf_ref, gv_ref, gn_ref = refs[:3]
    pos = 3
    s0_refs = refs[pos:pos + 2] if has_s0 else None
    pos += 2 if has_s0 else 0
    o_ref = refs[pos]
    pos += 1
    sfin_refs = refs[pos:pos + 2] if emit_state else None
    pos += 2 if emit_state else 0
    cum_ref, kv_ref, sent_ref = refs[pos:]

    C = GLA_CHUNK
    n_pairs = GLA_QK // LANES
    lane = lax.broadcasted_iota(jnp.int32, (1, LANES), 1)
    half_of = [lane < HALF, lane >= HALF]
    r_i = lax.broadcasted_iota(jnp.int32, (C, C), 0)
    c_i = lax.broadcasted_iota(jnp.int32, (C, C), 1)
    lower = c_i <= r_i
    upper = c_i >= r_i
    tri = jnp.concatenate([jnp.where(lower, 1.0, 0.0), jnp.where(upper, 1.0, 0.0)],
                          axis=0).astype(BF16)
    zeros_cc = jnp.zeros((C, C), F32)
    gnorm = gn_ref[...]
    qscale = GLA_DK ** -0.5

    def one_sequence(s):
        def chunk_rows(n):
            return slice((s * n_chunks + n) * C, (s * n_chunks + n + 1) * C)

        for n in range(n_chunks):
            rows = chunk_rows(n)
            hi, lo = _split2(gf_ref[rows, G_LA:GLA_F_W])
            sums = _dot(tri, jnp.concatenate([hi, lo], axis=1))
            cum_f = sums[:C, 0:GLA_QK] + sums[:C, 2 * GLA_QK:3 * GLA_QK]
            cum_b = sums[C:, GLA_QK:2 * GLA_QK] + sums[C:, 3 * GLA_QK:]
            cum_ref[0, rows, :] = cum_f
            cum_ref[1, rows, :] = cum_b
            k = gf_ref[rows, G_K:G_G]
            k_in = (k * jnp.exp(cum_f[C - 1:C, :] - cum_f), k * jnp.exp(cum_b[0:1, :] - cum_b))
            for p in range(n_pairs):
                sl = slice(p * LANES, (p + 1) * LANES)
                kv_t = None
                for e in range(2):
                    h = 2 * p + e
                    v_t = gv_ref[rows, h * GLA_DV:(h + 1) * GLA_DV].T
                    k_e = jnp.concatenate(
                        [jnp.where(half_of[e], k_in[d][:, sl], 0.0) for d in range(2)], axis=1)
                    part = _dot(v_t, k_e.astype(BF16))
                    kv_t = part if kv_t is None else kv_t + part
                kv_ref[s, n, p] = kv_t

        for p in range(n_pairs):
            sl = slice(p * LANES, (p + 1) * LANES)
            st = []
            for d in range(2):
                if has_s0:
                    s0 = s0_refs[d][s, 0, 2 * p:2 * p + 2, :, :].reshape(2 * GLA_DK, GLA_DV)
                    st.append(s0.T)
                else:
                    st.append(jnp.zeros((GLA_DV, 2 * GLA_DK), F32))
            for i in range(n_chunks):
                for d, n in ((0, i), (1, n_chunks - 1 - i)):
                    tot_row = (s * n_chunks + n) * C + (C - 1 if d == 0 else 0)
                    decay = jnp.exp(cum_ref[d, tot_row:tot_row + 1, sl])
                    sent_ref[s, n, p, :, d * LANES:(d + 1) * LANES] = st[d].astype(BF16)
                    st[d] = decay * st[d] + kv_ref[s, n, p, :, d * LANES:(d + 1) * LANES]
            if emit_state:
                for d in range(2):
                    sfin_refs[d][s, 0, 2 * p:2 * p + 2, :, :] = st[d].T.reshape(2, GLA_DK, GLA_DV)

        for n in range(n_chunks):
            rows = chunk_rows(n)
            q = gf_ref[rows, G_Q:G_K] * qscale
            k = gf_ref[rows, G_K:G_G]
            qs, ks, qin = [], [], []
            for d in range(2):
                cum = cum_ref[d, rows, :]
                ref = cum[C // 2:C // 2 + 1, :]
                qs.append(q * jnp.exp(cum - ref))
                ks.append((k * jnp.exp(ref - cum)).astype(BF16))
                qin.append(q * jnp.exp(cum))
            for p in range(n_pairs):
                sl = slice(p * LANES, (p + 1) * LANES)
                lhs = jnp.concatenate(
                    [jnp.where(half_of[e], qs[d][:, sl], 0.0) for e in range(2) for d in range(2)],
                    axis=0).astype(BF16)
                sc = _dot_t(lhs, jnp.concatenate([ks[0][:, sl], ks[1][:, sl]], axis=0))
                prob = []
                for e in range(2):
                    s_f = sc[(2 * e) * C:(2 * e + 1) * C, :C]
                    s_b = sc[(2 * e + 1) * C:(2 * e + 2) * C, C:]
                    prob.append(jnp.where(lower, s_f, 0.0) + jnp.where(upper, s_b, 0.0))
                p_blk = jnp.concatenate(
                    [jnp.concatenate([prob[0], zeros_cc], axis=1),
                     jnp.concatenate([zeros_cc, prob[1]], axis=1)], axis=0).astype(BF16)
                v2 = jnp.concatenate(
                    [gv_ref[rows, (2 * p + e) * GLA_DV:(2 * p + e + 1) * GLA_DV] for e in range(2)],
                    axis=0)
                q_in = jnp.concatenate(
                    [jnp.concatenate([jnp.where(half_of[e], qin[d][:, sl], 0.0) for d in range(2)],
                                     axis=1) for e in range(2)], axis=0).astype(BF16)
                o2 = _dot(p_blk, v2) + _dot_t(q_in, sent_ref[s, n, p])
                for e in range(2):
                    h = 2 * p + e
                    o = o2[e * C:(e + 1) * C, :]
                    o = o * lax.rsqrt(jnp.mean(o * o, axis=-1, keepdims=True) + EPS) * gnorm
                    gate = gf_ref[rows, G_G + h * GLA_DV:G_G + (h + 1) * GLA_DV]
                    o_ref[rows, h * GLA_DV:(h + 1) * GLA_DV] = (o * _silu(gate)).astype(o_ref.dtype)

    for s in range(kv_ref.shape[0]):
        one_sequence(s)


def _gla(gla_f, gv, gnorm, batch, seq_len, seqs_per_step, s0=None, emit_state=False):
    n_chunks = seq_len // GLA_CHUNK
    has_s0 = s0 is not None
    rows = seqs_per_step * seq_len

    def seq(width):
        return pl.BlockSpec((rows, width), lambda b: (b, 0))
    state_spec = pl.BlockSpec((seqs_per_step, 1, GLA_HEADS, GLA_DK, GLA_DV),
                              lambda b: (b, 0, 0, 0, 0))
    in_specs = [seq(GLA_F_W), seq(GLA_V), pl.BlockSpec((1, GLA_DV), lambda b: (0, 0))]
    args = [gla_f, gv, gnorm]
    if has_s0:
        in_specs += [state_spec, state_spec]
        args += list(s0)
    out_shape = [jax.ShapeDtypeStruct((batch * seq_len, GLA_V), BF16)]
    out_specs = [seq(GLA_V)]
    if emit_state:
        out_shape += [jax.ShapeDtypeStruct((batch, 1, GLA_HEADS, GLA_DK, GLA_DV), F32)] * 2
        out_specs += [state_spec, state_spec]
    n_pairs = GLA_QK // LANES
    return pl.pallas_call(
        functools.partial(_gla_kernel, n_chunks=n_chunks, has_s0=has_s0, emit_state=emit_state),
        out_shape=out_shape,
        grid=(batch // seqs_per_step,),
        in_specs=in_specs,
        out_specs=out_specs,
        scratch_shapes=[pltpu.VMEM((2, rows, GLA_QK), F32),
                        pltpu.VMEM((seqs_per_step, n_chunks, n_pairs, GLA_DV, 2 * LANES), F32),
                        pltpu.VMEM((seqs_per_step, n_chunks, n_pairs, GLA_DV, 2 * LANES), BF16)],
        compiler_params=_params(1),
        name="gla",
    )(*args)


def kernel(x_prompt, x_sample, cache_k, cache_v, state_gla_fwd, state_gla_bwd, c, c_ctx,
           w_ada, b_ada, norm_pre, norm_post, ffn_w1, ffn_w3, ffn_w2, w_in,
           gla_w_up, gla_b_up, gla_norm, attn_sink, w_out):
    depth = w_in.shape[0]
    assert depth == 1, "single trunk layer"
    batch, seq = x_prompt.shape[0], x_prompt.shape[1]
    dec_batch, dec_seq = x_sample.shape[0], x_sample.shape[1]
    past = cache_k.shape[2]
    l = 0

    cond = jnp.concatenate(
        [c_ctx[None, :], c, jnp.zeros((MOD_ROWS - 1 - dec_batch, D_MODEL), F32)], axis=0)
    ada = (cond, w_ada[l], b_ada[l][None, :])
    mod_early = _ada_modulation(*ada, EARLY_MODS)

    npre, npost = norm_pre[l], norm_post[l]
    ffn_first_b = [w[l, :1].astype(BF16) for w in (ffn_w1, ffn_w3, ffn_w2)]
    ffn_second_f32 = [(w[l], 1) for w in (ffn_w1, ffn_w3, ffn_w2)]
    w_in_b = (w_in[l:l + 1, :, :OFF_LR].astype(BF16), w_in[l:l + 1, :, OFF_LR:].astype(BF16))
    w_out_b = w_out[l:l + 1].astype(BF16)
    zeros = jnp.zeros((GLA_LOW_RANK, GLA_QK), F32)
    w_up = jnp.concatenate(
        [jnp.concatenate([gla_w_up[l, 0], zeros], axis=1),
         jnp.concatenate([zeros, gla_w_up[l, 1]], axis=1)], axis=0).astype(BF16)
    b_up = gla_b_up[l].reshape(1, 2 * GLA_QK)
    gnorm = gla_norm[l][None, :]
    sink = attn_sink[l]

    x1, mod, *ffn_second_b = _ffn_first(
        x_prompt.reshape(batch * seq, D_MODEL), x_sample.reshape(dec_batch * dec_seq, D_MODEL),
        mod_early, dec_seq, npre, npost, *ffn_first_b, ada=ada, cast=ffn_second_f32)
    ctx_tiles = batch * seq // TOKEN_TILE

    def trunk(latent):
        n_batch, n_seq = (dec_batch, dec_seq) if latent else (batch, seq)
        latent_len = n_seq if latent else None
        first_tile = ctx_tiles if latent else 0
        att_in, gla_f, gv, *cache_t = _project(
            x1, first_tile, n_batch * n_seq, mod, latent_len, None if latent else n_seq, npre,
            w_in_b, w_up, b_up)
        if latent:
            att = _latent_attention(sink, att_in, cache_k[:, l].reshape(dec_batch, past, ATT_KV),
                                    cache_v[:, l].reshape(dec_batch, past, ATT_KV), n_batch, n_seq)
            (gla,) = _gla(gla_f, gv, gnorm, n_batch, n_seq, 1,
                          s0=(state_gla_fwd[:, l:l + 1], state_gla_bwd[:, l:l + 1]))
            extras = ()
        else:
            att = _context_attention(sink, att_in, n_batch, n_seq)
            gla, s_f, s_b = _gla(gla_f, gv, gnorm, n_batch, n_seq, CTX_SEQS_PER_STEP,
                                 emit_state=True)
            k_new, v_new = (jnp.transpose(c_t, (0, 1, 4, 2, 3)) for c_t in cache_t)
            extras = (k_new, v_new, s_f, s_b)
        y = _mix_ffn(x1, first_tile, att, gla, w_out_b, mod, latent_len, npre, npost,
                     *ffn_second_b)
        return y.reshape(n_batch, n_seq, D_MODEL), extras

    y_prompt, (k_new, v_new, s_f, s_b) = trunk(False)
    y_sample, _ = trunk(True)
    return (y_prompt, y_sample, k_new, v_new, s_f, s_b)
```

```python
import functools

import numpy as np
import jax
import jax.numpy as jnp
from jax import lax
from jax.experimental import pallas as pl
from jax.experimental.pallas import tpu as pltpu

F32 = jnp.float32
BF16 = jnp.bfloat16

D_MODEL = 1024
GRID_W = 64
N_Q_HEADS = 8
N_KV_HEADS = 2
HEAD_DIM = 64
WINDOW = 128
BLOCK = 128
ROPE_BASE = 10000.0
GLA_HEADS = 4
GLA_DK = 64
GLA_DV = 128
GLA_LOW_RANK = 16
GLA_TAU = 16.0
D_FF = 2816
N_MOD = 9
EPS = 1e-6
NEG = -1e30

ATT_Q = N_Q_HEADS * HEAD_DIM
ATT_KV = N_KV_HEADS * HEAD_DIM
GLA_QK = GLA_HEADS * GLA_DK
GLA_V = GLA_HEADS * GLA_DV
OFF_Q = 0
OFF_K = OFF_Q + ATT_Q
OFF_V = OFF_K + ATT_KV
OFF_GQ = OFF_V + ATT_KV
OFF_GK = OFF_GQ + GLA_QK
OFF_GV = OFF_GK + GLA_QK
OFF_GG = OFF_GV + GLA_V
OFF_LR = OFF_GG + GLA_V
IN_WIDTH = OFF_LR + 2 * GLA_LOW_RANK
A_Q = 0
A_K = A_Q + ATT_Q
A_KS = A_K + ATT_KV
A_V = A_KS + ATT_KV
A_VS = A_V + ATT_KV
ATT_IN_W = A_VS + ATT_KV
G_Q = 0
G_K = G_Q + GLA_QK
G_G = G_K + GLA_QK
G_LA = G_G + GLA_V
GLA_F_W = G_LA + 2 * GLA_QK
ROPE_PAIR = HEAD_DIM // 4
LOG2_E = 1.4426950408889634
ATT_Q_SCALE = HEAD_DIM ** -0.5 * LOG2_E

LANES = 128
SUBLANES = 8
HALF = LANES // 2
VMEM_LIMIT = 56 * 1024 * 1024

TOKEN_TILE = 512
FF_SPLITS = (0, 1536, D_FF)
EARLY_MODS = 3
CAST_SLABS = 16
ADA_K_TILE = 512
CTX_SEQS_PER_STEP = 4
ATT_CTX_SEQS_PER_STEP = 8
LAT_BLOCKS_PER_STEP = 8
GLA_CHUNK = 128
MOD_ROWS = 8


def _params(n_axes):
    return pltpu.CompilerParams(
        dimension_semantics=("arbitrary",) * n_axes, vmem_limit_bytes=VMEM_LIMIT)


def _resident(shape):
    zeros = (0,) * len(shape)
    return pl.BlockSpec(shape, lambda *_: zeros, pipeline_mode=pl.Buffered(1))


def _sigmoid(x):
    return 1.0 / (1.0 + jnp.exp(-x))


def _silu(x):
    return x * _sigmoid(x)


def _rms(x, g):
    return x * lax.rsqrt(jnp.mean(x * x, axis=-1, keepdims=True) + EPS) * g


def _dot(a, b):
    return jnp.dot(a, b, preferred_element_type=F32)


def _dot_t(a, b):
    return lax.dot_general(a, b, (((1,), (1,)), ((), ())), preferred_element_type=F32)


def _ada_kernel(cond_ref, w_ref, b_ref, o_ref):
    n_mod = o_ref.shape[0]

    @pl.when(pl.program_id(0) == 0)
    def _():
        for m in range(n_mod):
            o_ref[m] = jnp.broadcast_to(b_ref[:, m * D_MODEL:(m + 1) * D_MODEL],
                                        (MOD_ROWS, D_MODEL))
    part = _dot(_silu(cond_ref[...]).astype(BF16), w_ref[...].astype(BF16))
    for m in range(n_mod):
        o_ref[m] += part[:, m * D_MODEL:(m + 1) * D_MODEL]


def _ada_modulation(cond, w_ada, b_ada, n_mod):
    k_dim = w_ada.shape[0]
    n = n_mod * D_MODEL
    return pl.pallas_call(
        _ada_kernel,
        out_shape=jax.ShapeDtypeStruct((n_mod, MOD_ROWS, D_MODEL), F32),
        grid=(k_dim // ADA_K_TILE,),
        in_specs=[
            pl.BlockSpec((MOD_ROWS, ADA_K_TILE), lambda k: (0, k)),
            pl.BlockSpec((ADA_K_TILE, n), lambda k: (k, 0)),
            pl.BlockSpec((1, n), lambda k: (0, 0)),
        ],
        out_specs=pl.BlockSpec((n_mod, MOD_ROWS, D_MODEL), lambda k: (0, 0, 0)),
        compiler_params=_params(1),
        name="ada_modulation",
    )(cond, w_ada, b_ada)


class _Mod:
    def __init__(self, mod_ref, tiles_per_seq, first_latent_tile=0, first_mod=0):
        self.ref = mod_ref
        self.first_mod = first_mod
        if tiles_per_seq is None:
            self.row = 0
        else:
            tile = pl.program_id(0) - first_latent_tile
            self.row = jnp.where(tile >= 0, 1 + tile // tiles_per_seq, 0)

    def __getitem__(self, m):
        return self.ref[m - self.first_mod, pl.ds(self.row, 1), :]


def _modulated(x, mod, npre_ref, i):
    return _rms(x, npre_ref[i:i + 1, :]) * (1.0 + mod[3 * i + 1]) + mod[3 * i]


def _residual(x, out, mod, npost_ref, i, weight):
    return x + (weight * mod[3 * i + 2]) * _rms(out, npost_ref[i:i + 1, :])


def _ffn_sublayer(x, i, mod, npre_ref, npost_ref, w1_ref, w3_ref, w2_ref):
    h = _modulated(x, mod, npre_ref, i).astype(BF16)
    acc = None
    for lo, hi in zip(FF_SPLITS[:-1], FF_SPLITS[1:]):
        a = _dot(h, w1_ref[:, lo:hi])
        g = _dot(h, w3_ref[:, lo:hi])
        part = _dot((_silu(a) * g).astype(BF16), w2_ref[lo:hi, :])
        acc = part if acc is None else acc + part
    return _residual(x, acc, mod, npost_ref, i, 0.5)


def _ffn_first_kernel(*refs, ctx_tiles, tiles_per_seq, n_cast):
    xc_ref, xl_ref, mod_ref, npre_ref, npost_ref, w1_ref, w3_ref, w2_ref = refs[:8]
    cond_ref, wada_ref, bada_ref = refs[8:11]
    cast_src = refs[11:11 + n_cast]
    o_ref, late_mod_ref = refs[11 + n_cast:13 + n_cast]
    cast_dst = refs[13 + n_cast:]
    mod = _Mod(mod_ref, tiles_per_seq, ctx_tiles)
    x = jnp.where(pl.program_id(0) >= ctx_tiles, xl_ref[...], xc_ref[...])
    o_ref[...] = _ffn_sublayer(x, 0, mod, npre_ref, npost_ref, w1_ref, w3_ref, w2_ref)
    for src, dst in zip(cast_src, cast_dst):
        dst[...] = src[...].astype(BF16)
    late_mod_ref[...] = _dot(_silu(cond_ref[...]).astype(BF16),
                             wada_ref[...].astype(BF16)) + bada_ref[...]


def _mix_ffn_kernel(x_ref, att_ref, gla_ref, wo_ref, mod_ref, npre_ref, npost_ref,
                    w1_ref, w3_ref, w2_ref, o_ref, *, tiles_per_seq):
    mod = _Mod(mod_ref, tiles_per_seq, first_mod=EARLY_MODS)
    mix = _dot(att_ref[...], wo_ref[0:ATT_Q, :]) + _dot(gla_ref[...], wo_ref[ATT_Q:, :])
    x = _residual(x_ref[...], mix, mod, npost_ref, 1, 1.0)
    o_ref[...] = _ffn_sublayer(x, 2, mod, npre_ref, npost_ref, w1_ref, w3_ref, w2_ref)


def _rope_tile(x, cos, sin_up, sin_dn):
    up = pltpu.roll(x, LANES - ROPE_PAIR, axis=1)
    dn = pltpu.roll(x, ROPE_PAIR, axis=1)
    return x * cos + up * sin_up + dn * sin_dn


def _proj_kernel(*refs, tiles_per_seq, cache_seq):
    x_ref, mod_ref, npre_ref, win_ref, wup_ref, bup_ref = refs[:6]
    rope_refs = refs[6:9] if tiles_per_seq is not None else None
    outs = refs[6 + (3 if rope_refs else 0):]
    att_ref, glaf_ref, gv_ref = outs[:3]
    mod = _Mod(mod_ref, tiles_per_seq, first_mod=EARLY_MODS)
    h = _modulated(x_ref[...], mod, npre_ref, 1).astype(BF16)
    q = _dot(h, win_ref[:, OFF_Q:OFF_K])
    kv = _dot(h, win_ref[:, OFF_K:OFF_GQ])
    k, v = kv[:, :ATT_KV], kv[:, ATT_KV:]
    lr = _dot(h, win_ref[:, OFF_LR:IN_WIDTH])
    z = _dot(lr.astype(BF16), wup_ref[...]) + bup_ref[...]
    if rope_refs:
        cos, sup, sdn = (r[...] for r in rope_refs)
        q = jnp.concatenate([_rope_tile(q[:, j * LANES:(j + 1) * LANES], cos, sup, sdn)
                             for j in range(ATT_Q // LANES)], axis=1)
        k_att = _rope_tile(k, cos, sup, sdn)
    else:
        k_att = k
    att_ref[:, A_Q:A_K] = (q * ATT_Q_SCALE).astype(BF16)
    att_ref[:, A_K:A_KS] = k_att.astype(BF16)
    att_ref[:, A_KS:A_V] = pltpu.roll(k_att, HALF, axis=1).astype(BF16)
    att_ref[:, A_V:A_VS] = v.astype(BF16)
    att_ref[:, A_VS:ATT_IN_W] = pltpu.roll(v, HALF, axis=1).astype(BF16)
    if cache_seq is not None:
        for t, dst in ((k.T, outs[3]), (v.T, outs[4])):
            for b in range(TOKEN_TILE // cache_seq):
                dst[b, 0] = t[:, b * cache_seq:(b + 1) * cache_seq].reshape(
                    N_KV_HEADS, HEAD_DIM, cache_seq)
    glaf_ref[:, G_Q:G_G] = _dot(h, win_ref[:, OFF_GQ:OFF_GV])
    log_sig = jnp.minimum(z, 0.0) - jnp.log(1.0 + jnp.exp(-jnp.abs(z)))
    glaf_ref[:, G_LA:GLA_F_W] = log_sig * (1.0 / GLA_TAU)
    gv_ref[...] = _dot(h, win_ref[:, OFF_GV:OFF_GG]).astype(BF16)
    glaf_ref[:, G_G:G_LA] = _dot(h, win_ref[:, OFF_GG:OFF_LR])


def _token_spec(width, first_tile=0):
    return pl.BlockSpec((TOKEN_TILE, width), lambda i: (i + first_tile, 0))


def _stack_spec(shape, j):
    return pl.BlockSpec((None,) + tuple(shape[1:]), lambda i: (j, 0, 0),
                        pipeline_mode=pl.Buffered(1))


def _tiles_per_seq(latent_len):
    return None if latent_len is None else latent_len // TOKEN_TILE


def _ffn_first(x_ctx, x_lat, mod, latent_len, npre, npost, w1, w3, w2, ada, cast):
    ctx_tiles = x_ctx.shape[0] // TOKEN_TILE
    steps = ctx_tiles + x_lat.shape[0] // TOKEN_TILE
    assert steps >= CAST_SLABS, "every slab needs its own grid step"
    cond, w_ada, b_ada = ada
    late_mods = N_MOD - EARLY_MODS
    ada_cols = late_mods * D_MODEL // steps
    per_mod = D_MODEL // ada_cols
    assert ada_cols * steps == late_mods * D_MODEL and ada_cols % LANES == 0
    first_slab = EARLY_MODS * per_mod
    in_specs = [pl.BlockSpec((TOKEN_TILE, D_MODEL), lambda i: (jnp.minimum(i, ctx_tiles - 1), 0)),
                pl.BlockSpec((TOKEN_TILE, D_MODEL), lambda i: (jnp.maximum(i - ctx_tiles, 0), 0)),
                _resident(mod.shape), _resident(npre.shape),
                _resident(npost.shape), _stack_spec(w1.shape, 0),
                _stack_spec(w3.shape, 0), _stack_spec(w2.shape, 0),
                _resident(cond.shape),
                pl.BlockSpec((D_MODEL, ada_cols), lambda i: (0, first_slab + i)),
                pl.BlockSpec((1, ada_cols), lambda i: (0, first_slab + i))]
    out_shape = [jax.ShapeDtypeStruct((steps * TOKEN_TILE, D_MODEL), F32),
                 jax.ShapeDtypeStruct((late_mods, MOD_ROWS, D_MODEL), F32)]
    out_specs = [_token_spec(D_MODEL),
                 pl.BlockSpec((None, MOD_ROWS, ada_cols), lambda i: (i // per_mod, 0, i % per_mod))]
    for stack, j in cast:
        _, rows, cols = stack.shape
        slab = rows // CAST_SLABS
        assert slab * CAST_SLABS == rows and slab % (2 * SUBLANES) == 0, "whole bf16 row tiles"
        in_specs.append(pl.BlockSpec((None, slab, cols), functools.partial(
            lambda i, j: (j, i * CAST_SLABS // steps, 0), j=j)))
        out_shape.append(jax.ShapeDtypeStruct((1, rows, cols), BF16))
        out_specs.append(pl.BlockSpec((None, slab, cols), lambda i: (0, i * CAST_SLABS // steps, 0)))
    return pl.pallas_call(
        functools.partial(_ffn_first_kernel, ctx_tiles=ctx_tiles,
                          tiles_per_seq=_tiles_per_seq(latent_len), n_cast=len(cast)),
        out_shape=out_shape,
        grid=(steps,),
        in_specs=in_specs,
        out_specs=out_specs,
        compiler_params=_params(1),
        name="ffn_first",
    )(x_ctx, x_lat, mod, npre, npost, w1, w3, w2, cond, w_ada, b_ada,
      *[stack for stack, _ in cast])


def _mix_ffn(x, first_tile, att, gla, w_out, mod, latent_len, npre, npost, w1, w3, w2):
    t = att.shape[0]
    return pl.pallas_call(
        functools.partial(_mix_ffn_kernel, tiles_per_seq=_tiles_per_seq(latent_len)),
        out_shape=jax.ShapeDtypeStruct((t, D_MODEL), F32),
        grid=(t // TOKEN_TILE,),
        in_specs=[_token_spec(D_MODEL, first_tile), _token_spec(ATT_Q), _token_spec(GLA_V),
                  _stack_spec(w_out.shape, 0), _resident(mod.shape), _resident(npre.shape),
                  _resident(npost.shape), _stack_spec(w1.shape, 0),
                  _stack_spec(w3.shape, 0), _stack_spec(w2.shape, 0)],
        out_specs=_token_spec(D_MODEL),
        compiler_params=_params(1),
        name="mix_ffn",
    )(x, att, gla, w_out, mod, npre, npost, w1, w3, w2)


def _project(x, first_tile, t, mod, latent_len, cache_seq, npre, w_in, w_up, b_up):
    tiles = _tiles_per_seq(latent_len)
    in_specs = [_token_spec(D_MODEL, first_tile), _resident(mod.shape), _resident(npre.shape),
                _stack_spec(w_in.shape, 0), _resident(w_up.shape), _resident(b_up.shape)]
    args = [x, mod, npre, w_in, w_up, b_up]
    if tiles is not None:
        in_specs += [pl.BlockSpec((TOKEN_TILE, LANES), lambda i: (i % tiles, 0))] * 3
        args += [jnp.asarray(tab) for tab in _rope_tables(latent_len)]
    outs = ((ATT_IN_W, BF16), (GLA_F_W, F32), (GLA_V, BF16))
    out_shape = [jax.ShapeDtypeStruct((t, w), dt) for w, dt in outs]
    out_specs = [_token_spec(w) for w, _ in outs]
    if cache_seq is not None:
        seqs = TOKEN_TILE // cache_seq
        cache = (t // cache_seq, 1, N_KV_HEADS, HEAD_DIM, cache_seq)
        out_shape += [jax.ShapeDtypeStruct(cache, F32)] * 2
        out_specs += [pl.BlockSpec((seqs,) + cache[1:], lambda i: (i, 0, 0, 0, 0))] * 2
    return pl.pallas_call(
        functools.partial(_proj_kernel, tiles_per_seq=tiles, cache_seq=cache_seq),
        out_shape=out_shape,
        grid=(t // TOKEN_TILE,),
        in_specs=in_specs,
        out_specs=out_specs,
        compiler_params=_params(1),
        name="project",
    )(*args)


def _rope_tables(seq_len):
    half = HEAD_DIM // 2
    inv_freq = np.float32(ROPE_BASE) ** (-np.arange(0, half, 2, dtype=np.float32) / half)
    pos = np.arange(seq_len)
    row = (pos // GRID_W).astype(np.float32)
    col = (pos % GRID_W).astype(np.float32)
    within = np.arange(LANES) % HEAD_DIM
    idx = within % half
    freq = inv_freq[idx % ROPE_PAIR].astype(np.float32)
    p = np.where((within // half == 0)[None, :], row[:, None], col[:, None])
    ang = (p * freq[None, :]).astype(np.float32)
    cos, sin = np.cos(ang).astype(np.float32), np.sin(ang).astype(np.float32)
    first = (idx < ROPE_PAIR)[None, :]
    zero = np.float32(0.0)
    return cos, np.where(first, -sin, zero), np.where(first, zero, sin)


def _attend(q_ref, q_rows, nq, segs, sink_ref):
    lane = lax.broadcasted_iota(jnp.int32, (1, LANES), 1)
    half_of = [lane < HALF, lane >= HALF]
    rows = lax.broadcasted_iota(jnp.int32, (2 * nq, 1), 0)
    zero = jnp.zeros((), BF16)
    out = [None] * (ATT_Q // LANES)
    for g in range(N_KV_HEADS):
        tiles = (2 * g, 2 * g + 1)
        for e in range(2):
            qm = jnp.concatenate(
                [jnp.where(half_of[e], q_ref[q_rows, j * LANES:(j + 1) * LANES], zero)
                 for j in tiles], axis=0)
            sink = jnp.where(rows < nq, sink_ref[2 * tiles[0] + e],
                             sink_ref[2 * tiles[1] + e]) * LOG2_E
            scores = []
            for k, k_sw, _, _, m in segs:
                s = _dot_t(qm, k if e == g else k_sw)
                scores.append(s if m is None else jnp.where(m, s, NEG))
            mx = sink
            for s in scores:
                mx = jnp.maximum(mx, jnp.max(s, axis=-1, keepdims=True))
            probs = [jnp.exp2(s - mx) for s in scores]
            den = jnp.exp2(sink - mx)
            for p in probs:
                den = den + jnp.sum(p, axis=-1, keepdims=True)
            o = None
            for p, (_, _, v, v_sw, _) in zip(probs, segs):
                part = _dot(p.astype(BF16), jnp.where(half_of[e], v if e == g else v_sw, zero))
                o = part if o is None else o + part
            o = o * (1.0 / den)
            for r, j in enumerate(tiles):
                blk = o[r * nq:(r + 1) * nq, :]
                out[j] = blk if out[j] is None else out[j] + blk
    return out


def _context_attn_kernel(sink_ref, a_ref, o_ref, *, seq_len):
    for s in range(a_ref.shape[0] // seq_len):
        rows = slice(s * seq_len, (s + 1) * seq_len)
        segs = [(a_ref[rows, A_K:A_KS], a_ref[rows, A_KS:A_V],
                 a_ref[rows, A_V:A_VS], a_ref[rows, A_VS:ATT_IN_W], None)]
        for j, t in enumerate(_attend(a_ref, rows, seq_len, segs, sink_ref)):
            o_ref[rows, j * LANES:(j + 1) * LANES] = t.astype(o_ref.dtype)


def _latent_attn_kernel(sink_ref, q_ref, kv_ref, kc_ref, vc_ref, o_ref, *, seq_len):
    k_ctx, v_ctx = kc_ref[0], vc_ref[0]
    ctx = (k_ctx.astype(BF16), pltpu.roll(k_ctx, HALF, axis=1).astype(BF16),
           v_ctx.astype(BF16), pltpu.roll(v_ctx, HALF, axis=1).astype(BF16), None)
    span = 3 * BLOCK
    per_step = q_ref.shape[0] // BLOCK
    for s in range(per_step):
        i = pl.program_id(1) * per_step + s
        start = pl.multiple_of(jnp.clip((i - 1) * BLOCK, 0, seq_len - span), BLOCK)
        kpos = start + lax.broadcasted_iota(jnp.int32, (1, span), 1)
        qpos = i * BLOCK + lax.broadcasted_iota(jnp.int32, (BLOCK, 1), 0)
        valid = jnp.abs(kpos - qpos) <= WINDOW
        valid2 = jnp.concatenate([valid, valid], axis=0)
        kv = kv_ref[pl.ds(start, span), :]
        segs = [tuple(kv[:, c * ATT_KV:(c + 1) * ATT_KV] for c in range(4)) + (valid2,), ctx]
        rows = slice(s * BLOCK, (s + 1) * BLOCK)
        for j, t in enumerate(_attend(q_ref, rows, BLOCK, segs, sink_ref)):
            o_ref[rows, j * LANES:(j + 1) * LANES] = t.astype(o_ref.dtype)


def _smem_spec():
    return pl.BlockSpec(memory_space=pltpu.SMEM)


def _context_attention(sink, att_in, batch, seq_len):
    rows = ATT_CTX_SEQS_PER_STEP * seq_len

    def seq(width):
        return pl.BlockSpec((rows, width), lambda b: (b, 0))
    return pl.pallas_call(
        functools.partial(_context_attn_kernel, seq_len=seq_len),
        out_shape=jax.ShapeDtypeStruct((batch * seq_len, ATT_Q), BF16),
        grid=(batch // ATT_CTX_SEQS_PER_STEP,),
        in_specs=[_smem_spec(), seq(ATT_IN_W)],
        out_specs=seq(ATT_Q),
        compiler_params=_params(1),
        name="context_attention",
    )(sink, att_in)


def _latent_attention(sink, att_in, k_ctx, v_ctx, batch, seq_len):
    steps = seq_len // (LAT_BLOCKS_PER_STEP * BLOCK)
    rows = LAT_BLOCKS_PER_STEP * BLOCK
    past = k_ctx.shape[1]
    assert A_K == ATT_IN_W - A_K, "q and the key / value columns are the two halves of att_in"
    return pl.pallas_call(
        functools.partial(_latent_attn_kernel, seq_len=seq_len),
        out_shape=jax.ShapeDtypeStruct((batch * seq_len, ATT_Q), BF16),
        grid=(batch, steps),
        in_specs=[_smem_spec(),
                  pl.BlockSpec((rows, ATT_Q), lambda b, i: (b * steps + i, 0)),
                  pl.BlockSpec((seq_len, ATT_IN_W - A_K), lambda b, i: (b, 1)),
                  pl.BlockSpec((1, past, ATT_KV), lambda b, i: (b, 0, 0)),
                  pl.BlockSpec((1, past, ATT_KV), lambda b, i: (b, 0, 0))],
        out_specs=pl.BlockSpec((rows, ATT_Q), lambda b, i: (b * steps + i, 0)),
        compiler_params=_params(2),
        name="latent_attention",
    )(sink, att_in, att_in, k_ctx, v_ctx)


def _split2(x):
    hi = x.astype(BF16)
    lo = (x - hi.astype(F32)).astype(BF16)
    return hi, lo


def _gla_kernel(*refs, n_chunks, has_s0, emit_state):
    refs = list(refs)
    gf_ref, gv_ref, gn_ref = refs[:3]
    pos = 3
    s0_refs = refs[pos:pos + 2] if has_s0 else None
    pos += 2 if has_s0 else 0
    o_ref = refs[pos]
    pos += 1
    sfin_refs = refs[pos:pos + 2] if emit_state else None
    pos += 2 if emit_state else 0
    cum_ref, kv_ref, sent_ref = refs[pos:]

    C = GLA_CHUNK
    n_pairs = GLA_QK // LANES
    lane = lax.broadcasted_iota(jnp.int32, (1, LANES), 1)
    half_of = [lane < HALF, lane >= HALF]
    r_i = lax.broadcasted_iota(jnp.int32, (C, C), 0)
    c_i = lax.broadcasted_iota(jnp.int32, (C, C), 1)
    lower = c_i <= r_i
    upper = c_i >= r_i
    tri = jnp.concatenate([jnp.where(lower, 1.0, 0.0), jnp.where(upper, 1.0, 0.0)],
                          axis=0).astype(BF16)
    zeros_cc = jnp.zeros((C, C), F32)
    gnorm = gn_ref[...]
    qscale = GLA_DK ** -0.5

    def one_sequence(s):
        def chunk_rows(n):
            return slice((s * n_chunks + n) * C, (s * n_chunks + n + 1) * C)

        for n in range(n_chunks):
            rows = chunk_rows(n)
            hi, lo = _split2(gf_ref[rows, G_LA:GLA_F_W])
            sums = _dot(tri, jnp.concatenate([hi, lo], axis=1))
            cum_f = sums[:C, 0:GLA_QK] + sums[:C, 2 * GLA_QK:3 * GLA_QK]
            cum_b = sums[C:, GLA_QK:2 * GLA_QK] + sums[C:, 3 * GLA_QK:]
            cum_ref[0, rows, :] = cum_f
            cum_ref[1, rows, :] = cum_b
            k = gf_ref[rows, G_K:G_G]
            k_in = (k * jnp.exp(cum_f[C - 1:C, :] - cum_f), k * jnp.exp(cum_b[0:1, :] - cum_b))
            for p in range(n_pairs):
                sl = slice(p * LANES, (p + 1) * LANES)
                kv_t = None
                for e in range(2):
                    h = 2 * p + e
                    v_t = gv_ref[rows, h * GLA_DV:(h + 1) * GLA_DV].T
                    k_e = jnp.concatenate(
                        [jnp.where(half_of[e], k_in[d][:, sl], 0.0) for d in range(2)], axis=1)
                    part = _dot(v_t, k_e.astype(BF16))
                    kv_t = part if kv_t is None else kv_t + part
                kv_ref[s, n, p] = kv_t

        for p in range(n_pairs):
            sl = slice(p * LANES, (p + 1) * LANES)
            st = []
            for d in range(2):
                if has_s0:
                    s0 = s0_refs[d][s, 0, 2 * p:2 * p + 2, :, :].reshape(2 * GLA_DK, GLA_DV)
                    st.append(s0.T)
                else:
                    st.append(jnp.zeros((GLA_DV, 2 * GLA_DK), F32))
            for i in range(n_chunks):
                for d, n in ((0, i), (1, n_chunks - 1 - i)):
                    tot_row = (s * n_chunks + n) * C + (C - 1 if d == 0 else 0)
                    decay = jnp.exp(cum_ref[d, tot_row:tot_row + 1, sl])
                    sent_ref[s, n, p, :, d * LANES:(d + 1) * LANES] = st[d].astype(BF16)
                    st[d] = decay * st[d] + kv_ref[s, n, p, :, d * LANES:(d + 1) * LANES]
            if emit_state:
                for d in range(2):
                    sfin_refs[d][s, 0, 2 * p:2 * p + 2, :, :] = st[d].T.reshape(2, GLA_DK, GLA_DV)

        for n in range(n_chunks):
            rows = chunk_rows(n)
            q = gf_ref[rows, G_Q:G_K] * qscale
            k = gf_ref[rows, G_K:G_G]
            qs, ks, qin = [], [], []
            for d in range(2):
                cum = cum_ref[d, rows, :]
                ref = cum[C // 2:C // 2 + 1, :]
                qs.append(q * jnp.exp(cum - ref))
                ks.append((k * jnp.exp(ref - cum)).astype(BF16))
                qin.append(q * jnp.exp(cum))
            for p in range(n_pairs):
                sl = slice(p * LANES, (p + 1) * LANES)
                lhs = jnp.concatenate(
                    [jnp.where(half_of[e], qs[d][:, sl], 0.0) for e in range(2) for d in range(2)],
                    axis=0).astype(BF16)
                sc = _dot_t(lhs, jnp.concatenate([ks[0][:, sl], ks[1][:, sl]], axis=0))
                prob = []
                for e in range(2):
                    s_f = sc[(2 * e) * C:(2 * e + 1) * C, :C]
                    s_b = sc[(2 * e + 1) * C:(2 * e + 2) * C, C:]
                    prob.append(jnp.where(lower, s_f, 0.0) + jnp.where(upper, s_b, 0.0))
                p_blk = jnp.concatenate(
                    [jnp.concatenate([prob[0], zeros_cc], axis=1),
                     jnp.concatenate([zeros_cc, prob[1]], axis=1)], axis=0).astype(BF16)
                v2 = jnp.concatenate(
                    [gv_ref[rows, (2 * p + e) * GLA_DV:(2 * p + e + 1) * GLA_DV] for e in range(2)],
                    axis=0)
                q_in = jnp.concatenate(
                    [jnp.concatenate([jnp.where(half_of[e], qin[d][:, sl], 0.0) for d in range(2)],
                                     axis=1) for e in range(2)], axis=0).astype(BF16)
                o2 = _dot(p_blk, v2) + _dot_t(q_in, sent_ref[s, n, p])
                for e in range(2):
                    h = 2 * p + e
                    o = o2[e * C:(e + 1) * C, :]
                    o = o * lax.rsqrt(jnp.mean(o * o, axis=-1, keepdims=True) + EPS) * gnorm
                    gate = gf_ref[rows, G_G + h * GLA_DV:G_G + (h + 1) * GLA_DV]
                    o_ref[rows, h * GLA_DV:(h + 1) * GLA_DV] = (o * _silu(gate)).astype(o_ref.dtype)

    for s in range(kv_ref.shape[0]):
        one_sequence(s)


def _gla(gla_f, gv, gnorm, batch, seq_len, seqs_per_step, s0=None, emit_state=False):
    n_chunks = seq_len // GLA_CHUNK
    has_s0 = s0 is not None
    rows = seqs_per_step * seq_len

    def seq(width):
        return pl.BlockSpec((rows, width), lambda b: (b, 0))
    state_spec = pl.BlockSpec((seqs_per_step, 1, GLA_HEADS, GLA_DK, GLA_DV),
                              lambda b: (b, 0, 0, 0, 0))
    in_specs = [seq(GLA_F_W), seq(GLA_V), pl.BlockSpec((1, GLA_DV), lambda b: (0, 0))]
    args = [gla_f, gv, gnorm]
    if has_s0:
        in_specs += [state_spec, state_spec]
        args += list(s0)
    out_shape = [jax.ShapeDtypeStruct((batch * seq_len, GLA_V), BF16)]
    out_specs = [seq(GLA_V)]
    if emit_state:
        out_shape += [jax.ShapeDtypeStruct((batch, 1, GLA_HEADS, GLA_DK, GLA_DV), F32)] * 2
        out_specs += [state_spec, state_spec]
    n_pairs = GLA_QK // LANES
    return pl.pallas_call(
        functools.partial(_gla_kernel, n_chunks=n_chunks, has_s0=has_s0, emit_state=emit_state),
        out_shape=out_shape,
        grid=(batch // seqs_per_step,),
        in_specs=in_specs,
        out_specs=out_specs,
        scratch_shapes=[pltpu.VMEM((2, rows, GLA_QK), F32),
                        pltpu.VMEM((seqs_per_step, n_chunks, n_pairs, GLA_DV, 2 * LANES), F32),
                        pltpu.VMEM((seqs_per_step, n_chunks, n_pairs, GLA_DV, 2 * LANES), BF16)],
        compiler_params=_params(1),
        name="gla",
    )(*args)


def kernel(x_prompt, x_sample, cache_k, cache_v, state_gla_fwd, state_gla_bwd, c, c_ctx,
           w_ada, b_ada, norm_pre, norm_post, ffn_w1, ffn_w3, ffn_w2, w_in,
           gla_w_up, gla_b_up, gla_norm, attn_sink, w_out):
    depth = w_in.shape[0]
    assert depth == 1, "single trunk layer"
    batch, seq = x_prompt.shape[0], x_prompt.shape[1]
    dec_batch, dec_seq = x_sample.shape[0], x_sample.shape[1]
    past = cache_k.shape[2]
    l = 0

    cond = jnp.concatenate(
        [c_ctx[None, :], c, jnp.zeros((MOD_ROWS - 1 - dec_batch, D_MODEL), F32)], axis=0)
    ada = (cond, w_ada[l], b_ada[l][None, :])
    mod_early = _ada_modulation(*ada, EARLY_MODS)

    npre, npost = norm_pre[l], norm_post[l]
    ffn_first_b = [w[l, :1].astype(BF16) for w in (ffn_w1, ffn_w3, ffn_w2)]
    ffn_second_f32 = [(w[l], 1) for w in (ffn_w1, ffn_w3, ffn_w2)]
    w_in_b = w_in[l:l + 1].astype(BF16)
    w_out_b = w_out[l:l + 1].astype(BF16)
    zeros = jnp.zeros((GLA_LOW_RANK, GLA_QK), F32)
    w_up = jnp.concatenate(
        [jnp.concatenate([gla_w_up[l, 0], zeros], axis=1),
         jnp.concatenate([zeros, gla_w_up[l, 1]], axis=1)], axis=0).astype(BF16)
    b_up = gla_b_up[l].reshape(1, 2 * GLA_QK)
    gnorm = gla_norm[l][None, :]
    sink = attn_sink[l]

    x1, mod, *ffn_second_b = _ffn_first(
        x_prompt.reshape(batch * seq, D_MODEL), x_sample.reshape(dec_batch * dec_seq, D_MODEL),
        mod_early, dec_seq, npre, npost, *ffn_first_b, ada=ada, cast=ffn_second_f32)
    ctx_tiles = batch * seq // TOKEN_TILE

    def trunk(latent):
        n_batch, n_seq = (dec_batch, dec_seq) if latent else (batch, seq)
        latent_len = n_seq if latent else None
        first_tile = ctx_tiles if latent else 0
        att_in, gla_f, gv, *cache_t = _project(
            x1, first_tile, n_batch * n_seq, mod, latent_len, None if latent else n_seq, npre,
            w_in_b, w_up, b_up)
        if latent:
            att = _latent_attention(sink, att_in, cache_k[:, l].reshape(dec_batch, past, ATT_KV),
                                    cache_v[:, l].reshape(dec_batch, past, ATT_KV), n_batch, n_seq)
            (gla,) = _gla(gla_f, gv, gnorm, n_batch, n_seq, 1,
                          s0=(state_gla_fwd[:, l:l + 1], state_gla_bwd[:, l:l + 1]))
            extras = ()
        else:
            att = _context_attention(sink, att_in, n_batch, n_seq)
            gla, s_f, s_b = _gla(gla_f, gv, gnorm, n_batch, n_seq, CTX_SEQS_PER_STEP,
                                 emit_state=True)
            k_new, v_new = (jnp.transpose(c_t, (0, 1, 4, 2, 3)) for c_t in cache_t)
            extras = (k_new, v_new, s_f, s_b)
        y = _mix_ffn(x1, first_tile, att, gla, w_out_b, mod, latent_len, npre, npost,
                     *ffn_second_b)
        return y.reshape(n_batch, n_seq, D_MODEL), extras

    y_prompt, (k_new, v_new, s_f, s_b) = trunk(False)
    y_sample, _ = trunk(True)
    return (y_prompt, y_sample, k_new, v_new, s_f, s_b)
```

```python
import functools

import numpy as np
import jax
import jax.numpy as jnp
from jax import lax
from jax.experimental import pallas as pl
from jax.experimental.pallas import tpu as pltpu

F32 = jnp.float32
BF16 = jnp.bfloat16

D_MODEL = 1024
GRID_W = 64
N_Q_HEADS = 8
N_KV_HEADS = 2
HEAD_DIM = 64
WINDOW = 128
BLOCK = 128
ROPE_BASE = 10000.0
GLA_HEADS = 4
GLA_DK = 64
GLA_DV = 128
GLA_LOW_RANK = 16
GLA_TAU = 16.0
D_FF = 2816
N_MOD = 9
EPS = 1e-6
NEG = -1e30

ATT_Q = N_Q_HEADS * HEAD_DIM
ATT_KV = N_KV_HEADS * HEAD_DIM
GLA_QK = GLA_HEADS * GLA_DK
GLA_V = GLA_HEADS * GLA_DV
OFF_Q = 0
OFF_K = OFF_Q + ATT_Q
OFF_V = OFF_K + ATT_KV
OFF_GQ = OFF_V + ATT_KV
OFF_GK = OFF_GQ + GLA_QK
OFF_GV = OFF_GK + GLA_QK
OFF_GG = OFF_GV + GLA_V
OFF_LR = OFF_GG + GLA_V
IN_WIDTH = OFF_LR + 2 * GLA_LOW_RANK
A_Q = 0
A_K = A_Q + ATT_Q
A_KS = A_K + ATT_KV
A_V = A_KS + ATT_KV
A_VS = A_V + ATT_KV
ATT_IN_W = A_VS + ATT_KV
G_Q = 0
G_K = G_Q + GLA_QK
G_G = G_K + GLA_QK
G_LA = G_G + GLA_V
GLA_F_W = G_LA + 2 * GLA_QK
ROPE_PAIR = HEAD_DIM // 4
LOG2_E = 1.4426950408889634
ATT_Q_SCALE = HEAD_DIM ** -0.5 * LOG2_E

LANES = 128
SUBLANES = 8
HALF = LANES // 2
VMEM_LIMIT = 56 * 1024 * 1024
FFN_FIRST_VMEM_MIB = 50
MIX_FFN_VMEM_MIB = 46
PROJECT_VMEM_MIB = 28

TOKEN_TILE = 512
FF_SPLITS = (0, 1536, D_FF)
EARLY_MODS = 3
CAST_SLABS = 16
ADA_K_TILE = 512
CTX_SEQS_PER_STEP = 4
LAT_BLOCKS_PER_STEP = 4
GLA_CHUNK = 128
MOD_ROWS = 8


def _params(n_axes, vmem_mib=None):
    limit = VMEM_LIMIT if vmem_mib is None else vmem_mib * 1024 * 1024
    return pltpu.CompilerParams(
        dimension_semantics=("arbitrary",) * n_axes, vmem_limit_bytes=limit)


def _resident(shape):
    zeros = (0,) * len(shape)
    return pl.BlockSpec(shape, lambda *_: zeros, pipeline_mode=pl.Buffered(1))


def _sigmoid(x):
    return 1.0 / (1.0 + jnp.exp(-x))


def _silu(x):
    return x * _sigmoid(x)


def _rms(x, g):
    return x * lax.rsqrt(jnp.mean(x * x, axis=-1, keepdims=True) + EPS) * g


def _dot(a, b):
    return jnp.dot(a, b, preferred_element_type=F32)


def _dot_t(a, b):
    return lax.dot_general(a, b, (((1,), (1,)), ((), ())), preferred_element_type=F32)


def _ada_kernel(cond_ref, w_ref, b_ref, o_ref):
    n_mod = o_ref.shape[0]

    @pl.when(pl.program_id(0) == 0)
    def _():
        for m in range(n_mod):
            o_ref[m] = jnp.broadcast_to(b_ref[:, m * D_MODEL:(m + 1) * D_MODEL],
                                        (MOD_ROWS, D_MODEL))
    part = _dot(_silu(cond_ref[...]).astype(BF16), w_ref[...].astype(BF16))
    for m in range(n_mod):
        o_ref[m] += part[:, m * D_MODEL:(m + 1) * D_MODEL]


def _ada_modulation(cond, w_ada, b_ada, n_mod):
    k_dim = w_ada.shape[0]
    n = n_mod * D_MODEL
    return pl.pallas_call(
        _ada_kernel,
        out_shape=jax.ShapeDtypeStruct((n_mod, MOD_ROWS, D_MODEL), F32),
        grid=(k_dim // ADA_K_TILE,),
        in_specs=[
            pl.BlockSpec((MOD_ROWS, ADA_K_TILE), lambda k: (0, k)),
            pl.BlockSpec((ADA_K_TILE, n), lambda k: (k, 0)),
            pl.BlockSpec((1, n), lambda k: (0, 0)),
        ],
        out_specs=pl.BlockSpec((n_mod, MOD_ROWS, D_MODEL), lambda k: (0, 0, 0)),
        compiler_params=_params(1),
        name="ada_modulation",
    )(cond, w_ada, b_ada)


class _Mod:
    def __init__(self, mod_ref, tiles_per_seq, first_latent_tile=0, first_mod=0):
        self.ref = mod_ref
        self.first_mod = first_mod
        if tiles_per_seq is None:
            self.row = 0
        else:
            tile = pl.program_id(0) - first_latent_tile
            self.row = jnp.where(tile >= 0, 1 + tile // tiles_per_seq, 0)

    def __getitem__(self, m):
        return self.ref[m - self.first_mod, pl.ds(self.row, 1), :]


def _modulated(x, mod, npre_ref, i):
    return _rms(x, npre_ref[i:i + 1, :]) * (1.0 + mod[3 * i + 1]) + mod[3 * i]


def _residual(x, out, mod, npost_ref, i, weight):
    return x + (weight * mod[3 * i + 2]) * _rms(out, npost_ref[i:i + 1, :])


def _ffn_sublayer(x, i, mod, npre_ref, npost_ref, w1_ref, w3_ref, w2_ref):
    h = _modulated(x, mod, npre_ref, i).astype(BF16)
    acc = None
    for lo, hi in zip(FF_SPLITS[:-1], FF_SPLITS[1:]):
        a = _dot(h, w1_ref[:, lo:hi])
        g = _dot(h, w3_ref[:, lo:hi])
        part = _dot((_silu(a) * g).astype(BF16), w2_ref[lo:hi, :])
        acc = part if acc is None else acc + part
    return _residual(x, acc, mod, npost_ref, i, 0.5)


def _ffn_first_kernel(*refs, ctx_tiles, tiles_per_seq, n_cast):
    xc_ref, xl_ref, mod_ref, npre_ref, npost_ref, w1_ref, w3_ref, w2_ref = refs[:8]
    cond_ref, wada_ref, bada_ref = refs[8:11]
    cast_src = refs[11:11 + n_cast]
    o_ref, late_mod_ref = refs[11 + n_cast:13 + n_cast]
    cast_dst = refs[13 + n_cast:]
    mod = _Mod(mod_ref, tiles_per_seq, ctx_tiles)
    x = jnp.where(pl.program_id(0) >= ctx_tiles, xl_ref[...], xc_ref[...])
    o_ref[...] = _ffn_sublayer(x, 0, mod, npre_ref, npost_ref, w1_ref, w3_ref, w2_ref)
    for src, dst in zip(cast_src, cast_dst):
        dst[...] = src[...].astype(BF16)
    late_mod_ref[...] = _dot(_silu(cond_ref[...]).astype(BF16),
                             wada_ref[...].astype(BF16)) + bada_ref[...]


def _mix_ffn_kernel(x_ref, att_ref, gla_ref, wo_ref, mod_ref, npre_ref, npost_ref,
                    w1_ref, w3_ref, w2_ref, o_ref, *, tiles_per_seq):
    mod = _Mod(mod_ref, tiles_per_seq, first_mod=EARLY_MODS)
    mix = _dot(att_ref[...], wo_ref[0:ATT_Q, :]) + _dot(gla_ref[...], wo_ref[ATT_Q:, :])
    x = _residual(x_ref[...], mix, mod, npost_ref, 1, 1.0)
    o_ref[...] = _ffn_sublayer(x, 2, mod, npre_ref, npost_ref, w1_ref, w3_ref, w2_ref)


def _rope_tile(x, cos, sin_up, sin_dn):
    up = pltpu.roll(x, LANES - ROPE_PAIR, axis=1)
    dn = pltpu.roll(x, ROPE_PAIR, axis=1)
    return x * cos + up * sin_up + dn * sin_dn


def _proj_kernel(*refs, tiles_per_seq, cache_seq):
    x_ref, mod_ref, npre_ref, win_ref, wup_ref, bup_ref = refs[:6]
    rope_refs = refs[6:9] if tiles_per_seq is not None else None
    outs = refs[6 + (3 if rope_refs else 0):]
    att_ref, glaf_ref, gv_ref = outs[:3]
    mod = _Mod(mod_ref, tiles_per_seq, first_mod=EARLY_MODS)
    h = _modulated(x_ref[...], mod, npre_ref, 1).astype(BF16)
    q = _dot(h, win_ref[:, OFF_Q:OFF_K])
    kv = _dot(h, win_ref[:, OFF_K:OFF_GQ])
    k, v = kv[:, :ATT_KV], kv[:, ATT_KV:]
    lr = _dot(h, win_ref[:, OFF_LR:IN_WIDTH])
    z = _dot(lr.astype(BF16), wup_ref[...]) + bup_ref[...]
    if rope_refs:
        cos, sup, sdn = (r[...] for r in rope_refs)
        q = jnp.concatenate([_rope_tile(q[:, j * LANES:(j + 1) * LANES], cos, sup, sdn)
                             for j in range(ATT_Q // LANES)], axis=1)
        k_att = _rope_tile(k, cos, sup, sdn)
    else:
        k_att = k
    att_ref[:, A_Q:A_K] = (q * ATT_Q_SCALE).astype(BF16)
    att_ref[:, A_K:A_KS] = k_att.astype(BF16)
    att_ref[:, A_KS:A_V] = pltpu.roll(k_att, HALF, axis=1).astype(BF16)
    att_ref[:, A_V:A_VS] = v.astype(BF16)
    att_ref[:, A_VS:ATT_IN_W] = pltpu.roll(v, HALF, axis=1).astype(BF16)
    if cache_seq is not None:
        for t, dst in ((k.T, outs[3]), (v.T, outs[4])):
            for b in range(TOKEN_TILE // cache_seq):
                dst[b, 0] = t[:, b * cache_seq:(b + 1) * cache_seq].reshape(
                    N_KV_HEADS, HEAD_DIM, cache_seq)
    glaf_ref[:, G_Q:G_G] = _dot(h, win_ref[:, OFF_GQ:OFF_GV])
    log_sig = jnp.minimum(z, 0.0) - jnp.log(1.0 + jnp.exp(-jnp.abs(z)))
    glaf_ref[:, G_LA:GLA_F_W] = log_sig * (1.0 / GLA_TAU)
    gv_ref[...] = _dot(h, win_ref[:, OFF_GV:OFF_GG]).astype(BF16)
    glaf_ref[:, G_G:G_LA] = _dot(h, win_ref[:, OFF_GG:OFF_LR])


def _token_spec(width, first_tile=0):
    return pl.BlockSpec((TOKEN_TILE, width), lambda i: (i + first_tile, 0))


def _stack_spec(shape, j):
    return pl.BlockSpec((None,) + tuple(shape[1:]), lambda i: (j, 0, 0),
                        pipeline_mode=pl.Buffered(1))


def _tiles_per_seq(latent_len):
    return None if latent_len is None else latent_len // TOKEN_TILE


def _ffn_first(x_ctx, x_lat, mod, latent_len, npre, npost, w1, w3, w2, ada, cast):
    ctx_tiles = x_ctx.shape[0] // TOKEN_TILE
    steps = ctx_tiles + x_lat.shape[0] // TOKEN_TILE
    assert steps >= CAST_SLABS, "every slab needs its own grid step"
    cond, w_ada, b_ada = ada
    late_mods = N_MOD - EARLY_MODS
    ada_cols = late_mods * D_MODEL // steps
    per_mod = D_MODEL // ada_cols
    assert ada_cols * steps == late_mods * D_MODEL and ada_cols % LANES == 0
    first_slab = EARLY_MODS * per_mod
    in_specs = [pl.BlockSpec((TOKEN_TILE, D_MODEL), lambda i: (jnp.minimum(i, ctx_tiles - 1), 0)),
                pl.BlockSpec((TOKEN_TILE, D_MODEL), lambda i: (jnp.maximum(i - ctx_tiles, 0), 0)),
                _resident(mod.shape), _resident(npre.shape),
                _resident(npost.shape), _stack_spec(w1.shape, 0),
                _stack_spec(w3.shape, 0), _stack_spec(w2.shape, 0),
                _resident(cond.shape),
                pl.BlockSpec((D_MODEL, ada_cols), lambda i: (0, first_slab + i)),
                pl.BlockSpec((1, ada_cols), lambda i: (0, first_slab + i))]
    out_shape = [jax.ShapeDtypeStruct((steps * TOKEN_TILE, D_MODEL), F32),
                 jax.ShapeDtypeStruct((late_mods, MOD_ROWS, D_MODEL), F32)]
    out_specs = [_token_spec(D_MODEL),
                 pl.BlockSpec((None, MOD_ROWS, ada_cols), lambda i: (i // per_mod, 0, i % per_mod))]
    for stack, j in cast:
        _, rows, cols = stack.shape
        slab = rows // CAST_SLABS
        assert slab * CAST_SLABS == rows and slab % (2 * SUBLANES) == 0, "whole bf16 row tiles"
        in_specs.append(pl.BlockSpec((None, slab, cols), functools.partial(
            lambda i, j: (j, i * CAST_SLABS // steps, 0), j=j)))
        out_shape.append(jax.ShapeDtypeStruct((1, rows, cols), BF16))
        out_specs.append(pl.BlockSpec((None, slab, cols), lambda i: (0, i * CAST_SLABS // steps, 0)))
    return pl.pallas_call(
        functools.partial(_ffn_first_kernel, ctx_tiles=ctx_tiles,
                          tiles_per_seq=_tiles_per_seq(latent_len), n_cast=len(cast)),
        out_shape=out_shape,
        grid=(steps,),
        in_specs=in_specs,
        out_specs=out_specs,
        compiler_params=_params(1, FFN_FIRST_VMEM_MIB),
        name="ffn_first",
    )(x_ctx, x_lat, mod, npre, npost, w1, w3, w2, cond, w_ada, b_ada,
      *[stack for stack, _ in cast])


def _mix_ffn(x, first_tile, att, gla, w_out, mod, latent_len, npre, npost, w1, w3, w2):
    t = att.shape[0]
    return pl.pallas_call(
        functools.partial(_mix_ffn_kernel, tiles_per_seq=_tiles_per_seq(latent_len)),
        out_shape=jax.ShapeDtypeStruct((t, D_MODEL), F32),
        grid=(t // TOKEN_TILE,),
        in_specs=[_token_spec(D_MODEL, first_tile), _token_spec(ATT_Q), _token_spec(GLA_V),
                  _stack_spec(w_out.shape, 0), _resident(mod.shape), _resident(npre.shape),
                  _resident(npost.shape), _stack_spec(w1.shape, 0),
                  _stack_spec(w3.shape, 0), _stack_spec(w2.shape, 0)],
        out_specs=_token_spec(D_MODEL),
        compiler_params=_params(1, MIX_FFN_VMEM_MIB),
        name="mix_ffn",
    )(x, att, gla, w_out, mod, npre, npost, w1, w3, w2)


def _project(x, first_tile, t, mod, latent_len, cache_seq, npre, w_in, w_up, b_up):
    tiles = _tiles_per_seq(latent_len)
    in_specs = [_token_spec(D_MODEL, first_tile), _resident(mod.shape), _resident(npre.shape),
                _stack_spec(w_in.shape, 0), _resident(w_up.shape), _resident(b_up.shape)]
    args = [x, mod, npre, w_in, w_up, b_up]
    if tiles is not None:
        in_specs += [pl.BlockSpec((TOKEN_TILE, LANES), lambda i: (i % tiles, 0))] * 3
        args += [jnp.asarray(tab) for tab in _rope_tables(latent_len)]
    outs = ((ATT_IN_W, BF16), (GLA_F_W, F32), (GLA_V, BF16))
    out_shape = [jax.ShapeDtypeStruct((t, w), dt) for w, dt in outs]
    out_specs = [_token_spec(w) for w, _ in outs]
    if cache_seq is not None:
        seqs = TOKEN_TILE // cache_seq
        cache = (t // cache_seq, 1, N_KV_HEADS, HEAD_DIM, cache_seq)
        out_shape += [jax.ShapeDtypeStruct(cache, F32)] * 2
        out_specs += [pl.BlockSpec((seqs,) + cache[1:], lambda i: (i, 0, 0, 0, 0))] * 2
    return pl.pallas_call(
        functools.partial(_proj_kernel, tiles_per_seq=tiles, cache_seq=cache_seq),
        out_shape=out_shape,
        grid=(t // TOKEN_TILE,),
        in_specs=in_specs,
        out_specs=out_specs,
        compiler_params=_params(1, PROJECT_VMEM_MIB),
        name="project",
    )(*args)


def _rope_tables(seq_len):
    half = HEAD_DIM // 2
    inv_freq = np.float32(ROPE_BASE) ** (-np.arange(0, half, 2, dtype=np.float32) / half)
    pos = np.arange(seq_len)
    row = (pos // GRID_W).astype(np.float32)
    col = (pos % GRID_W).astype(np.float32)
    within = np.arange(LANES) % HEAD_DIM
    idx = within % half
    freq = inv_freq[idx % ROPE_PAIR].astype(np.float32)
    p = np.where((within // half == 0)[None, :], row[:, None], col[:, None])
    ang = (p * freq[None, :]).astype(np.float32)
    cos, sin = np.cos(ang).astype(np.float32), np.sin(ang).astype(np.float32)
    first = (idx < ROPE_PAIR)[None, :]
    zero = np.float32(0.0)
    return cos, np.where(first, -sin, zero), np.where(first, zero, sin)


def _attend(q_ref, q_rows, nq, segs, sink_ref):
    lane = lax.broadcasted_iota(jnp.int32, (1, LANES), 1)
    half_of = [lane < HALF, lane >= HALF]
    rows = lax.broadcasted_iota(jnp.int32, (2 * nq, 1), 0)
    zero = jnp.zeros((), BF16)
    out = [None] * (ATT_Q // LANES)
    for g in range(N_KV_HEADS):
        tiles = (2 * g, 2 * g + 1)
        for e in range(2):
            qm = jnp.concatenate(
                [jnp.where(half_of[e], q_ref[q_rows, j * LANES:(j + 1) * LANES], zero)
                 for j in tiles], axis=0)
            sink = jnp.where(rows < nq, sink_ref[2 * tiles[0] + e],
                             sink_ref[2 * tiles[1] + e]) * LOG2_E
            scores = []
            for k, k_sw, _, _, m in segs:
                s = _dot_t(qm, k if e == g else k_sw)
                scores.append(s if m is None else jnp.where(m, s, NEG))
            mx = sink
            for s in scores:
                mx = jnp.maximum(mx, jnp.max(s, axis=-1, keepdims=True))
            probs = [jnp.exp2(s - mx) for s in scores]
            den = jnp.exp2(sink - mx)
            for p in probs:
                den = den + jnp.sum(p, axis=-1, keepdims=True)
            o = None
            for p, (_, _, v, v_sw, _) in zip(probs, segs):
                part = _dot(p.astype(BF16), jnp.where(half_of[e], v if e == g else v_sw, zero))
                o = part if o is None else o + part
            o = o * (1.0 / den)
            for r, j in enumerate(tiles):
                blk = o[r * nq:(r + 1) * nq, :]
                out[j] = blk if out[j] is None else out[j] + blk
    return out


def _context_attn_kernel(sink_ref, a_ref, o_ref, *, seq_len):
    for s in range(a_ref.shape[0] // seq_len):
        rows = slice(s * seq_len, (s + 1) * seq_len)
        segs = [(a_ref[rows, A_K:A_KS], a_ref[rows, A_KS:A_V],
                 a_ref[rows, A_V:A_VS], a_ref[rows, A_VS:ATT_IN_W], None)]
        for j, t in enumerate(_attend(a_ref, rows, seq_len, segs, sink_ref)):
            o_ref[rows, j * LANES:(j + 1) * LANES] = t.astype(o_ref.dtype)


def _latent_attn_kernel(sink_ref, q_ref, kv_ref, kc_ref, vc_ref, o_ref, *, seq_len):
    k_ctx, v_ctx = kc_ref[0], vc_ref[0]
    ctx = (k_ctx.astype(BF16), pltpu.roll(k_ctx, HALF, axis=1).astype(BF16),
           v_ctx.astype(BF16), pltpu.roll(v_ctx, HALF, axis=1).astype(BF16), None)
    span = 3 * BLOCK
    per_step = q_ref.shape[0] // BLOCK
    for s in range(per_step):
        i = pl.program_id(1) * per_step + s
        start = pl.multiple_of(jnp.clip((i - 1) * BLOCK, 0, seq_len - span), BLOCK)
        kpos = start + lax.broadcasted_iota(jnp.int32, (1, span), 1)
        qpos = i * BLOCK + lax.broadcasted_iota(jnp.int32, (BLOCK, 1), 0)
        valid = jnp.abs(kpos - qpos) <= WINDOW
        valid2 = jnp.concatenate([valid, valid], axis=0)
        kv = kv_ref[pl.ds(start, span), :]
        segs = [tuple(kv[:, c * ATT_KV:(c + 1) * ATT_KV] for c in range(4)) + (valid2,), ctx]
        rows = slice(s * BLOCK, (s + 1) * BLOCK)
        for j, t in enumerate(_attend(q_ref, rows, BLOCK, segs, sink_ref)):
            o_ref[rows, j * LANES:(j + 1) * LANES] = t.astype(o_ref.dtype)


def _smem_spec():
    return pl.BlockSpec(memory_space=pltpu.SMEM)


def _context_attention(sink, att_in, batch, seq_len):
    rows = CTX_SEQS_PER_STEP * seq_len

    def seq(width):
        return pl.BlockSpec((rows, width), lambda b: (b, 0))
    return pl.pallas_call(
        functools.partial(_context_attn_kernel, seq_len=seq_len),
        out_shape=jax.ShapeDtypeStruct((batch * seq_len, ATT_Q), BF16),
        grid=(batch // CTX_SEQS_PER_STEP,),
        in_specs=[_smem_spec(), seq(ATT_IN_W)],
        out_specs=seq(ATT_Q),
        compiler_params=_params(1),
        name="context_attention",
    )(sink, att_in)


def _latent_attention(sink, att_in, k_ctx, v_ctx, batch, seq_len):
    steps = seq_len // (LAT_BLOCKS_PER_STEP * BLOCK)
    rows = LAT_BLOCKS_PER_STEP * BLOCK
    past = k_ctx.shape[1]
    assert A_K == ATT_IN_W - A_K, "q and the key / value columns are the two halves of att_in"
    return pl.pallas_call(
        functools.partial(_latent_attn_kernel, seq_len=seq_len),
        out_shape=jax.ShapeDtypeStruct((batch * seq_len, ATT_Q), BF16),
        grid=(batch, steps),
        in_specs=[_smem_spec(),
                  pl.BlockSpec((rows, ATT_Q), lambda b, i: (b * steps + i, 0)),
                  pl.BlockSpec((seq_len, ATT_IN_W - A_K), lambda b, i: (b, 1)),
                  pl.BlockSpec((1, past, ATT_KV), lambda b, i: (b, 0, 0)),
                  pl.BlockSpec((1, past, ATT_KV), lambda b, i: (b, 0, 0))],
        out_specs=pl.BlockSpec((rows, ATT_Q), lambda b, i: (b * steps + i, 0)),
        compiler_params=_params(2),
        name="latent_attention",
    )(sink, att_in, att_in, k_ctx, v_ctx)


def _split2(x):
    hi = x.astype(BF16)
    lo = (x - hi.astype(F32)).astype(BF16)
    return hi, lo


def _gla_kernel(*refs, n_chunks, has_s0, emit_state):
    refs = list(refs)
    gf_ref, gv_ref, gn_ref = refs[:3]
    pos = 3
    s0_refs = refs[pos:pos + 2] if has_s0 else None
    pos += 2 if has_s0 else 0
    o_ref = refs[pos]
    pos += 1
    sfin_refs = refs[pos:pos + 2] if emit_state else None
    pos += 2 if emit_state else 0
    cum_ref, kv_ref, sent_ref = refs[pos:]

    C = GLA_CHUNK
    n_pairs = GLA_QK // LANES
    lane = lax.broadcasted_iota(jnp.int32, (1, LANES), 1)
    half_of = [lane < HALF, lane >= HALF]
    r_i = lax.broadcasted_iota(jnp.int32, (C, C), 0)
    c_i = lax.broadcasted_iota(jnp.int32, (C, C), 1)
    lower = c_i <= r_i
    upper = c_i >= r_i
    tri = jnp.concatenate([jnp.where(lower, 1.0, 0.0), jnp.where(upper, 1.0, 0.0)],
                          axis=0).astype(BF16)
    zeros_cc = jnp.zeros((C, C), F32)
    gnorm = gn_ref[...]
    qscale = GLA_DK ** -0.5

    def one_sequence(s):
        def chunk_rows(n):
            return slice((s * n_chunks + n) * C, (s * n_chunks + n + 1) * C)

        for n in range(n_chunks):
            rows = chunk_rows(n)
            hi, lo = _split2(gf_ref[rows, G_LA:GLA_F_W])
            sums = _dot(tri, jnp.concatenate([hi, lo], axis=1))
            cum_f = sums[:C, 0:GLA_QK] + sums[:C, 2 * GLA_QK:3 * GLA_QK]
            cum_b = sums[C:, GLA_QK:2 * GLA_QK] + sums[C:, 3 * GLA_QK:]
            cum_ref[0, rows, :] = cum_f
            cum_ref[1, rows, :] = cum_b
            k = gf_ref[rows, G_K:G_G]
            k_in = (k * jnp.exp(cum_f[C - 1:C, :] - cum_f), k * jnp.exp(cum_b[0:1, :] - cum_b))
            for p in range(n_pairs):
                sl = slice(p * LANES, (p + 1) * LANES)
                kv_t = None
                for e in range(2):
                    h = 2 * p + e
                    v_t = gv_ref[rows, h * GLA_DV:(h + 1) * GLA_DV].T
                    k_e = jnp.concatenate(
                        [jnp.where(half_of[e], k_in[d][:, sl], 0.0) for d in range(2)], axis=1)
                    part = _dot(v_t, k_e.astype(BF16))
                    kv_t = part if kv_t is None else kv_t + part
                kv_ref[s, n, p] = kv_t

        for p in range(n_pairs):
            sl = slice(p * LANES, (p + 1) * LANES)
            st = []
            for d in range(2):
                if has_s0:
                    s0 = s0_refs[d][s, 0, 2 * p:2 * p + 2, :, :].reshape(2 * GLA_DK, GLA_DV)
                    st.append(s0.T)
                else:
                    st.append(jnp.zeros((GLA_DV, 2 * GLA_DK), F32))
            for i in range(n_chunks):
                for d, n in ((0, i), (1, n_chunks - 1 - i)):
                    tot_row = (s * n_chunks + n) * C + (C - 1 if d == 0 else 0)
                    decay = jnp.exp(cum_ref[d, tot_row:tot_row + 1, sl])
                    sent_ref[s, n, p, :, d * LANES:(d + 1) * LANES] = st[d].astype(BF16)
                    st[d] = decay * st[d] + kv_ref[s, n, p, :, d * LANES:(d + 1) * LANES]
            if emit_state:
                for d in range(2):
                    sfin_refs[d][s, 0, 2 * p:2 * p + 2, :, :] = st[d].T.reshape(2, GLA_DK, GLA_DV)

        for n in range(n_chunks):
            rows = chunk_rows(n)
            q = gf_ref[rows, G_Q:G_K] * qscale
            k = gf_ref[rows, G_K:G_G]
            qs, ks, qin = [], [], []
            for d in range(2):
                cum = cum_ref[d, rows, :]
                ref = cum[C // 2:C // 2 + 1, :]
                qs.append(q * jnp.exp(cum - ref))
                ks.append((k * jnp.exp(ref - cum)).astype(BF16))
                qin.append(q * jnp.exp(cum))
            for p in range(n_pairs):
                sl = slice(p * LANES, (p + 1) * LANES)
                lhs = jnp.concatenate(
                    [jnp.where(half_of[e], qs[d][:, sl], 0.0) for e in range(2) for d in range(2)],
                    axis=0).astype(BF16)
                sc = _dot_t(lhs, jnp.concatenate([ks[0][:, sl], ks[1][:, sl]], axis=0))
                prob = []
                for e in range(2):
                    s_f = sc[(2 * e) * C:(2 * e + 1) * C, :C]
                    s_b = sc[(2 * e + 1) * C:(2 * e + 2) * C, C:]
                    prob.append(jnp.where(lower, s_f, 0.0) + jnp.where(upper, s_b, 0.0))
                p_blk = jnp.concatenate(
                    [jnp.concatenate([prob[0], zeros_cc], axis=1),
                     jnp.concatenate([zeros_cc, prob[1]], axis=1)], axis=0).astype(BF16)
                v2 = jnp.concatenate(
                    [gv_ref[rows, (2 * p + e) * GLA_DV:(2 * p + e + 1) * GLA_DV] for e in range(2)],
                    axis=0)
                q_in = jnp.concatenate(
                    [jnp.concatenate([jnp.where(half_of[e], qin[d][:, sl], 0.0) for d in range(2)],
                                     axis=1) for e in range(2)], axis=0).astype(BF16)
                o2 = _dot(p_blk, v2) + _dot_t(q_in, sent_ref[s, n, p])
                for e in range(2):
                    h = 2 * p + e
                    o = o2[e * C:(e + 1) * C, :]
                    o = o * lax.rsqrt(jnp.mean(o * o, axis=-1, keepdims=True) + EPS) * gnorm
                    gate = gf_ref[rows, G_G + h * GLA_DV:G_G + (h + 1) * GLA_DV]
                    o_ref[rows, h * GLA_DV:(h + 1) * GLA_DV] = (o * _silu(gate)).astype(o_ref.dtype)

    for s in range(kv_ref.shape[0]):
        one_sequence(s)


def _gla(gla_f, gv, gnorm, batch, seq_len, seqs_per_step, s0=None, emit_state=False):
    n_chunks = seq_len // GLA_CHUNK
    has_s0 = s0 is not None
    rows = seqs_per_step * seq_len

    def seq(width):
        return pl.BlockSpec((rows, width), lambda b: (b, 0))
    state_spec = pl.BlockSpec((seqs_per_step, 1, GLA_HEADS, GLA_DK, GLA_DV),
                              lambda b: (b, 0, 0, 0, 0))
    in_specs = [seq(GLA_F_W), seq(GLA_V), pl.BlockSpec((1, GLA_DV), lambda b: (0, 0))]
    args = [gla_f, gv, gnorm]
    if has_s0:
        in_specs += [state_spec, state_spec]
        args += list(s0)
    out_shape = [jax.ShapeDtypeStruct((batch * seq_len, GLA_V), BF16)]
    out_specs = [seq(GLA_V)]
    if emit_state:
        out_shape += [jax.ShapeDtypeStruct((batch, 1, GLA_HEADS, GLA_DK, GLA_DV), F32)] * 2
        out_specs += [state_spec, state_spec]
    n_pairs = GLA_QK // LANES
    return pl.pallas_call(
        functools.partial(_gla_kernel, n_chunks=n_chunks, has_s0=has_s0, emit_state=emit_state),
        out_shape=out_shape,
        grid=(batch // seqs_per_step,),
        in_specs=in_specs,
        out_specs=out_specs,
        scratch_shapes=[pltpu.VMEM((2, rows, GLA_QK), F32),
                        pltpu.VMEM((seqs_per_step, n_chunks, n_pairs, GLA_DV, 2 * LANES), F32),
                        pltpu.VMEM((seqs_per_step, n_chunks, n_pairs, GLA_DV, 2 * LANES), BF16)],
        compiler_params=_params(1),
        name="gla",
    )(*args)


def kernel(x_prompt, x_sample, cache_k, cache_v, state_gla_fwd, state_gla_bwd, c, c_ctx,
           w_ada, b_ada, norm_pre, norm_post, ffn_w1, ffn_w3, ffn_w2, w_in,
           gla_w_up, gla_b_up, gla_norm, attn_sink, w_out):
    depth = w_in.shape[0]
    assert depth == 1, "single trunk layer"
    batch, seq = x_prompt.shape[0], x_prompt.shape[1]
    dec_batch, dec_seq = x_sample.shape[0], x_sample.shape[1]
    past = cache_k.shape[2]
    l = 0

    cond = jnp.concatenate(
        [c_ctx[None, :], c, jnp.zeros((MOD_ROWS - 1 - dec_batch, D_MODEL), F32)], axis=0)
    ada = (cond, w_ada[l], b_ada[l][None, :])
    mod_early = _ada_modulation(*ada, EARLY_MODS)

    npre, npost = norm_pre[l], norm_post[l]
    ffn_first_b = [w[l, :1].astype(BF16) for w in (ffn_w1, ffn_w3, ffn_w2)]
    ffn_second_f32 = [(w[l], 1) for w in (ffn_w1, ffn_w3, ffn_w2)]
    w_in_b = w_in[l:l + 1].astype(BF16)
    w_out_b = w_out[l:l + 1].astype(BF16)
    zeros = jnp.zeros((GLA_LOW_RANK, GLA_QK), F32)
    w_up = jnp.concatenate(
        [jnp.concatenate([gla_w_up[l, 0], zeros], axis=1),
         jnp.concatenate([zeros, gla_w_up[l, 1]], axis=1)], axis=0).astype(BF16)
    b_up = gla_b_up[l].reshape(1, 2 * GLA_QK)
    gnorm = gla_norm[l][None, :]
    sink = attn_sink[l]

    x1, mod, *ffn_second_b = _ffn_first(
        x_prompt.reshape(batch * seq, D_MODEL), x_sample.reshape(dec_batch * dec_seq, D_MODEL),
        mod_early, dec_seq, npre, npost, *ffn_first_b, ada=ada, cast=ffn_second_f32)
    ctx_tiles = batch * seq // TOKEN_TILE

    def trunk(latent):
        n_batch, n_seq = (dec_batch, dec_seq) if latent else (batch, seq)
        latent_len = n_seq if latent else None
        first_tile = ctx_tiles if latent else 0
        att_in, gla_f, gv, *cache_t = _project(
            x1, first_tile, n_batch * n_seq, mod, latent_len, None if latent else n_seq, npre,
            w_in_b, w_up, b_up)
        if latent:
            att = _latent_attention(sink, att_in, cache_k[:, l].reshape(dec_batch, past, ATT_KV),
                                    cache_v[:, l].reshape(dec_batch, past, ATT_KV), n_batch, n_seq)
            (gla,) = _gla(gla_f, gv, gnorm, n_batch, n_seq, 1,
                          s0=(state_gla_fwd[:, l:l + 1], state_gla_bwd[:, l:l + 1]))
            extras = ()
        else:
            att = _context_attention(sink, att_in, n_batch, n_seq)
            gla, s_f, s_b = _gla(gla_f, gv, gnorm, n_batch, n_seq, CTX_SEQS_PER_STEP,
                                 emit_state=True)
            k_new, v_new = (jnp.transpose(c_t, (0, 1, 4, 2, 3)) for c_t in cache_t)
            extras = (k_new, v_new, s_f, s_b)
        y = _mix_ffn(x1, first_tile, att, gla, w_out_b, mod, latent_len, npre, npost,
                     *ffn_second_b)
        return y.reshape(n_batch, n_seq, D_MODEL), extras

    y_prompt, (k_new, v_new, s_f, s_b) = trunk(False)
    y_sample, _ = trunk(True)
    return (y_prompt, y_sample, k_new, v_new, s_f, s_b)
```
